```python
import math
import jax, jax.numpy as jnp
from jax import lax
import numpy as np

D_MODEL = 2048
BATCH = 1
SEQ = 8192
DEPTH = 1
DEC_BATCH = 4
DEC_SEQ = 4096
PAST_LEN = 128

HEAD_DIM = 128
A_HEADS = 16
A_KV = 4
A_GROUP = A_HEADS // A_KV
A_HALF = 128
A_Q = A_HEADS * HEAD_DIM
A_KVW = A_KV * HEAD_DIM
DILATED_GROUPS = ((128, 1), (512, 4), (2048, 16))
N_DIL = len(DILATED_GROUPS)
B_HPG = 4
B_HEADS = N_DIL * B_HPG
B_W = B_HEADS * HEAD_DIM
B_OUT = B_HPG * HEAD_DIM
IN_COLS = A_Q + 2 * A_KVW + 3 * B_W
SPLITS = (A_Q, A_Q + A_KVW, A_Q + 2 * A_KVW, A_Q + 2 * A_KVW + B_W, A_Q + 2 * A_KVW + 2 * B_W)
N_GROUPS = 8
EXPERTS_PER_GROUP = 8
N_EXPERTS = N_GROUPS * EXPERTS_PER_GROUP
TOP_K = 2
D_FF = 512
ROW_BLOCK = 128
EPS = 1e-6

kernel_name = "hybrid_window_dilated_hmoe_encoder"


def rms_norm(x, g):
    x32 = x.astype(jnp.float32)
    y = x32 * lax.rsqrt(jnp.mean(x32 * x32, axis=-1, keepdims=True) + EPS)
    return (y * g.astype(jnp.float32)).astype(x.dtype)


def alibi_slopes(n):
    return jnp.asarray(np.power(2.0, -8.0 * (np.arange(n) + 1) / n), dtype=jnp.float32)


def banded_attention(q, k, v, slopes, half, step):
    n, L, hk, g, dh = q.shape
    blk = half
    nb = -(-L // blk)
    lp = nb * blk
    pad = lp - L
    qp = jnp.pad(q, ((0, 0), (0, pad), (0, 0), (0, 0), (0, 0))).reshape(n, nb, blk, hk, g, dh)
    kp = jnp.pad(k, ((0, 0), (blk, pad + blk), (0, 0), (0, 0))).reshape(n, nb + 2, blk, hk, dh)
    vp = jnp.pad(v, ((0, 0), (blk, pad + blk), (0, 0), (0, 0))).reshape(n, nb + 2, blk, hk, dh)
    kb = jnp.concatenate([kp[:, :-2], kp[:, 1:-1], kp[:, 2:]], axis=2)
    vb = jnp.concatenate([vp[:, :-2], vp[:, 1:-1], vp[:, 2:]], axis=2)
    s = jnp.einsum('nbqhgd,nbkhd->nbhgqk', qp, kb, preferred_element_type=jnp.float32) * (dh ** -0.5)
    qi = jnp.arange(blk)
    kj = jnp.arange(3 * blk)
    rel = kj[None, :] - blk - qi[:, None]
    kpos = jnp.arange(nb)[:, None] * blk - blk + kj[None, :]
    valid = (jnp.abs(rel) <= half)[None] & ((kpos >= 0) & (kpos < L))[:, None, :]
    dist = (step * jnp.abs(rel)).astype(jnp.float32)
    s = s - slopes.astype(jnp.float32)[:, :, None, None] * dist
    s = jnp.where(valid[None, :, None, None], s, -jnp.inf)
    m = jnp.max(s, axis=-1, keepdims=True)
    p = jnp.exp(s - m)
    l = jnp.sum(p, axis=-1, keepdims=True)
    o = jnp.einsum('nbhgqk,nbkhd->nbhgqd', p, vb.astype(jnp.float32)) / l
    lse = (m + jnp.log(l))[..., 0]
    o = o.transpose(0, 1, 4, 2, 3, 5).reshape(n, lp, hk, g, dh)[:, :L]
    lse = lse.transpose(0, 1, 4, 2, 3).reshape(n, lp, hk, g)[:, :L]
    return o, lse


def window_gqa(q, k, v, sink):
    b, s, _ = q.shape
    qh = q.reshape(b, s, A_KV, A_GROUP, HEAD_DIM)
    kh = k.reshape(b, s, A_KV, HEAD_DIM)
    vh = v.reshape(b, s, A_KV, HEAD_DIM)
    slopes = alibi_slopes(A_HEADS).reshape(A_KV, A_GROUP)
    o, lse = banded_attention(qh, kh, vh, slopes, A_HALF, 1)
    sk = sink.astype(jnp.float32).reshape(A_KV, A_GROUP)
    lse_tot = jnp.logaddexp(lse, sk)
    o = o * jnp.exp(lse - lse_tot)[..., None]
    return o.reshape(b, s, A_Q).astype(q.dtype)


def dilated_attention(q, k, v):
    b, s, _ = q.shape
    slopes = alibi_slopes(B_HEADS).reshape(N_DIL, B_HPG)
    outs, lses = [], []
    for gi, (w, r) in enumerate(DILATED_GROUPS):
        lo, hi = gi * B_HPG * HEAD_DIM, (gi + 1) * B_HPG * HEAD_DIM
        L = s // r

        def strided(t):
            return t[..., lo:hi].reshape(b, L, r, B_HPG, HEAD_DIM).transpose(0, 2, 1, 3, 4).reshape(b * r, L, B_HPG, HEAD_DIM)

        qg, kg, vg = strided(q), strided(k), strided(v)
        o, lse = banded_attention(qg[:, :, :, None], kg, vg, slopes[gi][:, None], w // (2 * r), r)
        o = o[:, :, :, 0].reshape(b, r, L, B_HPG, HEAD_DIM).transpose(0, 2, 1, 3, 4).reshape(b, s, B_HPG, HEAD_DIM)
        lse = lse[:, :, :, 0].reshape(b, r, L, B_HPG).transpose(0, 2, 1, 3).reshape(b, s, B_HPG)
        outs.append(o)
        lses.append(lse)
    alpha = jax.nn.softmax(jnp.stack(lses), axis=0)
    o = jnp.sum(alpha[..., None] * jnp.stack(outs), axis=0)
    return o.reshape(b, s, B_OUT).astype(q.dtype)


def hier_moe(xn, w_rg, b_rg, w_re, b_re, w1, w3, w2):
    bsz, s, d = xn.shape
    xt = xn.reshape(-1, d)
    n = xt.shape[0]
    lg = jnp.matmul(xt, w_rg, preferred_element_type=jnp.float32) + b_rg.astype(jnp.float32)
    pg = jax.nn.softmax(lg, axis=-1)
    grp = jnp.argmax(lg, axis=-1)
    pgrp = jnp.take_along_axis(pg, grp[:, None], axis=1)[:, 0]
    le = jnp.einsum('nd,gde->nge', xt, w_re, preferred_element_type=jnp.float32) + b_re.astype(jnp.float32)
    le_sel = jnp.take_along_axis(le, grp[:, None, None], axis=1)[:, 0]
    top_l, top_i = lax.top_k(le_sel, TOP_K)
    wts = jax.nn.softmax(top_l, axis=-1) * pgrp[:, None]
    eid = grp[:, None] * EXPERTS_PER_GROUP + top_i
    flat_e = eid.reshape(-1).astype(jnp.int32)
    flat_w = wts.reshape(-1)
    flat_t = jnp.repeat(jnp.arange(n, dtype=jnp.int32), TOP_K)
    order = jnp.argsort(flat_e)
    se, st, sw = flat_e[order], flat_t[order], flat_w[order]
    counts = jnp.bincount(flat_e, length=N_EXPERTS)
    starts = jnp.cumsum(counts) - counts
    pcounts = (counts + ROW_BLOCK - 1) // ROW_BLOCK * ROW_BLOCK
    pends = jnp.cumsum(pcounts)
    pstarts = pends - pcounts
    dest = pstarts[se] + (jnp.arange(n * TOP_K) - starts[se])
    nblk = (n * TOP_K + N_EXPERTS * (ROW_BLOCK - 1) + ROW_BLOCK - 1) // ROW_BLOCK
    rows = nblk * ROW_BLOCK
    row_tok = jnp.full((rows,), n, dtype=jnp.int32).at[dest].set(st)
    row_w = jnp.zeros((rows,), jnp.float32).at[dest].set(sw)
    blk_e = jnp.minimum(jnp.searchsorted(pends, jnp.arange(nblk) * ROW_BLOCK, side='right'), N_EXPERTS - 1)
    xpad = jnp.concatenate([xt, jnp.zeros((1, d), xt.dtype)], axis=0)
    xr = xpad[row_tok].reshape(nblk, ROW_BLOCK, d)

    def expert_rows(args):
        xb, e = args
        h = jax.nn.silu(xb @ w1[e]) * (xb @ w3[e])
        return h @ w2[e]

    yr = lax.map(expert_rows, (xr, blk_e)).reshape(rows, d)
    y = jnp.zeros((n + 1, d), jnp.float32).at[row_tok].add(row_w[:, None] * yr.astype(jnp.float32))[:n]
    return y.reshape(bsz, s, d).astype(xn.dtype)


def encoder_layer(x, norm1, w_in, sink, w_proj_a, w_proj_b, w_gate, b_gate, w_out,
                  norm2, w_rg, b_rg, w_re, b_re, w1, w3, w2):
    xn = rms_norm(x, norm1)
    proj = xn @ w_in
    qa, ka, va, qb, kb, vb = jnp.split(proj, SPLITS, axis=-1)
    oa = window_gqa(qa, ka, va, sink)
    ob = dilated_attention(qb, kb, vb)
    gates = jax.nn.sigmoid((xn @ w_gate + b_gate).astype(jnp.float32))
    ga, gb = jnp.split(gates, 2, axis=-1)
    merged = ga * (oa @ w_proj_a).astype(jnp.float32) + gb * (ob @ w_proj_b).astype(jnp.float32)
    h = x + merged.astype(x.dtype) @ w_out
    return h + hier_moe(rms_norm(h, norm2), w_rg, b_rg, w_re, b_re, w1, w3, w2)


def trunk(x, norm1, w_in, sink, w_proj_a, w_proj_b, w_gate, b_gate, w_out,
          norm2, w_rg, b_rg, w_re, b_re, w1, w3, w2, norm_final):
    for l in range(DEPTH):
        x = encoder_layer(x, norm1[l], w_in[l], sink[l], w_proj_a[l], w_proj_b[l], w_gate[l], b_gate[l],
                          w_out[l], norm2[l], w_rg[l], b_rg[l], w_re[l], b_re[l], w1[l], w3[l], w2[l])
    return rms_norm(x, norm_final)


def setup_inputs(seed: int = 0) -> dict:
    key = jax.random.key(seed)
    ks = jax.random.split(key, 20)
    f32 = jnp.float32
    D = D_MODEL

    def nrm(k, shape, scale):
        return jax.random.normal(k, shape, f32) * scale

    return {
        "x_prompt": jax.random.normal(ks[0], (BATCH, SEQ, D), f32),
        "x_sample": jax.random.normal(ks[1], (DEC_BATCH, DEC_SEQ, D), f32),
        "norm1": 1.0 + nrm(ks[2], (DEPTH, D), 0.02),
        "w_in": nrm(ks[3], (DEPTH, D, IN_COLS), D ** -0.5),
        "attn_sink": nrm(ks[4], (DEPTH, A_HEADS), 0.5),
        "w_proj_a": nrm(ks[5], (DEPTH, A_Q, D), A_Q ** -0.5),
        "w_proj_b": nrm(ks[6], (DEPTH, B_OUT, D), B_OUT ** -0.5),
        "w_gate": nrm(ks[7], (DEPTH, D, 2 * D), D ** -0.5),
        "b_gate": nrm(ks[8], (DEPTH, 2 * D), 0.02),
        "w_out": nrm(ks[9], (DEPTH, D, D), D ** -0.5),
        "norm2": 1.0 + nrm(ks[10], (DEPTH, D), 0.02),
        "w_router_group": nrm(ks[11], (DEPTH, D, N_GROUPS), D ** -0.5),
        "b_router_group": nrm(ks[12], (DEPTH, N_GROUPS), 0.01),
        "w_router_expert": nrm(ks[13], (DEPTH, N_GROUPS, D, EXPERTS_PER_GROUP), D ** -0.5),
        "b_router_expert": nrm(ks[14], (DEPTH, N_GROUPS, EXPERTS_PER_GROUP), 0.01),
        "w_expert_gate": nrm(ks[15], (DEPTH, N_EXPERTS, D, D_FF), D ** -0.5),
        "w_expert_up": nrm(ks[16], (DEPTH, N_EXPERTS, D, D_FF), D ** -0.5),
        "w_expert_down": nrm(ks[17], (DEPTH, N_EXPERTS, D_FF, D), D_FF ** -0.5),
        "norm_final": 1.0 + nrm(ks[18], (D,), 0.02),
    }


def reference(x_prompt, x_sample, norm1, w_in, attn_sink, w_proj_a, w_proj_b, w_gate, b_gate, w_out,
              norm2, w_router_group, b_router_group, w_router_expert, b_router_expert,
              w_expert_gate, w_expert_up, w_expert_down, norm_final):
    y_prompt = trunk(x_prompt, norm1, w_in, attn_sink, w_proj_a, w_proj_b, w_gate, b_gate, w_out,
                     norm2, w_router_group, b_router_group, w_router_expert, b_router_expert,
                     w_expert_gate, w_expert_up, w_expert_down, norm_final)
    y_sample = trunk(x_sample, norm1, w_in, attn_sink, w_proj_a, w_proj_b, w_gate, b_gate, w_out,
                     norm2, w_router_group, b_router_group, w_router_expert, b_router_expert,
                     w_expert_gate, w_expert_up, w_expert_down, norm_final)
    return (y_prompt, y_sample)
```

```python
import functools
import math

import jax
import jax.numpy as jnp
import numpy as np
from jax import lax
from jax.experimental import pallas as pl
from jax.experimental.pallas import tpu as pltpu

HEAD_DIM = 128
A_HEADS = 16
A_KV = 4
A_GROUP = A_HEADS // A_KV
A_HALF = 128
DILATED_GROUPS = ((128, 1), (512, 4), (2048, 16))
N_DIL = len(DILATED_GROUPS)
B_HPG = 4
B_HEADS = N_DIL * B_HPG
N_GROUPS = 8
EXPERTS_PER_GROUP = 8
N_EXPERTS = N_GROUPS * EXPERTS_PER_GROUP
EPS = 1e-6

LANES = 128
MASKED = -1e30
VMEM_LIMIT = 56 * 1024 * 1024
SUBQ = 128
EXPERT_ROWS = 128

_BF = jnp.bfloat16
_F32 = jnp.float32


def _alibi(n):
    return np.power(2.0, -8.0 * (np.arange(n) + 1) / n).astype(np.float32)


def _params(*sem):
    return pltpu.CompilerParams(dimension_semantics=sem, vmem_limit_bytes=VMEM_LIMIT)


def _norm_matmul_kernel(xp_ref, xs_ref, g_ref, w_ref, b_ref, o_ref, xn_ref, *, n_prompt_tiles, gate):
    i = pl.program_id(0)
    j = pl.program_id(1)

    def norm_into_scratch(x_ref):
        x = x_ref[...]
        y = x * lax.rsqrt(jnp.mean(x * x, axis=-1, keepdims=True) + EPS)
        xn_ref[...] = (y * g_ref[...]).astype(_BF)

    @pl.when((j == 0) & (i < n_prompt_tiles))
    def _():
        norm_into_scratch(xp_ref)

    @pl.when((j == 0) & (i >= n_prompt_tiles))
    def _():
        norm_into_scratch(xs_ref)

    acc = jnp.dot(xn_ref[...], w_ref[...], preferred_element_type=_F32)
    if gate:
        z = acc + b_ref[...]
        acc = 1.0 / (1.0 + jnp.exp(-z))
    o_ref[...] = acc.astype(o_ref.dtype)


def _norm_matmul(xp, xs, g, w, b, *, gate, tm, tn):
    n_p, d = xp.shape
    n_s = xs.shape[0]
    n_cols = w.shape[1]
    npt = n_p // tm
    grid = ((n_p + n_s) // tm, n_cols // tn)
    return pl.pallas_call(
        functools.partial(_norm_matmul_kernel, n_prompt_tiles=npt, gate=gate),
        grid=grid,
        in_specs=[
            pl.BlockSpec((tm, d), lambda i, j: (jnp.minimum(i, npt - 1), 0)),
            pl.BlockSpec((tm, d), lambda i, j: (jnp.maximum(i - npt, 0), 0)),
            pl.BlockSpec((1, d), lambda i, j: (0, 0)),
            pl.BlockSpec((d, tn), lambda i, j: (0, j)),
            pl.BlockSpec((1, tn), lambda i, j: (0, j)),
        ],
        out_specs=pl.BlockSpec((tm, tn), lambda i, j: (i, j)),
        out_shape=jax.ShapeDtypeStruct((n_p + n_s, n_cols), _BF),
        scratch_shapes=[pltpu.VMEM((tm, d), _BF)],
        compiler_params=_params("parallel", "arbitrary"),
        name="norm_gate" if gate else "norm_proj",
    )(xp, xs, g, w, b)


def _band_attn_kernel(slope_ref, sink_ref, q_ref, kl_ref, km_ref, kr_ref, vl_ref, vm_ref, vr_ref,
                      *rest, bq, half, step, shared_kv, has_sink, prompt_rows, prompt_seq, sample_seq):
    if has_sink:
        o_ref, kcat, vcat = rest
        lse_ref = None
    else:
        o_ref, lse_ref, kcat, vcat = rest
    i = pl.program_id(0)
    c = pl.program_id(1)
    win = SUBQ + 2 * half

    kcat[0:half, :] = kl_ref[...]
    kcat[half:half + bq, :] = km_ref[...]
    kcat[half + bq:, :] = kr_ref[...]
    vcat[0:half, :] = vl_ref[...]
    vcat[half:half + bq, :] = vm_ref[...]
    vcat[half + bq:, :] = vr_ref[...]

    u0 = i * bq
    in_prompt = u0 < prompt_rows
    lo = jnp.where(in_prompt, (u0 // prompt_seq) * prompt_seq,
                   prompt_rows + ((u0 - prompt_rows) // sample_seq) * sample_seq)
    hi = lo + jnp.where(in_prompt, prompt_seq, sample_seq)

    qi = lax.broadcasted_iota(jnp.int32, (SUBQ, win), 0)
    kj = lax.broadcasted_iota(jnp.int32, (SUBQ, win), 1)
    rel = kj - half - qi
    absrel = jnp.abs(rel)
    in_band = absrel <= half
    neg_dist = -(absrel * step).astype(_F32)
    scale = HEAD_DIM ** -0.5
    lane = lax.broadcasted_iota(jnp.int32, (SUBQ, LANES), 1)

    for sb in range(bq // SUBQ):
        kpos = kj + (u0 + sb * SUBQ - half)
        valid = in_band & (kpos >= lo) & (kpos < hi)
        bias = jnp.where(valid, neg_dist, MASKED)
        lse_tile = jnp.zeros((SUBQ, LANES), _F32)
        for h in range(4):
            head = c * 4 + h if shared_kv else h
            kv_cols = slice(0, HEAD_DIM) if shared_kv else slice(h * HEAD_DIM, (h + 1) * HEAD_DIM)
            q = q_ref[sb * SUBQ:(sb + 1) * SUBQ, h * HEAD_DIM:(h + 1) * HEAD_DIM]
            k = kcat[sb * SUBQ:sb * SUBQ + win, kv_cols]
            v = vcat[sb * SUBQ:sb * SUBQ + win, kv_cols]
            s = lax.dot_general(q, k, (((1,), (1,)), ((), ())), preferred_element_type=_F32)
            s = s * scale + slope_ref[head] * bias
            m = jnp.max(s, axis=-1, keepdims=True)
            p = jnp.exp(s - m)
            l = jnp.sum(p, axis=-1, keepdims=True)
            pv = jnp.dot(p.astype(_BF), v, preferred_element_type=_F32)
            if has_sink:
                l = l + jnp.exp(sink_ref[head] - m)
            else:
                lse_tile = jnp.where(lane == h, m + jnp.log(l), lse_tile)
            o_ref[sb * SUBQ:(sb + 1) * SUBQ, h * HEAD_DIM:(h + 1) * HEAD_DIM] = (pv / l).astype(o_ref.dtype)
        if lse_ref is not None:
            lse_ref[sb * SUBQ:(sb + 1) * SUBQ, :] = lse_tile


def _band_attention(proj_view, slopes, sinks, *, q_col, k_col, v_col, col_stride, n_inner, kv_w, out_cols,
                    bq, half, step, shared_kv, has_sink, prompt_rows, prompt_seq, sample_seq, name):
    rows = proj_view.shape[0]
    assert prompt_seq % bq == 0 and sample_seq % bq == 0 and bq % SUBQ == 0 and bq % half == 0
    nq = rows // bq
    hb = bq // half
    last_halo = rows // half - 1
    kv_stride = col_stride * (512 // kv_w) if not shared_kv else 1

    def q_map(i, c):
        return (i, q_col + c * col_stride)

    def main_map(base):
        return lambda i, c: (i, base + c * kv_stride)

    def left_map(base):
        return lambda i, c: (jnp.maximum(i * hb - 1, 0), base + c * kv_stride)

    def right_map(base):
        return lambda i, c: (jnp.minimum((i + 1) * hb, last_halo), base + c * kv_stride)

    smem = pl.BlockSpec(memory_space=pltpu.SMEM)
    in_specs = [
        smem, smem,
        pl.BlockSpec((bq, 512), q_map),
        pl.BlockSpec((half, kv_w), left_map(k_col)),
        pl.BlockSpec((bq, kv_w), main_map(k_col)),
        pl.BlockSpec((half, kv_w), right_map(k_col)),
        pl.BlockSpec((half, kv_w), left_map(v_col)),
        pl.BlockSpec((bq, kv_w), main_map(v_col)),
        pl.BlockSpec((half, kv_w), right_map(v_col)),
    ]
    o_spec = pl.BlockSpec((bq, 512), lambda i, c: (i, c))
    o_shape = jax.ShapeDtypeStruct((rows, out_cols), _BF)
    if has_sink:
        out_specs, out_shape = o_spec, o_shape
    else:
        out_specs = (o_spec, pl.BlockSpec((bq, LANES), lambda i, c: (i, c)))
        out_shape = (o_shape, jax.ShapeDtypeStruct((rows, n_inner * LANES), _F32))
    return pl.pallas_call(
        functools.partial(_band_attn_kernel, bq=bq, half=half, step=step, shared_kv=shared_kv,
                          has_sink=has_sink, prompt_rows=prompt_rows, prompt_seq=prompt_seq,
                          sample_seq=sample_seq),
        grid=(nq, n_inner),
        in_specs=in_specs,
        out_specs=out_specs,
        out_shape=out_shape,
        scratch_shapes=[pltpu.VMEM((bq + 2 * half, kv_w), _BF), pltpu.VMEM((bq + 2 * half, kv_w), _BF)],
        compiler_params=_params("parallel", "arbitrary"),
        name=name,
    )(slopes, sinks, proj_view, proj_view, proj_view, proj_view, proj_view, proj_view, proj_view)


def _mix_kernel(xp_ref, xs_ref, oa_ref, ob0_ref, ob1_ref, ob2_ref, l0_ref, l1_ref, l2_ref, g_ref,
                wpa_ref, wpb_ref, wout_ref, n2_ref, wr_ref, br_ref, h_ref, hn_ref, lg_ref, *, n_prompt_tiles, d):
    i = pl.program_id(0)
    l0, l1, l2 = l0_ref[...], l1_ref[...], l2_ref[...]
    mx = jnp.maximum(jnp.maximum(l0, l1), l2)
    e0, e1, e2 = jnp.exp(l0 - mx), jnp.exp(l1 - mx), jnp.exp(l2 - mx)
    den = e0 + e1 + e2
    a0, a1, a2 = e0 / den, e1 / den, e2 / den
    parts = []
    for h in range(B_HPG):
        cs = slice(h * HEAD_DIM, (h + 1) * HEAD_DIM)
        parts.append(a0[:, h:h + 1] * ob0_ref[:, cs].astype(_F32)
                     + a1[:, h:h + 1] * ob1_ref[:, cs].astype(_F32)
                     + a2[:, h:h + 1] * ob2_ref[:, cs].astype(_F32))
    ob = jnp.concatenate(parts, axis=1).astype(_BF)

    ta = jnp.dot(oa_ref[...], wpa_ref[...], preferred_element_type=_F32)
    tb = jnp.dot(ob, wpb_ref[...], preferred_element_type=_F32)
    merged = g_ref[:, :d].astype(_F32) * ta + g_ref[:, d:].astype(_F32) * tb
    x = jnp.where(i < n_prompt_tiles, xp_ref[...], xs_ref[...])
    h_new = x + jnp.dot(merged.astype(_BF), wout_ref[...], preferred_element_type=_F32)
    h_ref[...] = h_new
    hn = h_new * lax.rsqrt(jnp.mean(h_new * h_new, axis=-1, keepdims=True) + EPS) * n2_ref[...]
    hn_ref[...] = hn
    hn_hi = hn.astype(_BF)
    hn_lo = (hn - hn_hi.astype(_F32)).astype(_BF)
    r = (jnp.dot(hn_hi, wr_ref[...], preferred_element_type=_F32)
         + jnp.dot(hn_lo, wr_ref[...], preferred_element_type=_F32))
    lg_ref[...] = r[:, :LANES] + r[:, LANES:] + br_ref[...]


def _mix(xp, xs, oa, obs, lses, gates, wpa, wpb, wout, n2, wr, br, *, tm):
    n_p, d = xp.shape
    n_s = xs.shape[0]
    t = n_p + n_s
    npt = n_p // tm
    row = lambda i: (i, 0)
    const = lambda i: (0, 0)

    def resident(shape):
        return pl.BlockSpec(shape, const, pipeline_mode=pl.Buffered(1))

    in_specs = [
        pl.BlockSpec((tm, d), lambda i: (jnp.minimum(i, npt - 1), 0)),
        pl.BlockSpec((tm, d), lambda i: (jnp.maximum(i - npt, 0), 0)),
        pl.BlockSpec((tm, oa.shape[1]), row),
        pl.BlockSpec((tm, 512), row), pl.BlockSpec((tm, 512), row), pl.BlockSpec((tm, 512), row),
        pl.BlockSpec((tm, LANES), row), pl.BlockSpec((tm, LANES), row), pl.BlockSpec((tm, LANES), row),
        pl.BlockSpec((tm, 2 * d), row),
        resident(wpa.shape), resident(wpb.shape), resident(wout.shape),
        resident((1, d)), resident(wr.shape), resident((1, LANES)),
    ]
    return pl.pallas_call(
        functools.partial(_mix_kernel, n_prompt_tiles=npt, d=d),
        grid=(t // tm,),
        in_specs=in_specs,
        out_specs=(pl.BlockSpec((tm, d), row), pl.BlockSpec((tm, d), row), pl.BlockSpec((tm, LANES), row)),
        out_shape=(jax.ShapeDtypeStruct((t, d), _F32), jax.ShapeDtypeStruct((t, d), _F32),
                   jax.ShapeDtypeStruct((t, LANES), _F32)),
        compiler_params=_params("parallel"),
        name="mix_out_router",
    )(xp, xs, oa, *obs, *lses, gates, wpa, wpb, wout, n2, wr, br)


def _route_kernel(lg_ref, ri_ref, rw_ref, cnt_ref, carry_ref, *, tr):
    i = pl.program_id(0)

    @pl.when(i == 0)
    def _():
        carry_ref[...] = jnp.zeros_like(carry_ref)

    lg = lg_ref[...]
    lane_i = lax.broadcasted_iota(jnp.int32, (tr, LANES), 1)
    lane = lane_i.astype(_F32)
    no_lane = float(LANES)
    is_grp = lane_i < N_GROUPS
    glog = jnp.where(is_grp, lg, MASKED)
    gmax = jnp.max(glog, axis=-1, keepdims=True)
    grp = jnp.min(jnp.where(glog == gmax, lane, no_lane), axis=-1, keepdims=True)
    gsum = jnp.sum(jnp.where(is_grp, jnp.exp(glog - gmax), 0.0), axis=-1, keepdims=True)
    pgrp = 1.0 / gsum
    lane_grp = ((lane_i - N_GROUPS) // EXPERTS_PER_GROUP).astype(_F32)
    in_grp = (lane_i >= N_GROUPS) & (lane_i < N_GROUPS + N_EXPERTS) & (lane_grp == grp)
    elog = jnp.where(in_grp, lg, MASKED)
    t1 = jnp.max(elog, axis=-1, keepdims=True)
    i1 = jnp.min(jnp.where(elog == t1, lane, no_lane), axis=-1, keepdims=True)
    elog2 = jnp.where(lane == i1, MASKED, elog)
    t2 = jnp.max(elog2, axis=-1, keepdims=True)
    i2 = jnp.min(jnp.where(elog2 == t2, lane, no_lane), axis=-1, keepdims=True)
    e21 = jnp.exp(t2 - t1)
    w1 = pgrp / (1.0 + e21)
    w2 = pgrp * e21 / (1.0 + e21)
    eid1 = i1 - N_GROUPS
    eid2 = i2 - N_GROUPS
    hot1 = lane == eid1
    hot2 = lane == eid2
    onehot = jnp.where(hot1, 1.0, 0.0) + jnp.where(hot2, 1.0, 0.0)
    r_i = lax.broadcasted_iota(jnp.int32, (tr, tr), 0)
    c_i = lax.broadcasted_iota(jnp.int32, (tr, tr), 1)
    lower = jnp.where(c_i < r_i, 1.0, 0.0).astype(_BF)
    before = jnp.dot(lower, onehot.astype(_BF), preferred_element_type=_F32) + carry_ref[0:1, :]
    rank1 = jnp.sum(jnp.where(hot1, before, 0.0), axis=-1, keepdims=True)
    rank2 = jnp.sum(jnp.where(hot2, before, 0.0), axis=-1, keepdims=True)
    total = carry_ref[0:1, :] + jnp.sum(onehot, axis=0, keepdims=True)
    carry_ref[...] = jnp.broadcast_to(total, carry_ref.shape)
    cnt_ref[...] = jnp.broadcast_to(total, cnt_ref.shape).astype(jnp.int32)
    ri = jnp.where(lane_i == 0, eid1, jnp.where(lane_i == 1, eid2, jnp.where(lane_i == 2, rank1,
                   jnp.where(lane_i == 3, rank2, 0.0))))
    ri_ref[...] = ri.astype(jnp.int32)
    rw_ref[...] = jnp.where(lane_i == 0, w1, jnp.where(lane_i == 1, w2, 0.0))


def _route(logits, *, tr):
    t = logits.shape[0]
    row = lambda i: (i, 0)
    return pl.pallas_call(
        functools.partial(_route_kernel, tr=tr),
        grid=(t // tr,),
        in_specs=[pl.BlockSpec((tr, LANES), row)],
        out_specs=(pl.BlockSpec((tr, LANES), row), pl.BlockSpec((tr, LANES), row),
                   pl.BlockSpec((8, LANES), lambda i: (0, 0))),
        out_shape=(jax.ShapeDtypeStruct((t, LANES), jnp.int32), jax.ShapeDtypeStruct((t, LANES), _F32),
                   jax.ShapeDtypeStruct((8, LANES), jnp.int32)),
        scratch_shapes=[pltpu.VMEM((8, LANES), _F32)],
        compiler_params=_params("arbitrary"),
        name="route",
    )(logits)


def _dispatch_kernel(dest_ref, hn_ref, xr_in_ref, xr_ref, sem, *, td):
    del xr_in_ref
    i = pl.program_id(0)

    def row_copy(j, k):
        dst = dest_ref[2 * (i * td + j) + k]
        return pltpu.make_async_copy(hn_ref.at[pl.ds(j, 1), :], xr_ref.at[pl.ds(dst, 1), :], sem)

    def start(j, carry):
        row_copy(j, 0).start()
        row_copy(j, 1).start()
        return carry

    def wait(j, carry):
        row_copy(j, 0).wait()
        row_copy(j, 1).wait()
        return carry

    lax.fori_loop(0, td, start, 0)
    lax.fori_loop(0, td, wait, 0)


def _dispatch(dest, hn, xr_init, *, td):
    t, d = hn.shape
    return pl.pallas_call(
        functools.partial(_dispatch_kernel, td=td),
        grid_spec=pltpu.PrefetchScalarGridSpec(
            num_scalar_prefetch=1,
            grid=(t // td,),
            in_specs=[pl.BlockSpec((td, d), lambda i, dest: (i, 0)), pl.BlockSpec(memory_space=pl.ANY)],
            out_specs=pl.BlockSpec(memory_space=pl.ANY),
            scratch_shapes=[pltpu.SemaphoreType.DMA(())],
        ),
        out_shape=jax.ShapeDtypeStruct(xr_init.shape, xr_init.dtype),
        input_output_aliases={2: 0},
        compiler_params=_params("arbitrary"),
        name="dispatch",
    )(dest, hn, xr_init)


def _expert_kernel(blk_e_ref, nvalid_ref, x_ref, w1_ref, w3_ref, w2_ref, y_ref):
    del blk_e_ref
    b = pl.program_id(0)

    @pl.when(b < nvalid_ref[0])
    def _():
        xb = x_ref[...].astype(_BF)
        a = jnp.dot(xb, w1_ref[0], preferred_element_type=_F32)
        u = jnp.dot(xb, w3_ref[0], preferred_element_type=_F32)
        hmid = (a / (1.0 + jnp.exp(-a))) * u
        y_ref[...] = jnp.dot(hmid.astype(_BF), w2_ref[0], preferred_element_type=_F32)

    @pl.when(b >= nvalid_ref[0])
    def _():
        y_ref[...] = jnp.zeros_like(y_ref)


def _experts(blk_e, nvalid, xr, w1, w3, w2):
    rows, d = xr.shape
    f = w1.shape[2]
    nblk = rows // EXPERT_ROWS
    return pl.pallas_call(
        _expert_kernel,
        grid_spec=pltpu.PrefetchScalarGridSpec(
            num_scalar_prefetch=2,
            grid=(nblk,),
            in_specs=[
                pl.BlockSpec((EXPERT_ROWS, d), lambda b, be, nv: (b, 0)),
                pl.BlockSpec((1, d, f), lambda b, be, nv: (be[b], 0, 0)),
                pl.BlockSpec((1, d, f), lambda b, be, nv: (be[b], 0, 0)),
                pl.BlockSpec((1, f, d), lambda b, be, nv: (be[b], 0, 0)),
            ],
            out_specs=pl.BlockSpec((EXPERT_ROWS, d), lambda b, be, nv: (b, 0)),
        ),
        out_shape=jax.ShapeDtypeStruct((rows, d), _F32),
        compiler_params=_params("arbitrary"),
        name="experts",
    )(blk_e, nvalid, xr, w1, w3, w2)


def _combine_kernel(dest_ref, h_ref, rw_ref, g_ref, yr_ref, op_ref, os_ref, ybuf, sem, *, tc, n_prompt_tiles):
    i = pl.program_id(0)

    def row_copy(j, k):
        src = dest_ref[2 * (i * tc + j) + k]
        return pltpu.make_async_copy(yr_ref.at[pl.ds(src, 1), :], ybuf.at[k, pl.ds(j, 1), :], sem)

    def start(j, carry):
        row_copy(j, 0).start()
        row_copy(j, 1).start()
        return carry

    def wait(j, carry):
        row_copy(j, 0).wait()
        row_copy(j, 1).wait()
        return carry

    lax.fori_loop(0, tc, start, 0)
    lax.fori_loop(0, tc, wait, 0)

    rw = rw_ref[...]
    y = rw[:, 0:1] * ybuf[0] + rw[:, 1:2] * ybuf[1]
    z = h_ref[...] + y
    out = z * lax.rsqrt(jnp.mean(z * z, axis=-1, keepdims=True) + EPS) * g_ref[...]

    @pl.when(i < n_prompt_tiles)
    def _():
        op_ref[...] = out

    @pl.when(i >= n_prompt_tiles)
    def _():
        os_ref[...] = out


def _combine(dest, h, rw, g, yr, *, n_p, tc):
    t, d = h.shape
    npt = n_p // tc
    return pl.pallas_call(
        functools.partial(_combine_kernel, tc=tc, n_prompt_tiles=npt),
        grid_spec=pltpu.PrefetchScalarGridSpec(
            num_scalar_prefetch=1,
            grid=(t // tc,),
            in_specs=[
                pl.BlockSpec((tc, d), lambda i, dest: (i, 0)),
                pl.BlockSpec((tc, LANES), lambda i, dest: (i, 0)),
                pl.BlockSpec((1, d), lambda i, dest: (0, 0)),
                pl.BlockSpec(memory_space=pl.ANY),
            ],
            out_specs=(pl.BlockSpec((tc, d), lambda i, dest: (jnp.minimum(i, npt - 1), 0)),
                       pl.BlockSpec((tc, d), lambda i, dest: (jnp.maximum(i - npt, 0), 0))),
            scratch_shapes=[pltpu.VMEM((2, tc, d), _F32), pltpu.SemaphoreType.DMA(())],
        ),
        out_shape=(jax.ShapeDtypeStruct((n_p, d), _F32), jax.ShapeDtypeStruct((t - n_p, d), _F32)),
        compiler_params=_params("arbitrary"),
        name="combine_norm",
    )(dest, h, rw, g, yr)


def _layer(xp, xs, prompt_seq, sample_seq, norm1, w_in, sink, w_proj_a, w_proj_b, w_gate, b_gate, w_out,
           norm2, w_rg, b_rg, w_re, b_re, w1, w3, w2, norm_final):
    n_p, d = xp.shape
    n_s = xs.shape[0]
    t = n_p + n_s
    in_cols = w_in.shape[1]
    a_q = A_HEADS * HEAD_DIM
    a_kvw = A_KV * HEAD_DIM
    b_w = B_HEADS * HEAD_DIM

    g1 = norm1.reshape(1, d)
    proj = _norm_matmul(xp, xs, g1, w_in.astype(_BF), jnp.zeros((1, in_cols), _F32), gate=False, tm=1024, tn=512)
    gates = _norm_matmul(xp, xs, g1, w_gate.astype(_BF), b_gate.reshape(1, 2 * d), gate=True, tm=1024, tn=512)

    oa = _band_attention(
        proj, jnp.asarray(_alibi(A_HEADS)), sink.astype(_F32),
        q_col=0, k_col=a_q // HEAD_DIM, v_col=(a_q + a_kvw) // HEAD_DIM, col_stride=1, n_inner=A_KV,
        kv_w=HEAD_DIM, out_cols=a_q, bq=min(512, prompt_seq, sample_seq), half=A_HALF, step=1,
        shared_kv=True, has_sink=True, prompt_rows=n_p, prompt_seq=prompt_seq, sample_seq=sample_seq,
        name="attn_window")

    slopes_b = _alibi(B_HEADS)
    obs, lses = [], []
    q0 = a_q + 2 * a_kvw
    for gi, (w, r) in enumerate(DILATED_GROUPS):
        view = proj.reshape(t // r, r * in_cols)
        blocks_per_row = in_cols // 512
        o_g, lse_g = _band_attention(
            view, jnp.asarray(slopes_b[gi * B_HPG:(gi + 1) * B_HPG]), jnp.zeros((B_HPG,), _F32),
            q_col=q0 // 512 + gi, k_col=(q0 + b_w) // 512 + gi, v_col=(q0 + 2 * b_w) // 512 + gi,
            col_stride=blocks_per_row, n_inner=r, kv_w=512, out_cols=r * 512,
            bq=min(256, prompt_seq // r, sample_seq // r), half=w // (2 * r), step=r, shared_kv=False,
            has_sink=False, prompt_rows=n_p // r, prompt_seq=prompt_seq // r, sample_seq=sample_seq // r,
            name=f"attn_dilated_{r}")
        obs.append(o_g.reshape(t, 512))
        lses.append(lse_g.reshape(t, LANES))

    n_r = N_GROUPS + N_EXPERTS
    wr = jnp.concatenate([w_rg, jnp.transpose(w_re, (1, 0, 2)).reshape(d, N_EXPERTS)], axis=1)
    wr = jnp.pad(wr, ((0, 0), (0, LANES - n_r)))
    wr_hi = wr.astype(_BF)
    wr_lo = (wr - wr_hi.astype(_F32)).astype(_BF)
    wr2 = jnp.concatenate([wr_hi, wr_lo], axis=1)
    br = jnp.pad(jnp.concatenate([b_rg, b_re.reshape(-1)]), (0, LANES - n_r)).reshape(1, LANES).astype(_F32)

    h, hn, logits = _mix(xp, xs, oa, obs, lses, gates, w_proj_a.astype(_BF), w_proj_b.astype(_BF),
                         w_out.astype(_BF), norm2.reshape(1, d), wr2, br, tm=256)

    ri, rw, cnt = _route(logits, tr=256)
    counts = cnt[0, :N_EXPERTS]
    eid = ri[:, 0:2]
    rank = ri[:, 2:4]
    pcounts = (counts + EXPERT_ROWS - 1) // EXPERT_ROWS * EXPERT_ROWS
    pends = jnp.cumsum(pcounts)
    pstarts = pends - pcounts
    dest = (pstarts[eid] + rank).reshape(-1).astype(jnp.int32)
    nblk = (2 * t + N_EXPERTS * (EXPERT_ROWS - 1) + EXPERT_ROWS - 1) // EXPERT_ROWS
    blk_e = jnp.minimum(jnp.searchsorted(pends, jnp.arange(nblk, dtype=jnp.int32) * EXPERT_ROWS, side='right'),
                        N_EXPERTS - 1).astype(jnp.int32)
    nvalid = (pends[-1] // EXPERT_ROWS).astype(jnp.int32).reshape(1)

    xr = _dispatch(dest, hn, jnp.zeros((nblk * EXPERT_ROWS, d), _F32), td=256)
    yr = _experts(blk_e, nvalid, xr, w1.astype(_BF), w3.astype(_BF), w2.astype(_BF))
    return _combine(dest, h, rw, norm_final.reshape(1, d), yr, n_p=n_p, tc=256)


def kernel(x_prompt, x_sample, norm1, w_in, attn_sink, w_proj_a, w_proj_b, w_gate, b_gate, w_out, norm2,
           w_router_group, b_router_group, w_router_expert, b_router_expert, w_expert_gate, w_expert_up,
           w_expert_down, norm_final):
    assert norm1.shape[0] == 1, "one layer"
    d = x_prompt.shape[-1]
    xp = x_prompt.reshape(-1, d)
    xs = x_sample.reshape(-1, d)
    yp, ys = _layer(xp, xs, x_prompt.shape[1], x_sample.shape[1], norm1[0], w_in[0], attn_sink[0], w_proj_a[0],
                    w_proj_b[0], w_gate[0], b_gate[0], w_out[0], norm2[0], w_router_group[0],
                    b_router_group[0], w_router_expert[0], b_router_expert[0], w_expert_gate[0],
                    w_expert_up[0], w_expert_down[0], norm_final)
    return yp.reshape(x_prompt.shape), ys.reshape(x_sample.shape)
```

```python
import functools

import jax
import jax.numpy as jnp
import numpy as np
from jax import lax
from jax.experimental import pallas as pl
from jax.experimental.pallas import tpu as pltpu

HEAD_DIM = 128
A_HEADS = 16
A_KV = 4
A_HALF = 128
DILATED_GROUPS = ((128, 1), (512, 4), (2048, 16))
N_DIL = len(DILATED_GROUPS)
B_HPG = 4
B_HEADS = N_DIL * B_HPG
N_GROUPS = 8
EXPERTS_PER_GROUP = 8
N_EXPERTS = N_GROUPS * EXPERTS_PER_GROUP
EPS = 1e-6

LANES = 128
MASKED = -1e30
VMEM_LIMIT = 56 * 1024 * 1024
SUBQ = 128
HEADS_PER_STEP = 4
EXPERT_ROWS = 128
PROJ_TM, PROJ_TN = 1024, 512
ROW_TILE = 256

_BF = jnp.bfloat16
_F32 = jnp.float32


def _alibi(n):
    return np.power(2.0, -8.0 * (np.arange(n) + 1) / n).astype(np.float32)


def _params(*sem):
    return pltpu.CompilerParams(dimension_semantics=sem, vmem_limit_bytes=VMEM_LIMIT)


def _proj_kernel(xp_ref, xs_ref, g_ref, w_ref, b_ref, oa_ref, q0_ref, q1_ref, q2_ref, gt_ref, xn_ref, acc_ref,
                 *, n_prompt_tiles, part_starts, tm):
    i = pl.program_id(0)
    j = pl.program_id(1)

    def norm_into_scratch(x_ref):
        x = x_ref[...]
        y = x * lax.rsqrt(jnp.mean(x * x, axis=-1, keepdims=True) + EPS)
        xn_ref[...] = (y * g_ref[...]).astype(_BF)

    @pl.when((j == 0) & (i < n_prompt_tiles))
    def _():
        norm_into_scratch(xp_ref)

    @pl.when((j == 0) & (i >= n_prompt_tiles))
    def _():
        norm_into_scratch(xs_ref)

    acc = jnp.dot(xn_ref[...], w_ref[...], preferred_element_type=_F32)
    s0, s1, s2, sg = part_starts

    @pl.when(j < s0)
    def _():
        oa_ref[...] = acc.astype(_BF)

    for (lo, hi, q_ref, (_, r)) in ((s0, s1, q0_ref, DILATED_GROUPS[0]), (s1, s2, q1_ref, DILATED_GROUPS[1]),
                                    (s2, sg, q2_ref, DILATED_GROUPS[2])):
        @pl.when((j >= lo) & (j < hi))
        def _(q_ref=q_ref, r=r):
            if r == 1:
                q_ref[0] = acc.astype(_BF)
            else:
                for cb in range(acc.shape[1] // LANES):
                    acc_ref[cb] = acc[:, cb * LANES:(cb + 1) * LANES]
                for p in range(r):
                    for cb in range(acc.shape[1] // LANES):
                        q_ref[p, :, cb * LANES:(cb + 1) * LANES] = (
                            acc_ref[cb, pl.ds(p, tm // r, stride=r), :].astype(_BF))

    @pl.when(j >= sg)
    def _():
        z = acc + b_ref[...]
        gt_ref[...] = (1.0 / (1.0 + jnp.exp(-z))).astype(_BF)


def _project(xp, xs, g, w, b_gate, *, a_cols, b_cols, gate_cols):
    n_p, d = xp.shape
    n_s = xs.shape[0]
    t = n_p + n_s
    tm, tn = PROJ_TM, PROJ_TN
    npt = n_p // tm
    na, nb, ng = a_cols // tn, b_cols // tn, gate_cols // tn
    s0, s1, s2, sg = na, na + nb, na + 2 * nb, na + 3 * nb
    once = pl.Buffered(1)

    def part_map(lo, n):
        return lambda i, j: (0, i, jnp.clip(j - lo, 0, n - 1))

    q_shapes = [jax.ShapeDtypeStruct((r, t // r, b_cols), _BF) for _, r in DILATED_GROUPS]
    q_specs = [pl.BlockSpec((r, tm // r, tn), part_map(lo, nb))
               for lo, (_, r) in zip((s0, s1, s2), DILATED_GROUPS)]
    return pl.pallas_call(
        functools.partial(_proj_kernel, n_prompt_tiles=npt, part_starts=(s0, s1, s2, sg), tm=tm),
        grid=(t // tm, sg + ng),
        in_specs=[
            pl.BlockSpec((tm, d), lambda i, j: (jnp.minimum(i, npt - 1), 0), pipeline_mode=once),
            pl.BlockSpec((tm, d), lambda i, j: (jnp.maximum(i - npt, 0), 0), pipeline_mode=once),
            pl.BlockSpec((1, d), lambda i, j: (0, 0)),
            pl.BlockSpec((d, tn), lambda i, j: (0, j)),
            pl.BlockSpec((1, tn), lambda i, j: (0, jnp.clip(j - sg, 0, ng - 1))),
        ],
        out_specs=[pl.BlockSpec((tm, tn), lambda i, j: (i, jnp.minimum(j, na - 1)))] + q_specs
                  + [pl.BlockSpec((tm, tn), lambda i, j: (i, jnp.clip(j - sg, 0, ng - 1)))],
        out_shape=[jax.ShapeDtypeStruct((t, a_cols), _BF)] + q_shapes
                  + [jax.ShapeDtypeStruct((t, gate_cols), _BF)],
        scratch_shapes=[pltpu.VMEM((tm, d), _BF), pltpu.VMEM((tn // LANES, tm, LANES), _F32)],
        compiler_params=_params("arbitrary", "arbitrary"),
        name="norm_proj",
    )(xp, xs, g, w, b_gate)


def _band_attn_kernel(slope_ref, sink_ref, q_ref, kl_ref, km_ref, kr_ref, vl_ref, vm_ref, vr_ref,
                      *rest, bq, half, step, shared_kv, has_sink, prompt_rows, prompt_seq, sample_seq):
    if has_sink:
        o_ref, kcat, vcat = rest
        lse_ref = None
    else:
        o_ref, lse_ref, kcat, vcat = rest
    i = pl.program_id(0)
    c = pl.program_id(1)
    win = SUBQ + 2 * half

    kcat[0:half, :] = kl_ref[...]
    kcat[half:half + bq, :] = km_ref[...]
    kcat[half + bq:, :] = kr_ref[...]
    vcat[0:half, :] = vl_ref[...]
    vcat[half:half + bq, :] = vm_ref[...]
    vcat[half + bq:, :] = vr_ref[...]

    u0 = i * bq
    in_prompt = u0 < prompt_rows
    lo = jnp.where(in_prompt, (u0 // prompt_seq) * prompt_seq,
                   prompt_rows + ((u0 - prompt_rows) // sample_seq) * sample_seq)
    hi = lo + jnp.where(in_prompt, prompt_seq, sample_seq)

    qi = lax.broadcasted_iota(jnp.int32, (SUBQ, win), 0)
    kj = lax.broadcasted_iota(jnp.int32, (SUBQ, win), 1)
    rel = kj - half - qi
    absrel = jnp.abs(rel)
    in_band = absrel <= half
    neg_dist = -(absrel * step).astype(_F32)
    scale = HEAD_DIM ** -0.5
    lane = lax.broadcasted_iota(jnp.int32, (SUBQ, LANES), 1)

    for sb in range(bq // SUBQ):
        kpos = kj + (u0 + sb * SUBQ - half)
        valid = in_band & (kpos >= lo) & (kpos < hi)
        bias = jnp.where(valid, neg_dist, MASKED)
        lse_tile = jnp.zeros((SUBQ, LANES), _F32)
        for h in range(HEADS_PER_STEP):
            head = c * HEADS_PER_STEP + h if shared_kv else h
            kv_cols = slice(0, HEAD_DIM) if shared_kv else slice(h * HEAD_DIM, (h + 1) * HEAD_DIM)
            q = q_ref[sb * SUBQ:(sb + 1) * SUBQ, h * HEAD_DIM:(h + 1) * HEAD_DIM]
            k = kcat[sb * SUBQ:sb * SUBQ + win, kv_cols]
            v = vcat[sb * SUBQ:sb * SUBQ + win, kv_cols]
            s = lax.dot_general(q, k, (((1,), (1,)), ((), ())), preferred_element_type=_F32)
            s = s * scale + slope_ref[head] * bias
            m = jnp.max(s, axis=-1, keepdims=True)
            p = jnp.exp(s - m)
            l = jnp.sum(p, axis=-1, keepdims=True)
            pv = jnp.dot(p.astype(_BF), v, preferred_element_type=_F32)
            if has_sink:
                l = l + jnp.exp(sink_ref[head] - m)
            else:
                lse_tile = jnp.where(lane == h, m + jnp.log(l), lse_tile)
            o_ref[sb * SUBQ:(sb + 1) * SUBQ, h * HEAD_DIM:(h + 1) * HEAD_DIM] = (pv / l).astype(o_ref.dtype)
        if lse_ref is not None:
            lse_ref[sb * SUBQ:(sb + 1) * SUBQ, :] = lse_tile


def _band_attention(qkv, slopes, sinks, *, n_inner, bq, half, step, shared_kv, has_sink,
                    prompt_rows, prompt_seq, sample_seq, name):
    lead, rows, _ = qkv.shape
    assert prompt_seq % bq == 0 and sample_seq % bq == 0 and bq % SUBQ == 0 and bq % half == 0
    nq = rows // bq
    hb = bq // half
    last_halo = rows // half - 1
    width = HEADS_PER_STEP * HEAD_DIM
    if shared_kv:
        kv_w = HEAD_DIM
        k_base, v_base = width * n_inner // HEAD_DIM, width * n_inner // HEAD_DIM + n_inner
        lead_of = lambda c: 0
        q_col = lambda c: c
        kv_col = lambda base: (lambda c: base + c)
    else:
        kv_w = width
        k_base, v_base = 1, 2
        lead_of = lambda c: c
        q_col = lambda c: 0
        kv_col = lambda base: (lambda c: base)

    def main_map(col):
        return lambda i, c: (lead_of(c), i, col(c))

    def left_map(col):
        return lambda i, c: (lead_of(c), jnp.maximum(i * hb - 1, 0), col(c))

    def right_map(col):
        return lambda i, c: (lead_of(c), jnp.minimum((i + 1) * hb, last_halo), col(c))

    smem = pl.BlockSpec(memory_space=pltpu.SMEM)
    in_specs = [
        smem, smem,
        pl.BlockSpec((None, bq, width), main_map(q_col)),
        pl.BlockSpec((None, half, kv_w), left_map(kv_col(k_base))),
        pl.BlockSpec((None, bq, kv_w), main_map(kv_col(k_base))),
        pl.BlockSpec((None, half, kv_w), right_map(kv_col(k_base))),
        pl.BlockSpec((None, half, kv_w), left_map(kv_col(v_base))),
        pl.BlockSpec((None, bq, kv_w), main_map(kv_col(v_base))),
        pl.BlockSpec((None, half, kv_w), right_map(kv_col(v_base))),
    ]
    if has_sink:
        out_specs = pl.BlockSpec((bq, width), lambda i, c: (i, c))
        out_shape = jax.ShapeDtypeStruct((rows, n_inner * width), _BF)
    else:
        out_specs = (pl.BlockSpec((None, bq, width), lambda i, c: (c, i, 0)),
                     pl.BlockSpec((None, bq, LANES), lambda i, c: (c, i, 0)))
        out_shape = (jax.ShapeDtypeStruct((lead, rows, width), _BF),
                     jax.ShapeDtypeStruct((lead, rows, LANES), _F32))
    return pl.pallas_call(
        functools.partial(_band_attn_kernel, bq=bq, half=half, step=step, shared_kv=shared_kv,
                          has_sink=has_sink, prompt_rows=prompt_rows, prompt_seq=prompt_seq,
                          sample_seq=sample_seq),
        grid=(nq, n_inner),
        in_specs=in_specs,
        out_specs=out_specs,
        out_shape=out_shape,
        scratch_shapes=[pltpu.VMEM((bq + 2 * half, kv_w), _BF), pltpu.VMEM((bq + 2 * half, kv_w), _BF)],
        compiler_params=_params("parallel", "arbitrary"),
        name=name,
    )(slopes, sinks, qkv, qkv, qkv, qkv, qkv, qkv, qkv)


def _mix_kernel(xp_ref, xs_ref, oa_ref, ob0_ref, ob1_ref, ob2_ref, l0_ref, l1_ref, l2_ref, g_ref,
                wpa_ref, wpb_ref, wout_ref, n2_ref, wr_ref, br_ref, h_ref, hn_ref, lg_ref, o_scr, l_scr,
                *, n_prompt_tiles, d, tm):
    i = pl.program_id(0)

    def token_order(o_ref, l_ref, r, slot):
        if r == 1:
            return [o_ref[0, :, h * HEAD_DIM:(h + 1) * HEAD_DIM].astype(_F32) for h in range(B_HPG)], l_ref[0]
        for p in range(r):
            for h in range(B_HPG):
                o_scr[slot, h, pl.ds(p, tm // r, stride=r), :] = (
                    o_ref[p, :, h * HEAD_DIM:(h + 1) * HEAD_DIM].astype(_F32))
            l_scr[slot, pl.ds(p, tm // r, stride=r), :] = l_ref[p]
        return [o_scr[slot, h] for h in range(B_HPG)], l_scr[slot]

    o0, l0 = token_order(ob0_ref, l0_ref, DILATED_GROUPS[0][1], 0)
    o1, l1 = token_order(ob1_ref, l1_ref, DILATED_GROUPS[1][1], 0)
    o2, l2 = token_order(ob2_ref, l2_ref, DILATED_GROUPS[2][1], 1)
    mx = jnp.maximum(jnp.maximum(l0, l1), l2)
    e0, e1, e2 = jnp.exp(l0 - mx), jnp.exp(l1 - mx), jnp.exp(l2 - mx)
    den = e0 + e1 + e2
    a0, a1, a2 = e0 / den, e1 / den, e2 / den
    parts = []
    for h in range(B_HPG):
        parts.append(a0[:, h:h + 1] * o0[h] + a1[:, h:h + 1] * o1[h] + a2[:, h:h + 1] * o2[h])
    ob = jnp.concatenate(parts, axis=1).astype(_BF)

    ta = jnp.dot(oa_ref[...], wpa_ref[...], preferred_element_type=_F32)
    tb = jnp.dot(ob, wpb_ref[...], preferred_element_type=_F32)
    merged = g_ref[:, :d].astype(_F32) * ta + g_ref[:, d:].astype(_F32) * tb
    x = jnp.where(i < n_prompt_tiles, xp_ref[...], xs_ref[...])
    h_new = x + jnp.dot(merged.astype(_BF), wout_ref[...], preferred_element_type=_F32)
    h_ref[...] = h_new
    hn = h_new * lax.rsqrt(jnp.mean(h_new * h_new, axis=-1, keepdims=True) + EPS) * n2_ref[...]
    hn_ref[...] = hn
    hn_hi = hn.astype(_BF)
    hn_lo = (hn - hn_hi.astype(_F32)).astype(_BF)
    r = (jnp.dot(hn_hi, wr_ref[...], preferred_element_type=_F32)
         + jnp.dot(hn_lo, wr_ref[...], preferred_element_type=_F32))
    lg_ref[...] = r[:, :LANES] + r[:, LANES:] + br_ref[...]


def _mix(xp, xs, oa, obs, lses, gates, wpa, wpb, wout, n2, wr, br):
    n_p, d = xp.shape
    n_s = xs.shape[0]
    t = n_p + n_s
    tm = ROW_TILE
    npt = n_p // tm
    row = lambda i: (i, 0)
    const = lambda i: (0, 0)
    width = B_HPG * HEAD_DIM

    def resident(shape):
        return pl.BlockSpec(shape, const, pipeline_mode=pl.Buffered(1))

    def phase_blocks(cols):
        return [pl.BlockSpec((r, tm // r, cols), lambda i: (0, i, 0)) for _, r in DILATED_GROUPS]

    in_specs = [
        pl.BlockSpec((tm, d), lambda i: (jnp.minimum(i, npt - 1), 0)),
        pl.BlockSpec((tm, d), lambda i: (jnp.maximum(i - npt, 0), 0)),
        pl.BlockSpec((tm, oa.shape[1]), row),
        *phase_blocks(width), *phase_blocks(LANES),
        pl.BlockSpec((tm, 2 * d), row),
        resident(wpa.shape), resident(wpb.shape), resident(wout.shape),
        resident((1, d)), resident(wr.shape), resident((1, LANES)),
    ]
    return pl.pallas_call(
        functools.partial(_mix_kernel, n_prompt_tiles=npt, d=d, tm=tm),
        grid=(t // tm,),
        in_specs=in_specs,
        out_specs=(pl.BlockSpec((tm, d), row), pl.BlockSpec((tm, d), row), pl.BlockSpec((tm, LANES), row)),
        out_shape=(jax.ShapeDtypeStruct((t, d), _F32), jax.ShapeDtypeStruct((t, d), _F32),
                   jax.ShapeDtypeStruct((t, LANES), _F32)),
        scratch_shapes=[pltpu.VMEM((2, B_HPG, tm, HEAD_DIM), _F32), pltpu.VMEM((2, tm, LANES), _F32)],
        compiler_params=_params("parallel"),
        name="mix_out_router",
    )(xp, xs, oa, *obs, *lses, gates, wpa, wpb, wout, n2, wr, br)


def _route_kernel(lg_ref, ri_ref, rw_ref, cnt_ref, carry_ref, *, tr):
    i = pl.program_id(0)

    @pl.when(i == 0)
    def _():
        carry_ref[...] = jnp.zeros_like(carry_ref)

    lg = lg_ref[...]
    lane_i = lax.broadcasted_iota(jnp.int32, (tr, LANES), 1)
    lane = lane_i.astype(_F32)
    no_lane = float(LANES)
    is_grp = lane_i < N_GROUPS
    glog = jnp.where(is_grp, lg, MASKED)
    gmax = jnp.max(glog, axis=-1, keepdims=True)
    grp = jnp.min(jnp.where(glog == gmax, lane, no_lane), axis=-1, keepdims=True)
    gsum = jnp.sum(jnp.where(is_grp, jnp.exp(glog - gmax), 0.0), axis=-1, keepdims=True)
    pgrp = 1.0 / gsum
    lane_grp = ((lane_i - N_GROUPS) // EXPERTS_PER_GROUP).astype(_F32)
    in_grp = (lane_i >= N_GROUPS) & (lane_i < N_GROUPS + N_EXPERTS) & (lane_grp == grp)
    elog = jnp.where(in_grp, lg, MASKED)
    t1 = jnp.max(elog, axis=-1, keepdims=True)
    i1 = jnp.min(jnp.where(elog == t1, lane, no_lane), axis=-1, keepdims=True)
    elog2 = jnp.where(lane == i1, MASKED, elog)
    t2 = jnp.max(elog2, axis=-1, keepdims=True)
    i2 = jnp.min(jnp.where(elog2 == t2, lane, no_lane), axis=-1, keepdims=True)
    e21 = jnp.exp(t2 - t1)
    w1 = pgrp / (1.0 + e21)
    w2 = pgrp * e21 / (1.0 + e21)
    eid1 = i1 - N_GROUPS
    eid2 = i2 - N_GROUPS
    hot1 = lane == eid1
    hot2 = lane == eid2
    onehot = jnp.where(hot1, 1.0, 0.0) + jnp.where(hot2, 1.0, 0.0)
    r_i = lax.broadcasted_iota(jnp.int32, (tr, tr), 0)
    c_i = lax.broadcasted_iota(jnp.int32, (tr, tr), 1)
    lower = jnp.where(c_i < r_i, 1.0, 0.0).astype(_BF)
    before = jnp.dot(lower, onehot.astype(_BF), preferred_element_type=_F32) + carry_ref[0:1, :]
    rank1 = jnp.sum(jnp.where(hot1, before, 0.0), axis=-1, keepdims=True)
    rank2 = jnp.sum(jnp.where(hot2, before, 0.0), axis=-1, keepdims=True)
    total = carry_ref[0:1, :] + jnp.sum(onehot, axis=0, keepdims=True)
    carry_ref[...] = jnp.broadcast_to(total, carry_ref.shape)
    cnt_ref[...] = jnp.broadcast_to(total, cnt_ref.shape).astype(jnp.int32)
    ri = jnp.where(lane_i == 0, eid1, jnp.where(lane_i == 1, eid2, jnp.where(lane_i == 2, rank1,
                   jnp.where(lane_i == 3, rank2, 0.0))))
    ri_ref[...] = ri.astype(jnp.int32)
    rw_ref[...] = jnp.where(lane_i == 0, w1, jnp.where(lane_i == 1, w2, 0.0))


def _route(logits):
    t = logits.shape[0]
    tr = ROW_TILE
    row = lambda i: (i, 0)
    return pl.pallas_call(
        functools.partial(_route_kernel, tr=tr),
        grid=(t // tr,),
        in_specs=[pl.BlockSpec((tr, LANES), row)],
        out_specs=(pl.BlockSpec((tr, LANES), row), pl.BlockSpec((tr, LANES), row),
                   pl.BlockSpec((8, LANES), lambda i: (0, 0))),
        out_shape=(jax.ShapeDtypeStruct((t, LANES), jnp.int32), jax.ShapeDtypeStruct((t, LANES), _F32),
                   jax.ShapeDtypeStruct((8, LANES), jnp.int32)),
        scratch_shapes=[pltpu.VMEM((8, LANES), _F32)],
        compiler_params=_params("arbitrary"),
        name="route",
    )(logits)


def _dispatch_kernel(dest_ref, hn_ref, xr_in_ref, xr_ref, sem, *, td):
    del xr_in_ref
    i = pl.program_id(0)

    def row_copy(j, k):
        dst = dest_ref[2 * (i * td + j) + k]
        return pltpu.make_async_copy(hn_ref.at[pl.ds(j, 1), :], xr_ref.at[pl.ds(dst, 1), :], sem)

    def start(j, carry):
        row_copy(j, 0).start()
        row_copy(j, 1).start()
        return carry

    def wait(j, carry):
        row_copy(j, 0).wait()
        row_copy(j, 1).wait()
        return carry

    lax.fori_loop(0, td, start, 0)
    lax.fori_loop(0, td, wait, 0)


def _dispatch(dest, hn, xr_init):
    t, d = hn.shape
    td = ROW_TILE
    return pl.pallas_call(
        functools.partial(_dispatch_kernel, td=td),
        grid_spec=pltpu.PrefetchScalarGridSpec(
            num_scalar_prefetch=1,
            grid=(t // td,),
            in_specs=[pl.BlockSpec((td, d), lambda i, dest: (i, 0)), pl.BlockSpec(memory_space=pl.ANY)],
            out_specs=pl.BlockSpec(memory_space=pl.ANY),
            scratch_shapes=[pltpu.SemaphoreType.DMA(())],
        ),
        out_shape=jax.ShapeDtypeStruct(xr_init.shape, xr_init.dtype),
        input_output_aliases={2: 0},
        compiler_params=_params("arbitrary"),
        name="dispatch",
    )(dest, hn, xr_init)


def _expert_kernel(blk_e_ref, nvalid_ref, x_ref, w1_ref, w3_ref, w2_ref, y_ref):
    del blk_e_ref
    b = pl.program_id(0)

    @pl.when(b < nvalid_ref[0])
    def _():
        xb = x_ref[...].astype(_BF)
        a = jnp.dot(xb, w1_ref[0], preferred_element_type=_F32)
        u = jnp.dot(xb, w3_ref[0], preferred_element_type=_F32)
        hmid = (a / (1.0 + jnp.exp(-a))) * u
        y_ref[...] = jnp.dot(hmid.astype(_BF), w2_ref[0], preferred_element_type=_F32)

    @pl.when(b >= nvalid_ref[0])
    def _():
        y_ref[...] = jnp.zeros_like(y_ref)


def _experts(blk_e, nvalid, xr, w1, w3, w2):
    rows, d = xr.shape
    f = w1.shape[2]
    nblk = rows // EXPERT_ROWS
    return pl.pallas_call(
        _expert_kernel,
        grid_spec=pltpu.PrefetchScalarGridSpec(
            num_scalar_prefetch=2,
            grid=(nblk,),
            in_specs=[
                pl.BlockSpec((EXPERT_ROWS, d), lambda b, be, nv: (b, 0)),
                pl.BlockSpec((1, d, f), lambda b, be, nv: (be[b], 0, 0)),
                pl.BlockSpec((1, d, f), lambda b, be, nv: (be[b], 0, 0)),
                pl.BlockSpec((1, f, d), lambda b, be, nv: (be[b], 0, 0)),
            ],
            out_specs=pl.BlockSpec((EXPERT_ROWS, d), lambda b, be, nv: (b, 0)),
        ),
        out_shape=jax.ShapeDtypeStruct((rows, d), _F32),
        compiler_params=_params("arbitrary"),
        name="experts",
    )(blk_e, nvalid, xr, w1, w3, w2)


def _combine_kernel(dest_ref, h_ref, rw_ref, g_ref, yr_ref, op_ref, os_ref, ybuf, sem, *, tc, n_prompt_tiles):
    i = pl.program_id(0)

    def row_copy(j, k):
        src = dest_ref[2 * (i * tc + j) + k]
        return pltpu.make_async_copy(yr_ref.at[pl.ds(src, 1), :], ybuf.at[k, pl.ds(j, 1), :], sem)

    def start(j, carry):
        row_copy(j, 0).start()
        row_copy(j, 1).start()
        return carry

    def wait(j, carry):
        row_copy(j, 0).wait()
        row_copy(j, 1).wait()
        return carry

    lax.fori_loop(0, tc, start, 0)
    lax.fori_loop(0, tc, wait, 0)

    rw = rw_ref[...]
    y = rw[:, 0:1] * ybuf[0] + rw[:, 1:2] * ybuf[1]
    z = h_ref[...] + y
    out = z * lax.rsqrt(jnp.mean(z * z, axis=-1, keepdims=True) + EPS) * g_ref[...]

    @pl.when(i < n_prompt_tiles)
    def _():
        op_ref[...] = out

    @pl.when(i >= n_prompt_tiles)
    def _():
        os_ref[...] = out


def _combine(dest, h, rw, g, yr, *, n_p):
    t, d = h.shape
    tc = ROW_TILE
    npt = n_p // tc
    return pl.pallas_call(
        functools.partial(_combine_kernel, tc=tc, n_prompt_tiles=npt),
        grid_spec=pltpu.PrefetchScalarGridSpec(
            num_scalar_prefetch=1,
            grid=(t // tc,),
            in_specs=[
                pl.BlockSpec((tc, d), lambda i, dest: (i, 0)),
                pl.BlockSpec((tc, LANES), lambda i, dest: (i, 0)),
                pl.BlockSpec((1, d), lambda i, dest: (0, 0)),
                pl.BlockSpec(memory_space=pl.ANY),
            ],
            out_specs=(pl.BlockSpec((tc, d), lambda i, dest: (jnp.minimum(i, npt - 1), 0)),
                       pl.BlockSpec((tc, d), lambda i, dest: (jnp.maximum(i - npt, 0), 0))),
            scratch_shapes=[pltpu.VMEM((2, tc, d), _F32), pltpu.SemaphoreType.DMA(())],
        ),
        out_shape=(jax.ShapeDtypeStruct((n_p, d), _F32), jax.ShapeDtypeStruct((t - n_p, d), _F32)),
        compiler_params=_params("arbitrary"),
        name="combine_norm",
    )(dest, h, rw, g, yr)


def _layer(xp, xs, prompt_seq, sample_seq, norm1, w_in, sink, w_proj_a, w_proj_b, w_gate, b_gate, w_out,
           norm2, w_rg, b_rg, w_re, b_re, w1, w3, w2, norm_final):
    n_p, d = xp.shape
    n_s = xs.shape[0]
    t = n_p + n_s
    a_q = A_HEADS * HEAD_DIM
    a_cols = a_q + 2 * A_KV * HEAD_DIM
    b_w = B_HEADS * HEAD_DIM
    gw = B_HPG * HEAD_DIM

    def group_cols(gi):
        return [w_in[:, a_cols + s * b_w + gi * gw: a_cols + s * b_w + (gi + 1) * gw] for s in range(3)]

    w_all = jnp.concatenate([w_in[:, :a_cols]] + [c for gi in range(N_DIL) for c in group_cols(gi)] + [w_gate],
                            axis=1).astype(_BF)
    proj_a, q0, q1, q2, gates = _project(xp, xs, norm1.reshape(1, d), w_all, b_gate.reshape(1, 2 * d),
                                         a_cols=a_cols, b_cols=3 * gw, gate_cols=2 * d)

    oa = _band_attention(
        proj_a.reshape(1, t, a_cols), jnp.asarray(_alibi(A_HEADS)), sink.astype(_F32), n_inner=A_KV,
        bq=min(512, prompt_seq, sample_seq), half=A_HALF, step=1, shared_kv=True, has_sink=True,
        prompt_rows=n_p, prompt_seq=prompt_seq, sample_seq=sample_seq, name="attn_window")

    slopes_b = _alibi(B_HEADS)
    obs, lses = [], []
    for gi, ((w, r), qkv) in enumerate(zip(DILATED_GROUPS, (q0, q1, q2))):
        o_g, lse_g = _band_attention(
            qkv, jnp.asarray(slopes_b[gi * B_HPG:(gi + 1) * B_HPG]), jnp.zeros((B_HPG,), _F32), n_inner=r,
            bq=min(256, prompt_seq // r, sample_seq // r), half=w // (2 * r), step=r, shared_kv=False,
            has_sink=False, prompt_rows=n_p // r, prompt_seq=prompt_seq // r, sample_seq=sample_seq // r,
            name=f"attn_dilated_{r}")
        obs.append(o_g)
        lses.append(lse_g)

    n_r = N_GROUPS + N_EXPERTS
    wr = jnp.concatenate([w_rg, jnp.transpose(w_re, (1, 0, 2)).reshape(d, N_EXPERTS)], axis=1)
    wr = jnp.pad(wr, ((0, 0), (0, LANES - n_r)))
    wr_hi = wr.astype(_BF)
    wr_lo = (wr - wr_hi.astype(_F32)).astype(_BF)
    wr2 = jnp.concatenate([wr_hi, wr_lo], axis=1)
    br = jnp.pad(jnp.concatenate([b_rg, b_re.reshape(-1)]), (0, LANES - n_r)).reshape(1, LANES).astype(_F32)

    h, hn, logits = _mix(xp, xs, oa, obs, lses, gates, w_proj_a.astype(_BF), w_proj_b.astype(_BF),
                         w_out.astype(_BF), norm2.reshape(1, d), wr2, br)

    ri, rw, cnt = _route(logits)
    counts = cnt[0, :N_EXPERTS]
    eid = ri[:, 0:2]
    rank = ri[:, 2:4]
    pcounts = (counts + EXPERT_ROWS - 1) // EXPERT_ROWS * EXPERT_ROWS
    pends = jnp.cumsum(pcounts)
    pstarts = pends - pcounts
    dest = (pstarts[eid] + rank).reshape(-1).astype(jnp.int32)
    nblk = (2 * t + N_EXPERTS * (EXPERT_ROWS - 1) + EXPERT_ROWS - 1) // EXPERT_ROWS
    blk_start = jnp.arange(nblk, dtype=jnp.int32) * EXPERT_ROWS
    blk_e = jnp.minimum(jnp.sum(pends[None, :] <= blk_start[:, None], axis=1), N_EXPERTS - 1).astype(jnp.int32)
    nvalid = (pends[-1] // EXPERT_ROWS).astype(jnp.int32).reshape(1)

    xr = _dispatch(dest, hn, jnp.zeros((nblk * EXPERT_ROWS, d), _F32))
    yr = _experts(blk_e, nvalid, xr, w1.astype(_BF), w3.astype(_BF), w2.astype(_BF))
    return _combine(dest, h, rw, norm_final.reshape(1, d), yr, n_p=n_p)


def kernel(x_prompt, x_sample, norm1, w_in, attn_sink, w_proj_a, w_proj_b, w_gate, b_gate, w_out, norm2,
           w_router_group, b_router_group, w_router_expert, b_router_expert, w_expert_gate, w_expert_up,
           w_expert_down, norm_final):
    assert norm1.shape[0] == 1, "one layer"
    d = x_prompt.shape[-1]
    xp = x_prompt.reshape(-1, d)
    xs = x_sample.reshape(-1, d)
    yp, ys = _layer(xp, xs, x_prompt.shape[1], x_sample.shape[1], norm1[0], w_in[0], attn_sink[0], w_proj_a[0],
                    w_proj_b[0], w_gate[0], b_gate[0], w_out[0], norm2[0], w_router_group[0],
                    b_router_group[0], w_router_expert[0], b_router_expert[0], w_expert_gate[0],
                    w_expert_up[0], w_expert_down[0], norm_final)
    return yp.reshape(x_prompt.shape), ys.reshape(x_sample.shape)
```

```python
import functools

import jax
import jax.numpy as jnp
import numpy as np
from jax import lax
from jax.experimental import pallas as pl
from jax.experimental.pallas import tpu as pltpu

HEAD_DIM = 128
A_HEADS = 16
A_KV = 4
A_HALF = 128
DILATED_GROUPS = ((128, 1), (512, 4), (2048, 16))
N_DIL = len(DILATED_GROUPS)
B_HPG = 4
B_HEADS = N_DIL * B_HPG
N_GROUPS = 8
EXPERTS_PER_GROUP = 8
N_EXPERTS = N_GROUPS * EXPERTS_PER_GROUP
EPS = 1e-6

LANES = 128
MASKED = -1e30
VMEM_LIMIT = 56 * 1024 * 1024
SUBQ = 128
HEADS_PER_STEP = 4
EXPERT_ROWS = 256
DMA_UNROLL = 8
PROJ_TM, PROJ_TN = 1024, 512
ROW_TILE = 256

_BF = jnp.bfloat16
_F32 = jnp.float32


def _alibi(n):
    return np.power(2.0, -8.0 * (np.arange(n) + 1) / n).astype(np.float32)


def _params(*sem):
    return pltpu.CompilerParams(dimension_semantics=sem, vmem_limit_bytes=VMEM_LIMIT)


def _proj_kernel(xp_ref, xs_ref, g_ref, w_ref, b_ref, oa_ref, q0_ref, q1_ref, q2_ref, gt_ref, xn_ref, acc_ref,
                 *, n_prompt_tiles, part_starts, tm):
    i = pl.program_id(0)
    j = pl.program_id(1)

    def norm_into_scratch(x_ref):
        x = x_ref[...]
        y = x * lax.rsqrt(jnp.mean(x * x, axis=-1, keepdims=True) + EPS)
        xn_ref[...] = (y * g_ref[...]).astype(_BF)

    @pl.when((j == 0) & (i < n_prompt_tiles))
    def _():
        norm_into_scratch(xp_ref)

    @pl.when((j == 0) & (i >= n_prompt_tiles))
    def _():
        norm_into_scratch(xs_ref)

    def tile():
        return jnp.dot(xn_ref[...], w_ref[...], preferred_element_type=_F32)

    s0, s1, s2, sg = part_starts

    @pl.when(j < s0)
    def _():
        oa_ref[...] = tile().astype(_BF)

    for (lo, hi, q_ref, (_, r)) in ((s0, s1, q0_ref, DILATED_GROUPS[0]), (s1, s2, q1_ref, DILATED_GROUPS[1]),
                                    (s2, sg, q2_ref, DILATED_GROUPS[2])):
        @pl.when((j >= lo) & (j < hi))
        def _(q_ref=q_ref, r=r):
            acc = tile()
            if r == 1:
                q_ref[0] = acc.astype(_BF)
            else:
                for cb in range(acc.shape[1] // LANES):
                    acc_ref[cb] = acc[:, cb * LANES:(cb + 1) * LANES]
                for p in range(r):
                    for cb in range(acc.shape[1] // LANES):
                        q_ref[p, :, cb * LANES:(cb + 1) * LANES] = (
                            acc_ref[cb, pl.ds(p, tm // r, stride=r), :].astype(_BF))

    @pl.when(j >= sg)
    def _():
        z = tile() + b_ref[...]
        gt_ref[...] = (0.5 * jnp.tanh(0.5 * z) + 0.5).astype(_BF)


def _project(xp, xs, g, w, b_gate, *, a_cols, b_cols, gate_cols):
    n_p, d = xp.shape
    n_s = xs.shape[0]
    t = n_p + n_s
    tm, tn = PROJ_TM, PROJ_TN
    npt = n_p // tm
    na, nb, ng = a_cols // tn, b_cols // tn, gate_cols // tn
    s0, s1, s2, sg = na, na + nb, na + 2 * nb, na + 3 * nb
    once = pl.Buffered(1)

    def part_map(lo, n):
        return lambda i, j: (0, i, jnp.clip(j - lo, 0, n - 1))

    q_shapes = [jax.ShapeDtypeStruct((r, t // r, b_cols), _BF) for _, r in DILATED_GROUPS]
    q_specs = [pl.BlockSpec((r, tm // r, tn), part_map(lo, nb))
               for lo, (_, r) in zip((s0, s1, s2), DILATED_GROUPS)]
    return pl.pallas_call(
        functools.partial(_proj_kernel, n_prompt_tiles=npt, part_starts=(s0, s1, s2, sg), tm=tm),
        grid=(t // tm, sg + ng),
        in_specs=[
            pl.BlockSpec((tm, d), lambda i, j: (jnp.minimum(i, npt - 1), 0), pipeline_mode=once),
            pl.BlockSpec((tm, d), lambda i, j: (jnp.maximum(i - npt, 0), 0), pipeline_mode=once),
            pl.BlockSpec((1, d), lambda i, j: (0, 0)),
            pl.BlockSpec((d, tn), lambda i, j: (0, j)),
            pl.BlockSpec((1, tn), lambda i, j: (0, jnp.clip(j - sg, 0, ng - 1))),
        ],
        out_specs=[pl.BlockSpec((tm, tn), lambda i, j: (i, jnp.minimum(j, na - 1)))] + q_specs
                  + [pl.BlockSpec((tm, tn), lambda i, j: (i, jnp.clip(j - sg, 0, ng - 1)))],
        out_shape=[jax.ShapeDtypeStruct((t, a_cols), _BF)] + q_shapes
                  + [jax.ShapeDtypeStruct((t, gate_cols), _BF)],
        scratch_shapes=[pltpu.VMEM((tm, d), _BF), pltpu.VMEM((tn // LANES, tm, LANES), _F32)],
        compiler_params=_params("arbitrary", "arbitrary"),
        name="norm_proj",
    )(xp, xs, g, w, b_gate)


def _band_attn_kernel(slope_ref, sink_ref, q_ref, kl_ref, km_ref, kr_ref, vl_ref, vm_ref, vr_ref,
                      *rest, bq, half, step, shared_kv, has_sink, prompt_rows, prompt_seq, sample_seq):
    if has_sink:
        o_ref, kcat, vcat = rest
        lse_ref = None
    else:
        o_ref, lse_ref, kcat, vcat = rest
    i = pl.program_id(0)
    c = pl.program_id(1)
    win = SUBQ + 2 * half

    kcat[0:half, :] = kl_ref[...]
    kcat[half:half + bq, :] = km_ref[...]
    kcat[half + bq:, :] = kr_ref[...]
    vcat[0:half, :] = vl_ref[...]
    vcat[half:half + bq, :] = vm_ref[...]
    vcat[half + bq:, :] = vr_ref[...]

    u0 = i * bq
    in_prompt = u0 < prompt_rows
    lo = jnp.where(in_prompt, (u0 // prompt_seq) * prompt_seq,
                   prompt_rows + ((u0 - prompt_rows) // sample_seq) * sample_seq)
    hi = lo + jnp.where(in_prompt, prompt_seq, sample_seq)

    qi = lax.broadcasted_iota(jnp.int32, (SUBQ, win), 0)
    kj = lax.broadcasted_iota(jnp.int32, (SUBQ, win), 1)
    rel = kj - half - qi
    absrel = jnp.abs(rel)
    in_band = absrel <= half
    neg_dist = -(absrel * step).astype(_F32)
    scale = HEAD_DIM ** -0.5
    lane = lax.broadcasted_iota(jnp.int32, (SUBQ, LANES), 1)

    heads = range(HEADS_PER_STEP)
    for sb in range(bq // SUBQ):
        kpos = kj + (u0 + sb * SUBQ - half)
        valid = in_band & (kpos >= lo) & (kpos < hi)
        bias = jnp.where(valid, neg_dist, MASKED)
        rows = slice(sb * SUBQ, (sb + 1) * SUBQ)
        if shared_kv:
            k = kcat[sb * SUBQ:sb * SUBQ + win, :]
            v = vcat[sb * SUBQ:sb * SUBQ + win, :]
            q4 = jnp.concatenate([q_ref[rows, h * HEAD_DIM:(h + 1) * HEAD_DIM] for h in heads], axis=0)
            s4 = lax.dot_general(q4, k, (((1,), (1,)), ((), ())), preferred_element_type=_F32)
            ps, ms, ls = [], [], []
            for h in heads:
                s = s4[h * SUBQ:(h + 1) * SUBQ] * scale + slope_ref[c * HEADS_PER_STEP + h] * bias
                m = jnp.max(s, axis=-1, keepdims=True)
                p = jnp.exp(s - m)
                ms.append(m)
                ls.append(jnp.sum(p, axis=-1, keepdims=True))
                ps.append(p.astype(_BF))
            pv4 = jnp.dot(jnp.concatenate(ps, axis=0), v, preferred_element_type=_F32)
            for h in heads:
                l = ls[h] + jnp.exp(sink_ref[c * HEADS_PER_STEP + h] - ms[h])
                o_ref[rows, h * HEAD_DIM:(h + 1) * HEAD_DIM] = (
                    pv4[h * SUBQ:(h + 1) * SUBQ] / l).astype(o_ref.dtype)
            continue
        lse_tile = jnp.zeros((SUBQ, LANES), _F32)
        for h in heads:
            cols = slice(h * HEAD_DIM, (h + 1) * HEAD_DIM)
            k = kcat[sb * SUBQ:sb * SUBQ + win, cols]
            v = vcat[sb * SUBQ:sb * SUBQ + win, cols]
            s = lax.dot_general(q_ref[rows, cols], k, (((1,), (1,)), ((), ())), preferred_element_type=_F32)
            s = s * scale + slope_ref[h] * bias
            m = jnp.max(s, axis=-1, keepdims=True)
            p = jnp.exp(s - m)
            l = jnp.sum(p, axis=-1, keepdims=True)
            pv = jnp.dot(p.astype(_BF), v, preferred_element_type=_F32)
            lse_tile = jnp.where(lane == h, m + jnp.log(l), lse_tile)
            o_ref[rows, cols] = (pv / l).astype(o_ref.dtype)
        lse_ref[rows, :] = lse_tile


def _band_attention(qkv, slopes, sinks, *, n_inner, bq, half, step, shared_kv, has_sink,
                    prompt_rows, prompt_seq, sample_seq, name):
    lead, rows, _ = qkv.shape
    assert prompt_seq % bq == 0 and sample_seq % bq == 0 and bq % SUBQ == 0 and bq % half == 0
    nq = rows // bq
    hb = bq // half
    last_halo = rows // half - 1
    width = HEADS_PER_STEP * HEAD_DIM
    if shared_kv:
        kv_w = HEAD_DIM
        k_base, v_base = width * n_inner // HEAD_DIM, width * n_inner // HEAD_DIM + n_inner
        lead_of = lambda c: 0
        q_col = lambda c: c
        kv_col = lambda base: (lambda c: base + c)
    else:
        kv_w = width
        k_base, v_base = 1, 2
        lead_of = lambda c: c
        q_col = lambda c: 0
        kv_col = lambda base: (lambda c: base)

    def main_map(col):
        return lambda i, c: (lead_of(c), i, col(c))

    def left_map(col):
        return lambda i, c: (lead_of(c), jnp.maximum(i * hb - 1, 0), col(c))

    def right_map(col):
        return lambda i, c: (lead_of(c), jnp.minimum((i + 1) * hb, last_halo), col(c))

    smem = pl.BlockSpec(memory_space=pltpu.SMEM)
    in_specs = [
        smem, smem,
        pl.BlockSpec((None, bq, width), main_map(q_col)),
        pl.BlockSpec((None, half, kv_w), left_map(kv_col(k_base))),
        pl.BlockSpec((None, bq, kv_w), main_map(kv_col(k_base))),
        pl.BlockSpec((None, half, kv_w), right_map(kv_col(k_base))),
        pl.BlockSpec((None, half, kv_w), left_map(kv_col(v_base))),
        pl.BlockSpec((None, bq, kv_w), main_map(kv_col(v_base))),
        pl.BlockSpec((None, half, kv_w), right_map(kv_col(v_base))),
    ]
    if has_sink:
        out_specs = pl.BlockSpec((bq, width), lambda i, c: (i, c))
        out_shape = jax.ShapeDtypeStruct((rows, n_inner * width), _BF)
    else:
        out_specs = (pl.BlockSpec((None, bq, width), lambda i, c: (c, i, 0)),
                     pl.BlockSpec((None, bq, LANES), lambda i, c: (c, i, 0)))
        out_shape = (jax.ShapeDtypeStruct((lead, rows, width), _BF),
                     jax.ShapeDtypeStruct((lead, rows, LANES), _F32))
    return pl.pallas_call(
        functools.partial(_band_attn_kernel, bq=bq, half=half, step=step, shared_kv=shared_kv,
                          has_sink=has_sink, prompt_rows=prompt_rows, prompt_seq=prompt_seq,
                          sample_seq=sample_seq),
        grid=(nq, n_inner),
        in_specs=in_specs,
        out_specs=out_specs,
        out_shape=out_shape,
        scratch_shapes=[pltpu.VMEM((bq + 2 * half, kv_w), _BF), pltpu.VMEM((bq + 2 * half, kv_w), _BF)],
        compiler_params=_params("parallel", "arbitrary"),
        name=name,
    )(slopes, sinks, qkv, qkv, qkv, qkv, qkv, qkv, qkv)


def _mix_kernel(xp_ref, xs_ref, oa_ref, ob0_ref, ob1_ref, ob2_ref, l0_ref, l1_ref, l2_ref, g_ref,
                wpa_ref, wpb_ref, wout_ref, n2_ref, wr_ref, br_ref, h_ref, hn_ref, lg_ref, o_scr, l_scr,
                *, n_prompt_tiles, d, tm):
    i = pl.program_id(0)

    def token_order(o_ref, l_ref, r, slot):
        if r == 1:
            return [o_ref[0, :, h * HEAD_DIM:(h + 1) * HEAD_DIM].astype(_F32) for h in range(B_HPG)], l_ref[0]
        for p in range(r):
            for h in range(B_HPG):
                o_scr[slot, h, pl.ds(p, tm // r, stride=r), :] = (
                    o_ref[p, :, h * HEAD_DIM:(h + 1) * HEAD_DIM].astype(_F32))
            l_scr[slot, pl.ds(p, tm // r, stride=r), :] = l_ref[p]
        return [o_scr[slot, h] for h in range(B_HPG)], l_scr[slot]

    o0, l0 = token_order(ob0_ref, l0_ref, DILATED_GROUPS[0][1], 0)
    o1, l1 = token_order(ob1_ref, l1_ref, DILATED_GROUPS[1][1], 0)
    o2, l2 = token_order(ob2_ref, l2_ref, DILATED_GROUPS[2][1], 1)
    mx = jnp.maximum(jnp.maximum(l0, l1), l2)
    e0, e1, e2 = jnp.exp(l0 - mx), jnp.exp(l1 - mx), jnp.exp(l2 - mx)
    den = e0 + e1 + e2
    a0, a1, a2 = e0 / den, e1 / den, e2 / den
    parts = []
    for h in range(B_HPG):
        parts.append(a0[:, h:h + 1] * o0[h] + a1[:, h:h + 1] * o1[h] + a2[:, h:h + 1] * o2[h])
    ob = jnp.concatenate(parts, axis=1).astype(_BF)

    ta = jnp.dot(oa_ref[...], wpa_ref[...], preferred_element_type=_F32)
    tb = jnp.dot(ob, wpb_ref[...], preferred_element_type=_F32)
    merged = g_ref[:, :d].astype(_F32) * ta + g_ref[:, d:].astype(_F32) * tb
    x = jnp.where(i < n_prompt_tiles, xp_ref[...], xs_ref[...])
    h_new = x + jnp.dot(merged.astype(_BF), wout_ref[...], preferred_element_type=_F32)
    h_ref[...] = h_new
    hn = h_new * lax.rsqrt(jnp.mean(h_new * h_new, axis=-1, keepdims=True) + EPS) * n2_ref[...]
    hn_ref[...] = hn
    hn_hi = hn.astype(_BF)
    hn_lo = (hn - hn_hi.astype(_F32)).astype(_BF)
    r = (jnp.dot(hn_hi, wr_ref[...], preferred_element_type=_F32)
         + jnp.dot(hn_lo, wr_ref[...], preferred_element_type=_F32))
    lg_ref[...] = r[:, :LANES] + r[:, LANES:] + br_ref[...]


def _mix(xp, xs, oa, obs, lses, gates, wpa, wpb, wout, n2, wr, br):
    n_p, d = xp.shape
    n_s = xs.shape[0]
    t = n_p + n_s
    tm = ROW_TILE
    npt = n_p // tm
    row = lambda i: (i, 0)
    const = lambda i: (0, 0)
    width = B_HPG * HEAD_DIM

    def resident(shape):
        return pl.BlockSpec(shape, const, pipeline_mode=pl.Buffered(1))

    def phase_blocks(cols):
        return [pl.BlockSpec((r, tm // r, cols), lambda i: (0, i, 0)) for _, r in DILATED_GROUPS]

    in_specs = [
        pl.BlockSpec((tm, d), lambda i: (jnp.minimum(i, npt - 1), 0)),
        pl.BlockSpec((tm, d), lambda i: (jnp.maximum(i - npt, 0), 0)),
        pl.BlockSpec((tm, oa.shape[1]), row),
        *phase_blocks(width), *phase_blocks(LANES),
        pl.BlockSpec((tm, 2 * d), row),
        resident(wpa.shape), resident(wpb.shape), resident(wout.shape),
        resident((1, d)), resident(wr.shape), resident((1, LANES)),
    ]
    return pl.pallas_call(
        functools.partial(_mix_kernel, n_prompt_tiles=npt, d=d, tm=tm),
        grid=(t // tm,),
        in_specs=in_specs,
        out_specs=(pl.BlockSpec((tm, d), row), pl.BlockSpec((tm, d), row), pl.BlockSpec((tm, LANES), row)),
        out_shape=(jax.ShapeDtypeStruct((t, d), _F32), jax.ShapeDtypeStruct((t, d), _F32),
                   jax.ShapeDtypeStruct((t, LANES), _F32)),
        scratch_shapes=[pltpu.VMEM((2, B_HPG, tm, HEAD_DIM), _F32), pltpu.VMEM((2, tm, LANES), _F32)],
        compiler_params=_params("parallel"),
        name="mix_out_router",
    )(xp, xs, oa, *obs, *lses, gates, wpa, wpb, wout, n2, wr, br)


def _route_kernel(lg_ref, ri_ref, rw_ref, cnt_ref, carry_ref, *, tr):
    i = pl.program_id(0)

    @pl.when(i == 0)
    def _():
        carry_ref[...] = jnp.zeros_like(carry_ref)

    lg = lg_ref[...]
    lane_i = lax.broadcasted_iota(jnp.int32, (tr, LANES), 1)
    lane = lane_i.astype(_F32)
    no_lane = float(LANES)
    is_grp = lane_i < N_GROUPS
    glog = jnp.where(is_grp, lg, MASKED)
    gmax = jnp.max(glog, axis=-1, keepdims=True)
    grp = jnp.min(jnp.where(glog == gmax, lane, no_lane), axis=-1, keepdims=True)
    gsum = jnp.sum(jnp.where(is_grp, jnp.exp(glog - gmax), 0.0), axis=-1, keepdims=True)
    pgrp = 1.0 / gsum
    lane_grp = ((lane_i - N_GROUPS) // EXPERTS_PER_GROUP).astype(_F32)
    in_grp = (lane_i >= N_GROUPS) & (lane_i < N_GROUPS + N_EXPERTS) & (lane_grp == grp)
    elog = jnp.where(in_grp, lg, MASKED)
    t1 = jnp.max(elog, axis=-1, keepdims=True)
    i1 = jnp.min(jnp.where(elog == t1, lane, no_lane), axis=-1, keepdims=True)
    elog2 = jnp.where(lane == i1, MASKED, elog)
    t2 = jnp.max(elog2, axis=-1, keepdims=True)
    i2 = jnp.min(jnp.where(elog2 == t2, lane, no_lane), axis=-1, keepdims=True)
    e21 = jnp.exp(t2 - t1)
    w1 = pgrp / (1.0 + e21)
    w2 = pgrp * e21 / (1.0 + e21)
    eid1 = i1 - N_GROUPS
    eid2 = i2 - N_GROUPS
    hot1 = lane == eid1
    hot2 = lane == eid2
    onehot = jnp.where(hot1, 1.0, 0.0) + jnp.where(hot2, 1.0, 0.0)
    r_i = lax.broadcasted_iota(jnp.int32, (tr, tr), 0)
    c_i = lax.broadcasted_iota(jnp.int32, (tr, tr), 1)
    lower = jnp.where(c_i < r_i, 1.0, 0.0).astype(_BF)
    before = jnp.dot(lower, onehot.astype(_BF), preferred_element_type=_F32) + carry_ref[0:1, :]
    rank1 = jnp.sum(jnp.where(hot1, before, 0.0), axis=-1, keepdims=True)
    rank2 = jnp.sum(jnp.where(hot2, before, 0.0), axis=-1, keepdims=True)
    total = carry_ref[0:1, :] + jnp.sum(onehot, axis=0, keepdims=True)
    carry_ref[...] = jnp.broadcast_to(total, carry_ref.shape)
    cnt_ref[...] = jnp.broadcast_to(total, cnt_ref.shape).astype(jnp.int32)
    ri = jnp.where(lane_i == 0, eid1, jnp.where(lane_i == 1, eid2, jnp.where(lane_i == 2, rank1,
                   jnp.where(lane_i == 3, rank2, 0.0))))
    ri_ref[...] = ri.astype(jnp.int32)
    rw_ref[...] = jnp.where(lane_i == 0, w1, jnp.where(lane_i == 1, w2, 0.0))


def _route(logits):
    t = logits.shape[0]
    tr = ROW_TILE
    row = lambda i: (i, 0)
    return pl.pallas_call(
        functools.partial(_route_kernel, tr=tr),
        grid=(t // tr,),
        in_specs=[pl.BlockSpec((tr, LANES), row)],
        out_specs=(pl.BlockSpec((tr, LANES), row), pl.BlockSpec((tr, LANES), row),
                   pl.BlockSpec((8, LANES), lambda i: (0, 0))),
        out_shape=(jax.ShapeDtypeStruct((t, LANES), jnp.int32), jax.ShapeDtypeStruct((t, LANES), _F32),
                   jax.ShapeDtypeStruct((8, LANES), jnp.int32)),
        scratch_shapes=[pltpu.VMEM((8, LANES), _F32)],
        compiler_params=_params("arbitrary"),
        name="route",
    )(logits)


def _dispatch_kernel(dest_ref, hn_ref, xr_in_ref, xr_ref, sems, *, td):
    del xr_in_ref
    i = pl.program_id(0)
    slot = i % 2

    def start(j, carry):
        tok = i * td + j
        for k in range(2):
            dst = dest_ref[2 * tok + k]
            pltpu.make_async_copy(hn_ref.at[pl.ds(tok, 1), :], xr_ref.at[pl.ds(dst, 1), :], sems.at[slot]).start()
        return carry

    def wait_step(s):
        for _ in range(2):
            pltpu.make_async_copy(hn_ref.at[pl.ds(0, td), :], xr_ref.at[pl.ds(0, td), :], sems.at[s]).wait()

    lax.fori_loop(0, td, start, 0, unroll=DMA_UNROLL)

    @pl.when(i > 0)
    def _():
        wait_step(1 - slot)

    @pl.when(i == pl.num_programs(0) - 1)
    def _():
        wait_step(slot)


def _dispatch(dest, hn, xr_init):
    t, d = hn.shape
    td = ROW_TILE
    return pl.pallas_call(
        functools.partial(_dispatch_kernel, td=td),
        grid_spec=pltpu.PrefetchScalarGridSpec(
            num_scalar_prefetch=1,
            grid=(t // td,),
            in_specs=[pl.BlockSpec(memory_space=pl.ANY), pl.BlockSpec(memory_space=pl.ANY)],
            out_specs=pl.BlockSpec(memory_space=pl.ANY),
            scratch_shapes=[pltpu.SemaphoreType.DMA((2,))],
        ),
        out_shape=jax.ShapeDtypeStruct(xr_init.shape, xr_init.dtype),
        input_output_aliases={2: 0},
        compiler_params=_params("arbitrary"),
        name="dispatch",
    )(dest, hn, xr_init)


def _expert_kernel(blk_e_ref, nvalid_ref, x_ref, w1_ref, w3_ref, w2_ref, y_ref):
    del blk_e_ref
    b = pl.program_id(0)

    @pl.when(b < nvalid_ref[0])
    def _():
        xb = x_ref[...].astype(_BF)
        a = jnp.dot(xb, w1_ref[0], preferred_element_type=_F32)
        u = jnp.dot(xb, w3_ref[0], preferred_element_type=_F32)
        hmid = (a / (1.0 + jnp.exp(-a))) * u
        y_ref[...] = jnp.dot(hmid.astype(_BF), w2_ref[0], preferred_element_type=_F32)

    @pl.when(b >= nvalid_ref[0])
    def _():
        y_ref[...] = jnp.zeros_like(y_ref)


def _experts(blk_e, nvalid, xr, w1, w3, w2):
    rows, d = xr.shape
    f = w1.shape[2]
    nblk = rows // EXPERT_ROWS
    return pl.pallas_call(
        _expert_kernel,
        grid_spec=pltpu.PrefetchScalarGridSpec(
            num_scalar_prefetch=2,
            grid=(nblk,),
            in_specs=[
                pl.BlockSpec((EXPERT_ROWS, d), lambda b, be, nv: (b, 0)),
                pl.BlockSpec((1, d, f), lambda b, be, nv: (be[b], 0, 0)),
                pl.BlockSpec((1, d, f), lambda b, be, nv: (be[b], 0, 0)),
                pl.BlockSpec((1, f, d), lambda b, be, nv: (be[b], 0, 0)),
            ],
            out_specs=pl.BlockSpec((EXPERT_ROWS, d), lambda b, be, nv: (b, 0)),
        ),
        out_shape=jax.ShapeDtypeStruct((rows, d), _F32),
        compiler_params=_params("arbitrary"),
        name="experts",
    )(blk_e, nvalid, xr, w1, w3, w2)


def _combine_kernel(dest_ref, h_ref, rw_ref, g_ref, yr_ref, op_ref, os_ref, ybuf, sems, *, tc, n_prompt_tiles):
    i = pl.program_id(0)
    slot = i % 2

    def gather(tile, s):
        def start(j, carry):
            for k in range(2):
                src = dest_ref[2 * (tile * tc + j) + k]
                pltpu.make_async_copy(yr_ref.at[pl.ds(src, 1), :], ybuf.at[s, k, pl.ds(j, 1), :],
                                      sems.at[s]).start()
            return carry

        lax.fori_loop(0, tc, start, 0, unroll=DMA_UNROLL)

    @pl.when(i == 0)
    def _():
        gather(0, 0)

    @pl.when(i + 1 < pl.num_programs(0))
    def _():
        gather(i + 1, 1 - slot)

    for k in range(2):
        pltpu.make_async_copy(yr_ref.at[pl.ds(0, tc), :], ybuf.at[slot, k], sems.at[slot]).wait()

    rw = rw_ref[...]
    y = rw[:, 0:1] * ybuf[slot, 0] + rw[:, 1:2] * ybuf[slot, 1]
    z = h_ref[...] + y
    out = z * lax.rsqrt(jnp.mean(z * z, axis=-1, keepdims=True) + EPS) * g_ref[...]

    @pl.when(i < n_prompt_tiles)
    def _():
        op_ref[...] = out

    @pl.when(i >= n_prompt_tiles)
    def _():
        os_ref[...] = out


def _combine(dest, h, rw, g, yr, *, n_p):
    t, d = h.shape
    tc = ROW_TILE
    npt = n_p // tc
    return pl.pallas_call(
        functools.partial(_combine_kernel, tc=tc, n_prompt_tiles=npt),
        grid_spec=pltpu.PrefetchScalarGridSpec(
            num_scalar_prefetch=1,
            grid=(t // tc,),
            in_specs=[
                pl.BlockSpec((tc, d), lambda i, dest: (i, 0)),
                pl.BlockSpec((tc, LANES), lambda i, dest: (i, 0)),
                pl.BlockSpec((1, d), lambda i, dest: (0, 0)),
                pl.BlockSpec(memory_space=pl.ANY),
            ],
            out_specs=(pl.BlockSpec((tc, d), lambda i, dest: (jnp.minimum(i, npt - 1), 0)),
                       pl.BlockSpec((tc, d), lambda i, dest: (jnp.maximum(i - npt, 0), 0))),
            scratch_shapes=[pltpu.VMEM((2, 2, tc, d), _F32), pltpu.SemaphoreType.DMA((2,))],
        ),
        out_shape=(jax.ShapeDtypeStruct((n_p, d), _F32), jax.ShapeDtypeStruct((t - n_p, d), _F32)),
        compiler_params=_params("arbitrary"),
        name="combine_norm",
    )(dest, h, rw, g, yr)


def _layer(xp, xs, prompt_seq, sample_seq, norm1, w_in, sink, w_proj_a, w_proj_b, w_gate, b_gate, w_out,
           norm2, w_rg, b_rg, w_re, b_re, w1, w3, w2, norm_final):
    n_p, d = xp.shape
    n_s = xs.shape[0]
    t = n_p + n_s
    a_q = A_HEADS * HEAD_DIM
    a_cols = a_q + 2 * A_KV * HEAD_DIM
    b_w = B_HEADS * HEAD_DIM
    gw = B_HPG * HEAD_DIM

    def group_cols(gi):
        return [w_in[:, a_cols + s * b_w + gi * gw: a_cols + s * b_w + (gi + 1) * gw] for s in range(3)]

    w_all = jnp.concatenate([w_in[:, :a_cols]] + [c for gi in range(N_DIL) for c in group_cols(gi)] + [w_gate],
                            axis=1).astype(_BF)
    proj_a, q0, q1, q2, gates = _project(xp, xs, norm1.reshape(1, d), w_all, b_gate.reshape(1, 2 * d),
                                         a_cols=a_cols, b_cols=3 * gw, gate_cols=2 * d)

    oa = _band_attention(
        proj_a.reshape(1, t, a_cols), jnp.asarray(_alibi(A_HEADS)), sink.astype(_F32), n_inner=A_KV,
        bq=min(512, prompt_seq, sample_seq), half=A_HALF, step=1, shared_kv=True, has_sink=True,
        prompt_rows=n_p, prompt_seq=prompt_seq, sample_seq=sample_seq, name="attn_window")

    slopes_b = _alibi(B_HEADS)
    obs, lses = [], []
    for gi, ((w, r), qkv) in enumerate(zip(DILATED_GROUPS, (q0, q1, q2))):
        o_g, lse_g = _band_attention(
            qkv, jnp.asarray(slopes_b[gi * B_HPG:(gi + 1) * B_HPG]), jnp.zeros((B_HPG,), _F32), n_inner=r,
            bq=min(256, prompt_seq // r, sample_seq // r), half=w // (2 * r), step=r, shared_kv=False,
            has_sink=False, prompt_rows=n_p // r, prompt_seq=prompt_seq // r, sample_seq=sample_seq // r,
            name=f"attn_dilated_{r}")
        obs.append(o_g)
        lses.append(lse_g)

    n_r = N_GROUPS + N_EXPERTS
    wr = jnp.concatenate([w_rg, jnp.transpose(w_re, (1, 0, 2)).reshape(d, N_EXPERTS)], axis=1)
    wr = jnp.pad(wr, ((0, 0), (0, LANES - n_r)))
    wr_hi = wr.astype(_BF)
    wr_lo = (wr - wr_hi.astype(_F32)).astype(_BF)
    wr2 = jnp.concatenate([wr_hi, wr_lo], axis=1)
    br = jnp.pad(jnp.concatenate([b_rg, b_re.reshape(-1)]), (0, LANES - n_r)).reshape(1, LANES).astype(_F32)

    h, hn, logits = _mix(xp, xs, oa, obs, lses, gates, w_proj_a.astype(_BF), w_proj_b.astype(_BF),
                         w_out.astype(_BF), norm2.reshape(1, d), wr2, br)

    ri, rw, cnt = _route(logits)
    counts = cnt[0, :N_EXPERTS]
    eid = ri[:, 0:2]
    rank = ri[:, 2:4]
    pcounts = (counts + EXPERT_ROWS - 1) // EXPERT_ROWS * EXPERT_ROWS
    pends = jnp.cumsum(pcounts)
    pstarts = pends - pcounts
    dest = (pstarts[eid] + rank).reshape(-1).astype(jnp.int32)
    nblk = (2 * t + N_EXPERTS * (EXPERT_ROWS - 1) + EXPERT_ROWS - 1) // EXPERT_ROWS
    blk_start = jnp.arange(nblk, dtype=jnp.int32) * EXPERT_ROWS
    blk_e = jnp.minimum(jnp.sum(pends[None, :] <= blk_start[:, None], axis=1), N_EXPERTS - 1).astype(jnp.int32)
    nvalid = (pends[-1] // EXPERT_ROWS).astype(jnp.int32).reshape(1)

    xr = _dispatch(dest, hn, jnp.zeros((nblk * EXPERT_ROWS, d), _F32))
    yr = _experts(blk_e, nvalid, xr, w1.astype(_BF), w3.astype(_BF), w2.astype(_BF))
    return _combine(dest, h, rw, norm_final.reshape(1, d), yr, n_p=n_p)


def kernel(x_prompt, x_sample, norm1, w_in, attn_sink, w_proj_a, w_proj_b, w_gate, b_gate, w_out, norm2,
           w_router_group, b_router_group, w_router_expert, b_router_expert, w_expert_gate, w_expert_up,
           w_expert_down, norm_final):
    assert norm1.shape[0] == 1, "one layer"
    d = x_prompt.shape[-1]
    xp = x_prompt.reshape(-1, d)
    xs = x_sample.reshape(-1, d)
    yp, ys = _layer(xp, xs, x_prompt.shape[1], x_sample.shape[1], norm1[0], w_in[0], attn_sink[0], w_proj_a[0],
                    w_proj_b[0], w_gate[0], b_gate[0], w_out[0], norm2[0], w_router_group[0],
                    b_router_group[0], w_router_expert[0], b_router_expert[0], w_expert_gate[0],
                    w_expert_up[0], w_expert_down[0], norm_final)
    return yp.reshape(x_prompt.shape), ys.reshape(x_sample.shape)
```

```python
import functools

import jax
import jax.numpy as jnp
import numpy as np
from jax import lax
from jax.experimental import pallas as pl
from jax.experimental.pallas import tpu as pltpu

HEAD_DIM = 128
A_HEADS = 16
A_KV = 4
A_HALF = 128
DILATED_GROUPS = ((128, 1), (512, 4), (2048, 16))
N_DIL = len(DILATED_GROUPS)
B_HPG = 4
B_HEADS = N_DIL * B_HPG
N_GROUPS = 8
EXPERTS_PER_GROUP = 8
N_EXPERTS = N_GROUPS * EXPERTS_PER_GROUP
EPS = 1e-6

LANES = 128
MASKED = -1e30
VMEM_LIMIT = 56 * 1024 * 1024
SUBQ = 128
HEADS_PER_STEP = 4
EXPERT_ROWS = 256
DMA_UNROLL = 8
PROJ_TM, PROJ_TN = 1024, 512
ROW_TILE = 256

_BF = jnp.bfloat16
_F32 = jnp.float32


def _alibi(n):
    return np.power(2.0, -8.0 * (np.arange(n) + 1) / n).astype(np.float32)


def _params(*sem):
    return pltpu.CompilerParams(dimension_semantics=sem, vmem_limit_bytes=VMEM_LIMIT)


def _proj_kernel(xp_ref, xs_ref, g_ref, w_ref, b_ref, oa_ref, q0_ref, q1_ref, q2_ref, gt_ref, xn_ref, acc_ref,
                 *, n_prompt_tiles, part_starts, tm):
    i = pl.program_id(0)
    j = pl.program_id(1)

    def norm_into_scratch(x_ref):
        x = x_ref[...]
        y = x * lax.rsqrt(jnp.mean(x * x, axis=-1, keepdims=True) + EPS)
        xn_ref[...] = (y * g_ref[...]).astype(_BF)

    @pl.when((j == 0) & (i < n_prompt_tiles))
    def _():
        norm_into_scratch(xp_ref)

    @pl.when((j == 0) & (i >= n_prompt_tiles))
    def _():
        norm_into_scratch(xs_ref)

    def tile():
        return jnp.dot(xn_ref[...], w_ref[...], preferred_element_type=_F32)

    s0, s1, s2, sg = part_starts

    @pl.when(j < s0)
    def _():
        oa_ref[...] = tile().astype(_BF)

    for (lo, hi, q_ref, (_, r)) in ((s0, s1, q0_ref, DILATED_GROUPS[0]), (s1, s2, q1_ref, DILATED_GROUPS[1]),
                                    (s2, sg, q2_ref, DILATED_GROUPS[2])):
        @pl.when((j >= lo) & (j < hi))
        def _(q_ref=q_ref, r=r):
            acc = tile()
            if r == 1:
                q_ref[0] = acc.astype(_BF)
            else:
                for cb in range(acc.shape[1] // LANES):
                    acc_ref[cb] = acc[:, cb * LANES:(cb + 1) * LANES]
                for p in range(r):
                    for cb in range(acc.shape[1] // LANES):
                        q_ref[p, :, cb * LANES:(cb + 1) * LANES] = (
                            acc_ref[cb, pl.ds(p, tm // r, stride=r), :].astype(_BF))

    @pl.when(j >= sg)
    def _():
        z = tile() + b_ref[...]
        gt_ref[...] = (0.5 * jnp.tanh(0.5 * z) + 0.5).astype(_BF)


def _project(xp, xs, g, w, b_gate, *, a_cols, b_cols, gate_cols):
    n_p, d = xp.shape
    n_s = xs.shape[0]
    t = n_p + n_s
    tm, tn = PROJ_TM, PROJ_TN
    npt = n_p // tm
    na, nb, ng = a_cols // tn, b_cols // tn, gate_cols // tn
    s0, s1, s2, sg = na, na + nb, na + 2 * nb, na + 3 * nb
    once = pl.Buffered(1)

    def part_map(lo, n):
        return lambda i, j: (0, i, jnp.clip(j - lo, 0, n - 1))

    q_shapes = [jax.ShapeDtypeStruct((r, t // r, b_cols), _BF) for _, r in DILATED_GROUPS]
    q_specs = [pl.BlockSpec((r, tm // r, tn), part_map(lo, nb))
               for lo, (_, r) in zip((s0, s1, s2), DILATED_GROUPS)]
    return pl.pallas_call(
        functools.partial(_proj_kernel, n_prompt_tiles=npt, part_starts=(s0, s1, s2, sg), tm=tm),
        grid=(t // tm, sg + ng),
        in_specs=[
            pl.BlockSpec((tm, d), lambda i, j: (jnp.minimum(i, npt - 1), 0), pipeline_mode=once),
            pl.BlockSpec((tm, d), lambda i, j: (jnp.maximum(i - npt, 0), 0), pipeline_mode=once),
            pl.BlockSpec((1, d), lambda i, j: (0, 0)),
            pl.BlockSpec((d, tn), lambda i, j: (0, j)),
            pl.BlockSpec((1, tn), lambda i, j: (0, jnp.clip(j - sg, 0, ng - 1))),
        ],
        out_specs=[pl.BlockSpec((tm, tn), lambda i, j: (i, jnp.minimum(j, na - 1)))] + q_specs
                  + [pl.BlockSpec((tm, tn), lambda i, j: (i, jnp.clip(j - sg, 0, ng - 1)))],
        out_shape=[jax.ShapeDtypeStruct((t, a_cols), _BF)] + q_shapes
                  + [jax.ShapeDtypeStruct((t, gate_cols), _BF)],
        scratch_shapes=[pltpu.VMEM((tm, d), _BF), pltpu.VMEM((tn // LANES, tm, LANES), _F32)],
        compiler_params=_params("arbitrary", "arbitrary"),
        name="norm_proj",
    )(xp, xs, g, w, b_gate)


def _band_attn_kernel(slope_ref, sink_ref, q_ref, kl_ref, km_ref, kr_ref, vl_ref, vm_ref, vr_ref,
                      *rest, bq, half, step, shared_kv, has_sink, prompt_rows, prompt_seq, sample_seq):
    if has_sink:
        o_ref, kcat, vcat = rest
        lse_ref = None
    else:
        o_ref, lse_ref, kcat, vcat = rest
    i = pl.program_id(0)
    c = pl.program_id(1)
    win = SUBQ + 2 * half

    kcat[0:half, :] = kl_ref[...]
    kcat[half:half + bq, :] = km_ref[...]
    kcat[half + bq:, :] = kr_ref[...]
    vcat[0:half, :] = vl_ref[...]
    vcat[half:half + bq, :] = vm_ref[...]
    vcat[half + bq:, :] = vr_ref[...]

    u0 = i * bq
    in_prompt = u0 < prompt_rows
    lo = jnp.where(in_prompt, (u0 // prompt_seq) * prompt_seq,
                   prompt_rows + ((u0 - prompt_rows) // sample_seq) * sample_seq)
    hi = lo + jnp.where(in_prompt, prompt_seq, sample_seq)

    qi = lax.broadcasted_iota(jnp.int32, (SUBQ, win), 0)
    kj = lax.broadcasted_iota(jnp.int32, (SUBQ, win), 1)
    rel = kj - half - qi
    absrel = jnp.abs(rel)
    in_band = absrel <= half
    neg_dist = -(absrel * step).astype(_F32)
    scale = HEAD_DIM ** -0.5
    lane = lax.broadcasted_iota(jnp.int32, (SUBQ, LANES), 1)

    heads = range(HEADS_PER_STEP)
    for sb in range(bq // SUBQ):
        kpos = kj + (u0 + sb * SUBQ - half)
        valid = in_band & (kpos >= lo) & (kpos < hi)
        bias = jnp.where(valid, neg_dist, MASKED)
        rows = slice(sb * SUBQ, (sb + 1) * SUBQ)
        if shared_kv:
            k = kcat[sb * SUBQ:sb * SUBQ + win, :]
            v = vcat[sb * SUBQ:sb * SUBQ + win, :]
            q4 = jnp.concatenate([q_ref[rows, h * HEAD_DIM:(h + 1) * HEAD_DIM] for h in heads], axis=0)
            s4 = lax.dot_general(q4, k, (((1,), (1,)), ((), ())), preferred_element_type=_F32)
            ps, ms, ls = [], [], []
            for h in heads:
                s = s4[h * SUBQ:(h + 1) * SUBQ] * scale + slope_ref[c * HEADS_PER_STEP + h] * bias
                m = jnp.max(s, axis=-1, keepdims=True)
                p = jnp.exp(s - m)
                ms.append(m)
                ls.append(jnp.sum(p, axis=-1, keepdims=True))
                ps.append(p.astype(_BF))
            pv4 = jnp.dot(jnp.concatenate(ps, axis=0), v, preferred_element_type=_F32)
            for h in heads:
                l = ls[h] + jnp.exp(sink_ref[c * HEADS_PER_STEP + h] - ms[h])
                o_ref[rows, h * HEAD_DIM:(h + 1) * HEAD_DIM] = (
                    pv4[h * SUBQ:(h + 1) * SUBQ] / l).astype(o_ref.dtype)
            continue
        lse_tile = jnp.zeros((SUBQ, LANES), _F32)
        for h in heads:
            cols = slice(h * HEAD_DIM, (h + 1) * HEAD_DIM)
            k = kcat[sb * SUBQ:sb * SUBQ + win, cols]
            v = vcat[sb * SUBQ:sb * SUBQ + win, cols]
            s = lax.dot_general(q_ref[rows, cols], k, (((1,), (1,)), ((), ())), preferred_element_type=_F32)
            s = s * scale + slope_ref[h] * bias
            m = jnp.max(s, axis=-1, keepdims=True)
            p = jnp.exp(s - m)
            l = jnp.sum(p, axis=-1, keepdims=True)
            pv = jnp.dot(p.astype(_BF), v, preferred_element_type=_F32)
            lse_tile = jnp.where(lane == h, m + jnp.log(l), lse_tile)
            o_ref[rows, cols] = (pv / l).astype(o_ref.dtype)
        lse_ref[rows, :] = lse_tile


def _band_attention(qkv, slopes, sinks, *, n_inner, bq, half, step, shared_kv, has_sink,
                    prompt_rows, prompt_seq, sample_seq, name):
    lead, rows, _ = qkv.shape
    assert prompt_seq % bq == 0 and sample_seq % bq == 0 and bq % SUBQ == 0 and bq % half == 0
    nq = rows // bq
    hb = bq // half
    last_halo = rows // half - 1
    width = HEADS_PER_STEP * HEAD_DIM
    if shared_kv:
        kv_w = HEAD_DIM
        k_base, v_base = width * n_inner // HEAD_DIM, width * n_inner // HEAD_DIM + n_inner
        lead_of = lambda c: 0
        q_col = lambda c: c
        kv_col = lambda base: (lambda c: base + c)
    else:
        kv_w = width
        k_base, v_base = 1, 2
        lead_of = lambda c: c
        q_col = lambda c: 0
        kv_col = lambda base: (lambda c: base)

    def main_map(col):
        return lambda i, c: (lead_of(c), i, col(c))

    def left_map(col):
        return lambda i, c: (lead_of(c), jnp.maximum(i * hb - 1, 0), col(c))

    def right_map(col):
        return lambda i, c: (lead_of(c), jnp.minimum((i + 1) * hb, last_halo), col(c))

    smem = pl.BlockSpec(memory_space=pltpu.SMEM)
    in_specs = [
        smem, smem,
        pl.BlockSpec((None, bq, width), main_map(q_col)),
        pl.BlockSpec((None, half, kv_w), left_map(kv_col(k_base))),
        pl.BlockSpec((None, bq, kv_w), main_map(kv_col(k_base))),
        pl.BlockSpec((None, half, kv_w), right_map(kv_col(k_base))),
        pl.BlockSpec((None, half, kv_w), left_map(kv_col(v_base))),
        pl.BlockSpec((None, bq, kv_w), main_map(kv_col(v_base))),
        pl.BlockSpec((None, half, kv_w), right_map(kv_col(v_base))),
    ]
    if has_sink:
        out_specs = pl.BlockSpec((bq, width), lambda i, c: (i, c))
        out_shape = jax.ShapeDtypeStruct((rows, n_inner * width), _BF)
    else:
        out_specs = (pl.BlockSpec((None, bq, width), lambda i, c: (c, i, 0)),
                     pl.BlockSpec((None, bq, LANES), lambda i, c: (c, i, 0)))
        out_shape = (jax.ShapeDtypeStruct((lead, rows, width), _BF),
                     jax.ShapeDtypeStruct((lead, rows, LANES), _F32))
    return pl.pallas_call(
        functools.partial(_band_attn_kernel, bq=bq, half=half, step=step, shared_kv=shared_kv,
                          has_sink=has_sink, prompt_rows=prompt_rows, prompt_seq=prompt_seq,
                          sample_seq=sample_seq),
        grid=(nq, n_inner),
        in_specs=in_specs,
        out_specs=out_specs,
        out_shape=out_shape,
        scratch_shapes=[pltpu.VMEM((bq + 2 * half, kv_w), _BF), pltpu.VMEM((bq + 2 * half, kv_w), _BF)],
        compiler_params=_params("parallel", "arbitrary"),
        name=name,
    )(slopes, sinks, qkv, qkv, qkv, qkv, qkv, qkv, qkv)


def _mix_kernel(xp_ref, xs_ref, oa_ref, ob0_ref, ob1_ref, ob2_ref, l0_ref, l1_ref, l2_ref, g_ref,
                wpa_ref, wpb_ref, wout_ref, n2_ref, wr_ref, br_ref, h_ref, hn_ref, lg_ref, o_scr, l_scr,
                *, n_prompt_tiles, d, tm):
    i = pl.program_id(0)

    def token_order(o_ref, l_ref, r, slot):
        if r == 1:
            return [o_ref[0, :, h * HEAD_DIM:(h + 1) * HEAD_DIM].astype(_F32) for h in range(B_HPG)], l_ref[0]
        for p in range(r):
            for h in range(B_HPG):
                o_scr[slot, h, pl.ds(p, tm // r, stride=r), :] = (
                    o_ref[p, :, h * HEAD_DIM:(h + 1) * HEAD_DIM].astype(_F32))
            l_scr[slot, pl.ds(p, tm // r, stride=r), :] = l_ref[p]
        return [o_scr[slot, h] for h in range(B_HPG)], l_scr[slot]

    o0, l0 = token_order(ob0_ref, l0_ref, DILATED_GROUPS[0][1], 0)
    o1, l1 = token_order(ob1_ref, l1_ref, DILATED_GROUPS[1][1], 0)
    o2, l2 = token_order(ob2_ref, l2_ref, DILATED_GROUPS[2][1], 1)
    mx = jnp.maximum(jnp.maximum(l0, l1), l2)
    e0, e1, e2 = jnp.exp(l0 - mx), jnp.exp(l1 - mx), jnp.exp(l2 - mx)
    den = e0 + e1 + e2
    a0, a1, a2 = e0 / den, e1 / den, e2 / den
    parts = []
    for h in range(B_HPG):
        parts.append(a0[:, h:h + 1] * o0[h] + a1[:, h:h + 1] * o1[h] + a2[:, h:h + 1] * o2[h])
    ob = jnp.concatenate(parts, axis=1).astype(_BF)

    ta = jnp.dot(oa_ref[...], wpa_ref[...], preferred_element_type=_F32)
    tb = jnp.dot(ob, wpb_ref[...], preferred_element_type=_F32)
    merged = g_ref[:, :d].astype(_F32) * ta + g_ref[:, d:].astype(_F32) * tb
    x = jnp.where(i < n_prompt_tiles, xp_ref[...], xs_ref[...])
    h_new = x + jnp.dot(merged.astype(_BF), wout_ref[...], preferred_element_type=_F32)
    h_ref[...] = h_new
    hn = h_new * lax.rsqrt(jnp.mean(h_new * h_new, axis=-1, keepdims=True) + EPS) * n2_ref[...]
    hn_ref[...] = hn
    hn_hi = hn.astype(_BF)
    hn_lo = (hn - hn_hi.astype(_F32)).astype(_BF)
    r = (jnp.dot(hn_hi, wr_ref[...], preferred_element_type=_F32)
         + jnp.dot(hn_lo, wr_ref[...], preferred_element_type=_F32))
    lg_ref[...] = r[:, :LANES] + r[:, LANES:] + br_ref[...]


def _mix(xp, xs, oa, obs, lses, gates, wpa, wpb, wout, n2, wr, br):
    n_p, d = xp.shape
    n_s = xs.shape[0]
    t = n_p + n_s
    tm = ROW_TILE
    npt = n_p // tm
    row = lambda i: (i, 0)
    const = lambda i: (0, 0)
    width = B_HPG * HEAD_DIM

    def resident(shape):
        return pl.BlockSpec(shape, const, pipeline_mode=pl.Buffered(1))

    def phase_blocks(cols):
        return [pl.BlockSpec((r, tm // r, cols), lambda i: (0, i, 0)) for _, r in DILATED_GROUPS]

    in_specs = [
        pl.BlockSpec((tm, d), lambda i: (jnp.minimum(i, npt - 1), 0)),
        pl.BlockSpec((tm, d), lambda i: (jnp.maximum(i - npt, 0), 0)),
        pl.BlockSpec((tm, oa.shape[1]), row),
        *phase_blocks(width), *phase_blocks(LANES),
        pl.BlockSpec((tm, 2 * d), row),
        resident(wpa.shape), resident(wpb.shape), resident(wout.shape),
        resident((1, d)), resident(wr.shape), resident((1, LANES)),
    ]
    return pl.pallas_call(
        functools.partial(_mix_kernel, n_prompt_tiles=npt, d=d, tm=tm),
        grid=(t // tm,),
        in_specs=in_specs,
        out_specs=(pl.BlockSpec((tm, d), row), pl.BlockSpec((tm, d), row), pl.BlockSpec((tm, LANES), row)),
        out_shape=(jax.ShapeDtypeStruct((t, d), _F32), jax.ShapeDtypeStruct((t, d), _F32),
                   jax.ShapeDtypeStruct((t, LANES), _F32)),
        scratch_shapes=[pltpu.VMEM((2, B_HPG, tm, HEAD_DIM), _F32), pltpu.VMEM((2, tm, LANES), _F32)],
        compiler_params=_params("parallel"),
        name="mix_out_router",
    )(xp, xs, oa, *obs, *lses, gates, wpa, wpb, wout, n2, wr, br)


def _route_kernel(lg_ref, ri_ref, rw_ref, cnt_ref, carry_ref, *, tr):
    i = pl.program_id(0)

    @pl.when(i == 0)
    def _():
        carry_ref[...] = jnp.zeros_like(carry_ref)

    lg = lg_ref[...]
    lane_i = lax.broadcasted_iota(jnp.int32, (tr, LANES), 1)
    lane = lane_i.astype(_F32)
    no_lane = float(LANES)
    is_grp = lane_i < N_GROUPS
    glog = jnp.where(is_grp, lg, MASKED)
    gmax = jnp.max(glog, axis=-1, keepdims=True)
    grp = jnp.min(jnp.where(glog == gmax, lane, no_lane), axis=-1, keepdims=True)
    gsum = jnp.sum(jnp.where(is_grp, jnp.exp(glog - gmax), 0.0), axis=-1, keepdims=True)
    pgrp = 1.0 / gsum
    lane_grp = ((lane_i - N_GROUPS) // EXPERTS_PER_GROUP).astype(_F32)
    in_grp = (lane_i >= N_GROUPS) & (lane_i < N_GROUPS + N_EXPERTS) & (lane_grp == grp)
    elog = jnp.where(in_grp, lg, MASKED)
    t1 = jnp.max(elog, axis=-1, keepdims=True)
    i1 = jnp.min(jnp.where(elog == t1, lane, no_lane), axis=-1, keepdims=True)
    elog2 = jnp.where(lane == i1, MASKED, elog)
    t2 = jnp.max(elog2, axis=-1, keepdims=True)
    i2 = jnp.min(jnp.where(elog2 == t2, lane, no_lane), axis=-1, keepdims=True)
    e21 = jnp.exp(t2 - t1)
    w1 = pgrp / (1.0 + e21)
    w2 = pgrp * e21 / (1.0 + e21)
    eid1 = i1 - N_GROUPS
    eid2 = i2 - N_GROUPS
    hot1 = lane == eid1
    hot2 = lane == eid2
    onehot = jnp.where(hot1, 1.0, 0.0) + jnp.where(hot2, 1.0, 0.0)
    r_i = lax.broadcasted_iota(jnp.int32, (tr, tr), 0)
    c_i = lax.broadcasted_iota(jnp.int32, (tr, tr), 1)
    lower = jnp.where(c_i < r_i, 1.0, 0.0).astype(_BF)
    before = jnp.dot(lower, onehot.astype(_BF), preferred_element_type=_F32) + carry_ref[0:1, :]
    rank1 = jnp.sum(jnp.where(hot1, before, 0.0), axis=-1, keepdims=True)
    rank2 = jnp.sum(jnp.where(hot2, before, 0.0), axis=-1, keepdims=True)
    total = carry_ref[0:1, :] + jnp.sum(onehot, axis=0, keepdims=True)
    carry_ref[...] = jnp.broadcast_to(total, carry_ref.shape)
    cnt_ref[...] = jnp.broadcast_to(total, cnt_ref.shape).astype(jnp.int32)
    ri = jnp.where(lane_i == 0, eid1, jnp.where(lane_i == 1, eid2, jnp.where(lane_i == 2, rank1,
                   jnp.where(lane_i == 3, rank2, 0.0))))
    ri_ref[...] = ri.astype(jnp.int32)
    rw_ref[...] = jnp.where(lane_i == 0, w1, jnp.where(lane_i == 1, w2, 0.0))


def _route(logits):
    t = logits.shape[0]
    tr = ROW_TILE
    row = lambda i: (i, 0)
    return pl.pallas_call(
        functools.partial(_route_kernel, tr=tr),
        grid=(t // tr,),
        in_specs=[pl.BlockSpec((tr, LANES), row)],
        out_specs=(pl.BlockSpec((tr, LANES), row), pl.BlockSpec((tr, LANES), row),
                   pl.BlockSpec((8, LANES), lambda i: (0, 0))),
        out_shape=(jax.ShapeDtypeStruct((t, LANES), jnp.int32), jax.ShapeDtypeStruct((t, LANES), _F32),
                   jax.ShapeDtypeStruct((8, LANES), jnp.int32)),
        scratch_shapes=[pltpu.VMEM((8, LANES), _F32)],
        compiler_params=_params("arbitrary"),
        name="route",
    )(logits)


def _dispatch_kernel(dest_ref, hn_ref, xr_in_ref, xr_ref, sem, *, td):
    del xr_in_ref
    i = pl.program_id(0)

    def start(j, carry):
        for k in range(2):
            dst = dest_ref[2 * (i * td + j) + k]
            pltpu.make_async_copy(hn_ref.at[pl.ds(j, 1), :], xr_ref.at[pl.ds(dst, 1), :], sem).start()
        return carry

    lax.fori_loop(0, td, start, 0, unroll=DMA_UNROLL)
    for _ in range(2):
        pltpu.make_async_copy(hn_ref, xr_ref.at[pl.ds(0, td), :], sem).wait()


def _dispatch(dest, hn, xr_init):
    t, d = hn.shape
    td = ROW_TILE
    return pl.pallas_call(
        functools.partial(_dispatch_kernel, td=td),
        grid_spec=pltpu.PrefetchScalarGridSpec(
            num_scalar_prefetch=1,
            grid=(t // td,),
            in_specs=[pl.BlockSpec((td, d), lambda i, dest: (i, 0)), pl.BlockSpec(memory_space=pl.ANY)],
            out_specs=pl.BlockSpec(memory_space=pl.ANY),
            scratch_shapes=[pltpu.SemaphoreType.DMA(())],
        ),
        out_shape=jax.ShapeDtypeStruct(xr_init.shape, xr_init.dtype),
        input_output_aliases={2: 0},
        compiler_params=_params("arbitrary"),
        name="dispatch",
    )(dest, hn, xr_init)


def _expert_kernel(blk_e_ref, nvalid_ref, x_ref, w1_ref, w3_ref, w2_ref, y_ref):
    del blk_e_ref
    b = pl.program_id(0)

    @pl.when(b < nvalid_ref[0])
    def _():
        xb = x_ref[...].astype(_BF)
        a = jnp.dot(xb, w1_ref[0], preferred_element_type=_F32)
        u = jnp.dot(xb, w3_ref[0], preferred_element_type=_F32)
        hmid = (a / (1.0 + jnp.exp(-a))) * u
        y_ref[...] = jnp.dot(hmid.astype(_BF), w2_ref[0], preferred_element_type=_F32)

    @pl.when(b >= nvalid_ref[0])
    def _():
        y_ref[...] = jnp.zeros_like(y_ref)


def _experts(blk_e, nvalid, xr, w1, w3, w2):
    rows, d = xr.shape
    f = w1.shape[2]
    nblk = rows // EXPERT_ROWS
    return pl.pallas_call(
        _expert_kernel,
        grid_spec=pltpu.PrefetchScalarGridSpec(
            num_scalar_prefetch=2,
            grid=(nblk,),
            in_specs=[
                pl.BlockSpec((EXPERT_ROWS, d), lambda b, be, nv: (b, 0)),
                pl.BlockSpec((1, d, f), lambda b, be, nv: (be[b], 0, 0)),
                pl.BlockSpec((1, d, f), lambda b, be, nv: (be[b], 0, 0)),
                pl.BlockSpec((1, f, d), lambda b, be, nv: (be[b], 0, 0)),
            ],
            out_specs=pl.BlockSpec((EXPERT_ROWS, d), lambda b, be, nv: (b, 0)),
        ),
        out_shape=jax.ShapeDtypeStruct((rows, d), _F32),
        compiler_params=_params("arbitrary"),
        name="experts",
    )(blk_e, nvalid, xr, w1, w3, w2)


def _combine_kernel(dest_ref, h_ref, rw_ref, g_ref, yr_ref, op_ref, os_ref, ybuf, sems, *, tc, n_prompt_tiles):
    i = pl.program_id(0)
    slot = i % 2

    def gather(tile, s):
        def start(j, carry):
            for k in range(2):
                src = dest_ref[2 * (tile * tc + j) + k]
                pltpu.make_async_copy(yr_ref.at[pl.ds(src, 1), :], ybuf.at[s, k, pl.ds(j, 1), :],
                                      sems.at[s]).start()
            return carry

        lax.fori_loop(0, tc, start, 0, unroll=DMA_UNROLL)

    @pl.when(i == 0)
    def _():
        gather(0, 0)

    @pl.when(i + 1 < pl.num_programs(0))
    def _():
        gather(i + 1, 1 - slot)

    for k in range(2):
        pltpu.make_async_copy(yr_ref.at[pl.ds(0, tc), :], ybuf.at[slot, k], sems.at[slot]).wait()

    rw = rw_ref[...]
    y = rw[:, 0:1] * ybuf[slot, 0] + rw[:, 1:2] * ybuf[slot, 1]
    z = h_ref[...] + y
    out = z * lax.rsqrt(jnp.mean(z * z, axis=-1, keepdims=True) + EPS) * g_ref[...]

    @pl.when(i < n_prompt_tiles)
    def _():
        op_ref[...] = out

    @pl.when(i >= n_prompt_tiles)
    def _():
        os_ref[...] = out


def _combine(dest, h, rw, g, yr, *, n_p):
    t, d = h.shape
    tc = ROW_TILE
    npt = n_p // tc
    return pl.pallas_call(
        functools.partial(_combine_kernel, tc=tc, n_prompt_tiles=npt),
        grid_spec=pltpu.PrefetchScalarGridSpec(
            num_scalar_prefetch=1,
            grid=(t // tc,),
            in_specs=[
                pl.BlockSpec((tc, d), lambda i, dest: (i, 0)),
                pl.BlockSpec((tc, LANES), lambda i, dest: (i, 0)),
                pl.BlockSpec((1, d), lambda i, dest: (0, 0)),
                pl.BlockSpec(memory_space=pl.ANY),
            ],
            out_specs=(pl.BlockSpec((tc, d), lambda i, dest: (jnp.minimum(i, npt - 1), 0)),
                       pl.BlockSpec((tc, d), lambda i, dest: (jnp.maximum(i - npt, 0), 0))),
            scratch_shapes=[pltpu.VMEM((2, 2, tc, d), _F32), pltpu.SemaphoreType.DMA((2,))],
        ),
        out_shape=(jax.ShapeDtypeStruct((n_p, d), _F32), jax.ShapeDtypeStruct((t - n_p, d), _F32)),
        compiler_params=_params("arbitrary"),
        name="combine_norm",
    )(dest, h, rw, g, yr)


def _layer(xp, xs, prompt_seq, sample_seq, norm1, w_in, sink, w_proj_a, w_proj_b, w_gate, b_gate, w_out,
           norm2, w_rg, b_rg, w_re, b_re, w1, w3, w2, norm_final):
    n_p, d = xp.shape
    n_s = xs.shape[0]
    t = n_p + n_s
    a_q = A_HEADS * HEAD_DIM
    a_cols = a_q + 2 * A_KV * HEAD_DIM
    b_w = B_HEADS * HEAD_DIM
    gw = B_HPG * HEAD_DIM

    def group_cols(gi):
        return [w_in[:, a_cols + s * b_w + gi * gw: a_cols + s * b_w + (gi + 1) * gw] for s in range(3)]

    w_all = jnp.concatenate([w_in[:, :a_cols]] + [c for gi in range(N_DIL) for c in group_cols(gi)] + [w_gate],
                            axis=1).astype(_BF)
    proj_a, q0, q1, q2, gates = _project(xp, xs, norm1.reshape(1, d), w_all, b_gate.reshape(1, 2 * d),
                                         a_cols=a_cols, b_cols=3 * gw, gate_cols=2 * d)

    oa = _band_attention(
        proj_a.reshape(1, t, a_cols), jnp.asarray(_alibi(A_HEADS)), sink.astype(_F32), n_inner=A_KV,
        bq=min(512, prompt_seq, sample_seq), half=A_HALF, step=1, shared_kv=True, has_sink=True,
        prompt_rows=n_p, prompt_seq=prompt_seq, sample_seq=sample_seq, name="attn_window")

    slopes_b = _alibi(B_HEADS)
    obs, lses = [], []
    for gi, ((w, r), qkv) in enumerate(zip(DILATED_GROUPS, (q0, q1, q2))):
        o_g, lse_g = _band_attention(
            qkv, jnp.asarray(slopes_b[gi * B_HPG:(gi + 1) * B_HPG]), jnp.zeros((B_HPG,), _F32), n_inner=r,
            bq=min(256, prompt_seq // r, sample_seq // r), half=w // (2 * r), step=r, shared_kv=False,
            has_sink=False, prompt_rows=n_p // r, prompt_seq=prompt_seq // r, sample_seq=sample_seq // r,
            name=f"attn_dilated_{r}")
        obs.append(o_g)
        lses.append(lse_g)

    n_r = N_GROUPS + N_EXPERTS
    wr = jnp.concatenate([w_rg, jnp.transpose(w_re, (1, 0, 2)).reshape(d, N_EXPERTS)], axis=1)
    wr = jnp.pad(wr, ((0, 0), (0, LANES - n_r)))
    wr_hi = wr.astype(_BF)
    wr_lo = (wr - wr_hi.astype(_F32)).astype(_BF)
    wr2 = jnp.concatenate([wr_hi, wr_lo], axis=1)
    br = jnp.pad(jnp.concatenate([b_rg, b_re.reshape(-1)]), (0, LANES - n_r)).reshape(1, LANES).astype(_F32)

    h, hn, logits = _mix(xp, xs, oa, obs, lses, gates, w_proj_a.astype(_BF), w_proj_b.astype(_BF),
                         w_out.astype(_BF), norm2.reshape(1, d), wr2, br)

    ri, rw, cnt = _route(logits)
    counts = cnt[0, :N_EXPERTS]
    eid = ri[:, 0:2]
    rank = ri[:, 2:4]
    pcounts = (counts + EXPERT_ROWS - 1) // EXPERT_ROWS * EXPERT_ROWS
    pends = jnp.cumsum(pcounts)
    pstarts = pends - pcounts
    expert_ids = jnp.arange(N_EXPERTS, dtype=jnp.int32)
    start_of = jnp.sum(jnp.where(eid[:, :, None] == expert_ids, pstarts.astype(jnp.int32), 0), axis=-1)
    dest = (start_of + rank).reshape(-1).astype(jnp.int32)
    nblk = (2 * t + N_EXPERTS * (EXPERT_ROWS - 1) + EXPERT_ROWS - 1) // EXPERT_ROWS
    blk_start = jnp.arange(nblk, dtype=jnp.int32) * EXPERT_ROWS
    blk_e = jnp.minimum(jnp.sum(pends[None, :] <= blk_start[:, None], axis=1), N_EXPERTS - 1).astype(jnp.int32)
    nvalid = (pends[-1] // EXPERT_ROWS).astype(jnp.int32).reshape(1)

    xr = _dispatch(dest, hn, jnp.zeros((nblk * EXPERT_ROWS, d), _F32))
    yr = _experts(blk_e, nvalid, xr, w1.astype(_BF), w3.astype(_BF), w2.astype(_BF))
    return _combine(dest, h, rw, norm_final.reshape(1, d), yr, n_p=n_p)


def kernel(x_prompt, x_sample, norm1, w_in, attn_sink, w_proj_a, w_proj_b, w_gate, b_gate, w_out, norm2,
           w_router_group, b_router_group, w_router_expert, b_router_expert, w_expert_gate, w_expert_up,
           w_expert_down, norm_final):
    assert norm1.shape[0] == 1, "one layer"
    d = x_prompt.shape[-1]
    xp = x_prompt.reshape(-1, d)
    xs = x_sample.reshape(-1, d)
    yp, ys = _layer(xp, xs, x_prompt.shape[1], x_sample.shape[1], norm1[0], w_in[0], attn_sink[0], w_proj_a[0],
                    w_proj_b[0], w_gate[0], b_gate[0], w_out[0], norm2[0], w_router_group[0],
                    b_router_group[0], w_router_expert[0], b_router_expert[0], w_expert_gate[0],
                    w_expert_up[0], w_expert_down[0], norm_final)
    return yp.reshape(x_prompt.shape), ys.reshape(x_sample.shape)
```

```python
import functools

import jax
import jax.numpy as jnp
import numpy as np
from jax import lax
from jax.experimental import pallas as pl
from jax.experimental.pallas import tpu as pltpu

HEAD_DIM = 128
A_HEADS = 16
A_KV = 4
A_HALF = 128
DILATED_GROUPS = ((128, 1), (512, 4), (2048, 16))
N_DIL = len(DILATED_GROUPS)
B_HPG = 4
B_HEADS = N_DIL * B_HPG
N_GROUPS = 8
EXPERTS_PER_GROUP = 8
N_EXPERTS = N_GROUPS * EXPERTS_PER_GROUP
EPS = 1e-6

LANES = 128
MASKED = -1e30
VMEM_LIMIT = 56 * 1024 * 1024
SUBQ = 128
HEADS_PER_STEP = 4
EXPERT_ROWS = 256
DMA_UNROLL = 8
PROJ_TM, PROJ_TN = 1024, 512
ROW_TILE = 256

_BF = jnp.bfloat16
_F32 = jnp.float32


def _alibi(n):
    return np.power(2.0, -8.0 * (np.arange(n) + 1) / n).astype(np.float32)


def _params(*sem):
    return pltpu.CompilerParams(dimension_semantics=sem, vmem_limit_bytes=VMEM_LIMIT)


def _proj_kernel(xp_ref, xs_ref, g_ref, win_ref, wg_ref, b_ref, oa_ref, q0_ref, q1_ref, q2_ref, gt_ref,
                 xn_ref, acc_ref, *, n_prompt_tiles, part_starts, tm):
    i = pl.program_id(0)
    j = pl.program_id(1)

    def norm_into_scratch(x_ref):
        x = x_ref[...]
        y = x * lax.rsqrt(jnp.mean(x * x, axis=-1, keepdims=True) + EPS)
        xn_ref[...] = (y * g_ref[...]).astype(_BF)

    @pl.when((j == 0) & (i < n_prompt_tiles))
    def _():
        norm_into_scratch(xp_ref)

    @pl.when((j == 0) & (i >= n_prompt_tiles))
    def _():
        norm_into_scratch(xs_ref)

    def tile(w_ref=win_ref):
        return jnp.dot(xn_ref[...], w_ref[...].astype(_BF), preferred_element_type=_F32)

    s0, s1, s2, sg = part_starts

    @pl.when(j < s0)
    def _():
        oa_ref[...] = tile().astype(_BF)

    for (lo, hi, q_ref, (_, r)) in ((s0, s1, q0_ref, DILATED_GROUPS[0]), (s1, s2, q1_ref, DILATED_GROUPS[1]),
                                    (s2, sg, q2_ref, DILATED_GROUPS[2])):
        @pl.when((j >= lo) & (j < hi))
        def _(q_ref=q_ref, r=r):
            acc = tile()
            if r == 1:
                q_ref[0] = acc.astype(_BF)
            else:
                for cb in range(acc.shape[1] // LANES):
                    acc_ref[cb] = acc[:, cb * LANES:(cb + 1) * LANES]
                for p in range(r):
                    for cb in range(acc.shape[1] // LANES):
                        q_ref[p, :, cb * LANES:(cb + 1) * LANES] = (
                            acc_ref[cb, pl.ds(p, tm // r, stride=r), :].astype(_BF))

    @pl.when(j >= sg)
    def _():
        z = tile(wg_ref) + b_ref[...]
        gt_ref[...] = (0.5 * jnp.tanh(0.5 * z) + 0.5).astype(_BF)


def _project(xp, xs, g, w_in, w_gate, b_gate, *, a_cols):
    n_p, d = xp.shape
    n_s = xs.shape[0]
    t = n_p + n_s
    tm, tn = PROJ_TM, PROJ_TN
    assert tn == B_HPG * HEAD_DIM and w_in.shape[1] == a_cols + 3 * N_DIL * tn
    b_cols, gate_cols = 3 * tn, w_gate.shape[1]
    npt = n_p // tm
    na, nb, ng = a_cols // tn, 3, gate_cols // tn
    s0, s1, s2, sg = na, na + nb, na + 2 * nb, na + 3 * nb
    once = pl.Buffered(1)

    def part_map(lo, n):
        return lambda i, j: (0, i, jnp.clip(j - lo, 0, n - 1))

    def w_in_map(i, j):
        jj = jnp.clip(j - na, 0, N_DIL * nb - 1)
        return (0, jnp.where(j < na, j, na + (jj % nb) * N_DIL + jj // nb))

    q_shapes = [jax.ShapeDtypeStruct((r, t // r, b_cols), _BF) for _, r in DILATED_GROUPS]
    q_specs = [pl.BlockSpec((r, tm // r, tn), part_map(lo, nb))
               for lo, (_, r) in zip((s0, s1, s2), DILATED_GROUPS)]
    return pl.pallas_call(
        functools.partial(_proj_kernel, n_prompt_tiles=npt, part_starts=(s0, s1, s2, sg), tm=tm),
        grid=(t // tm, sg + ng),
        in_specs=[
            pl.BlockSpec((tm, d), lambda i, j: (jnp.minimum(i, npt - 1), 0), pipeline_mode=once),
            pl.BlockSpec((tm, d), lambda i, j: (jnp.maximum(i - npt, 0), 0), pipeline_mode=once),
            pl.BlockSpec((1, d), lambda i, j: (0, 0)),
            pl.BlockSpec((d, tn), w_in_map),
            pl.BlockSpec((d, tn), lambda i, j: (0, jnp.clip(j - sg, 0, ng - 1))),
            pl.BlockSpec((1, tn), lambda i, j: (0, jnp.clip(j - sg, 0, ng - 1))),
        ],
        out_specs=[pl.BlockSpec((tm, tn), lambda i, j: (i, jnp.minimum(j, na - 1)))] + q_specs
                  + [pl.BlockSpec((tm, tn), lambda i, j: (i, jnp.clip(j - sg, 0, ng - 1)))],
        out_shape=[jax.ShapeDtypeStruct((t, a_cols), _BF)] + q_shapes
                  + [jax.ShapeDtypeStruct((t, gate_cols), _BF)],
        scratch_shapes=[pltpu.VMEM((tm, d), _BF), pltpu.VMEM((tn // LANES, tm, LANES), _F32)],
        compiler_params=_params("arbitrary", "arbitrary"),
        name="norm_proj",
    )(xp, xs, g, w_in, w_gate, b_gate)


def _band_attn_kernel(slope_ref, sink_ref, q_ref, kl_ref, km_ref, kr_ref, vl_ref, vm_ref, vr_ref,
                      *rest, bq, half, step, shared_kv, has_sink, prompt_rows, prompt_seq, sample_seq):
    if has_sink:
        o_ref, kcat, vcat = rest
        lse_ref = None
    else:
        o_ref, lse_ref, kcat, vcat = rest
    i = pl.program_id(0)
    c = pl.program_id(1)
    win = SUBQ + 2 * half

    kcat[0:half, :] = kl_ref[...]
    kcat[half:half + bq, :] = km_ref[...]
    kcat[half + bq:, :] = kr_ref[...]
    vcat[0:half, :] = vl_ref[...]
    vcat[half:half + bq, :] = vm_ref[...]
    vcat[half + bq:, :] = vr_ref[...]

    u0 = i * bq
    in_prompt = u0 < prompt_rows
    lo = jnp.where(in_prompt, (u0 // prompt_seq) * prompt_seq,
                   prompt_rows + ((u0 - prompt_rows) // sample_seq) * sample_seq)
    hi = lo + jnp.where(in_prompt, prompt_seq, sample_seq)

    qi = lax.broadcasted_iota(jnp.int32, (SUBQ, win), 0)
    kj = lax.broadcasted_iota(jnp.int32, (SUBQ, win), 1)
    rel = kj - half - qi
    absrel = jnp.abs(rel)
    in_band = absrel <= half
    neg_dist = -(absrel * step).astype(_F32)
    scale = HEAD_DIM ** -0.5
    lane = lax.broadcasted_iota(jnp.int32, (SUBQ, LANES), 1)

    heads = range(HEADS_PER_STEP)
    for sb in range(bq // SUBQ):
        kpos = kj + (u0 + sb * SUBQ - half)
        valid = in_band & (kpos >= lo) & (kpos < hi)
        bias = jnp.where(valid, neg_dist, MASKED)
        rows = slice(sb * SUBQ, (sb + 1) * SUBQ)
        if shared_kv:
            k = kcat[sb * SUBQ:sb * SUBQ + win, :]
            v = vcat[sb * SUBQ:sb * SUBQ + win, :]
            q4 = jnp.concatenate([q_ref[rows, h * HEAD_DIM:(h + 1) * HEAD_DIM] for h in heads], axis=0)
            s4 = lax.dot_general(q4, k, (((1,), (1,)), ((), ())), preferred_element_type=_F32)
            ps, ms, ls = [], [], []
            for h in heads:
                s = s4[h * SUBQ:(h + 1) * SUBQ] * scale + slope_ref[c * HEADS_PER_STEP + h] * bias
                m = jnp.max(s, axis=-1, keepdims=True)
                p = jnp.exp(s - m)
                ms.append(m)
                ls.append(jnp.sum(p, axis=-1, keepdims=True))
                ps.append(p.astype(_BF))
            pv4 = jnp.dot(jnp.concatenate(ps, axis=0), v, preferred_element_type=_F32)
            for h in heads:
                l = ls[h] + jnp.exp(sink_ref[c * HEADS_PER_STEP + h] - ms[h])
                o_ref[rows, h * HEAD_DIM:(h + 1) * HEAD_DIM] = (
                    pv4[h * SUBQ:(h + 1) * SUBQ] / l).astype(o_ref.dtype)
            continue
        lse_tile = jnp.zeros((SUBQ, LANES), _F32)
        for h in heads:
            cols = slice(h * HEAD_DIM, (h + 1) * HEAD_DIM)
            k = kcat[sb * SUBQ:sb * SUBQ + win, cols]
            v = vcat[sb * SUBQ:sb * SUBQ + win, cols]
            s = lax.dot_general(q_ref[rows, cols], k, (((1,), (1,)), ((), ())), preferred_element_type=_F32)
            s = s * scale + slope_ref[h] * bias
            m = jnp.max(s, axis=-1, keepdims=True)
            p = jnp.exp(s - m)
            l = jnp.sum(p, axis=-1, keepdims=True)
            pv = jnp.dot(p.astype(_BF), v, preferred_element_type=_F32)
            lse_tile = jnp.where(lane == h, m + jnp.log(l), lse_tile)
            o_ref[rows, cols] = (pv / l).astype(o_ref.dtype)
        lse_ref[rows, :] = lse_tile


def _band_attention(qkv, slopes, sinks, *, n_inner, bq, half, step, shared_kv, has_sink,
                    prompt_rows, prompt_seq, sample_seq, name):
    lead, rows, _ = qkv.shape
    assert prompt_seq % bq == 0 and sample_seq % bq == 0 and bq % SUBQ == 0 and bq % half == 0
    nq = rows // bq
    hb = bq // half
    last_halo = rows // half - 1
    width = HEADS_PER_STEP * HEAD_DIM
    if shared_kv:
        kv_w = HEAD_DIM
        k_base, v_base = width * n_inner // HEAD_DIM, width * n_inner // HEAD_DIM + n_inner
        lead_of = lambda c: 0
        q_col = lambda c: c
        kv_col = lambda base: (lambda c: base + c)
    else:
        kv_w = width
        k_base, v_base = 1, 2
        lead_of = lambda c: c
        q_col = lambda c: 0
        kv_col = lambda base: (lambda c: base)

    def main_map(col):
        return lambda i, c: (lead_of(c), i, col(c))

    def left_map(col):
        return lambda i, c: (lead_of(c), jnp.maximum(i * hb - 1, 0), col(c))

    def right_map(col):
        return lambda i, c: (lead_of(c), jnp.minimum((i + 1) * hb, last_halo), col(c))

    smem = pl.BlockSpec(memory_space=pltpu.SMEM)
    in_specs = [
        smem, smem,
        pl.BlockSpec((None, bq, width), main_map(q_col)),
        pl.BlockSpec((None, half, kv_w), left_map(kv_col(k_base))),
        pl.BlockSpec((None, bq, kv_w), main_map(kv_col(k_base))),
        pl.BlockSpec((None, half, kv_w), right_map(kv_col(k_base))),
        pl.BlockSpec((None, half, kv_w), left_map(kv_col(v_base))),
        pl.BlockSpec((None, bq, kv_w), main_map(kv_col(v_base))),
        pl.BlockSpec((None, half, kv_w), right_map(kv_col(v_base))),
    ]
    if has_sink:
        out_specs = pl.BlockSpec((bq, width), lambda i, c: (i, c))
        out_shape = jax.ShapeDtypeStruct((rows, n_inner * width), _BF)
    else:
        out_specs = (pl.BlockSpec((None, bq, width), lambda i, c: (c, i, 0)),
                     pl.BlockSpec((None, bq, LANES), lambda i, c: (c, i, 0)))
        out_shape = (jax.ShapeDtypeStruct((lead, rows, width), _BF),
                     jax.ShapeDtypeStruct((lead, rows, LANES), _F32))
    return pl.pallas_call(
        functools.partial(_band_attn_kernel, bq=bq, half=half, step=step, shared_kv=shared_kv,
                          has_sink=has_sink, prompt_rows=prompt_rows, prompt_seq=prompt_seq,
                          sample_seq=sample_seq),
        grid=(nq, n_inner),
        in_specs=in_specs,
        out_specs=out_specs,
        out_shape=out_shape,
        scratch_shapes=[pltpu.VMEM((bq + 2 * half, kv_w), _BF), pltpu.VMEM((bq + 2 * half, kv_w), _BF)],
        compiler_params=_params("parallel", "arbitrary"),
        name=name,
    )(slopes, sinks, qkv, qkv, qkv, qkv, qkv, qkv, qkv)


def _mix_kernel(xp_ref, xs_ref, oa_ref, ob0_ref, ob1_ref, ob2_ref, l0_ref, l1_ref, l2_ref, g_ref,
                wpa_ref, wpb_ref, wout_ref, n2_ref, wr_ref, br_ref, h_ref, hn_ref, lg_ref, o_scr, l_scr,
                *, n_prompt_tiles, d, tm):
    i = pl.program_id(0)

    def token_order(o_ref, l_ref, r, slot):
        if r == 1:
            return [o_ref[0, :, h * HEAD_DIM:(h + 1) * HEAD_DIM].astype(_F32) for h in range(B_HPG)], l_ref[0]
        for p in range(r):
            for h in range(B_HPG):
                o_scr[slot, h, pl.ds(p, tm // r, stride=r), :] = (
                    o_ref[p, :, h * HEAD_DIM:(h + 1) * HEAD_DIM].astype(_F32))
            l_scr[slot, pl.ds(p, tm // r, stride=r), :] = l_ref[p]
        return [o_scr[slot, h] for h in range(B_HPG)], l_scr[slot]

    o0, l0 = token_order(ob0_ref, l0_ref, DILATED_GROUPS[0][1], 0)
    o1, l1 = token_order(ob1_ref, l1_ref, DILATED_GROUPS[1][1], 0)
    o2, l2 = token_order(ob2_ref, l2_ref, DILATED_GROUPS[2][1], 1)
    mx = jnp.maximum(jnp.maximum(l0, l1), l2)
    e0, e1, e2 = jnp.exp(l0 - mx), jnp.exp(l1 - mx), jnp.exp(l2 - mx)
    den = e0 + e1 + e2
    a0, a1, a2 = e0 / den, e1 / den, e2 / den
    parts = []
    for h in range(B_HPG):
        parts.append(a0[:, h:h + 1] * o0[h] + a1[:, h:h + 1] * o1[h] + a2[:, h:h + 1] * o2[h])
    ob = jnp.concatenate(parts, axis=1).astype(_BF)

    ta = jnp.dot(oa_ref[...], wpa_ref[...], preferred_element_type=_F32)
    tb = jnp.dot(ob, wpb_ref[...], preferred_element_type=_F32)
    merged = g_ref[:, :d].astype(_F32) * ta + g_ref[:, d:].astype(_F32) * tb
    x = jnp.where(i < n_prompt_tiles, xp_ref[...], xs_ref[...])
    h_new = x + jnp.dot(merged.astype(_BF), wout_ref[...], preferred_element_type=_F32)
    h_ref[...] = h_new
    hn = h_new * lax.rsqrt(jnp.mean(h_new * h_new, axis=-1, keepdims=True) + EPS) * n2_ref[...]
    hn_ref[...] = hn
    hn_hi = hn.astype(_BF)
    hn_lo = (hn - hn_hi.astype(_F32)).astype(_BF)
    r = (jnp.dot(hn_hi, wr_ref[...], preferred_element_type=_F32)
         + jnp.dot(hn_lo, wr_ref[...], preferred_element_type=_F32))
    lg_ref[...] = r[:, :LANES] + r[:, LANES:] + br_ref[...]


def _mix(xp, xs, oa, obs, lses, gates, wpa, wpb, wout, n2, wr, br):
    n_p, d = xp.shape
    n_s = xs.shape[0]
    t = n_p + n_s
    tm = ROW_TILE
    npt = n_p // tm
    row = lambda i: (i, 0)
    const = lambda i: (0, 0)
    width = B_HPG * HEAD_DIM

    def resident(shape):
        return pl.BlockSpec(shape, const, pipeline_mode=pl.Buffered(1))

    def phase_blocks(cols):
        return [pl.BlockSpec((r, tm // r, cols), lambda i: (0, i, 0)) for _, r in DILATED_GROUPS]

    in_specs = [
        pl.BlockSpec((tm, d), lambda i: (jnp.minimum(i, npt - 1), 0)),
        pl.BlockSpec((tm, d), lambda i: (jnp.maximum(i - npt, 0), 0)),
        pl.BlockSpec((tm, oa.shape[1]), row),
        *phase_blocks(width), *phase_blocks(LANES),
        pl.BlockSpec((tm, 2 * d), row),
        resident(wpa.shape), resident(wpb.shape), resident(wout.shape),
        resident((1, d)), resident(wr.shape), resident((1, LANES)),
    ]
    return pl.pallas_call(
        functools.partial(_mix_kernel, n_prompt_tiles=npt, d=d, tm=tm),
        grid=(t // tm,),
        in_specs=in_specs,
        out_specs=(pl.BlockSpec((tm, d), row), pl.BlockSpec((tm, d), row), pl.BlockSpec((tm, LANES), row)),
        out_shape=(jax.ShapeDtypeStruct((t, d), _F32), jax.ShapeDtypeStruct((t, d), _F32),
                   jax.ShapeDtypeStruct((t, LANES), _F32)),
        scratch_shapes=[pltpu.VMEM((2, B_HPG, tm, HEAD_DIM), _F32), pltpu.VMEM((2, tm, LANES), _F32)],
        compiler_params=_params("parallel"),
        name="mix_out_router",
    )(xp, xs, oa, *obs, *lses, gates, wpa, wpb, wout, n2, wr, br)


def _route_kernel(lg_ref, ri_ref, rw_ref, cnt_ref, carry_ref, *, tr):
    i = pl.program_id(0)

    @pl.when(i == 0)
    def _():
        carry_ref[...] = jnp.zeros_like(carry_ref)

    lg = lg_ref[...]
    lane_i = lax.broadcasted_iota(jnp.int32, (tr, LANES), 1)
    lane = lane_i.astype(_F32)
    no_lane = float(LANES)
    is_grp = lane_i < N_GROUPS
    glog = jnp.where(is_grp, lg, MASKED)
    gmax = jnp.max(glog, axis=-1, keepdims=True)
    grp = jnp.min(jnp.where(glog == gmax, lane, no_lane), axis=-1, keepdims=True)
    gsum = jnp.sum(jnp.where(is_grp, jnp.exp(glog - gmax), 0.0), axis=-1, keepdims=True)
    pgrp = 1.0 / gsum
    lane_grp = ((lane_i - N_GROUPS) // EXPERTS_PER_GROUP).astype(_F32)
    in_grp = (lane_i >= N_GROUPS) & (lane_i < N_GROUPS + N_EXPERTS) & (lane_grp == grp)
    elog = jnp.where(in_grp, lg, MASKED)
    t1 = jnp.max(elog, axis=-1, keepdims=True)
    i1 = jnp.min(jnp.where(elog == t1, lane, no_lane), axis=-1, keepdims=True)
    elog2 = jnp.where(lane == i1, MASKED, elog)
    t2 = jnp.max(elog2, axis=-1, keepdims=True)
    i2 = jnp.min(jnp.where(elog2 == t2, lane, no_lane), axis=-1, keepdims=True)
    e21 = jnp.exp(t2 - t1)
    w1 = pgrp / (1.0 + e21)
    w2 = pgrp * e21 / (1.0 + e21)
    eid1 = i1 - N_GROUPS
    eid2 = i2 - N_GROUPS
    hot1 = lane == eid1
    hot2 = lane == eid2
    onehot = jnp.where(hot1, 1.0, 0.0) + jnp.where(hot2, 1.0, 0.0)
    r_i = lax.broadcasted_iota(jnp.int32, (tr, tr), 0)
    c_i = lax.broadcasted_iota(jnp.int32, (tr, tr), 1)
    lower = jnp.where(c_i < r_i, 1.0, 0.0).astype(_BF)
    before = jnp.dot(lower, onehot.astype(_BF), preferred_element_type=_F32) + carry_ref[0:1, :]
    rank1 = jnp.sum(jnp.where(hot1, before, 0.0), axis=-1, keepdims=True)
    rank2 = jnp.sum(jnp.where(hot2, before, 0.0), axis=-1, keepdims=True)
    total = carry_ref[0:1, :] + jnp.sum(onehot, axis=0, keepdims=True)
    carry_ref[...] = jnp.broadcast_to(total, carry_ref.shape)
    cnt_ref[...] = jnp.broadcast_to(total, cnt_ref.shape).astype(jnp.int32)
    ri = jnp.where(lane_i == 0, eid1, jnp.where(lane_i == 1, eid2, jnp.where(lane_i == 2, rank1,
                   jnp.where(lane_i == 3, rank2, 0.0))))
    ri_ref[...] = ri.astype(jnp.int32)
    rw_ref[...] = jnp.where(lane_i == 0, w1, jnp.where(lane_i == 1, w2, 0.0))


def _route(logits):
    t = logits.shape[0]
    tr = ROW_TILE
    row = lambda i: (i, 0)
    return pl.pallas_call(
        functools.partial(_route_kernel, tr=tr),
        grid=(t // tr,),
        in_specs=[pl.BlockSpec((tr, LANES), row)],
        out_specs=(pl.BlockSpec((tr, LANES), row), pl.BlockSpec((tr, LANES), row),
                   pl.BlockSpec((8, LANES), lambda i: (0, 0))),
        out_shape=(jax.ShapeDtypeStruct((t, LANES), jnp.int32), jax.ShapeDtypeStruct((t, LANES), _F32),
                   jax.ShapeDtypeStruct((8, LANES), jnp.int32)),
        scratch_shapes=[pltpu.VMEM((8, LANES), _F32)],
        compiler_params=_params("arbitrary"),
        name="route",
    )(logits)


def _dispatch_kernel(dest_ref, pad_start_ref, pad_n_ref, nvalid_ref, hn_ref, xr_ref, zeros, sem, pad_sem, *, td):
    i = pl.program_id(0)

    @pl.when(i == 0)
    def _():
        zeros[...] = jnp.zeros_like(zeros)

        def pad_row(e, r):
            return pltpu.make_async_copy(zeros.at[pl.ds(0, 1), :], xr_ref.at[pl.ds(pad_start_ref[e] + r, 1), :],
                                         pad_sem)

        def per_expert(act):
            def body(e, carry):
                lax.fori_loop(0, pad_n_ref[e], lambda r, c: (act(pad_row(e, r)), c)[1], 0)
                return carry
            return body

        def tail_block(b):
            row0 = pl.multiple_of(b * EXPERT_ROWS, EXPERT_ROWS)
            return pltpu.make_async_copy(zeros, xr_ref.at[pl.ds(row0, EXPERT_ROWS), :], pad_sem)

        n_blocks = xr_ref.shape[0] // EXPERT_ROWS
        lax.fori_loop(0, N_EXPERTS, per_expert(lambda cp: cp.start()), 0)
        lax.fori_loop(nvalid_ref[0], n_blocks, lambda b, c: (tail_block(b).start(), c)[1], 0)
        lax.fori_loop(0, N_EXPERTS, per_expert(lambda cp: cp.wait()), 0)
        lax.fori_loop(nvalid_ref[0], n_blocks, lambda b, c: (tail_block(b).wait(), c)[1], 0)

    def start(j, carry):
        for k in range(2):
            dst = dest_ref[2 * (i * td + j) + k]
            pltpu.make_async_copy(hn_ref.at[pl.ds(j, 1), :], xr_ref.at[pl.ds(dst, 1), :], sem).start()
        return carry

    lax.fori_loop(0, td, start, 0, unroll=DMA_UNROLL)
    for _ in range(2):
        pltpu.make_async_copy(hn_ref, xr_ref.at[pl.ds(0, td), :], sem).wait()


def _dispatch(dest, pad_start, pad_n, nvalid, hn, rows):
    t, d = hn.shape
    td = ROW_TILE
    return pl.pallas_call(
        functools.partial(_dispatch_kernel, td=td),
        grid_spec=pltpu.PrefetchScalarGridSpec(
            num_scalar_prefetch=4,
            grid=(t // td,),
            in_specs=[pl.BlockSpec((td, d), lambda i, *_: (i, 0))],
            out_specs=pl.BlockSpec(memory_space=pl.ANY),
            scratch_shapes=[pltpu.VMEM((EXPERT_ROWS, d), hn.dtype), pltpu.SemaphoreType.DMA(()),
                            pltpu.SemaphoreType.DMA(())],
        ),
        out_shape=jax.ShapeDtypeStruct((rows, d), hn.dtype),
        compiler_params=_params("arbitrary"),
        name="dispatch",
    )(dest, pad_start, pad_n, nvalid, hn)


def _expert_kernel(blk_e_ref, nvalid_ref, x_ref, w1_ref, w3_ref, w2_ref, y_ref):
    del blk_e_ref
    b = pl.program_id(0)

    @pl.when(b < nvalid_ref[0])
    def _():
        xb = x_ref[...].astype(_BF)
        a = jnp.dot(xb, w1_ref[0].astype(_BF), preferred_element_type=_F32)
        u = jnp.dot(xb, w3_ref[0].astype(_BF), preferred_element_type=_F32)
        hmid = (a / (1.0 + jnp.exp(-a))) * u
        y_ref[...] = jnp.dot(hmid.astype(_BF), w2_ref[0].astype(_BF), preferred_element_type=_F32)

    @pl.when(b >= nvalid_ref[0])
    def _():
        y_ref[...] = jnp.zeros_like(y_ref)


def _experts(blk_e, nvalid, xr, w1, w3, w2):
    rows, d = xr.shape
    f = w1.shape[2]
    nblk = rows // EXPERT_ROWS
    return pl.pallas_call(
        _expert_kernel,
        grid_spec=pltpu.PrefetchScalarGridSpec(
            num_scalar_prefetch=2,
            grid=(nblk,),
            in_specs=[
                pl.BlockSpec((EXPERT_ROWS, d), lambda b, be, nv: (jnp.minimum(b, nv[0] - 1), 0)),
                pl.BlockSpec((1, d, f), lambda b, be, nv: (be[b], 0, 0)),
                pl.BlockSpec((1, d, f), lambda b, be, nv: (be[b], 0, 0)),
                pl.BlockSpec((1, f, d), lambda b, be, nv: (be[b], 0, 0)),
            ],
            out_specs=pl.BlockSpec((EXPERT_ROWS, d), lambda b, be, nv: (b, 0)),
        ),
        out_shape=jax.ShapeDtypeStruct((rows, d), _F32),
        compiler_params=_params("arbitrary"),
        name="experts",
    )(blk_e, nvalid, xr, w1, w3, w2)


def _combine_kernel(dest_ref, h_ref, rw_ref, g_ref, yr_ref, op_ref, os_ref, ybuf, sems, *, tc, n_prompt_tiles):
    i = pl.program_id(0)
    slot = i % 2

    def gather(tile, s):
        def start(j, carry):
            for k in range(2):
                src = dest_ref[2 * (tile * tc + j) + k]
                pltpu.make_async_copy(yr_ref.at[pl.ds(src, 1), :], ybuf.at[s, k, pl.ds(j, 1), :],
                                      sems.at[s]).start()
            return carry

        lax.fori_loop(0, tc, start, 0, unroll=DMA_UNROLL)

    @pl.when(i == 0)
    def _():
        gather(0, 0)

    @pl.when(i + 1 < pl.num_programs(0))
    def _():
        gather(i + 1, 1 - slot)

    for k in range(2):
        pltpu.make_async_copy(yr_ref.at[pl.ds(0, tc), :], ybuf.at[slot, k], sems.at[slot]).wait()

    rw = rw_ref[...]
    y = rw[:, 0:1] * ybuf[slot, 0] + rw[:, 1:2] * ybuf[slot, 1]
    z = h_ref[...] + y
    out = z * lax.rsqrt(jnp.mean(z * z, axis=-1, keepdims=True) + EPS) * g_ref[...]

    @pl.when(i < n_prompt_tiles)
    def _():
        op_ref[...] = out

    @pl.when(i >= n_prompt_tiles)
    def _():
        os_ref[...] = out


def _combine(dest, h, rw, g, yr, *, n_p):
    t, d = h.shape
    tc = ROW_TILE
    npt = n_p // tc
    return pl.pallas_call(
        functools.partial(_combine_kernel, tc=tc, n_prompt_tiles=npt),
        grid_spec=pltpu.PrefetchScalarGridSpec(
            num_scalar_prefetch=1,
            grid=(t // tc,),
            in_specs=[
                pl.BlockSpec((tc, d), lambda i, dest: (i, 0)),
                pl.BlockSpec((tc, LANES), lambda i, dest: (i, 0)),
                pl.BlockSpec((1, d), lambda i, dest: (0, 0)),
                pl.BlockSpec(memory_space=pl.ANY),
            ],
            out_specs=(pl.BlockSpec((tc, d), lambda i, dest: (jnp.minimum(i, npt - 1), 0)),
                       pl.BlockSpec((tc, d), lambda i, dest: (jnp.maximum(i - npt, 0), 0))),
            scratch_shapes=[pltpu.VMEM((2, 2, tc, d), _F32), pltpu.SemaphoreType.DMA((2,))],
        ),
        out_shape=(jax.ShapeDtypeStruct((n_p, d), _F32), jax.ShapeDtypeStruct((t - n_p, d), _F32)),
        compiler_params=_params("arbitrary"),
        name="combine_norm",
    )(dest, h, rw, g, yr)


def _layer(xp, xs, prompt_seq, sample_seq, norm1, w_in, sink, w_proj_a, w_proj_b, w_gate, b_gate, w_out,
           norm2, w_rg, b_rg, w_re, b_re, w1, w3, w2, norm_final):
    n_p, d = xp.shape
    n_s = xs.shape[0]
    t = n_p + n_s
    a_q = A_HEADS * HEAD_DIM
    a_cols = a_q + 2 * A_KV * HEAD_DIM

    proj_a, q0, q1, q2, gates = _project(xp, xs, norm1.reshape(1, d), w_in, w_gate, b_gate.reshape(1, 2 * d),
                                         a_cols=a_cols)

    oa = _band_attention(
        proj_a.reshape(1, t, a_cols), jnp.asarray(_alibi(A_HEADS)), sink.astype(_F32), n_inner=A_KV,
        bq=min(512, prompt_seq, sample_seq), half=A_HALF, step=1, shared_kv=True, has_sink=True,
        prompt_rows=n_p, prompt_seq=prompt_seq, sample_seq=sample_seq, name="attn_window")

    slopes_b = _alibi(B_HEADS)
    obs, lses = [], []
    for gi, ((w, r), qkv) in enumerate(zip(DILATED_GROUPS, (q0, q1, q2))):
        o_g, lse_g = _band_attention(
            qkv, jnp.asarray(slopes_b[gi * B_HPG:(gi + 1) * B_HPG]), jnp.zeros((B_HPG,), _F32), n_inner=r,
            bq=min(256, prompt_seq // r, sample_seq // r), half=w // (2 * r), step=r, shared_kv=False,
            has_sink=False, prompt_rows=n_p // r, prompt_seq=prompt_seq // r, sample_seq=sample_seq // r,
            name=f"attn_dilated_{r}")
        obs.append(o_g)
        lses.append(lse_g)

    n_r = N_GROUPS + N_EXPERTS
    wr = jnp.concatenate([w_rg, jnp.transpose(w_re, (1, 0, 2)).reshape(d, N_EXPERTS)], axis=1)
    wr = jnp.pad(wr, ((0, 0), (0, LANES - n_r)))
    wr_hi = wr.astype(_BF)
    wr_lo = (wr - wr_hi.astype(_F32)).astype(_BF)
    wr2 = jnp.concatenate([wr_hi, wr_lo], axis=1)
    br = jnp.pad(jnp.concatenate([b_rg, b_re.reshape(-1)]), (0, LANES - n_r)).reshape(1, LANES).astype(_F32)

    h, hn, logits = _mix(xp, xs, oa, obs, lses, gates, w_proj_a.astype(_BF), w_proj_b.astype(_BF),
                         w_out.astype(_BF), norm2.reshape(1, d), wr2, br)

    ri, rw, cnt = _route(logits)
    counts = cnt[0, :N_EXPERTS]
    eid = ri[:, 0:2]
    rank = ri[:, 2:4]
    pcounts = (counts + EXPERT_ROWS - 1) // EXPERT_ROWS * EXPERT_ROWS
    pends = jnp.cumsum(pcounts)
    pstarts = pends - pcounts
    expert_ids = jnp.arange(N_EXPERTS, dtype=jnp.int32)
    start_of = jnp.sum(jnp.where(eid[:, :, None] == expert_ids, pstarts.astype(jnp.int32), 0), axis=-1)
    dest = (start_of + rank).reshape(-1).astype(jnp.int32)
    nblk = (2 * t + N_EXPERTS * (EXPERT_ROWS - 1) + EXPERT_ROWS - 1) // EXPERT_ROWS
    nvalid = (pends[-1] // EXPERT_ROWS).astype(jnp.int32)
    blk_start = jnp.minimum(jnp.arange(nblk, dtype=jnp.int32), nvalid - 1) * EXPERT_ROWS
    blk_e = jnp.sum(pends[None, :] <= blk_start[:, None], axis=1).astype(jnp.int32)

    xr = _dispatch(dest, (pstarts + counts).astype(jnp.int32), (pcounts - counts).astype(jnp.int32),
                   nvalid.reshape(1), hn, nblk * EXPERT_ROWS)
    yr = _experts(blk_e, nvalid.reshape(1), xr, w1, w3, w2)
    return _combine(dest, h, rw, norm_final.reshape(1, d), yr, n_p=n_p)


def kernel(x_prompt, x_sample, norm1, w_in, attn_sink, w_proj_a, w_proj_b, w_gate, b_gate, w_out, norm2,
           w_router_group, b_router_group, w_router_expert, b_router_expert, w_expert_gate, w_expert_up,
           w_expert_down, norm_final):
    assert norm1.shape[0] == 1, "one layer"
    d = x_prompt.shape[-1]
    xp = x_prompt.reshape(-1, d)
    xs = x_sample.reshape(-1, d)
    yp, ys = _layer(xp, xs, x_prompt.shape[1], x_sample.shape[1], norm1[0], w_in[0], attn_sink[0], w_proj_a[0],
                    w_proj_b[0], w_gate[0], b_gate[0], w_out[0], norm2[0], w_router_group[0],
                    b_router_group[0], w_router_expert[0], b_router_expert[0], w_expert_gate[0],
                    w_expert_up[0], w_expert_down[0], norm_final)
    return yp.reshape(x_prompt.shape), ys.reshape(x_sample.shape)
```

```python
import functools

import jax
import jax.numpy as jnp
import numpy as np
from jax import lax
from jax.experimental import pallas as pl
from jax.experimental.pallas import tpu as pltpu

HEAD_DIM = 128
A_HEADS = 16
A_KV = 4
A_HALF = 128
DILATED_GROUPS = ((128, 1), (512, 4), (2048, 16))
N_DIL = len(DILATED_GROUPS)
B_HPG = 4
B_HEADS = N_DIL * B_HPG
N_GROUPS = 8
EXPERTS_PER_GROUP = 8
N_EXPERTS = N_GROUPS * EXPERTS_PER_GROUP
EPS = 1e-6

LANES = 128
SUBLANES = 8
MASKED = -1e30
VMEM_LIMIT = 56 * 1024 * 1024
SUBQ = 128
HEADS_PER_STEP = 4
EXPERT_ROWS = 256
DMA_UNROLL = 8
PROJ_TM, PROJ_TN = 1024, 512
ROW_TILE = 256

_BF = jnp.bfloat16
_F32 = jnp.float32


def _alibi(n):
    return np.power(2.0, -8.0 * (np.arange(n) + 1) / n).astype(np.float32)


def _params(*sem):
    return pltpu.CompilerParams(dimension_semantics=sem, vmem_limit_bytes=VMEM_LIMIT)


def _proj_kernel(xp_ref, xs_ref, g_ref, win_ref, wg_ref, b_ref, oa_ref, q0_ref, q1_ref, q2_ref, gt_ref,
                 xn_ref, acc_ref, *, n_prompt_tiles, part_starts, tm):
    i = pl.program_id(0)
    j = pl.program_id(1)

    def norm_into_scratch(x_ref):
        x = x_ref[...]
        y = x * lax.rsqrt(jnp.mean(x * x, axis=-1, keepdims=True) + EPS)
        xn_ref[...] = (y * g_ref[...]).astype(_BF)

    @pl.when((j == 0) & (i < n_prompt_tiles))
    def _():
        norm_into_scratch(xp_ref)

    @pl.when((j == 0) & (i >= n_prompt_tiles))
    def _():
        norm_into_scratch(xs_ref)

    def tile(w_ref=win_ref):
        return jnp.dot(xn_ref[...], w_ref[...].astype(_BF), preferred_element_type=_F32)

    s0, s1, s2, sg = part_starts

    @pl.when(j < s0)
    def _():
        oa_ref[...] = tile().astype(_BF)

    for (lo, hi, q_ref, (_, r)) in ((s0, s1, q0_ref, DILATED_GROUPS[0]), (s1, s2, q1_ref, DILATED_GROUPS[1]),
                                    (s2, sg, q2_ref, DILATED_GROUPS[2])):
        @pl.when((j >= lo) & (j < hi))
        def _(q_ref=q_ref, r=r):
            acc = tile()
            if r == 1:
                q_ref[0] = acc.astype(_BF)
            else:
                for cb in range(acc.shape[1] // LANES):
                    acc_ref[cb] = acc[:, cb * LANES:(cb + 1) * LANES]
                for p in range(r):
                    for cb in range(acc.shape[1] // LANES):
                        q_ref[p, :, cb * LANES:(cb + 1) * LANES] = (
                            acc_ref[cb, pl.ds(p, tm // r, stride=r), :].astype(_BF))

    @pl.when(j >= sg)
    def _():
        z = tile(wg_ref) + b_ref[...]
        gt_ref[...] = (0.5 * jnp.tanh(0.5 * z) + 0.5).astype(_BF)


def _project(xp, xs, g, w_in, w_gate, b_gate, *, a_cols):
    n_p, d = xp.shape
    n_s = xs.shape[0]
    t = n_p + n_s
    tm, tn = PROJ_TM, PROJ_TN
    assert tn == B_HPG * HEAD_DIM and w_in.shape[1] == a_cols + 3 * N_DIL * tn
    b_cols, gate_cols = 3 * tn, w_gate.shape[1]
    npt = n_p // tm
    na, nb, ng = a_cols // tn, 3, gate_cols // tn
    s0, s1, s2, sg = na, na + nb, na + 2 * nb, na + 3 * nb
    once = pl.Buffered(1)

    def part_map(lo, n):
        return lambda i, j: (0, i, jnp.clip(j - lo, 0, n - 1))

    def w_in_map(i, j):
        jj = jnp.clip(j - na, 0, N_DIL * nb - 1)
        return (0, jnp.where(j < na, j, na + (jj % nb) * N_DIL + jj // nb))

    q_shapes = [jax.ShapeDtypeStruct((r, t // r, b_cols), _BF) for _, r in DILATED_GROUPS]
    q_specs = [pl.BlockSpec((r, tm // r, tn), part_map(lo, nb))
               for lo, (_, r) in zip((s0, s1, s2), DILATED_GROUPS)]
    return pl.pallas_call(
        functools.partial(_proj_kernel, n_prompt_tiles=npt, part_starts=(s0, s1, s2, sg), tm=tm),
        grid=(t // tm, sg + ng),
        in_specs=[
            pl.BlockSpec((tm, d), lambda i, j: (jnp.minimum(i, npt - 1), 0), pipeline_mode=once),
            pl.BlockSpec((tm, d), lambda i, j: (jnp.maximum(i - npt, 0), 0), pipeline_mode=once),
            pl.BlockSpec((1, d), lambda i, j: (0, 0)),
            pl.BlockSpec((d, tn), w_in_map),
            pl.BlockSpec((d, tn), lambda i, j: (0, jnp.clip(j - sg, 0, ng - 1))),
            pl.BlockSpec((1, tn), lambda i, j: (0, jnp.clip(j - sg, 0, ng - 1))),
        ],
        out_specs=[pl.BlockSpec((tm, tn), lambda i, j: (i, jnp.minimum(j, na - 1)))] + q_specs
                  + [pl.BlockSpec((tm, tn), lambda i, j: (i, jnp.clip(j - sg, 0, ng - 1)))],
        out_shape=[jax.ShapeDtypeStruct((t, a_cols), _BF)] + q_shapes
                  + [jax.ShapeDtypeStruct((t, gate_cols), _BF)],
        scratch_shapes=[pltpu.VMEM((tm, d), _BF), pltpu.VMEM((tn // LANES, tm, LANES), _F32)],
        compiler_params=_params("arbitrary", "arbitrary"),
        name="norm_proj",
    )(xp, xs, g, w_in, w_gate, b_gate)


def _band_attn_kernel(slope_ref, sink_ref, q_ref, kl_ref, km_ref, kr_ref, vl_ref, vm_ref, vr_ref,
                      *rest, bq, half, step, shared_kv, has_sink, prompt_rows, prompt_seq, sample_seq):
    if has_sink:
        o_ref, kcat, vcat = rest
        lse_ref = None
    else:
        o_ref, lse_ref, kcat, vcat = rest
    i = pl.program_id(0)
    c = pl.program_id(1)
    win = SUBQ + 2 * half

    kcat[0:half, :] = kl_ref[...]
    kcat[half:half + bq, :] = km_ref[...]
    kcat[half + bq:, :] = kr_ref[...]
    vcat[0:half, :] = vl_ref[...]
    vcat[half:half + bq, :] = vm_ref[...]
    vcat[half + bq:, :] = vr_ref[...]

    u0 = i * bq
    in_prompt = u0 < prompt_rows
    lo = jnp.where(in_prompt, (u0 // prompt_seq) * prompt_seq,
                   prompt_rows + ((u0 - prompt_rows) // sample_seq) * sample_seq)
    hi = lo + jnp.where(in_prompt, prompt_seq, sample_seq)

    qi = lax.broadcasted_iota(jnp.int32, (SUBQ, win), 0)
    kj = lax.broadcasted_iota(jnp.int32, (SUBQ, win), 1)
    rel = kj - half - qi
    absrel = jnp.abs(rel)
    in_band = absrel <= half
    neg_dist = -(absrel * step).astype(_F32)
    scale = HEAD_DIM ** -0.5
    lane = lax.broadcasted_iota(jnp.int32, (SUBQ, LANES), 1)

    heads = range(HEADS_PER_STEP)
    for sb in range(bq // SUBQ):
        kpos = kj + (u0 + sb * SUBQ - half)
        valid = in_band & (kpos >= lo) & (kpos < hi)
        bias = jnp.where(valid, neg_dist, MASKED)
        rows = slice(sb * SUBQ, (sb + 1) * SUBQ)
        if shared_kv:
            k = kcat[sb * SUBQ:sb * SUBQ + win, :]
            v = vcat[sb * SUBQ:sb * SUBQ + win, :]
            q4 = jnp.concatenate([q_ref[rows, h * HEAD_DIM:(h + 1) * HEAD_DIM] for h in heads], axis=0)
            s4 = lax.dot_general(q4, k, (((1,), (1,)), ((), ())), preferred_element_type=_F32)
            ps, ms, ls = [], [], []
            for h in heads:
                s = s4[h * SUBQ:(h + 1) * SUBQ] * scale + slope_ref[c * HEADS_PER_STEP + h] * bias
                m = jnp.max(s, axis=-1, keepdims=True)
                p = jnp.exp(s - m)
                ms.append(m)
                ls.append(jnp.sum(p, axis=-1, keepdims=True))
                ps.append(p.astype(_BF))
            pv4 = jnp.dot(jnp.concatenate(ps, axis=0), v, preferred_element_type=_F32)
            for h in heads:
                l = ls[h] + jnp.exp(sink_ref[c * HEADS_PER_STEP + h] - ms[h])
                o_ref[rows, h * HEAD_DIM:(h + 1) * HEAD_DIM] = (
                    pv4[h * SUBQ:(h + 1) * SUBQ] / l).astype(o_ref.dtype)
            continue
        lse_tile = jnp.zeros((SUBQ, LANES), _F32)
        for h in heads:
            cols = slice(h * HEAD_DIM, (h + 1) * HEAD_DIM)
            k = kcat[sb * SUBQ:sb * SUBQ + win, cols]
            v = vcat[sb * SUBQ:sb * SUBQ + win, cols]
            s = lax.dot_general(q_ref[rows, cols], k, (((1,), (1,)), ((), ())), preferred_element_type=_F32)
            s = s * scale + slope_ref[h] * bias
            m = jnp.max(s, axis=-1, keepdims=True)
            p = jnp.exp(s - m)
            l = jnp.sum(p, axis=-1, keepdims=True)
            pv = jnp.dot(p.astype(_BF), v, preferred_element_type=_F32)
            lse_tile = jnp.where(lane == h, m + jnp.log(l), lse_tile)
            o_ref[rows, cols] = (pv / l).astype(o_ref.dtype)
        lse_ref[rows, :] = lse_tile


def _band_attention(qkv, slopes, sinks, *, n_inner, bq, half, step, shared_kv, has_sink,
                    prompt_rows, prompt_seq, sample_seq, name):
    lead, rows, _ = qkv.shape
    assert prompt_seq % bq == 0 and sample_seq % bq == 0 and bq % SUBQ == 0 and bq % half == 0
    nq = rows // bq
    hb = bq // half
    last_halo = rows // half - 1
    width = HEADS_PER_STEP * HEAD_DIM
    if shared_kv:
        kv_w = HEAD_DIM
        k_base, v_base = width * n_inner // HEAD_DIM, width * n_inner // HEAD_DIM + n_inner
        lead_of = lambda c: 0
        q_col = lambda c: c
        kv_col = lambda base: (lambda c: base + c)
    else:
        kv_w = width
        k_base, v_base = 1, 2
        lead_of = lambda c: c
        q_col = lambda c: 0
        kv_col = lambda base: (lambda c: base)

    def main_map(col):
        return lambda i, c: (lead_of(c), i, col(c))

    def left_map(col):
        return lambda i, c: (lead_of(c), jnp.maximum(i * hb - 1, 0), col(c))

    def right_map(col):
        return lambda i, c: (lead_of(c), jnp.minimum((i + 1) * hb, last_halo), col(c))

    smem = pl.BlockSpec(memory_space=pltpu.SMEM)
    in_specs = [
        smem, smem,
        pl.BlockSpec((None, bq, width), main_map(q_col)),
        pl.BlockSpec((None, half, kv_w), left_map(kv_col(k_base))),
        pl.BlockSpec((None, bq, kv_w), main_map(kv_col(k_base))),
        pl.BlockSpec((None, half, kv_w), right_map(kv_col(k_base))),
        pl.BlockSpec((None, half, kv_w), left_map(kv_col(v_base))),
        pl.BlockSpec((None, bq, kv_w), main_map(kv_col(v_base))),
        pl.BlockSpec((None, half, kv_w), right_map(kv_col(v_base))),
    ]
    if has_sink:
        out_specs = pl.BlockSpec((bq, width), lambda i, c: (i, c))
        out_shape = jax.ShapeDtypeStruct((rows, n_inner * width), _BF)
    else:
        out_specs = (pl.BlockSpec((None, bq, width), lambda i, c: (c, i, 0)),
                     pl.BlockSpec((None, bq, LANES), lambda i, c: (c, i, 0)))
        out_shape = (jax.ShapeDtypeStruct((lead, rows, width), _BF),
                     jax.ShapeDtypeStruct((lead, rows, LANES), _F32))
    return pl.pallas_call(
        functools.partial(_band_attn_kernel, bq=bq, half=half, step=step, shared_kv=shared_kv,
                          has_sink=has_sink, prompt_rows=prompt_rows, prompt_seq=prompt_seq,
                          sample_seq=sample_seq),
        grid=(nq, n_inner),
        in_specs=in_specs,
        out_specs=out_specs,
        out_shape=out_shape,
        scratch_shapes=[pltpu.VMEM((bq + 2 * half, kv_w), _BF), pltpu.VMEM((bq + 2 * half, kv_w), _BF)],
        compiler_params=_params("parallel", "arbitrary"),
        name=name,
    )(slopes, sinks, qkv, qkv, qkv, qkv, qkv, qkv, qkv)


def _mix_kernel(xp_ref, xs_ref, oa_ref, ob0_ref, ob1_ref, ob2_ref, l0_ref, l1_ref, l2_ref, g_ref,
                wpa_ref, wpb_ref, wout_ref, n2_ref, wr_ref, br_ref, h_ref, hn_ref, lg_ref, o_scr, l_scr,
                *, n_prompt_tiles, d, tm):
    i = pl.program_id(0)

    def token_order(o_ref, l_ref, r, slot):
        if r == 1:
            return [o_ref[0, :, h * HEAD_DIM:(h + 1) * HEAD_DIM].astype(_F32) for h in range(B_HPG)], l_ref[0]
        for p in range(r):
            for h in range(B_HPG):
                o_scr[slot, h, pl.ds(p, tm // r, stride=r), :] = (
                    o_ref[p, :, h * HEAD_DIM:(h + 1) * HEAD_DIM].astype(_F32))
            l_scr[slot, pl.ds(p, tm // r, stride=r), :] = l_ref[p]
        return [o_scr[slot, h] for h in range(B_HPG)], l_scr[slot]

    o0, l0 = token_order(ob0_ref, l0_ref, DILATED_GROUPS[0][1], 0)
    o1, l1 = token_order(ob1_ref, l1_ref, DILATED_GROUPS[1][1], 0)
    o2, l2 = token_order(ob2_ref, l2_ref, DILATED_GROUPS[2][1], 1)
    mx = jnp.maximum(jnp.maximum(l0, l1), l2)
    e0, e1, e2 = jnp.exp(l0 - mx), jnp.exp(l1 - mx), jnp.exp(l2 - mx)
    den = e0 + e1 + e2
    a0, a1, a2 = e0 / den, e1 / den, e2 / den
    parts = []
    for h in range(B_HPG):
        parts.append(a0[:, h:h + 1] * o0[h] + a1[:, h:h + 1] * o1[h] + a2[:, h:h + 1] * o2[h])
    ob = jnp.concatenate(parts, axis=1).astype(_BF)

    ta = jnp.dot(oa_ref[...], wpa_ref[...], preferred_element_type=_F32)
    tb = jnp.dot(ob, wpb_ref[...], preferred_element_type=_F32)
    merged = g_ref[:, :d].astype(_F32) * ta + g_ref[:, d:].astype(_F32) * tb
    x = jnp.where(i < n_prompt_tiles, xp_ref[...], xs_ref[...])
    h_new = x + jnp.dot(merged.astype(_BF), wout_ref[...], preferred_element_type=_F32)
    h_ref[...] = h_new
    hn = h_new * lax.rsqrt(jnp.mean(h_new * h_new, axis=-1, keepdims=True) + EPS) * n2_ref[...]
    hn_ref[...] = hn
    hn_hi = hn.astype(_BF)
    hn_lo = (hn - hn_hi.astype(_F32)).astype(_BF)
    r = (jnp.dot(hn_hi, wr_ref[...], preferred_element_type=_F32)
         + jnp.dot(hn_lo, wr_ref[...], preferred_element_type=_F32))
    lg_ref[...] = r[:, :LANES] + r[:, LANES:] + br_ref[...]


def _mix(xp, xs, oa, obs, lses, gates, wpa, wpb, wout, n2, wr, br):
    n_p, d = xp.shape
    n_s = xs.shape[0]
    t = n_p + n_s
    tm = ROW_TILE
    npt = n_p // tm
    row = lambda i: (i, 0)
    const = lambda i: (0, 0)
    width = B_HPG * HEAD_DIM

    def resident(shape):
        return pl.BlockSpec(shape, const, pipeline_mode=pl.Buffered(1))

    def phase_blocks(cols):
        return [pl.BlockSpec((r, tm // r, cols), lambda i: (0, i, 0)) for _, r in DILATED_GROUPS]

    in_specs = [
        pl.BlockSpec((tm, d), lambda i: (jnp.minimum(i, npt - 1), 0)),
        pl.BlockSpec((tm, d), lambda i: (jnp.maximum(i - npt, 0), 0)),
        pl.BlockSpec((tm, oa.shape[1]), row),
        *phase_blocks(width), *phase_blocks(LANES),
        pl.BlockSpec((tm, 2 * d), row),
        resident(wpa.shape), resident(wpb.shape), resident(wout.shape),
        resident((1, d)), resident(wr.shape), resident((1, LANES)),
    ]
    return pl.pallas_call(
        functools.partial(_mix_kernel, n_prompt_tiles=npt, d=d, tm=tm),
        grid=(t // tm,),
        in_specs=in_specs,
        out_specs=(pl.BlockSpec((tm, d), row), pl.BlockSpec((tm, d), row), pl.BlockSpec((tm, LANES), row)),
        out_shape=(jax.ShapeDtypeStruct((t, d), _F32), jax.ShapeDtypeStruct((t, d), _F32),
                   jax.ShapeDtypeStruct((t, LANES), _F32)),
        scratch_shapes=[pltpu.VMEM((2, B_HPG, tm, HEAD_DIM), _F32), pltpu.VMEM((2, tm, LANES), _F32)],
        compiler_params=_params("parallel"),
        name="mix_out_router",
    )(xp, xs, oa, *obs, *lses, gates, wpa, wpb, wout, n2, wr, br)


def _route_kernel(lg_ref, ri_ref, rw_ref, cnt_ref, carry_ref, *, tr):
    i = pl.program_id(0)

    @pl.when(i == 0)
    def _():
        carry_ref[...] = jnp.zeros_like(carry_ref)

    lg = lg_ref[...]
    lane_i = lax.broadcasted_iota(jnp.int32, (tr, LANES), 1)
    lane = lane_i.astype(_F32)
    no_lane = float(LANES)
    is_grp = lane_i < N_GROUPS
    glog = jnp.where(is_grp, lg, MASKED)
    gmax = jnp.max(glog, axis=-1, keepdims=True)
    grp = jnp.min(jnp.where(glog == gmax, lane, no_lane), axis=-1, keepdims=True)
    gsum = jnp.sum(jnp.where(is_grp, jnp.exp(glog - gmax), 0.0), axis=-1, keepdims=True)
    pgrp = 1.0 / gsum
    lane_grp = ((lane_i - N_GROUPS) // EXPERTS_PER_GROUP).astype(_F32)
    in_grp = (lane_i >= N_GROUPS) & (lane_i < N_GROUPS + N_EXPERTS) & (lane_grp == grp)
    elog = jnp.where(in_grp, lg, MASKED)
    t1 = jnp.max(elog, axis=-1, keepdims=True)
    i1 = jnp.min(jnp.where(elog == t1, lane, no_lane), axis=-1, keepdims=True)
    elog2 = jnp.where(lane == i1, MASKED, elog)
    t2 = jnp.max(elog2, axis=-1, keepdims=True)
    i2 = jnp.min(jnp.where(elog2 == t2, lane, no_lane), axis=-1, keepdims=True)
    e21 = jnp.exp(t2 - t1)
    w1 = pgrp / (1.0 + e21)
    w2 = pgrp * e21 / (1.0 + e21)
    eid1 = i1 - N_GROUPS
    eid2 = i2 - N_GROUPS
    hot1 = lane == eid1
    hot2 = lane == eid2
    onehot = jnp.where(hot1, 1.0, 0.0) + jnp.where(hot2, 1.0, 0.0)
    r_i = lax.broadcasted_iota(jnp.int32, (tr, tr), 0)
    c_i = lax.broadcasted_iota(jnp.int32, (tr, tr), 1)
    lower = jnp.where(c_i < r_i, 1.0, 0.0).astype(_BF)
    before = jnp.dot(lower, onehot.astype(_BF), preferred_element_type=_F32) + carry_ref[0:1, :]
    rank1 = jnp.sum(jnp.where(hot1, before, 0.0), axis=-1, keepdims=True)
    rank2 = jnp.sum(jnp.where(hot2, before, 0.0), axis=-1, keepdims=True)
    total = carry_ref[0:1, :] + jnp.sum(onehot, axis=0, keepdims=True)
    carry_ref[...] = jnp.broadcast_to(total, carry_ref.shape)
    cnt_ref[...] = jnp.broadcast_to(total, cnt_ref.shape).astype(jnp.int32)
    ri = jnp.where(lane_i == 0, eid1, jnp.where(lane_i == 1, eid2, jnp.where(lane_i == 2, rank1,
                   jnp.where(lane_i == 3, rank2, 0.0))))
    ri_ref[...] = ri.astype(jnp.int32)
    rw_ref[...] = jnp.where(lane_i == 0, w1, jnp.where(lane_i == 1, w2, 0.0))


def _route(logits):
    t = logits.shape[0]
    tr = ROW_TILE
    row = lambda i: (i, 0)
    return pl.pallas_call(
        functools.partial(_route_kernel, tr=tr),
        grid=(t // tr,),
        in_specs=[pl.BlockSpec((tr, LANES), row)],
        out_specs=(pl.BlockSpec((tr, LANES), row), pl.BlockSpec((tr, LANES), row),
                   pl.BlockSpec((8, LANES), lambda i: (0, 0))),
        out_shape=(jax.ShapeDtypeStruct((t, LANES), jnp.int32), jax.ShapeDtypeStruct((t, LANES), _F32),
                   jax.ShapeDtypeStruct((8, LANES), jnp.int32)),
        scratch_shapes=[pltpu.VMEM((8, LANES), _F32)],
        compiler_params=_params("arbitrary"),
        name="route",
    )(logits)


def _dispatch_kernel(dest_ref, pad_start_ref, pad_n_ref, nvalid_ref, hn_ref, xr_ref, zeros, sem, pad_sem, *, td):
    i = pl.program_id(0)

    @pl.when(i == 0)
    def _():
        zeros[...] = jnp.zeros_like(zeros)

        def pad_chunk(e, r):
            row0 = pl.multiple_of(pad_start_ref[e] + r * SUBLANES, SUBLANES)
            return pltpu.make_async_copy(zeros.at[pl.ds(0, SUBLANES), :], xr_ref.at[pl.ds(row0, SUBLANES), :],
                                         pad_sem)

        def per_expert(act):
            def body(e, carry):
                lax.fori_loop(0, pad_n_ref[e], lambda r, c: (act(pad_chunk(e, r)), c)[1], 0)
                return carry
            return body

        def tail_block(b):
            row0 = pl.multiple_of(b * EXPERT_ROWS, EXPERT_ROWS)
            return pltpu.make_async_copy(zeros, xr_ref.at[pl.ds(row0, EXPERT_ROWS), :], pad_sem)

        n_blocks = xr_ref.shape[0] // EXPERT_ROWS
        lax.fori_loop(0, N_EXPERTS, per_expert(lambda cp: cp.start()), 0)
        lax.fori_loop(nvalid_ref[0], n_blocks, lambda b, c: (tail_block(b).start(), c)[1], 0)
        lax.fori_loop(0, N_EXPERTS, per_expert(lambda cp: cp.wait()), 0)
        lax.fori_loop(nvalid_ref[0], n_blocks, lambda b, c: (tail_block(b).wait(), c)[1], 0)

    def start(j, carry):
        for k in range(2):
            dst = dest_ref[2 * (i * td + j) + k]
            pltpu.make_async_copy(hn_ref.at[pl.ds(j, 1), :], xr_ref.at[pl.ds(dst, 1), :], sem).start()
        return carry

    lax.fori_loop(0, td, start, 0, unroll=DMA_UNROLL)
    for _ in range(2):
        pltpu.make_async_copy(hn_ref, xr_ref.at[pl.ds(0, td), :], sem).wait()


def _dispatch(dest, pad_start, pad_n, nvalid, hn, rows):
    t, d = hn.shape
    td = ROW_TILE
    return pl.pallas_call(
        functools.partial(_dispatch_kernel, td=td),
        grid_spec=pltpu.PrefetchScalarGridSpec(
            num_scalar_prefetch=4,
            grid=(t // td,),
            in_specs=[pl.BlockSpec((td, d), lambda i, *_: (i, 0))],
            out_specs=pl.BlockSpec(memory_space=pl.ANY),
            scratch_shapes=[pltpu.VMEM((EXPERT_ROWS, d), hn.dtype), pltpu.SemaphoreType.DMA(()),
                            pltpu.SemaphoreType.DMA(())],
        ),
        out_shape=jax.ShapeDtypeStruct((rows, d), hn.dtype),
        compiler_params=_params("arbitrary"),
        name="dispatch",
    )(dest, pad_start, pad_n, nvalid, hn)


def _expert_kernel(blk_e_ref, nvalid_ref, first_ref, slot_ref, next_e_ref, x_ref, w1_hbm, w3_hbm, w2_hbm, y_ref,
                   w1_buf, w3_buf, w2_buf, sems):
    b = pl.program_id(0)
    valid = b < nvalid_ref[0]

    def fetch(e, slot):
        return (pltpu.make_async_copy(w1_hbm.at[e], w1_buf.at[slot], sems.at[slot, 0]),
                pltpu.make_async_copy(w3_hbm.at[e], w3_buf.at[slot], sems.at[slot, 1]),
                pltpu.make_async_copy(w2_hbm.at[e], w2_buf.at[slot], sems.at[slot, 2]))

    @pl.when(b == 0)
    def _():
        for cp in fetch(blk_e_ref[0], 0):
            cp.start()

    @pl.when(valid & (first_ref[b] == 1))
    def _():
        for cp in fetch(blk_e_ref[b], slot_ref[b]):
            cp.wait()

        @pl.when(next_e_ref[b] >= 0)
        def _():
            for cp in fetch(next_e_ref[b], 1 - slot_ref[b]):
                cp.start()

    @pl.when(valid)
    def _():
        slot = slot_ref[b]
        xb = x_ref[...].astype(_BF)
        a = jnp.dot(xb, w1_buf[slot].astype(_BF), preferred_element_type=_F32)
        u = jnp.dot(xb, w3_buf[slot].astype(_BF), preferred_element_type=_F32)
        hmid = (a / (1.0 + jnp.exp(-a))) * u
        y_ref[...] = jnp.dot(hmid.astype(_BF), w2_buf[slot].astype(_BF), preferred_element_type=_F32)

    @pl.when(jnp.logical_not(valid))
    def _():
        y_ref[...] = jnp.zeros_like(y_ref)


def _experts(blk_e, nvalid, first, slot, next_e, xr, w1, w3, w2):
    rows, d = xr.shape
    f = w1.shape[2]
    nblk = rows // EXPERT_ROWS
    any_space = pl.BlockSpec(memory_space=pl.ANY)
    return pl.pallas_call(
        _expert_kernel,
        grid_spec=pltpu.PrefetchScalarGridSpec(
            num_scalar_prefetch=5,
            grid=(nblk,),
            in_specs=[pl.BlockSpec((EXPERT_ROWS, d), lambda b, be, nv, *_: (jnp.minimum(b, nv[0] - 1), 0)),
                      any_space, any_space, any_space],
            out_specs=pl.BlockSpec((EXPERT_ROWS, d), lambda b, *_: (b, 0)),
            scratch_shapes=[pltpu.VMEM((2, d, f), w1.dtype), pltpu.VMEM((2, d, f), w3.dtype),
                            pltpu.VMEM((2, f, d), w2.dtype), pltpu.SemaphoreType.DMA((2, 3))],
        ),
        out_shape=jax.ShapeDtypeStruct((rows, d), _F32),
        compiler_params=_params("arbitrary"),
        name="experts",
    )(blk_e, nvalid, first, slot, next_e, xr, w1, w3, w2)


def _combine_kernel(dest_ref, h_ref, rw_ref, g_ref, yr_ref, op_ref, os_ref, ybuf, sems, *, tc, n_prompt_tiles):
    i = pl.program_id(0)
    slot = i % 2

    def gather(tile, s):
        def start(j, carry):
            for k in range(2):
                src = dest_ref[2 * (tile * tc + j) + k]
                pltpu.make_async_copy(yr_ref.at[pl.ds(src, 1), :], ybuf.at[s, k, pl.ds(j, 1), :],
                                      sems.at[s]).start()
            return carry

        lax.fori_loop(0, tc, start, 0, unroll=DMA_UNROLL)

    @pl.when(i == 0)
    def _():
        gather(0, 0)

    @pl.when(i + 1 < pl.num_programs(0))
    def _():
        gather(i + 1, 1 - slot)

    for k in range(2):
        pltpu.make_async_copy(yr_ref.at[pl.ds(0, tc), :], ybuf.at[slot, k], sems.at[slot]).wait()

    rw = rw_ref[...]
    y = rw[:, 0:1] * ybuf[slot, 0] + rw[:, 1:2] * ybuf[slot, 1]
    z = h_ref[...] + y
    out = z * lax.rsqrt(jnp.mean(z * z, axis=-1, keepdims=True) + EPS) * g_ref[...]

    @pl.when(i < n_prompt_tiles)
    def _():
        op_ref[...] = out

    @pl.when(i >= n_prompt_tiles)
    def _():
        os_ref[...] = out


def _combine(dest, h, rw, g, yr, *, n_p):
    t, d = h.shape
    tc = ROW_TILE
    npt = n_p // tc
    return pl.pallas_call(
        functools.partial(_combine_kernel, tc=tc, n_prompt_tiles=npt),
        grid_spec=pltpu.PrefetchScalarGridSpec(
            num_scalar_prefetch=1,
            grid=(t // tc,),
            in_specs=[
                pl.BlockSpec((tc, d), lambda i, dest: (i, 0)),
                pl.BlockSpec((tc, LANES), lambda i, dest: (i, 0)),
                pl.BlockSpec((1, d), lambda i, dest: (0, 0)),
                pl.BlockSpec(memory_space=pl.ANY),
            ],
            out_specs=(pl.BlockSpec((tc, d), lambda i, dest: (jnp.minimum(i, npt - 1), 0)),
                       pl.BlockSpec((tc, d), lambda i, dest: (jnp.maximum(i - npt, 0), 0))),
            scratch_shapes=[pltpu.VMEM((2, 2, tc, d), _F32), pltpu.SemaphoreType.DMA((2,))],
        ),
        out_shape=(jax.ShapeDtypeStruct((n_p, d), _F32), jax.ShapeDtypeStruct((t - n_p, d), _F32)),
        compiler_params=_params("arbitrary"),
        name="combine_norm",
    )(dest, h, rw, g, yr)


def _layer(xp, xs, prompt_seq, sample_seq, norm1, w_in, sink, w_proj_a, w_proj_b, w_gate, b_gate, w_out,
           norm2, w_rg, b_rg, w_re, b_re, w1, w3, w2, norm_final):
    n_p, d = xp.shape
    n_s = xs.shape[0]
    t = n_p + n_s
    a_q = A_HEADS * HEAD_DIM
    a_cols = a_q + 2 * A_KV * HEAD_DIM

    proj_a, q0, q1, q2, gates = _project(xp, xs, norm1.reshape(1, d), w_in, w_gate, b_gate.reshape(1, 2 * d),
                                         a_cols=a_cols)

    oa = _band_attention(
        proj_a.reshape(1, t, a_cols), jnp.asarray(_alibi(A_HEADS)), sink.astype(_F32), n_inner=A_KV,
        bq=min(512, prompt_seq, sample_seq), half=A_HALF, step=1, shared_kv=True, has_sink=True,
        prompt_rows=n_p, prompt_seq=prompt_seq, sample_seq=sample_seq, name="attn_window")

    slopes_b = _alibi(B_HEADS)
    obs, lses = [], []
    for gi, ((w, r), qkv) in enumerate(zip(DILATED_GROUPS, (q0, q1, q2))):
        o_g, lse_g = _band_attention(
            qkv, jnp.asarray(slopes_b[gi * B_HPG:(gi + 1) * B_HPG]), jnp.zeros((B_HPG,), _F32), n_inner=r,
            bq=min(512, prompt_seq // r, sample_seq // r), half=w // (2 * r), step=r, shared_kv=False,
            has_sink=False, prompt_rows=n_p // r, prompt_seq=prompt_seq // r, sample_seq=sample_seq // r,
            name=f"attn_dilated_{r}")
        obs.append(o_g)
        lses.append(lse_g)

    n_r = N_GROUPS + N_EXPERTS
    wr = jnp.concatenate([w_rg, jnp.transpose(w_re, (1, 0, 2)).reshape(d, N_EXPERTS)], axis=1)
    wr = jnp.pad(wr, ((0, 0), (0, LANES - n_r)))
    wr_hi = wr.astype(_BF)
    wr_lo = (wr - wr_hi.astype(_F32)).astype(_BF)
    wr2 = jnp.concatenate([wr_hi, wr_lo], axis=1)
    br = jnp.pad(jnp.concatenate([b_rg, b_re.reshape(-1)]), (0, LANES - n_r)).reshape(1, LANES).astype(_F32)

    h, hn, logits = _mix(xp, xs, oa, obs, lses, gates, w_proj_a.astype(_BF), w_proj_b.astype(_BF),
                         w_out.astype(_BF), norm2.reshape(1, d), wr2, br)

    ri, rw, cnt = _route(logits)
    counts = cnt[0, :N_EXPERTS]
    eid = ri[:, 0:2]
    rank = ri[:, 2:4]
    pcounts = (counts + EXPERT_ROWS - 1) // EXPERT_ROWS * EXPERT_ROWS
    pends = jnp.cumsum(pcounts)
    pstarts = pends - pcounts
    expert_ids = jnp.arange(N_EXPERTS, dtype=jnp.int32)
    start_of = jnp.sum(jnp.where(eid[:, :, None] == expert_ids, pstarts.astype(jnp.int32), 0), axis=-1)
    dest = (start_of + rank).reshape(-1).astype(jnp.int32)
    nblk = (2 * t + N_EXPERTS * (EXPERT_ROWS - 1) + EXPERT_ROWS - 1) // EXPERT_ROWS
    nvalid = (pends[-1] // EXPERT_ROWS).astype(jnp.int32)
    blk_start = jnp.minimum(jnp.arange(nblk, dtype=jnp.int32), nvalid - 1) * EXPERT_ROWS
    blk_e = jnp.sum(pends[None, :] <= blk_start[:, None], axis=1).astype(jnp.int32)

    pad_start = (pstarts + counts) // SUBLANES * SUBLANES
    xr = _dispatch(dest, pad_start.astype(jnp.int32), ((pends - pad_start) // SUBLANES).astype(jnp.int32),
                   nvalid.reshape(1), hn, nblk * EXPERT_ROWS)
    blk_ids = jnp.arange(nblk, dtype=jnp.int32)
    first = ((blk_ids == 0) | (blk_e != jnp.roll(blk_e, 1))).astype(jnp.int32)
    slot = ((jnp.cumsum(first) - 1) % 2).astype(jnp.int32)
    later = (expert_ids[None, :] > expert_ids[:, None]) & (counts[None, :] > 0)
    next_of = jnp.min(jnp.where(later, expert_ids[None, :], N_EXPERTS), axis=1)
    next_e = jnp.sum(jnp.where(blk_e[:, None] == expert_ids, next_of, 0), axis=1)
    next_e = jnp.where(next_e < N_EXPERTS, next_e, -1).astype(jnp.int32)
    yr = _experts(blk_e, nvalid.reshape(1), first, slot, next_e, xr, w1, w3, w2)
    return _combine(dest, h, rw, norm_final.reshape(1, d), yr, n_p=n_p)


def kernel(x_prompt, x_sample, norm1, w_in, attn_sink, w_proj_a, w_proj_b, w_gate, b_gate, w_out, norm2,
           w_router_group, b_router_group, w_router_expert, b_router_expert, w_expert_gate, w_expert_up,
           w_expert_down, norm_final):
    assert norm1.shape[0] == 1, "one layer"
    d = x_prompt.shape[-1]
    xp = x_prompt.reshape(-1, d)
    xs = x_sample.reshape(-1, d)
    yp, ys = _layer(xp, xs, x_prompt.shape[1], x_sample.shape[1], norm1[0], w_in[0], attn_sink[0], w_proj_a[0],
                    w_proj_b[0], w_gate[0], b_gate[0], w_out[0], norm2[0], w_router_group[0],
                    b_router_group[0], w_router_expert[0], b_router_expert[0], w_expert_gate[0],
                    w_expert_up[0], w_expert_down[0], norm_final)
    return yp.reshape(x_prompt.shape), ys.reshape(x_sample.shape)
```

```python
import functools

import jax
import jax.numpy as jnp
import numpy as np
from jax import lax
from jax.experimental import pallas as pl
from jax.experimental.pallas import tpu as pltpu

HEAD_DIM = 128
A_HEADS = 16
A_KV = 4
A_HALF = 128
DILATED_GROUPS = ((128, 1), (512, 4), (2048, 16))
N_DIL = len(DILATED_GROUPS)
B_HPG = 4
B_HEADS = N_DIL * B_HPG
N_GROUPS = 8
EXPERTS_PER_GROUP = 8
N_EXPERTS = N_GROUPS * EXPERTS_PER_GROUP
EPS = 1e-6

LANES = 128
SUBLANES = 8
MASKED = -1e30
VMEM_LIMIT = 56 * 1024 * 1024
SUBQ = 128
HEADS_PER_STEP = 4
EXPERT_ROWS = 256
DMA_UNROLL = 8
PROJ_TM, PROJ_TN = 1024, 512
ROW_TILE = 256

_BF = jnp.bfloat16
_F32 = jnp.float32


def _alibi(n):
    return np.power(2.0, -8.0 * (np.arange(n) + 1) / n).astype(np.float32)


def _params(*sem):
    return pltpu.CompilerParams(dimension_semantics=sem, vmem_limit_bytes=VMEM_LIMIT)


def _pack_halves(x):
    n = x.shape[1] // 2

    def bf16_bits(v):
        return lax.bitcast_convert_type(v.astype(_BF).astype(_F32), jnp.uint32)

    return bf16_bits(x[:, n:]) | (bf16_bits(x[:, :n]) >> 16)


def _unpack_halves(p):
    lo = lax.bitcast_convert_type(p << 16, _F32)
    hi = lax.bitcast_convert_type(p & jnp.uint32(0xFFFF0000), _F32)
    return lo, hi


def _proj_kernel(xp_ref, xs_ref, g_ref, win_ref, wg_ref, b_ref, oa_ref, q0_ref, q1_ref, q2_ref, gt_ref,
                 xn_ref, acc_ref, *, n_prompt_tiles, part_starts, tm):
    i = pl.program_id(0)
    j = pl.program_id(1)

    def norm_into_scratch(x_ref):
        x = x_ref[...]
        y = x * lax.rsqrt(jnp.mean(x * x, axis=-1, keepdims=True) + EPS)
        xn_ref[...] = (y * g_ref[...]).astype(_BF)

    @pl.when((j == 0) & (i < n_prompt_tiles))
    def _():
        norm_into_scratch(xp_ref)

    @pl.when((j == 0) & (i >= n_prompt_tiles))
    def _():
        norm_into_scratch(xs_ref)

    def tile(w_ref=win_ref):
        return jnp.dot(xn_ref[...], w_ref[...].astype(_BF), preferred_element_type=_F32)

    s0, s1, s2, sg = part_starts

    @pl.when(j < s0)
    def _():
        oa_ref[...] = tile().astype(_BF)

    for (lo, hi, q_ref, (_, r)) in ((s0, s1, q0_ref, DILATED_GROUPS[0]), (s1, s2, q1_ref, DILATED_GROUPS[1]),
                                    (s2, sg, q2_ref, DILATED_GROUPS[2])):
        @pl.when((j >= lo) & (j < hi))
        def _(q_ref=q_ref, r=r):
            acc = tile()
            if r == 1:
                q_ref[0] = acc.astype(_BF)
            else:
                for cb in range(acc.shape[1] // LANES):
                    acc_ref[cb] = acc[:, cb * LANES:(cb + 1) * LANES]
                for p in range(r):
                    for cb in range(acc.shape[1] // LANES):
                        q_ref[p, :, cb * LANES:(cb + 1) * LANES] = (
                            acc_ref[cb, pl.ds(p, tm // r, stride=r), :].astype(_BF))

    @pl.when(j >= sg)
    def _():
        z = tile(wg_ref) + b_ref[...]
        gt_ref[...] = (0.5 * jnp.tanh(0.5 * z) + 0.5).astype(_BF)


def _project(xp, xs, g, w_in, w_gate, b_gate, *, a_cols):
    n_p, d = xp.shape
    n_s = xs.shape[0]
    t = n_p + n_s
    tm, tn = PROJ_TM, PROJ_TN
    assert tn == B_HPG * HEAD_DIM and w_in.shape[1] == a_cols + 3 * N_DIL * tn
    b_cols, gate_cols = 3 * tn, w_gate.shape[1]
    npt = n_p // tm
    na, nb, ng = a_cols // tn, 3, gate_cols // tn
    s0, s1, s2, sg = na, na + nb, na + 2 * nb, na + 3 * nb
    once = pl.Buffered(1)

    def part_map(lo, n):
        return lambda i, j: (0, i, jnp.clip(j - lo, 0, n - 1))

    def w_in_map(i, j):
        jj = jnp.clip(j - na, 0, N_DIL * nb - 1)
        return (0, jnp.where(j < na, j, na + (jj % nb) * N_DIL + jj // nb))

    q_shapes = [jax.ShapeDtypeStruct((r, t // r, b_cols), _BF) for _, r in DILATED_GROUPS]
    q_specs = [pl.BlockSpec((r, tm // r, tn), part_map(lo, nb))
               for lo, (_, r) in zip((s0, s1, s2), DILATED_GROUPS)]
    return pl.pallas_call(
        functools.partial(_proj_kernel, n_prompt_tiles=npt, part_starts=(s0, s1, s2, sg), tm=tm),
        grid=(t // tm, sg + ng),
        in_specs=[
            pl.BlockSpec((tm, d), lambda i, j: (jnp.minimum(i, npt - 1), 0), pipeline_mode=once),
            pl.BlockSpec((tm, d), lambda i, j: (jnp.maximum(i - npt, 0), 0), pipeline_mode=once),
            pl.BlockSpec((1, d), lambda i, j: (0, 0)),
            pl.BlockSpec((d, tn), w_in_map),
            pl.BlockSpec((d, tn), lambda i, j: (0, jnp.clip(j - sg, 0, ng - 1))),
            pl.BlockSpec((1, tn), lambda i, j: (0, jnp.clip(j - sg, 0, ng - 1))),
        ],
        out_specs=[pl.BlockSpec((tm, tn), lambda i, j: (i, jnp.minimum(j, na - 1)))] + q_specs
                  + [pl.BlockSpec((tm, tn), lambda i, j: (i, jnp.clip(j - sg, 0, ng - 1)))],
        out_shape=[jax.ShapeDtypeStruct((t, a_cols), _BF)] + q_shapes
                  + [jax.ShapeDtypeStruct((t, gate_cols), _BF)],
        scratch_shapes=[pltpu.VMEM((tm, d), _BF), pltpu.VMEM((tn // LANES, tm, LANES), _F32)],
        compiler_params=_params("arbitrary", "arbitrary"),
        name="norm_proj",
    )(xp, xs, g, w_in, w_gate, b_gate)


def _band_attn_kernel(slope_ref, sink_ref, q_ref, kl_ref, km_ref, kr_ref, vl_ref, vm_ref, vr_ref,
                      *rest, bq, half, step, shared_kv, has_sink, prompt_rows, prompt_seq, sample_seq):
    if has_sink:
        o_ref, kcat, vcat = rest
        lse_ref = None
    else:
        o_ref, lse_ref, kcat, vcat = rest
    i = pl.program_id(0)
    c = pl.program_id(1)
    win = SUBQ + 2 * half

    kcat[0:half, :] = kl_ref[...]
    kcat[half:half + bq, :] = km_ref[...]
    kcat[half + bq:, :] = kr_ref[...]
    vcat[0:half, :] = vl_ref[...]
    vcat[half:half + bq, :] = vm_ref[...]
    vcat[half + bq:, :] = vr_ref[...]

    u0 = i * bq
    in_prompt = u0 < prompt_rows
    lo = jnp.where(in_prompt, (u0 // prompt_seq) * prompt_seq,
                   prompt_rows + ((u0 - prompt_rows) // sample_seq) * sample_seq)
    hi = lo + jnp.where(in_prompt, prompt_seq, sample_seq)

    qi = lax.broadcasted_iota(jnp.int32, (SUBQ, win), 0)
    kj = lax.broadcasted_iota(jnp.int32, (SUBQ, win), 1)
    rel = kj - half - qi
    absrel = jnp.abs(rel)
    in_band = absrel <= half
    neg_dist = -(absrel * step).astype(_F32)
    scale = HEAD_DIM ** -0.5
    lane = lax.broadcasted_iota(jnp.int32, (SUBQ, LANES), 1)

    heads = range(HEADS_PER_STEP)
    for sb in range(bq // SUBQ):
        kpos = kj + (u0 + sb * SUBQ - half)
        valid = in_band & (kpos >= lo) & (kpos < hi)
        bias = jnp.where(valid, neg_dist, MASKED)
        rows = slice(sb * SUBQ, (sb + 1) * SUBQ)
        if shared_kv:
            k = kcat[sb * SUBQ:sb * SUBQ + win, :]
            v = vcat[sb * SUBQ:sb * SUBQ + win, :]
            q4 = jnp.concatenate([q_ref[rows, h * HEAD_DIM:(h + 1) * HEAD_DIM] for h in heads], axis=0)
            s4 = lax.dot_general(q4, k, (((1,), (1,)), ((), ())), preferred_element_type=_F32)
            ps, ms, ls = [], [], []
            for h in heads:
                s = s4[h * SUBQ:(h + 1) * SUBQ] * scale + slope_ref[c * HEADS_PER_STEP + h] * bias
                m = jnp.max(s, axis=-1, keepdims=True)
                p = jnp.exp(s - m)
                ms.append(m)
                ls.append(jnp.sum(p, axis=-1, keepdims=True))
                ps.append(p.astype(_BF))
            pv4 = jnp.dot(jnp.concatenate(ps, axis=0), v, preferred_element_type=_F32)
            for h in heads:
                l = ls[h] + jnp.exp(sink_ref[c * HEADS_PER_STEP + h] - ms[h])
                o_ref[rows, h * HEAD_DIM:(h + 1) * HEAD_DIM] = (
                    pv4[h * SUBQ:(h + 1) * SUBQ] / l).astype(o_ref.dtype)
            continue
        lse_tile = jnp.zeros((SUBQ, LANES), _F32)
        for h in heads:
            cols = slice(h * HEAD_DIM, (h + 1) * HEAD_DIM)
            k = kcat[sb * SUBQ:sb * SUBQ + win, cols]
            v = vcat[sb * SUBQ:sb * SUBQ + win, cols]
            s = lax.dot_general(q_ref[rows, cols], k, (((1,), (1,)), ((), ())), preferred_element_type=_F32)
            s = s * scale + slope_ref[h] * bias
            m = jnp.max(s, axis=-1, keepdims=True)
            p = jnp.exp(s - m)
            l = jnp.sum(p, axis=-1, keepdims=True)
            pv = jnp.dot(p.astype(_BF), v, preferred_element_type=_F32)
            lse_tile = jnp.where(lane == h, m + jnp.log(l), lse_tile)
            o_ref[rows, cols] = (pv / l).astype(o_ref.dtype)
        lse_ref[rows, :] = lse_tile


def _band_attention(qkv, slopes, sinks, *, n_inner, bq, half, step, shared_kv, has_sink,
                    prompt_rows, prompt_seq, sample_seq, name):
    lead, rows, _ = qkv.shape
    assert prompt_seq % bq == 0 and sample_seq % bq == 0 and bq % SUBQ == 0 and bq % half == 0
    nq = rows // bq
    hb = bq // half
    last_halo = rows // half - 1
    width = HEADS_PER_STEP * HEAD_DIM
    if shared_kv:
        kv_w = HEAD_DIM
        k_base, v_base = width * n_inner // HEAD_DIM, width * n_inner // HEAD_DIM + n_inner
        lead_of = lambda c: 0
        q_col = lambda c: c
        kv_col = lambda base: (lambda c: base + c)
    else:
        kv_w = width
        k_base, v_base = 1, 2
        lead_of = lambda c: c
        q_col = lambda c: 0
        kv_col = lambda base: (lambda c: base)

    def main_map(col):
        return lambda i, c: (lead_of(c), i, col(c))

    def left_map(col):
        return lambda i, c: (lead_of(c), jnp.maximum(i * hb - 1, 0), col(c))

    def right_map(col):
        return lambda i, c: (lead_of(c), jnp.minimum((i + 1) * hb, last_halo), col(c))

    smem = pl.BlockSpec(memory_space=pltpu.SMEM)
    in_specs = [
        smem, smem,
        pl.BlockSpec((None, bq, width), main_map(q_col)),
        pl.BlockSpec((None, half, kv_w), left_map(kv_col(k_base))),
        pl.BlockSpec((None, bq, kv_w), main_map(kv_col(k_base))),
        pl.BlockSpec((None, half, kv_w), right_map(kv_col(k_base))),
        pl.BlockSpec((None, half, kv_w), left_map(kv_col(v_base))),
        pl.BlockSpec((None, bq, kv_w), main_map(kv_col(v_base))),
        pl.BlockSpec((None, half, kv_w), right_map(kv_col(v_base))),
    ]
    if has_sink:
        out_specs = pl.BlockSpec((bq, width), lambda i, c: (i, c))
        out_shape = jax.ShapeDtypeStruct((rows, n_inner * width), _BF)
    else:
        out_specs = (pl.BlockSpec((None, bq, width), lambda i, c: (c, i, 0)),
                     pl.BlockSpec((None, bq, LANES), lambda i, c: (c, i, 0)))
        out_shape = (jax.ShapeDtypeStruct((lead, rows, width), _BF),
                     jax.ShapeDtypeStruct((lead, rows, LANES), _F32))
    return pl.pallas_call(
        functools.partial(_band_attn_kernel, bq=bq, half=half, step=step, shared_kv=shared_kv,
                          has_sink=has_sink, prompt_rows=prompt_rows, prompt_seq=prompt_seq,
                          sample_seq=sample_seq),
        grid=(nq, n_inner),
        in_specs=in_specs,
        out_specs=out_specs,
        out_shape=out_shape,
        scratch_shapes=[pltpu.VMEM((bq + 2 * half, kv_w), _BF), pltpu.VMEM((bq + 2 * half, kv_w), _BF)],
        compiler_params=_params("parallel", "arbitrary"),
        name=name,
    )(slopes, sinks, qkv, qkv, qkv, qkv, qkv, qkv, qkv)


def _mix_kernel(xp_ref, xs_ref, oa_ref, ob0_ref, ob1_ref, ob2_ref, l0_ref, l1_ref, l2_ref, g_ref,
                wpa_ref, wpb_ref, wout_ref, n2_ref, wr_ref, br_ref, h_ref, hn_ref, lg_ref, o_scr, l_scr,
                *, n_prompt_tiles, d, tm):
    i = pl.program_id(0)

    def token_order(o_ref, l_ref, r, slot):
        if r == 1:
            return [o_ref[0, :, h * HEAD_DIM:(h + 1) * HEAD_DIM].astype(_F32) for h in range(B_HPG)], l_ref[0]
        for p in range(r):
            for h in range(B_HPG):
                o_scr[slot, h, pl.ds(p, tm // r, stride=r), :] = (
                    o_ref[p, :, h * HEAD_DIM:(h + 1) * HEAD_DIM].astype(_F32))
            l_scr[slot, pl.ds(p, tm // r, stride=r), :] = l_ref[p]
        return [o_scr[slot, h] for h in range(B_HPG)], l_scr[slot]

    o0, l0 = token_order(ob0_ref, l0_ref, DILATED_GROUPS[0][1], 0)
    o1, l1 = token_order(ob1_ref, l1_ref, DILATED_GROUPS[1][1], 0)
    o2, l2 = token_order(ob2_ref, l2_ref, DILATED_GROUPS[2][1], 1)
    mx = jnp.maximum(jnp.maximum(l0, l1), l2)
    e0, e1, e2 = jnp.exp(l0 - mx), jnp.exp(l1 - mx), jnp.exp(l2 - mx)
    den = e0 + e1 + e2
    a0, a1, a2 = e0 / den, e1 / den, e2 / den
    parts = []
    for h in range(B_HPG):
        parts.append(a0[:, h:h + 1] * o0[h] + a1[:, h:h + 1] * o1[h] + a2[:, h:h + 1] * o2[h])
    ob = jnp.concatenate(parts, axis=1).astype(_BF)

    ta = jnp.dot(oa_ref[...], wpa_ref[...], preferred_element_type=_F32)
    tb = jnp.dot(ob, wpb_ref[...], preferred_element_type=_F32)
    merged = g_ref[:, :d].astype(_F32) * ta + g_ref[:, d:].astype(_F32) * tb
    x = jnp.where(i < n_prompt_tiles, xp_ref[...], xs_ref[...])
    h_new = x + jnp.dot(merged.astype(_BF), wout_ref[...], preferred_element_type=_F32)
    h_ref[...] = h_new
    hn = h_new * lax.rsqrt(jnp.mean(h_new * h_new, axis=-1, keepdims=True) + EPS) * n2_ref[...]
    hn_ref[...] = _pack_halves(hn)
    hn_hi = hn.astype(_BF)
    hn_lo = (hn - hn_hi.astype(_F32)).astype(_BF)
    r = (jnp.dot(hn_hi, wr_ref[...], preferred_element_type=_F32)
         + jnp.dot(hn_lo, wr_ref[...], preferred_element_type=_F32))
    lg_ref[...] = r[:, :LANES] + r[:, LANES:] + br_ref[...]


def _mix(xp, xs, oa, obs, lses, gates, wpa, wpb, wout, n2, wr, br):
    n_p, d = xp.shape
    n_s = xs.shape[0]
    t = n_p + n_s
    tm = ROW_TILE
    npt = n_p // tm
    row = lambda i: (i, 0)
    const = lambda i: (0, 0)
    width = B_HPG * HEAD_DIM

    def resident(shape):
        return pl.BlockSpec(shape, const, pipeline_mode=pl.Buffered(1))

    def phase_blocks(cols):
        return [pl.BlockSpec((r, tm // r, cols), lambda i: (0, i, 0)) for _, r in DILATED_GROUPS]

    in_specs = [
        pl.BlockSpec((tm, d), lambda i: (jnp.minimum(i, npt - 1), 0)),
        pl.BlockSpec((tm, d), lambda i: (jnp.maximum(i - npt, 0), 0)),
        pl.BlockSpec((tm, oa.shape[1]), row),
        *phase_blocks(width), *phase_blocks(LANES),
        pl.BlockSpec((tm, 2 * d), row),
        resident(wpa.shape), resident(wpb.shape), resident(wout.shape),
        resident((1, d)), resident(wr.shape), resident((1, LANES)),
    ]
    return pl.pallas_call(
        functools.partial(_mix_kernel, n_prompt_tiles=npt, d=d, tm=tm),
        grid=(t // tm,),
        in_specs=in_specs,
        out_specs=(pl.BlockSpec((tm, d), row), pl.BlockSpec((tm, d // 2), row), pl.BlockSpec((tm, LANES), row)),
        out_shape=(jax.ShapeDtypeStruct((t, d), _F32), jax.ShapeDtypeStruct((t, d // 2), jnp.uint32),
                   jax.ShapeDtypeStruct((t, LANES), _F32)),
        scratch_shapes=[pltpu.VMEM((2, B_HPG, tm, HEAD_DIM), _F32), pltpu.VMEM((2, tm, LANES), _F32)],
        compiler_params=_params("parallel"),
        name="mix_out_router",
    )(xp, xs, oa, *obs, *lses, gates, wpa, wpb, wout, n2, wr, br)


def _route_kernel(lg_ref, ri_ref, rw_ref, cnt_ref, carry_ref, *, tr):
    i = pl.program_id(0)

    @pl.when(i == 0)
    def _():
        carry_ref[...] = jnp.zeros_like(carry_ref)

    lg = lg_ref[...]
    lane_i = lax.broadcasted_iota(jnp.int32, (tr, LANES), 1)
    lane = lane_i.astype(_F32)
    no_lane = float(LANES)
    is_grp = lane_i < N_GROUPS
    glog = jnp.where(is_grp, lg, MASKED)
    gmax = jnp.max(glog, axis=-1, keepdims=True)
    grp = jnp.min(jnp.where(glog == gmax, lane, no_lane), axis=-1, keepdims=True)
    gsum = jnp.sum(jnp.where(is_grp, jnp.exp(glog - gmax), 0.0), axis=-1, keepdims=True)
    pgrp = 1.0 / gsum
    lane_grp = ((lane_i - N_GROUPS) // EXPERTS_PER_GROUP).astype(_F32)
    in_grp = (lane_i >= N_GROUPS) & (lane_i < N_GROUPS + N_EXPERTS) & (lane_grp == grp)
    elog = jnp.where(in_grp, lg, MASKED)
    t1 = jnp.max(elog, axis=-1, keepdims=True)
    i1 = jnp.min(jnp.where(elog == t1, lane, no_lane), axis=-1, keepdims=True)
    elog2 = jnp.where(lane == i1, MASKED, elog)
    t2 = jnp.max(elog2, axis=-1, keepdims=True)
    i2 = jnp.min(jnp.where(elog2 == t2, lane, no_lane), axis=-1, keepdims=True)
    e21 = jnp.exp(t2 - t1)
    w1 = pgrp / (1.0 + e21)
    w2 = pgrp * e21 / (1.0 + e21)
    eid1 = i1 - N_GROUPS
    eid2 = i2 - N_GROUPS
    hot1 = lane == eid1
    hot2 = lane == eid2
    onehot = jnp.where(hot1, 1.0, 0.0) + jnp.where(hot2, 1.0, 0.0)
    r_i = lax.broadcasted_iota(jnp.int32, (tr, tr), 0)
    c_i = lax.broadcasted_iota(jnp.int32, (tr, tr), 1)
    lower = jnp.where(c_i < r_i, 1.0, 0.0).astype(_BF)
    before = jnp.dot(lower, onehot.astype(_BF), preferred_element_type=_F32) + carry_ref[0:1, :]
    rank1 = jnp.sum(jnp.where(hot1, before, 0.0), axis=-1, keepdims=True)
    rank2 = jnp.sum(jnp.where(hot2, before, 0.0), axis=-1, keepdims=True)
    total = carry_ref[0:1, :] + jnp.sum(onehot, axis=0, keepdims=True)
    carry_ref[...] = jnp.broadcast_to(total, carry_ref.shape)
    cnt_ref[...] = jnp.broadcast_to(total, cnt_ref.shape).astype(jnp.int32)
    ri = jnp.where(lane_i == 0, eid1, jnp.where(lane_i == 1, eid2, jnp.where(lane_i == 2, rank1,
                   jnp.where(lane_i == 3, rank2, 0.0))))
    ri_ref[...] = ri.astype(jnp.int32)
    rw_ref[...] = jnp.where(lane_i == 0, w1, jnp.where(lane_i == 1, w2, 0.0))


def _route(logits):
    t = logits.shape[0]
    tr = ROW_TILE
    row = lambda i: (i, 0)
    return pl.pallas_call(
        functools.partial(_route_kernel, tr=tr),
        grid=(t // tr,),
        in_specs=[pl.BlockSpec((tr, LANES), row)],
        out_specs=(pl.BlockSpec((tr, LANES), row), pl.BlockSpec((tr, LANES), row),
                   pl.BlockSpec((8, LANES), lambda i: (0, 0))),
        out_shape=(jax.ShapeDtypeStruct((t, LANES), jnp.int32), jax.ShapeDtypeStruct((t, LANES), _F32),
                   jax.ShapeDtypeStruct((8, LANES), jnp.int32)),
        scratch_shapes=[pltpu.VMEM((8, LANES), _F32)],
        compiler_params=_params("arbitrary"),
        name="route",
    )(logits)


def _dispatch_kernel(dest_ref, pad_start_ref, pad_n_ref, nvalid_ref, hn_ref, xr_ref, zeros, sem, pad_sem, *, td):
    i = pl.program_id(0)

    @pl.when(i == 0)
    def _():
        zeros[...] = jnp.zeros_like(zeros)

        def pad_chunk(e, r):
            row0 = pl.multiple_of(pad_start_ref[e] + r * SUBLANES, SUBLANES)
            return pltpu.make_async_copy(zeros.at[pl.ds(0, SUBLANES), :], xr_ref.at[pl.ds(row0, SUBLANES), :],
                                         pad_sem)

        def per_expert(act):
            def body(e, carry):
                lax.fori_loop(0, pad_n_ref[e], lambda r, c: (act(pad_chunk(e, r)), c)[1], 0)
                return carry
            return body

        def tail_block(b):
            row0 = pl.multiple_of(b * EXPERT_ROWS, EXPERT_ROWS)
            return pltpu.make_async_copy(zeros, xr_ref.at[pl.ds(row0, EXPERT_ROWS), :], pad_sem)

        n_blocks = xr_ref.shape[0] // EXPERT_ROWS
        lax.fori_loop(0, N_EXPERTS, per_expert(lambda cp: cp.start()), 0)
        lax.fori_loop(nvalid_ref[0], n_blocks, lambda b, c: (tail_block(b).start(), c)[1], 0)
        lax.fori_loop(0, N_EXPERTS, per_expert(lambda cp: cp.wait()), 0)
        lax.fori_loop(nvalid_ref[0], n_blocks, lambda b, c: (tail_block(b).wait(), c)[1], 0)

    def start(j, carry):
        for k in range(2):
            dst = dest_ref[2 * (i * td + j) + k]
            pltpu.make_async_copy(hn_ref.at[pl.ds(j, 1), :], xr_ref.at[pl.ds(dst, 1), :], sem).start()
        return carry

    lax.fori_loop(0, td, start, 0, unroll=DMA_UNROLL)
    for _ in range(2):
        pltpu.make_async_copy(hn_ref, xr_ref.at[pl.ds(0, td), :], sem).wait()


def _dispatch(dest, pad_start, pad_n, nvalid, hn, rows):
    t, d = hn.shape
    td = ROW_TILE
    return pl.pallas_call(
        functools.partial(_dispatch_kernel, td=td),
        grid_spec=pltpu.PrefetchScalarGridSpec(
            num_scalar_prefetch=4,
            grid=(t // td,),
            in_specs=[pl.BlockSpec((td, d), lambda i, *_: (i, 0))],
            out_specs=pl.BlockSpec(memory_space=pl.ANY),
            scratch_shapes=[pltpu.VMEM((EXPERT_ROWS, d), hn.dtype), pltpu.SemaphoreType.DMA(()),
                            pltpu.SemaphoreType.DMA(())],
        ),
        out_shape=jax.ShapeDtypeStruct((rows, d), hn.dtype),
        compiler_params=_params("arbitrary"),
        name="dispatch",
    )(dest, pad_start, pad_n, nvalid, hn)


def _expert_kernel(blk_e_ref, nvalid_ref, first_ref, slot_ref, next_e_ref, x_ref, w1_hbm, w3_hbm, w2_hbm, y_ref,
                   w1_buf, w3_buf, w2_buf, sems):
    b = pl.program_id(0)
    valid = b < nvalid_ref[0]

    def fetch(e, slot):
        return (pltpu.make_async_copy(w1_hbm.at[e], w1_buf.at[slot], sems.at[slot, 0]),
                pltpu.make_async_copy(w3_hbm.at[e], w3_buf.at[slot], sems.at[slot, 1]),
                pltpu.make_async_copy(w2_hbm.at[e], w2_buf.at[slot], sems.at[slot, 2]))

    @pl.when(b == 0)
    def _():
        for cp in fetch(blk_e_ref[0], 0):
            cp.start()

    @pl.when(valid & (first_ref[b] == 1))
    def _():
        for cp in fetch(blk_e_ref[b], slot_ref[b]):
            cp.wait()

        @pl.when(next_e_ref[b] >= 0)
        def _():
            for cp in fetch(next_e_ref[b], 1 - slot_ref[b]):
                cp.start()

    @pl.when(valid)
    def _():
        slot = slot_ref[b]
        xb = jnp.concatenate([half.astype(_BF) for half in _unpack_halves(x_ref[...])], axis=1)
        a = jnp.dot(xb, w1_buf[slot].astype(_BF), preferred_element_type=_F32)
        u = jnp.dot(xb, w3_buf[slot].astype(_BF), preferred_element_type=_F32)
        hmid = (a / (1.0 + jnp.exp(-a))) * u
        y_ref[...] = _pack_halves(jnp.dot(hmid.astype(_BF), w2_buf[slot].astype(_BF), preferred_element_type=_F32))

    @pl.when(jnp.logical_not(valid))
    def _():
        y_ref[...] = jnp.zeros_like(y_ref)


def _experts(blk_e, nvalid, first, slot, next_e, xr, w1, w3, w2):
    rows, words = xr.shape
    _, d, f = w1.shape
    assert d == 2 * words
    nblk = rows // EXPERT_ROWS
    any_space = pl.BlockSpec(memory_space=pl.ANY)
    return pl.pallas_call(
        _expert_kernel,
        grid_spec=pltpu.PrefetchScalarGridSpec(
            num_scalar_prefetch=5,
            grid=(nblk,),
            in_specs=[pl.BlockSpec((EXPERT_ROWS, words), lambda b, be, nv, *_: (jnp.minimum(b, nv[0] - 1), 0)),
                      any_space, any_space, any_space],
            out_specs=pl.BlockSpec((EXPERT_ROWS, words), lambda b, *_: (b, 0)),
            scratch_shapes=[pltpu.VMEM((2, d, f), w1.dtype), pltpu.VMEM((2, d, f), w3.dtype),
                            pltpu.VMEM((2, f, d), w2.dtype), pltpu.SemaphoreType.DMA((2, 3))],
        ),
        out_shape=jax.ShapeDtypeStruct((rows, words), jnp.uint32),
        compiler_params=_params("arbitrary"),
        name="experts",
    )(blk_e, nvalid, first, slot, next_e, xr, w1, w3, w2)


def _combine_kernel(dest_ref, h_ref, rw_ref, g_ref, yr_ref, op_ref, os_ref, ybuf, sems, *, tc, n_prompt_tiles):
    i = pl.program_id(0)
    slot = i % 2

    def gather(tile, s):
        def start(j, carry):
            for k in range(2):
                src = dest_ref[2 * (tile * tc + j) + k]
                pltpu.make_async_copy(yr_ref.at[pl.ds(src, 1), :], ybuf.at[s, k, pl.ds(j, 1), :],
                                      sems.at[s]).start()
            return carry

        lax.fori_loop(0, tc, start, 0, unroll=DMA_UNROLL)

    @pl.when(i == 0)
    def _():
        gather(0, 0)

    @pl.when(i + 1 < pl.num_programs(0))
    def _():
        gather(i + 1, 1 - slot)

    for k in range(2):
        pltpu.make_async_copy(yr_ref.at[pl.ds(0, tc), :], ybuf.at[slot, k], sems.at[slot]).wait()

    rw = rw_ref[...]
    y0 = _unpack_halves(ybuf[slot, 0])
    y1 = _unpack_halves(ybuf[slot, 1])
    y = jnp.concatenate([rw[:, 0:1] * a + rw[:, 1:2] * b for a, b in zip(y0, y1)], axis=1)
    z = h_ref[...] + y
    out = z * lax.rsqrt(jnp.mean(z * z, axis=-1, keepdims=True) + EPS) * g_ref[...]

    @pl.when(i < n_prompt_tiles)
    def _():
        op_ref[...] = out

    @pl.when(i >= n_prompt_tiles)
    def _():
        os_ref[...] = out


def _combine(dest, h, rw, g, yr, *, n_p):
    t, d = h.shape
    tc = ROW_TILE
    npt = n_p // tc
    return pl.pallas_call(
        functools.partial(_combine_kernel, tc=tc, n_prompt_tiles=npt),
        grid_spec=pltpu.PrefetchScalarGridSpec(
            num_scalar_prefetch=1,
            grid=(t // tc,),
            in_specs=[
                pl.BlockSpec((tc, d), lambda i, dest: (i, 0)),
                pl.BlockSpec((tc, LANES), lambda i, dest: (i, 0)),
                pl.BlockSpec((1, d), lambda i, dest: (0, 0)),
                pl.BlockSpec(memory_space=pl.ANY),
            ],
            out_specs=(pl.BlockSpec((tc, d), lambda i, dest: (jnp.minimum(i, npt - 1), 0)),
                       pl.BlockSpec((tc, d), lambda i, dest: (jnp.maximum(i - npt, 0), 0))),
            scratch_shapes=[pltpu.VMEM((2, 2, tc, yr.shape[1]), yr.dtype), pltpu.SemaphoreType.DMA((2,))],
        ),
        out_shape=(jax.ShapeDtypeStruct((n_p, d), _F32), jax.ShapeDtypeStruct((t - n_p, d), _F32)),
        compiler_params=_params("arbitrary"),
        name="combine_norm",
    )(dest, h, rw, g, yr)


def _layer(xp, xs, prompt_seq, sample_seq, norm1, w_in, sink, w_proj_a, w_proj_b, w_gate, b_gate, w_out,
           norm2, w_rg, b_rg, w_re, b_re, w1, w3, w2, norm_final):
    n_p, d = xp.shape
    n_s = xs.shape[0]
    t = n_p + n_s
    a_q = A_HEADS * HEAD_DIM
    a_cols = a_q + 2 * A_KV * HEAD_DIM

    proj_a, q0, q1, q2, gates = _project(xp, xs, norm1.reshape(1, d), w_in, w_gate, b_gate.reshape(1, 2 * d),
                                         a_cols=a_cols)

    oa = _band_attention(
        proj_a.reshape(1, t, a_cols), jnp.asarray(_alibi(A_HEADS)), sink.astype(_F32), n_inner=A_KV,
        bq=min(512, prompt_seq, sample_seq), half=A_HALF, step=1, shared_kv=True, has_sink=True,
        prompt_rows=n_p, prompt_seq=prompt_seq, sample_seq=sample_seq, name="attn_window")

    slopes_b = _alibi(B_HEADS)
    obs, lses = [], []
    for gi, ((w, r), qkv) in enumerate(zip(DILATED_GROUPS, (q0, q1, q2))):
        o_g, lse_g = _band_attention(
            qkv, jnp.asarray(slopes_b[gi * B_HPG:(gi + 1) * B_HPG]), jnp.zeros((B_HPG,), _F32), n_inner=r,
            bq=min(512, prompt_seq // r, sample_seq // r), half=w // (2 * r), step=r, shared_kv=False,
            has_sink=False, prompt_rows=n_p // r, prompt_seq=prompt_seq // r, sample_seq=sample_seq // r,
            name=f"attn_dilated_{r}")
        obs.append(o_g)
        lses.append(lse_g)

    n_r = N_GROUPS + N_EXPERTS
    wr = jnp.concatenate([w_rg, jnp.transpose(w_re, (1, 0, 2)).reshape(d, N_EXPERTS)], axis=1)
    wr = jnp.pad(wr, ((0, 0), (0, LANES - n_r)))
    wr_hi = wr.astype(_BF)
    wr_lo = (wr - wr_hi.astype(_F32)).astype(_BF)
    wr2 = jnp.concatenate([wr_hi, wr_lo], axis=1)
    br = jnp.pad(jnp.concatenate([b_rg, b_re.reshape(-1)]), (0, LANES - n_r)).reshape(1, LANES).astype(_F32)

    h, hn, logits = _mix(xp, xs, oa, obs, lses, gates, w_proj_a.astype(_BF), w_proj_b.astype(_BF),
                         w_out.astype(_BF), norm2.reshape(1, d), wr2, br)

    ri, rw, cnt = _route(logits)
    counts = cnt[0, :N_EXPERTS]
    eid = ri[:, 0:2]
    rank = ri[:, 2:4]
    pcounts = (counts + EXPERT_ROWS - 1) // EXPERT_ROWS * EXPERT_ROWS
    pends = jnp.cumsum(pcounts)
    pstarts = pends - pcounts
    expert_ids = jnp.arange(N_EXPERTS, dtype=jnp.int32)
    start_of = jnp.sum(jnp.where(eid[:, :, None] == expert_ids, pstarts.astype(jnp.int32), 0), axis=-1)
    dest = (start_of + rank).reshape(-1).astype(jnp.int32)
    nblk = (2 * t + N_EXPERTS * (EXPERT_ROWS - 1) + EXPERT_ROWS - 1) // EXPERT_ROWS
    nvalid = (pends[-1] // EXPERT_ROWS).astype(jnp.int32)
    blk_start = jnp.minimum(jnp.arange(nblk, dtype=jnp.int32), nvalid - 1) * EXPERT_ROWS
    blk_e = jnp.sum(pends[None, :] <= blk_start[:, None], axis=1).astype(jnp.int32)

    pad_start = (pstarts + counts) // SUBLANES * SUBLANES
    xr = _dispatch(dest, pad_start.astype(jnp.int32), ((pends - pad_start) // SUBLANES).astype(jnp.int32),
                   nvalid.reshape(1), hn, nblk * EXPERT_ROWS)
    blk_ids = jnp.arange(nblk, dtype=jnp.int32)
    first = ((blk_ids == 0) | (blk_e != jnp.roll(blk_e, 1))).astype(jnp.int32)
    slot = ((jnp.cumsum(first) - 1) % 2).astype(jnp.int32)
    later = (expert_ids[None, :] > expert_ids[:, None]) & (counts[None, :] > 0)
    next_of = jnp.min(jnp.where(later, expert_ids[None, :], N_EXPERTS), axis=1)
    next_e = jnp.sum(jnp.where(blk_e[:, None] == expert_ids, next_of, 0), axis=1)
    next_e = jnp.where(next_e < N_EXPERTS, next_e, -1).astype(jnp.int32)
    yr = _experts(blk_e, nvalid.reshape(1), first, slot, next_e, xr, w1, w3, w2)
    return _combine(dest, h, rw, norm_final.reshape(1, d), yr, n_p=n_p)


def kernel(x_prompt, x_sample, norm1, w_in, attn_sink, w_proj_a, w_proj_b, w_gate, b_gate, w_out, norm2,
           w_router_group, b_router_group, w_router_expert, b_router_expert, w_expert_gate, w_expert_up,
           w_expert_down, norm_final):
    assert norm1.shape[0] == 1, "one layer"
    d = x_prompt.shape[-1]
    xp = x_prompt.reshape(-1, d)
    xs = x_sample.reshape(-1, d)
    yp, ys = _layer(xp, xs, x_prompt.shape[1], x_sample.shape[1], norm1[0], w_in[0], attn_sink[0], w_proj_a[0],
                    w_proj_b[0], w_gate[0], b_gate[0], w_out[0], norm2[0], w_router_group[0],
                    b_router_group[0], w_router_expert[0], b_router_expert[0], w_expert_gate[0],
                    w_expert_up[0], w_expert_down[0], norm_final)
    return yp.reshape(x_prompt.shape), ys.reshape(x_sample.shape)
```

```python
import functools

import jax
import jax.numpy as jnp
import numpy as np
from jax import lax
from jax.experimental import pallas as pl
from jax.experimental.pallas import tpu as pltpu

HEAD_DIM = 128
A_HEADS = 16
A_KV = 4
A_HALF = 128
DILATED_GROUPS = ((128, 1), (512, 4), (2048, 16))
N_DIL = len(DILATED_GROUPS)
B_HPG = 4
B_HEADS = N_DIL * B_HPG
N_GROUPS = 8
EXPERTS_PER_GROUP = 8
N_EXPERTS = N_GROUPS * EXPERTS_PER_GROUP
EPS = 1e-6

LANES = 128
SUBLANES = 8
MASKED = -1e30
VMEM_LIMIT = 56 * 1024 * 1024
SUBQ = 128
HEADS_PER_STEP = 4
EXPERT_ROWS = 256
DMA_UNROLL = 8
PROJ_TM, PROJ_TN = 1024, 512
MAX_ROW_STRIDE = 4
ROW_TILE = 256

_BF = jnp.bfloat16
_F32 = jnp.float32


def _alibi(n):
    return np.power(2.0, -8.0 * (np.arange(n) + 1) / n).astype(np.float32)


def _params(*sem):
    return pltpu.CompilerParams(dimension_semantics=sem, vmem_limit_bytes=VMEM_LIMIT)


def _pack_halves(x):
    n = x.shape[1] // 2

    def bf16_bits(v):
        return lax.bitcast_convert_type(v.astype(_BF).astype(_F32), jnp.uint32)

    return bf16_bits(x[:, n:]) | (bf16_bits(x[:, :n]) >> 16)


def _unpack_halves(p):
    lo = lax.bitcast_convert_type(p << 16, _F32)
    hi = lax.bitcast_convert_type(p & jnp.uint32(0xFFFF0000), _F32)
    return lo, hi


def _proj_kernel(g_ref, b_ref, xp_hbm, xs_hbm, win_hbm, wg_hbm, oa_hbm, q0_hbm, q1_hbm, q2_hbm, gt_hbm,
                 xbuf, xn_ref, wbuf, acc_ref, obuf, x_sem, w_sems, o_sems,
                 *, n_prompt_tiles, n_row_tiles, tm, tn, na, ng):
    i = pl.program_id(0)
    nb = 3
    tiles_per_row = na + N_DIL * nb + ng
    rows = pl.ds(pl.multiple_of(i * tm, tm), tm)

    def cols(jj):
        return pl.ds(pl.multiple_of(jj * tn, tn), tn)

    def x_copy(x_hbm, row_tile):
        return pltpu.make_async_copy(x_hbm.at[pl.ds(pl.multiple_of(row_tile * tm, tm), tm), :], xbuf, x_sem)

    def start_x(row_tile):
        @pl.when(row_tile < n_prompt_tiles)
        def _():
            x_copy(xp_hbm, row_tile).start()

        @pl.when(row_tile >= n_prompt_tiles)
        def _():
            x_copy(xs_hbm, row_tile - n_prompt_tiles).start()

    def w_copy(w_hbm, col, slot):
        return pltpu.make_async_copy(w_hbm.at[:, cols(col)], wbuf.at[slot], w_sems.at[slot])

    def wait_staging(slot):
        pltpu.make_async_copy(obuf.at[slot], oa_hbm.at[pl.ds(0, tm), pl.ds(0, tn)], o_sems.at[slot]).wait()

    def store_plain(out_hbm):
        def store(acc, jj, slot):
            obuf[slot] = acc.astype(_BF)
            pltpu.make_async_copy(obuf.at[slot], out_hbm.at[rows, cols(jj)], o_sems.at[slot]).start()
        return store

    def store_gate(acc, jj, slot):
        z = acc + b_ref[jj]
        obuf[slot] = (0.5 * jnp.tanh(0.5 * z) + 0.5).astype(_BF)
        pltpu.make_async_copy(obuf.at[slot], gt_hbm.at[rows, cols(jj)], o_sems.at[slot]).start()

    def store_phases(q_hbm, r):
        n = tm // r

        def store(acc, jj, slot):
            if r == 1:
                obuf[slot] = acc.astype(_BF)
            else:
                for cb in range(tn // LANES):
                    acc_ref[0, cb] = acc[:, cb * LANES:(cb + 1) * LANES]
                src, stride = 0, r
                if r > MAX_ROW_STRIDE:
                    stride = r // MAX_ROW_STRIDE
                    m = tm // MAX_ROW_STRIDE
                    for c in range(MAX_ROW_STRIDE):
                        for cb in range(tn // LANES):
                            acc_ref[1, cb, c * m:(c + 1) * m, :] = acc_ref[0, cb, pl.ds(c, m, stride=MAX_ROW_STRIDE), :]
                    src = 1
                for p in range(r):
                    start = p if src == 0 else (p % MAX_ROW_STRIDE) * (tm // MAX_ROW_STRIDE) + p // MAX_ROW_STRIDE
                    for cb in range(tn // LANES):
                        obuf[slot, p * n:(p + 1) * n, cb * LANES:(cb + 1) * LANES] = (
                            acc_ref[src, cb, pl.ds(start, n, stride=stride), :].astype(_BF))
            for p in range(r):
                pltpu.make_async_copy(obuf.at[slot, pl.ds(p * n, n), :],
                                      q_hbm.at[p, pl.ds(pl.multiple_of(i * n, n), n), cols(jj)],
                                      o_sems.at[slot]).start()
        return store

    parts = [(na, win_hbm, lambda jj: jj, store_plain(oa_hbm))]
    for gi, (q_hbm, (_, r)) in enumerate(zip((q0_hbm, q1_hbm, q2_hbm), DILATED_GROUPS)):
        parts.append((nb, win_hbm, lambda jj, gi=gi: na + jj * N_DIL + gi, store_phases(q_hbm, r)))
    parts.append((ng, wg_hbm, lambda jj: jj, store_gate))

    @pl.when(i == 0)
    def _():
        start_x(0)
        w_copy(win_hbm, 0, 0).start()

    x_copy(xp_hbm, 0).wait()
    x = xbuf[...]
    y = x * lax.rsqrt(jnp.mean(x * x, axis=-1, keepdims=True) + EPS)
    xn_ref[...] = (y * g_ref[...]).astype(_BF)

    @pl.when(i + 1 < n_row_tiles)
    def _():
        start_x(i + 1)

    base = 0
    for k, (length, w_hbm, w_col, store) in enumerate(parts):
        def tile(jj, carry, base=base, length=length, w_hbm=w_hbm, w_col=w_col, store=store, k=k):
            n = i * tiles_per_row + base + jj
            slot = n % 2

            @pl.when(jj + 1 < length)
            def _():
                w_copy(w_hbm, w_col(jj + 1), 1 - slot).start()

            @pl.when(jj + 1 == length)
            def _():
                if k + 1 < len(parts):
                    w_copy(parts[k + 1][1], parts[k + 1][2](0), 1 - slot).start()
                else:
                    @pl.when(i + 1 < n_row_tiles)
                    def _():
                        w_copy(win_hbm, 0, 1 - slot).start()

            @pl.when(n >= 2)
            def _():
                wait_staging(slot)

            w_copy(w_hbm, w_col(jj), slot).wait()
            acc = jnp.dot(xn_ref[...], wbuf[slot].astype(_BF), preferred_element_type=_F32)
            store(acc, jj, slot)
            return carry

        lax.fori_loop(0, length, tile, 0)
        base += length

    @pl.when(i == n_row_tiles - 1)
    def _():
        wait_staging(0)
        wait_staging(1)


def _project(xp, xs, g, w_in, w_gate, b_gate, *, a_cols):
    n_p, d = xp.shape
    n_s = xs.shape[0]
    t = n_p + n_s
    tm, tn = PROJ_TM, PROJ_TN
    assert tn == B_HPG * HEAD_DIM and w_in.shape[1] == a_cols + 3 * N_DIL * tn
    assert n_p % tm == 0 and n_s % tm == 0
    gate_cols = w_gate.shape[1]
    na, ng = a_cols // tn, gate_cols // tn
    any_space = pl.BlockSpec(memory_space=pl.ANY)
    return pl.pallas_call(
        functools.partial(_proj_kernel, n_prompt_tiles=n_p // tm, n_row_tiles=t // tm, tm=tm, tn=tn, na=na, ng=ng),
        grid=(t // tm,),
        in_specs=[pl.BlockSpec((1, d), lambda i: (0, 0)), pl.BlockSpec((ng, 1, tn), lambda i: (0, 0, 0)),
                  any_space, any_space, any_space, any_space],
        out_specs=[any_space] * 5,
        out_shape=[jax.ShapeDtypeStruct((t, a_cols), _BF)]
                  + [jax.ShapeDtypeStruct((r, t // r, 3 * tn), _BF) for _, r in DILATED_GROUPS]
                  + [jax.ShapeDtypeStruct((t, gate_cols), _BF)],
        scratch_shapes=[pltpu.VMEM((tm, d), xp.dtype), pltpu.VMEM((tm, d), _BF), pltpu.VMEM((2, d, tn), w_in.dtype),
                        pltpu.VMEM((2, tn // LANES, tm, LANES), _F32), pltpu.VMEM((2, tm, tn), _BF),
                        pltpu.SemaphoreType.DMA(()), pltpu.SemaphoreType.DMA((2,)), pltpu.SemaphoreType.DMA((2,))],
        compiler_params=_params("arbitrary"),
        name="norm_proj",
    )(g, b_gate.reshape(ng, 1, tn), xp, xs, w_in, w_gate)


def _band_attn_kernel(slope_ref, sink_ref, q_ref, kl_ref, km_ref, kr_ref, vl_ref, vm_ref, vr_ref,
                      *rest, bq, half, step, shared_kv, has_sink, prompt_rows, prompt_seq, sample_seq):
    if has_sink:
        o_ref, kcat, vcat = rest
        lse_ref = None
    else:
        o_ref, lse_ref, kcat, vcat = rest
    i = pl.program_id(0)
    c = pl.program_id(1)
    win = SUBQ + 2 * half

    kcat[0:half, :] = kl_ref[...]
    kcat[half:half + bq, :] = km_ref[...]
    kcat[half + bq:, :] = kr_ref[...]
    vcat[0:half, :] = vl_ref[...]
    vcat[half:half + bq, :] = vm_ref[...]
    vcat[half + bq:, :] = vr_ref[...]

    u0 = i * bq
    in_prompt = u0 < prompt_rows
    lo = jnp.where(in_prompt, (u0 // prompt_seq) * prompt_seq,
                   prompt_rows + ((u0 - prompt_rows) // sample_seq) * sample_seq)
    hi = lo + jnp.where(in_prompt, prompt_seq, sample_seq)

    qi = lax.broadcasted_iota(jnp.int32, (SUBQ, win), 0)
    kj = lax.broadcasted_iota(jnp.int32, (SUBQ, win), 1)
    rel = kj - half - qi
    absrel = jnp.abs(rel)
    in_band = absrel <= half
    neg_dist = -(absrel * step).astype(_F32)
    scale = HEAD_DIM ** -0.5
    lane = lax.broadcasted_iota(jnp.int32, (SUBQ, LANES), 1)

    heads = range(HEADS_PER_STEP)
    for sb in range(bq // SUBQ):
        kpos = kj + (u0 + sb * SUBQ - half)
        valid = in_band & (kpos >= lo) & (kpos < hi)
        bias = jnp.where(valid, neg_dist, MASKED)
        rows = slice(sb * SUBQ, (sb + 1) * SUBQ)
        if shared_kv:
            k = kcat[sb * SUBQ:sb * SUBQ + win, :]
            v = vcat[sb * SUBQ:sb * SUBQ + win, :]
            q4 = jnp.concatenate([q_ref[rows, h * HEAD_DIM:(h + 1) * HEAD_DIM] for h in heads], axis=0)
            s4 = lax.dot_general(q4, k, (((1,), (1,)), ((), ())), preferred_element_type=_F32)
            ps, ms, ls = [], [], []
            for h in heads:
                s = s4[h * SUBQ:(h + 1) * SUBQ] * scale + slope_ref[c * HEADS_PER_STEP + h] * bias
                m = jnp.max(s, axis=-1, keepdims=True)
                p = jnp.exp(s - m)
                ms.append(m)
                ls.append(jnp.sum(p, axis=-1, keepdims=True))
                ps.append(p.astype(_BF))
            pv4 = jnp.dot(jnp.concatenate(ps, axis=0), v, preferred_element_type=_F32)
            for h in heads:
                l = ls[h] + jnp.exp(sink_ref[c * HEADS_PER_STEP + h] - ms[h])
                o_ref[rows, h * HEAD_DIM:(h + 1) * HEAD_DIM] = (
                    pv4[h * SUBQ:(h + 1) * SUBQ] / l).astype(o_ref.dtype)
            continue
        lse_tile = jnp.zeros((SUBQ, LANES), _F32)
        for h in heads:
            cols = slice(h * HEAD_DIM, (h + 1) * HEAD_DIM)
            k = kcat[sb * SUBQ:sb * SUBQ + win, cols]
            v = vcat[sb * SUBQ:sb * SUBQ + win, cols]
            s = lax.dot_general(q_ref[rows, cols], k, (((1,), (1,)), ((), ())), preferred_element_type=_F32)
            s = s * scale + slope_ref[h] * bias
            m = jnp.max(s, axis=-1, keepdims=True)
            p = jnp.exp(s - m)
            l = jnp.sum(p, axis=-1, keepdims=True)
            pv = jnp.dot(p.astype(_BF), v, preferred_element_type=_F32)
            lse_tile = jnp.where(lane == h, m + jnp.log(l), lse_tile)
            o_ref[rows, cols] = (pv / l).astype(o_ref.dtype)
        lse_ref[rows, :] = lse_tile


def _band_attention(qkv, slopes, sinks, *, n_inner, bq, half, step, shared_kv, has_sink,
                    prompt_rows, prompt_seq, sample_seq, name):
    lead, rows, _ = qkv.shape
    assert prompt_seq % bq == 0 and sample_seq % bq == 0 and bq % SUBQ == 0 and bq % half == 0
    nq = rows // bq
    hb = bq // half
    last_halo = rows // half - 1
    width = HEADS_PER_STEP * HEAD_DIM
    if shared_kv:
        kv_w = HEAD_DIM
        k_base, v_base = width * n_inner // HEAD_DIM, width * n_inner // HEAD_DIM + n_inner
        lead_of = lambda c: 0
        q_col = lambda c: c
        kv_col = lambda base: (lambda c: base + c)
    else:
        kv_w = width
        k_base, v_base = 1, 2
        lead_of = lambda c: c
        q_col = lambda c: 0
        kv_col = lambda base: (lambda c: base)

    def main_map(col):
        return lambda i, c: (lead_of(c), i, col(c))

    def left_map(col):
        return lambda i, c: (lead_of(c), jnp.maximum(i * hb - 1, 0), col(c))

    def right_map(col):
        return lambda i, c: (lead_of(c), jnp.minimum((i + 1) * hb, last_halo), col(c))

    smem = pl.BlockSpec(memory_space=pltpu.SMEM)
    in_specs = [
        smem, smem,
        pl.BlockSpec((None, bq, width), main_map(q_col)),
        pl.BlockSpec((None, half, kv_w), left_map(kv_col(k_base))),
        pl.BlockSpec((None, bq, kv_w), main_map(kv_col(k_base))),
        pl.BlockSpec((None, half, kv_w), right_map(kv_col(k_base))),
        pl.BlockSpec((None, half, kv_w), left_map(kv_col(v_base))),
        pl.BlockSpec((None, bq, kv_w), main_map(kv_col(v_base))),
        pl.BlockSpec((None, half, kv_w), right_map(kv_col(v_base))),
    ]
    if has_sink:
        out_specs = pl.BlockSpec((bq, width), lambda i, c: (i, c))
        out_shape = jax.ShapeDtypeStruct((rows, n_inner * width), _BF)
    else:
        out_specs = (pl.BlockSpec((None, bq, width), lambda i, c: (c, i, 0)),
                     pl.BlockSpec((None, bq, LANES), lambda i, c: (c, i, 0)))
        out_shape = (jax.ShapeDtypeStruct((lead, rows, width), _BF),
                     jax.ShapeDtypeStruct((lead, rows, LANES), _F32))
    return pl.pallas_call(
        functools.partial(_band_attn_kernel, bq=bq, half=half, step=step, shared_kv=shared_kv,
                          has_sink=has_sink, prompt_rows=prompt_rows, prompt_seq=prompt_seq,
                          sample_seq=sample_seq),
        grid=(nq, n_inner),
        in_specs=in_specs,
        out_specs=out_specs,
        out_shape=out_shape,
        scratch_shapes=[pltpu.VMEM((bq + 2 * half, kv_w), _BF), pltpu.VMEM((bq + 2 * half, kv_w), _BF)],
        compiler_params=_params("parallel", "arbitrary"),
        name=name,
    )(slopes, sinks, qkv, qkv, qkv, qkv, qkv, qkv, qkv)


def _mix_kernel(xp_ref, xs_ref, oa_ref, ob0_ref, ob1_ref, ob2_ref, l0_ref, l1_ref, l2_ref, g_ref,
                wpa_ref, wpb_ref, wout_ref, n2_ref, wr_ref, br_ref, h_ref, hn_ref, lg_ref, o_scr, l_scr,
                *, n_prompt_tiles, d, tm):
    i = pl.program_id(0)

    def token_order(o_ref, l_ref, r, slot):
        if r == 1:
            return [o_ref[0, :, h * HEAD_DIM:(h + 1) * HEAD_DIM].astype(_F32) for h in range(B_HPG)], l_ref[0]
        for p in range(r):
            for h in range(B_HPG):
                o_scr[slot, h, pl.ds(p, tm // r, stride=r), :] = (
                    o_ref[p, :, h * HEAD_DIM:(h + 1) * HEAD_DIM].astype(_F32))
            l_scr[slot, pl.ds(p, tm // r, stride=r), :] = l_ref[p]
        return [o_scr[slot, h] for h in range(B_HPG)], l_scr[slot]

    o0, l0 = token_order(ob0_ref, l0_ref, DILATED_GROUPS[0][1], 0)
    o1, l1 = token_order(ob1_ref, l1_ref, DILATED_GROUPS[1][1], 0)
    o2, l2 = token_order(ob2_ref, l2_ref, DILATED_GROUPS[2][1], 1)
    mx = jnp.maximum(jnp.maximum(l0, l1), l2)
    e0, e1, e2 = jnp.exp(l0 - mx), jnp.exp(l1 - mx), jnp.exp(l2 - mx)
    den = e0 + e1 + e2
    a0, a1, a2 = e0 / den, e1 / den, e2 / den
    parts = []
    for h in range(B_HPG):
        parts.append(a0[:, h:h + 1] * o0[h] + a1[:, h:h + 1] * o1[h] + a2[:, h:h + 1] * o2[h])
    ob = jnp.concatenate(parts, axis=1).astype(_BF)

    ta = jnp.dot(oa_ref[...], wpa_ref[...], preferred_element_type=_F32)
    tb = jnp.dot(ob, wpb_ref[...], preferred_element_type=_F32)
    merged = g_ref[:, :d].astype(_F32) * ta + g_ref[:, d:].astype(_F32) * tb
    x = jnp.where(i < n_prompt_tiles, xp_ref[...], xs_ref[...])
    h_new = x + jnp.dot(merged.astype(_BF), wout_ref[...], preferred_element_type=_F32)
    h_ref[...] = h_new
    hn = h_new * lax.rsqrt(jnp.mean(h_new * h_new, axis=-1, keepdims=True) + EPS) * n2_ref[...]
    hn_ref[...] = _pack_halves(hn)
    hn_hi = hn.astype(_BF)
    hn_lo = (hn - hn_hi.astype(_F32)).astype(_BF)
    r = (jnp.dot(hn_hi, wr_ref[...], preferred_element_type=_F32)
         + jnp.dot(hn_lo, wr_ref[...], preferred_element_type=_F32))
    lg_ref[...] = r[:, :LANES] + r[:, LANES:] + br_ref[...]


def _mix(xp, xs, oa, obs, lses, gates, wpa, wpb, wout, n2, wr, br):
    n_p, d = xp.shape
    n_s = xs.shape[0]
    t = n_p + n_s
    tm = ROW_TILE
    npt = n_p // tm
    row = lambda i: (i, 0)
    const = lambda i: (0, 0)
    width = B_HPG * HEAD_DIM

    def resident(shape):
        return pl.BlockSpec(shape, const, pipeline_mode=pl.Buffered(1))

    def phase_blocks(cols):
        return [pl.BlockSpec((r, tm // r, cols), lambda i: (0, i, 0)) for _, r in DILATED_GROUPS]

    in_specs = [
        pl.BlockSpec((tm, d), lambda i: (jnp.minimum(i, npt - 1), 0)),
        pl.BlockSpec((tm, d), lambda i: (jnp.maximum(i - npt, 0), 0)),
        pl.BlockSpec((tm, oa.shape[1]), row),
        *phase_blocks(width), *phase_blocks(LANES),
        pl.BlockSpec((tm, 2 * d), row),
        resident(wpa.shape), resident(wpb.shape), resident(wout.shape),
        resident((1, d)), resident(wr.shape), resident((1, LANES)),
    ]
    return pl.pallas_call(
        functools.partial(_mix_kernel, n_prompt_tiles=npt, d=d, tm=tm),
        grid=(t // tm,),
        in_specs=in_specs,
        out_specs=(pl.BlockSpec((tm, d), row), pl.BlockSpec((tm, d // 2), row), pl.BlockSpec((tm, LANES), row)),
        out_shape=(jax.ShapeDtypeStruct((t, d), _F32), jax.ShapeDtypeStruct((t, d // 2), jnp.uint32),
                   jax.ShapeDtypeStruct((t, LANES), _F32)),
        scratch_shapes=[pltpu.VMEM((2, B_HPG, tm, HEAD_DIM), _F32), pltpu.VMEM((2, tm, LANES), _F32)],
        compiler_params=_params("parallel"),
        name="mix_out_router",
    )(xp, xs, oa, *obs, *lses, gates, wpa, wpb, wout, n2, wr, br)


def _route_kernel(lg_ref, ri_ref, rw_ref, cnt_ref, carry_ref, *, tr):
    i = pl.program_id(0)

    @pl.when(i == 0)
    def _():
        carry_ref[...] = jnp.zeros_like(carry_ref)

    lg = lg_ref[...]
    lane_i = lax.broadcasted_iota(jnp.int32, (tr, LANES), 1)
    lane = lane_i.astype(_F32)
    no_lane = float(LANES)
    is_grp = lane_i < N_GROUPS
    glog = jnp.where(is_grp, lg, MASKED)
    gmax = jnp.max(glog, axis=-1, keepdims=True)
    grp = jnp.min(jnp.where(glog == gmax, lane, no_lane), axis=-1, keepdims=True)
    gsum = jnp.sum(jnp.where(is_grp, jnp.exp(glog - gmax), 0.0), axis=-1, keepdims=True)
    pgrp = 1.0 / gsum
    lane_grp = ((lane_i - N_GROUPS) // EXPERTS_PER_GROUP).astype(_F32)
    in_grp = (lane_i >= N_GROUPS) & (lane_i < N_GROUPS + N_EXPERTS) & (lane_grp == grp)
    elog = jnp.where(in_grp, lg, MASKED)
    t1 = jnp.max(elog, axis=-1, keepdims=True)
    i1 = jnp.min(jnp.where(elog == t1, lane, no_lane), axis=-1, keepdims=True)
    elog2 = jnp.where(lane == i1, MASKED, elog)
    t2 = jnp.max(elog2, axis=-1, keepdims=True)
    i2 = jnp.min(jnp.where(elog2 == t2, lane, no_lane), axis=-1, keepdims=True)
    e21 = jnp.exp(t2 - t1)
    w1 = pgrp / (1.0 + e21)
    w2 = pgrp * e21 / (1.0 + e21)
    eid1 = i1 - N_GROUPS
    eid2 = i2 - N_GROUPS
    hot1 = lane == eid1
    hot2 = lane == eid2
    onehot = jnp.where(hot1, 1.0, 0.0) + jnp.where(hot2, 1.0, 0.0)
    r_i = lax.broadcasted_iota(jnp.int32, (tr, tr), 0)
    c_i = lax.broadcasted_iota(jnp.int32, (tr, tr), 1)
    lower = jnp.where(c_i < r_i, 1.0, 0.0).astype(_BF)
    before = jnp.dot(lower, onehot.astype(_BF), preferred_element_type=_F32) + carry_ref[0:1, :]
    rank1 = jnp.sum(jnp.where(hot1, before, 0.0), axis=-1, keepdims=True)
    rank2 = jnp.sum(jnp.where(hot2, before, 0.0), axis=-1, keepdims=True)
    total = carry_ref[0:1, :] + jnp.sum(onehot, axis=0, keepdims=True)
    carry_ref[...] = jnp.broadcast_to(total, carry_ref.shape)
    cnt_ref[...] = jnp.broadcast_to(total, cnt_ref.shape).astype(jnp.int32)
    ri = jnp.where(lane_i == 0, eid1, jnp.where(lane_i == 1, eid2, jnp.where(lane_i == 2, rank1,
                   jnp.where(lane_i == 3, rank2, 0.0))))
    ri_ref[...] = ri.astype(jnp.int32)
    rw_ref[...] = jnp.where(lane_i == 0, w1, jnp.where(lane_i == 1, w2, 0.0))


def _route(logits):
    t = logits.shape[0]
    tr = ROW_TILE
    row = lambda i: (i, 0)
    return pl.pallas_call(
        functools.partial(_route_kernel, tr=tr),
        grid=(t // tr,),
        in_specs=[pl.BlockSpec((tr, LANES), row)],
        out_specs=(pl.BlockSpec((tr, LANES), row), pl.BlockSpec((tr, LANES), row),
                   pl.BlockSpec((8, LANES), lambda i: (0, 0))),
        out_shape=(jax.ShapeDtypeStruct((t, LANES), jnp.int32), jax.ShapeDtypeStruct((t, LANES), _F32),
                   jax.ShapeDtypeStruct((8, LANES), jnp.int32)),
        scratch_shapes=[pltpu.VMEM((8, LANES), _F32)],
        compiler_params=_params("arbitrary"),
        name="route",
    )(logits)


def _dispatch_kernel(dest_ref, pad_start_ref, pad_n_ref, nvalid_ref, hn_ref, xr_ref, zeros, sem, pad_sem, *, td):
    i = pl.program_id(0)

    @pl.when(i == 0)
    def _():
        zeros[...] = jnp.zeros_like(zeros)

        def pad_chunk(e, r):
            row0 = pl.multiple_of(pad_start_ref[e] + r * SUBLANES, SUBLANES)
            return pltpu.make_async_copy(zeros.at[pl.ds(0, SUBLANES), :], xr_ref.at[pl.ds(row0, SUBLANES), :],
                                         pad_sem)

        def per_expert(act):
            def body(e, carry):
                lax.fori_loop(0, pad_n_ref[e], lambda r, c: (act(pad_chunk(e, r)), c)[1], 0)
                return carry
            return body

        def tail_block(b):
            row0 = pl.multiple_of(b * EXPERT_ROWS, EXPERT_ROWS)
            return pltpu.make_async_copy(zeros, xr_ref.at[pl.ds(row0, EXPERT_ROWS), :], pad_sem)

        n_blocks = xr_ref.shape[0] // EXPERT_ROWS
        lax.fori_loop(0, N_EXPERTS, per_expert(lambda cp: cp.start()), 0)
        lax.fori_loop(nvalid_ref[0], n_blocks, lambda b, c: (tail_block(b).start(), c)[1], 0)
        lax.fori_loop(0, N_EXPERTS, per_expert(lambda cp: cp.wait()), 0)
        lax.fori_loop(nvalid_ref[0], n_blocks, lambda b, c: (tail_block(b).wait(), c)[1], 0)

    def start(j, carry):
        for k in range(2):
            dst = dest_ref[2 * (i * td + j) + k]
            pltpu.make_async_copy(hn_ref.at[pl.ds(j, 1), :], xr_ref.at[pl.ds(dst, 1), :], sem).start()
        return carry

    lax.fori_loop(0, td, start, 0, unroll=DMA_UNROLL)
    for _ in range(2):
        pltpu.make_async_copy(hn_ref, xr_ref.at[pl.ds(0, td), :], sem).wait()


def _dispatch(dest, pad_start, pad_n, nvalid, hn, rows):
    t, d = hn.shape
    td = ROW_TILE
    return pl.pallas_call(
        functools.partial(_dispatch_kernel, td=td),
        grid_spec=pltpu.PrefetchScalarGridSpec(
            num_scalar_prefetch=4,
            grid=(t // td,),
            in_specs=[pl.BlockSpec((td, d), lambda i, *_: (i, 0))],
            out_specs=pl.BlockSpec(memory_space=pl.ANY),
            scratch_shapes=[pltpu.VMEM((EXPERT_ROWS, d), hn.dtype), pltpu.SemaphoreType.DMA(()),
                            pltpu.SemaphoreType.DMA(())],
        ),
        out_shape=jax.ShapeDtypeStruct((rows, d), hn.dtype),
        compiler_params=_params("arbitrary"),
        name="dispatch",
    )(dest, pad_start, pad_n, nvalid, hn)


def _expert_kernel(blk_e_ref, nvalid_ref, first_ref, slot_ref, next_e_ref, x_ref, w1_hbm, w3_hbm, w2_hbm, y_ref,
                   w1_buf, w3_buf, w2_buf, sems):
    b = pl.program_id(0)
    valid = b < nvalid_ref[0]

    def fetch(e, slot):
        return (pltpu.make_async_copy(w1_hbm.at[e], w1_buf.at[slot], sems.at[slot, 0]),
                pltpu.make_async_copy(w3_hbm.at[e], w3_buf.at[slot], sems.at[slot, 1]),
                pltpu.make_async_copy(w2_hbm.at[e], w2_buf.at[slot], sems.at[slot, 2]))

    @pl.when(b == 0)
    def _():
        for cp in fetch(blk_e_ref[0], 0):
            cp.start()

    @pl.when(valid & (first_ref[b] == 1))
    def _():
        for cp in fetch(blk_e_ref[b], slot_ref[b]):
            cp.wait()

        @pl.when(next_e_ref[b] >= 0)
        def _():
            for cp in fetch(next_e_ref[b], 1 - slot_ref[b]):
                cp.start()

    @pl.when(valid)
    def _():
        slot = slot_ref[b]
        xb = jnp.concatenate([half.astype(_BF) for half in _unpack_halves(x_ref[...])], axis=1)
        a = jnp.dot(xb, w1_buf[slot].astype(_BF), preferred_element_type=_F32)
        u = jnp.dot(xb, w3_buf[slot].astype(_BF), preferred_element_type=_F32)
        hmid = (a / (1.0 + jnp.exp(-a))) * u
        y_ref[...] = _pack_halves(jnp.dot(hmid.astype(_BF), w2_buf[slot].astype(_BF), preferred_element_type=_F32))

    @pl.when(jnp.logical_not(valid))
    def _():
        y_ref[...] = jnp.zeros_like(y_ref)


def _experts(blk_e, nvalid, first, slot, next_e, xr, w1, w3, w2):
    rows, words = xr.shape
    _, d, f = w1.shape
    assert d == 2 * words
    nblk = rows // EXPERT_ROWS
    any_space = pl.BlockSpec(memory_space=pl.ANY)
    return pl.pallas_call(
        _expert_kernel,
        grid_spec=pltpu.PrefetchScalarGridSpec(
            num_scalar_prefetch=5,
            grid=(nblk,),
            in_specs=[pl.BlockSpec((EXPERT_ROWS, words), lambda b, be, nv, *_: (jnp.minimum(b, nv[0] - 1), 0)),
                      any_space, any_space, any_space],
            out_specs=pl.BlockSpec((EXPERT_ROWS, words), lambda b, *_: (b, 0)),
            scratch_shapes=[pltpu.VMEM((2, d, f), w1.dtype), pltpu.VMEM((2, d, f), w3.dtype),
                            pltpu.VMEM((2, f, d), w2.dtype), pltpu.SemaphoreType.DMA((2, 3))],
        ),
        out_shape=jax.ShapeDtypeStruct((rows, words), jnp.uint32),
        compiler_params=_params("arbitrary"),
        name="experts",
    )(blk_e, nvalid, first, slot, next_e, xr, w1, w3, w2)


def _combine_kernel(dest_ref, h_ref, rw_ref, g_ref, yr_ref, op_ref, os_ref, ybuf, sems, *, tc, n_prompt_tiles):
    i = pl.program_id(0)
    slot = i % 2

    def gather(tile, s):
        def start(j, carry):
            for k in range(2):
                src = dest_ref[2 * (tile * tc + j) + k]
                pltpu.make_async_copy(yr_ref.at[pl.ds(src, 1), :], ybuf.at[s, k, pl.ds(j, 1), :],
                                      sems.at[s]).start()
            return carry

        lax.fori_loop(0, tc, start, 0, unroll=DMA_UNROLL)

    @pl.when(i == 0)
    def _():
        gather(0, 0)

    @pl.when(i + 1 < pl.num_programs(0))
    def _():
        gather(i + 1, 1 - slot)

    for k in range(2):
        pltpu.make_async_copy(yr_ref.at[pl.ds(0, tc), :], ybuf.at[slot, k], sems.at[slot]).wait()

    rw = rw_ref[...]
    y0 = _unpack_halves(ybuf[slot, 0])
    y1 = _unpack_halves(ybuf[slot, 1])
    y = jnp.concatenate([rw[:, 0:1] * a + rw[:, 1:2] * b for a, b in zip(y0, y1)], axis=1)
    z = h_ref[...] + y
    out = z * lax.rsqrt(jnp.mean(z * z, axis=-1, keepdims=True) + EPS) * g_ref[...]

    @pl.when(i < n_prompt_tiles)
    def _():
        op_ref[...] = out

    @pl.when(i >= n_prompt_tiles)
    def _():
        os_ref[...] = out


def _combine(dest, h, rw, g, yr, *, n_p):
    t, d = h.shape
    tc = ROW_TILE
    npt = n_p // tc
    return pl.pallas_call(
        functools.partial(_combine_kernel, tc=tc, n_prompt_tiles=npt),
        grid_spec=pltpu.PrefetchScalarGridSpec(
            num_scalar_prefetch=1,
            grid=(t // tc,),
            in_specs=[
                pl.BlockSpec((tc, d), lambda i, dest: (i, 0)),
                pl.BlockSpec((tc, LANES), lambda i, dest: (i, 0)),
                pl.BlockSpec((1, d), lambda i, dest: (0, 0)),
                pl.BlockSpec(memory_space=pl.ANY),
            ],
            out_specs=(pl.BlockSpec((tc, d), lambda i, dest: (jnp.minimum(i, npt - 1), 0)),
                       pl.BlockSpec((tc, d), lambda i, dest: (jnp.maximum(i - npt, 0), 0))),
            scratch_shapes=[pltpu.VMEM((2, 2, tc, yr.shape[1]), yr.dtype), pltpu.SemaphoreType.DMA((2,))],
        ),
        out_shape=(jax.ShapeDtypeStruct((n_p, d), _F32), jax.ShapeDtypeStruct((t - n_p, d), _F32)),
        compiler_params=_params("arbitrary"),
        name="combine_norm",
    )(dest, h, rw, g, yr)


def _layer(xp, xs, prompt_seq, sample_seq, norm1, w_in, sink, w_proj_a, w_proj_b, w_gate, b_gate, w_out,
           norm2, w_rg, b_rg, w_re, b_re, w1, w3, w2, norm_final):
    n_p, d = xp.shape
    n_s = xs.shape[0]
    t = n_p + n_s
    a_q = A_HEADS * HEAD_DIM
    a_cols = a_q + 2 * A_KV * HEAD_DIM

    proj_a, q0, q1, q2, gates = _project(xp, xs, norm1.reshape(1, d), w_in, w_gate, b_gate.reshape(1, 2 * d),
                                         a_cols=a_cols)

    oa = _band_attention(
        proj_a.reshape(1, t, a_cols), jnp.asarray(_alibi(A_HEADS)), sink.astype(_F32), n_inner=A_KV,
        bq=min(512, prompt_seq, sample_seq), half=A_HALF, step=1, shared_kv=True, has_sink=True,
        prompt_rows=n_p, prompt_seq=prompt_seq, sample_seq=sample_seq, name="attn_window")

    slopes_b = _alibi(B_HEADS)
    obs, lses = [], []
    for gi, ((w, r), qkv) in enumerate(zip(DILATED_GROUPS, (q0, q1, q2))):
        o_g, lse_g = _band_attention(
            qkv, jnp.asarray(slopes_b[gi * B_HPG:(gi + 1) * B_HPG]), jnp.zeros((B_HPG,), _F32), n_inner=r,
            bq=min(512, prompt_seq // r, sample_seq // r), half=w // (2 * r), step=r, shared_kv=False,
            has_sink=False, prompt_rows=n_p // r, prompt_seq=prompt_seq // r, sample_seq=sample_seq // r,
            name=f"attn_dilated_{r}")
        obs.append(o_g)
        lses.append(lse_g)

    n_r = N_GROUPS + N_EXPERTS
    wr = jnp.concatenate([w_rg, jnp.transpose(w_re, (1, 0, 2)).reshape(d, N_EXPERTS)], axis=1)
    wr = jnp.pad(wr, ((0, 0), (0, LANES - n_r)))
    wr_hi = wr.astype(_BF)
    wr_lo = (wr - wr_hi.astype(_F32)).astype(_BF)
    wr2 = jnp.concatenate([wr_hi, wr_lo], axis=1)
    br = jnp.pad(jnp.concatenate([b_rg, b_re.reshape(-1)]), (0, LANES - n_r)).reshape(1, LANES).astype(_F32)

    h, hn, logits = _mix(xp, xs, oa, obs, lses, gates, w_proj_a.astype(_BF), w_proj_b.astype(_BF),
                         w_out.astype(_BF), norm2.reshape(1, d), wr2, br)

    ri, rw, cnt = _route(logits)
    counts = cnt[0, :N_EXPERTS]
    eid = ri[:, 0:2]
    rank = ri[:, 2:4]
    pcounts = (counts + EXPERT_ROWS - 1) // EXPERT_ROWS * EXPERT_ROWS
    pends = jnp.cumsum(pcounts)
    pstarts = pends - pcounts
    expert_ids = jnp.arange(N_EXPERTS, dtype=jnp.int32)
    start_of = jnp.sum(jnp.where(eid[:, :, None] == expert_ids, pstarts.astype(jnp.int32), 0), axis=-1)
    dest = (start_of + rank).reshape(-1).astype(jnp.int32)
    nblk = (2 * t + N_EXPERTS * (EXPERT_ROWS - 1) + EXPERT_ROWS - 1) // EXPERT_ROWS
    nvalid = (pends[-1] // EXPERT_ROWS).astype(jnp.int32)
    blk_start = jnp.minimum(jnp.arange(nblk, dtype=jnp.int32), nvalid - 1) * EXPERT_ROWS
    blk_e = jnp.sum(pends[None, :] <= blk_start[:, None], axis=1).astype(jnp.int32)

    pad_start = (pstarts + counts) // SUBLANES * SUBLANES
    xr = _dispatch(dest, pad_start.astype(jnp.int32), ((pends - pad_start) // SUBLANES).astype(jnp.int32),
                   nvalid.reshape(1), hn, nblk * EXPERT_ROWS)
    blk_ids = jnp.arange(nblk, dtype=jnp.int32)
    first = ((blk_ids == 0) | (blk_e != jnp.roll(blk_e, 1))).astype(jnp.int32)
    slot = ((jnp.cumsum(first) - 1) % 2).astype(jnp.int32)
    later = (expert_ids[None, :] > expert_ids[:, None]) & (counts[None, :] > 0)
    next_of = jnp.min(jnp.where(later, expert_ids[None, :], N_EXPERTS), axis=1)
    next_e = jnp.sum(jnp.where(blk_e[:, None] == expert_ids, next_of, 0), axis=1)
    next_e = jnp.where(next_e < N_EXPERTS, next_e, -1).astype(jnp.int32)
    yr = _experts(blk_e, nvalid.reshape(1), first, slot, next_e, xr, w1, w3, w2)
    return _combine(dest, h, rw, norm_final.reshape(1, d), yr, n_p=n_p)


def kernel(x_prompt, x_sample, norm1, w_in, attn_sink, w_proj_a, w_proj_b, w_gate, b_gate, w_out, norm2,
           w_router_group, b_router_group, w_router_expert, b_router_expert, w_expert_gate, w_expert_up,
           w_expert_down, norm_final):
    assert norm1.shape[0] == 1, "one layer"
    d = x_prompt.shape[-1]
    xp = x_prompt.reshape(-1, d)
    xs = x_sample.reshape(-1, d)
    yp, ys = _layer(xp, xs, x_prompt.shape[1], x_sample.shape[1], norm1[0], w_in[0], attn_sink[0], w_proj_a[0],
                    w_proj_b[0], w_gate[0], b_gate[0], w_out[0], norm2[0], w_router_group[0],
                    b_router_group[0], w_router_expert[0], b_router_expert[0], w_expert_gate[0],
                    w_expert_up[0], w_expert_down[0], norm_final)
    return yp.reshape(x_prompt.shape), ys.reshape(x_sample.shape)
```

```python
import functools

import jax
import jax.numpy as jnp
import numpy as np
from jax import lax
from jax.experimental import pallas as pl
from jax.experimental.pallas import tpu as pltpu

HEAD_DIM = 128
A_HEADS = 16
A_KV = 4
A_HALF = 128
DILATED_GROUPS = ((128, 1), (512, 4), (2048, 16))
N_DIL = len(DILATED_GROUPS)
B_HPG = 4
B_HEADS = N_DIL * B_HPG
N_GROUPS = 8
EXPERTS_PER_GROUP = 8
N_EXPERTS = N_GROUPS * EXPERTS_PER_GROUP
EPS = 1e-6

LANES = 128
SUBLANES = 8
MASKED = -1e30
VMEM_LIMIT = 56 * 1024 * 1024
SUBQ = 128
HEADS_PER_STEP = 4
EXPERT_ROWS = 256
DMA_UNROLL = 8
PROJ_TM, PROJ_TN = 2048, 512
MAX_ROW_STRIDE = 4
ROW_TILE = 256

_BF = jnp.bfloat16
_F32 = jnp.float32


def _alibi(n):
    return np.power(2.0, -8.0 * (np.arange(n) + 1) / n).astype(np.float32)


def _params(*sem):
    return pltpu.CompilerParams(dimension_semantics=sem, vmem_limit_bytes=VMEM_LIMIT)


def _pack_halves(x):
    n = x.shape[1] // 2

    def bf16_bits(v):
        return lax.bitcast_convert_type(v.astype(_BF).astype(_F32), jnp.uint32)

    return bf16_bits(x[:, n:]) | (bf16_bits(x[:, :n]) >> 16)


def _unpack_halves(p):
    lo = lax.bitcast_convert_type(p << 16, _F32)
    hi = lax.bitcast_convert_type(p & jnp.uint32(0xFFFF0000), _F32)
    return lo, hi


def _proj_kernel(g_ref, b_ref, xp_hbm, xs_hbm, win_hbm, wg_hbm, oa_hbm, q0_hbm, q1_hbm, q2_hbm, gt_hbm,
                 xbuf, xn_ref, wbuf, acc_ref, obuf, x_sem, w_sems, o_sems,
                 *, n_prompt_tiles, n_row_tiles, tm, tn, na, ng):
    i = pl.program_id(0)
    nb = 3
    tiles_per_row = na + N_DIL * nb + ng
    rows = pl.ds(pl.multiple_of(i * tm, tm), tm)

    def cols(jj):
        return pl.ds(pl.multiple_of(jj * tn, tn), tn)

    def x_copy(x_hbm, row_tile):
        return pltpu.make_async_copy(x_hbm.at[pl.ds(pl.multiple_of(row_tile * tm, tm), tm), :], xbuf, x_sem)

    def start_x(row_tile):
        @pl.when(row_tile < n_prompt_tiles)
        def _():
            x_copy(xp_hbm, row_tile).start()

        @pl.when(row_tile >= n_prompt_tiles)
        def _():
            x_copy(xs_hbm, row_tile - n_prompt_tiles).start()

    def w_copy(w_hbm, col, slot):
        return pltpu.make_async_copy(w_hbm.at[:, cols(col)], wbuf.at[slot], w_sems.at[slot])

    def wait_staging(slot):
        pltpu.make_async_copy(obuf.at[slot], oa_hbm.at[pl.ds(0, tm), pl.ds(0, tn)], o_sems.at[slot]).wait()

    def store_plain(out_hbm):
        def store(acc, jj, slot):
            obuf[slot] = acc.astype(_BF)
            pltpu.make_async_copy(obuf.at[slot], out_hbm.at[rows, cols(jj)], o_sems.at[slot]).start()
        return store

    def store_gate(acc, jj, slot):
        z = acc + b_ref[jj]
        obuf[slot] = (0.5 * jnp.tanh(0.5 * z) + 0.5).astype(_BF)
        pltpu.make_async_copy(obuf.at[slot], gt_hbm.at[rows, cols(jj)], o_sems.at[slot]).start()

    def store_phases(q_hbm, r):
        n = tm // r

        def store(acc, jj, slot):
            if r == 1:
                obuf[slot] = acc.astype(_BF)
            else:
                for cb in range(tn // LANES):
                    acc_ref[0, cb] = acc[:, cb * LANES:(cb + 1) * LANES]
                src, stride = 0, r
                if r > MAX_ROW_STRIDE:
                    stride = r // MAX_ROW_STRIDE
                    m = tm // MAX_ROW_STRIDE
                    for c in range(MAX_ROW_STRIDE):
                        for cb in range(tn // LANES):
                            acc_ref[1, cb, c * m:(c + 1) * m, :] = acc_ref[0, cb, pl.ds(c, m, stride=MAX_ROW_STRIDE), :]
                    src = 1
                for p in range(r):
                    start = p if src == 0 else (p % MAX_ROW_STRIDE) * (tm // MAX_ROW_STRIDE) + p // MAX_ROW_STRIDE
                    for cb in range(tn // LANES):
                        obuf[slot, p * n:(p + 1) * n, cb * LANES:(cb + 1) * LANES] = (
                            acc_ref[src, cb, pl.ds(start, n, stride=stride), :].astype(_BF))
            for p in range(r):
                pltpu.make_async_copy(obuf.at[slot, pl.ds(p * n, n), :],
                                      q_hbm.at[p, pl.ds(pl.multiple_of(i * n, n), n), cols(jj)],
                                      o_sems.at[slot]).start()
        return store

    parts = [(na, win_hbm, lambda jj: jj, store_plain(oa_hbm))]
    for gi, (q_hbm, (_, r)) in enumerate(zip((q0_hbm, q1_hbm, q2_hbm), DILATED_GROUPS)):
        parts.append((nb, win_hbm, lambda jj, gi=gi: na + jj * N_DIL + gi, store_phases(q_hbm, r)))
    parts.append((ng, wg_hbm, lambda jj: jj, store_gate))

    @pl.when(i == 0)
    def _():
        start_x(0)
        w_copy(win_hbm, 0, 0).start()

    x_copy(xp_hbm, 0).wait()
    x = xbuf[...]
    y = x * lax.rsqrt(jnp.mean(x * x, axis=-1, keepdims=True) + EPS)
    xn_ref[...] = (y * g_ref[...]).astype(_BF)

    @pl.when(i + 1 < n_row_tiles)
    def _():
        start_x(i + 1)

    base = 0
    for k, (length, w_hbm, w_col, store) in enumerate(parts):
        def tile(jj, carry, base=base, length=length, w_hbm=w_hbm, w_col=w_col, store=store, k=k):
            n = i * tiles_per_row + base + jj
            slot = n % 2

            @pl.when(jj + 1 < length)
            def _():
                w_copy(w_hbm, w_col(jj + 1), 1 - slot).start()

            @pl.when(jj + 1 == length)
            def _():
                if k + 1 < len(parts):
                    w_copy(parts[k + 1][1], parts[k + 1][2](0), 1 - slot).start()
                else:
                    @pl.when(i + 1 < n_row_tiles)
                    def _():
                        w_copy(win_hbm, 0, 1 - slot).start()

            @pl.when(n >= 2)
            def _():
                wait_staging(slot)

            w_copy(w_hbm, w_col(jj), slot).wait()
            acc = jnp.dot(xn_ref[...], wbuf[slot].astype(_BF), preferred_element_type=_F32)
            store(acc, jj, slot)
            return carry

        lax.fori_loop(0, length, tile, 0)
        base += length

    @pl.when(i == n_row_tiles - 1)
    def _():
        wait_staging(0)
        wait_staging(1)


def _project(xp, xs, g, w_in, w_gate, b_gate, *, a_cols):
    n_p, d = xp.shape
    n_s = xs.shape[0]
    t = n_p + n_s
    tm, tn = PROJ_TM, PROJ_TN
    assert tn == B_HPG * HEAD_DIM and w_in.shape[1] == a_cols + 3 * N_DIL * tn
    assert n_p % tm == 0 and n_s % tm == 0
    gate_cols = w_gate.shape[1]
    na, ng = a_cols // tn, gate_cols // tn
    any_space = pl.BlockSpec(memory_space=pl.ANY)
    return pl.pallas_call(
        functools.partial(_proj_kernel, n_prompt_tiles=n_p // tm, n_row_tiles=t // tm, tm=tm, tn=tn, na=na, ng=ng),
        grid=(t // tm,),
        in_specs=[pl.BlockSpec((1, d), lambda i: (0, 0)), pl.BlockSpec((ng, 1, tn), lambda i: (0, 0, 0)),
                  any_space, any_space, any_space, any_space],
        out_specs=[any_space] * 5,
        out_shape=[jax.ShapeDtypeStruct((t, a_cols), _BF)]
                  + [jax.ShapeDtypeStruct((r, t // r, 3 * tn), _BF) for _, r in DILATED_GROUPS]
                  + [jax.ShapeDtypeStruct((t, gate_cols), _BF)],
        scratch_shapes=[pltpu.VMEM((tm, d), xp.dtype), pltpu.VMEM((tm, d), _BF), pltpu.VMEM((2, d, tn), w_in.dtype),
                        pltpu.VMEM((2, tn // LANES, tm, LANES), _F32), pltpu.VMEM((2, tm, tn), _BF),
                        pltpu.SemaphoreType.DMA(()), pltpu.SemaphoreType.DMA((2,)), pltpu.SemaphoreType.DMA((2,))],
        compiler_params=_params("arbitrary"),
        name="norm_proj",
    )(g, b_gate.reshape(ng, 1, tn), xp, xs, w_in, w_gate)


def _band_attn_kernel(slope_ref, sink_ref, q_ref, kl_ref, km_ref, kr_ref, vl_ref, vm_ref, vr_ref,
                      *rest, bq, half, step, shared_kv, has_sink, prompt_rows, prompt_seq, sample_seq):
    if has_sink:
        o_ref, kcat, vcat = rest
        lse_ref = None
    else:
        o_ref, lse_ref, kcat, vcat = rest
    i = pl.program_id(0)
    c = pl.program_id(1)
    win = SUBQ + 2 * half

    kcat[0:half, :] = kl_ref[...]
    kcat[half:half + bq, :] = km_ref[...]
    kcat[half + bq:, :] = kr_ref[...]
    vcat[0:half, :] = vl_ref[...]
    vcat[half:half + bq, :] = vm_ref[...]
    vcat[half + bq:, :] = vr_ref[...]

    u0 = i * bq
    in_prompt = u0 < prompt_rows
    lo = jnp.where(in_prompt, (u0 // prompt_seq) * prompt_seq,
                   prompt_rows + ((u0 - prompt_rows) // sample_seq) * sample_seq)
    hi = lo + jnp.where(in_prompt, prompt_seq, sample_seq)

    qi = lax.broadcasted_iota(jnp.int32, (SUBQ, win), 0)
    kj = lax.broadcasted_iota(jnp.int32, (SUBQ, win), 1)
    rel = kj - half - qi
    absrel = jnp.abs(rel)
    in_band = absrel <= half
    neg_dist = -(absrel * step).astype(_F32)
    scale = HEAD_DIM ** -0.5
    lane = lax.broadcasted_iota(jnp.int32, (SUBQ, LANES), 1)

    heads = range(HEADS_PER_STEP)
    for sb in range(bq // SUBQ):
        kpos = kj + (u0 + sb * SUBQ - half)
        valid = in_band & (kpos >= lo) & (kpos < hi)
        bias = jnp.where(valid, neg_dist, MASKED)
        rows = slice(sb * SUBQ, (sb + 1) * SUBQ)
        if shared_kv:
            k = kcat[sb * SUBQ:sb * SUBQ + win, :]
            v = vcat[sb * SUBQ:sb * SUBQ + win, :]
            q4 = jnp.concatenate([q_ref[rows, h * HEAD_DIM:(h + 1) * HEAD_DIM] for h in heads], axis=0)
            s4 = lax.dot_general(q4, k, (((1,), (1,)), ((), ())), preferred_element_type=_F32)
            ps, ms, ls = [], [], []
            for h in heads:
                s = s4[h * SUBQ:(h + 1) * SUBQ] * scale + slope_ref[c * HEADS_PER_STEP + h] * bias
                m = jnp.max(s, axis=-1, keepdims=True)
                p = jnp.exp(s - m)
                ms.append(m)
                ls.append(jnp.sum(p, axis=-1, keepdims=True))
                ps.append(p.astype(_BF))
            pv4 = jnp.dot(jnp.concatenate(ps, axis=0), v, preferred_element_type=_F32)
            for h in heads:
                l = ls[h] + jnp.exp(sink_ref[c * HEADS_PER_STEP + h] - ms[h])
                o_ref[rows, h * HEAD_DIM:(h + 1) * HEAD_DIM] = (
                    pv4[h * SUBQ:(h + 1) * SUBQ] / l).astype(o_ref.dtype)
            continue
        lse_tile = jnp.zeros((SUBQ, LANES), _F32)
        for h in heads:
            cols = slice(h * HEAD_DIM, (h + 1) * HEAD_DIM)
            k = kcat[sb * SUBQ:sb * SUBQ + win, cols]
            v = vcat[sb * SUBQ:sb * SUBQ + win, cols]
            s = lax.dot_general(q_ref[rows, cols], k, (((1,), (1,)), ((), ())), preferred_element_type=_F32)
            s = s * scale + slope_ref[h] * bias
            m = jnp.max(s, axis=-1, keepdims=True)
            p = jnp.exp(s - m)
            l = jnp.sum(p, axis=-1, keepdims=True)
            pv = jnp.dot(p.astype(_BF), v, preferred_element_type=_F32)
            lse_tile = jnp.where(lane == h, m + jnp.log(l), lse_tile)
            o_ref[rows, cols] = (pv / l).astype(o_ref.dtype)
        lse_ref[rows, :] = lse_tile


def _band_attention(qkv, slopes, sinks, *, n_inner, bq, half, step, shared_kv, has_sink,
                    prompt_rows, prompt_seq, sample_seq, name):
    lead, rows, _ = qkv.shape
    assert prompt_seq % bq == 0 and sample_seq % bq == 0 and bq % SUBQ == 0 and bq % half == 0
    nq = rows // bq
    hb = bq // half
    last_halo = rows // half - 1
    width = HEADS_PER_STEP * HEAD_DIM
    if shared_kv:
        kv_w = HEAD_DIM
        k_base, v_base = width * n_inner // HEAD_DIM, width * n_inner // HEAD_DIM + n_inner
        lead_of = lambda c: 0
        q_col = lambda c: c
        kv_col = lambda base: (lambda c: base + c)
    else:
        kv_w = width
        k_base, v_base = 1, 2
        lead_of = lambda c: c
        q_col = lambda c: 0
        kv_col = lambda base: (lambda c: base)

    def main_map(col):
        return lambda i, c: (lead_of(c), i, col(c))

    def left_map(col):
        return lambda i, c: (lead_of(c), jnp.maximum(i * hb - 1, 0), col(c))

    def right_map(col):
        return lambda i, c: (lead_of(c), jnp.minimum((i + 1) * hb, last_halo), col(c))

    smem = pl.BlockSpec(memory_space=pltpu.SMEM)
    in_specs = [
        smem, smem,
        pl.BlockSpec((None, bq, width), main_map(q_col)),
        pl.BlockSpec((None, half, kv_w), left_map(kv_col(k_base))),
        pl.BlockSpec((None, bq, kv_w), main_map(kv_col(k_base))),
        pl.BlockSpec((None, half, kv_w), right_map(kv_col(k_base))),
        pl.BlockSpec((None, half, kv_w), left_map(kv_col(v_base))),
        pl.BlockSpec((None, bq, kv_w), main_map(kv_col(v_base))),
        pl.BlockSpec((None, half, kv_w), right_map(kv_col(v_base))),
    ]
    if has_sink:
        out_specs = pl.BlockSpec((bq, width), lambda i, c: (i, c))
        out_shape = jax.ShapeDtypeStruct((rows, n_inner * width), _BF)
    else:
        out_specs = (pl.BlockSpec((None, bq, width), lambda i, c: (c, i, 0)),
                     pl.BlockSpec((None, bq, LANES), lambda i, c: (c, i, 0)))
        out_shape = (jax.ShapeDtypeStruct((lead, rows, width), _BF),
                     jax.ShapeDtypeStruct((lead, rows, LANES), _F32))
    return pl.pallas_call(
        functools.partial(_band_attn_kernel, bq=bq, half=half, step=step, shared_kv=shared_kv,
                          has_sink=has_sink, prompt_rows=prompt_rows, prompt_seq=prompt_seq,
                          sample_seq=sample_seq),
        grid=(nq, n_inner),
        in_specs=in_specs,
        out_specs=out_specs,
        out_shape=out_shape,
        scratch_shapes=[pltpu.VMEM((bq + 2 * half, kv_w), _BF), pltpu.VMEM((bq + 2 * half, kv_w), _BF)],
        compiler_params=_params("parallel", "arbitrary"),
        name=name,
    )(slopes, sinks, qkv, qkv, qkv, qkv, qkv, qkv, qkv)


def _mix_kernel(xp_ref, xs_ref, oa_ref, ob0_ref, ob1_ref, ob2_ref, l0_ref, l1_ref, l2_ref, g_ref,
                wpa_ref, wpb_ref, wout_ref, n2_ref, wr_ref, br_ref, h_ref, hn_ref, lg_ref, o_scr, l_scr,
                *, n_prompt_tiles, d, tm):
    i = pl.program_id(0)

    def token_order(o_ref, l_ref, r, slot):
        if r == 1:
            return [o_ref[0, :, h * HEAD_DIM:(h + 1) * HEAD_DIM].astype(_F32) for h in range(B_HPG)], l_ref[0]
        for p in range(r):
            for h in range(B_HPG):
                o_scr[slot, h, pl.ds(p, tm // r, stride=r), :] = (
                    o_ref[p, :, h * HEAD_DIM:(h + 1) * HEAD_DIM].astype(_F32))
            l_scr[slot, pl.ds(p, tm // r, stride=r), :] = l_ref[p]
        return [o_scr[slot, h] for h in range(B_HPG)], l_scr[slot]

    o0, l0 = token_order(ob0_ref, l0_ref, DILATED_GROUPS[0][1], 0)
    o1, l1 = token_order(ob1_ref, l1_ref, DILATED_GROUPS[1][1], 0)
    o2, l2 = token_order(ob2_ref, l2_ref, DILATED_GROUPS[2][1], 1)
    mx = jnp.maximum(jnp.maximum(l0, l1), l2)
    e0, e1, e2 = jnp.exp(l0 - mx), jnp.exp(l1 - mx), jnp.exp(l2 - mx)
    den = e0 + e1 + e2
    a0, a1, a2 = e0 / den, e1 / den, e2 / den
    parts = []
    for h in range(B_HPG):
        parts.append(a0[:, h:h + 1] * o0[h] + a1[:, h:h + 1] * o1[h] + a2[:, h:h + 1] * o2[h])
    ob = jnp.concatenate(parts, axis=1).astype(_BF)

    ta = jnp.dot(oa_ref[...], wpa_ref[...], preferred_element_type=_F32)
    tb = jnp.dot(ob, wpb_ref[...], preferred_element_type=_F32)
    merged = g_ref[:, :d].astype(_F32) * ta + g_ref[:, d:].astype(_F32) * tb
    x = jnp.where(i < n_prompt_tiles, xp_ref[...], xs_ref[...])
    h_new = x + jnp.dot(merged.astype(_BF), wout_ref[...], preferred_element_type=_F32)
    h_ref[...] = h_new
    hn = h_new * lax.rsqrt(jnp.mean(h_new * h_new, axis=-1, keepdims=True) + EPS) * n2_ref[...]
    hn_ref[...] = _pack_halves(hn)
    hn_hi = hn.astype(_BF)
    hn_lo = (hn - hn_hi.astype(_F32)).astype(_BF)
    r = (jnp.dot(hn_hi, wr_ref[...], preferred_element_type=_F32)
         + jnp.dot(hn_lo, wr_ref[...], preferred_element_type=_F32))
    lg_ref[...] = r[:, :LANES] + r[:, LANES:] + br_ref[...]


def _mix(xp, xs, oa, obs, lses, gates, wpa, wpb, wout, n2, wr, br):
    n_p, d = xp.shape
    n_s = xs.shape[0]
    t = n_p + n_s
    tm = ROW_TILE
    npt = n_p // tm
    row = lambda i: (i, 0)
    const = lambda i: (0, 0)
    width = B_HPG * HEAD_DIM

    def resident(shape):
        return pl.BlockSpec(shape, const, pipeline_mode=pl.Buffered(1))

    def phase_blocks(cols):
        return [pl.BlockSpec((r, tm // r, cols), lambda i: (0, i, 0)) for _, r in DILATED_GROUPS]

    in_specs = [
        pl.BlockSpec((tm, d), lambda i: (jnp.minimum(i, npt - 1), 0)),
        pl.BlockSpec((tm, d), lambda i: (jnp.maximum(i - npt, 0), 0)),
        pl.BlockSpec((tm, oa.shape[1]), row),
        *phase_blocks(width), *phase_blocks(LANES),
        pl.BlockSpec((tm, 2 * d), row),
        resident(wpa.shape), resident(wpb.shape), resident(wout.shape),
        resident((1, d)), resident(wr.shape), resident((1, LANES)),
    ]
    return pl.pallas_call(
        functools.partial(_mix_kernel, n_prompt_tiles=npt, d=d, tm=tm),
        grid=(t // tm,),
        in_specs=in_specs,
        out_specs=(pl.BlockSpec((tm, d), row), pl.BlockSpec((tm, d // 2), row), pl.BlockSpec((tm, LANES), row)),
        out_shape=(jax.ShapeDtypeStruct((t, d), _F32), jax.ShapeDtypeStruct((t, d // 2), jnp.uint32),
                   jax.ShapeDtypeStruct((t, LANES), _F32)),
        scratch_shapes=[pltpu.VMEM((2, B_HPG, tm, HEAD_DIM), _F32), pltpu.VMEM((2, tm, LANES), _F32)],
        compiler_params=_params("parallel"),
        name="mix_out_router",
    )(xp, xs, oa, *obs, *lses, gates, wpa, wpb, wout, n2, wr, br)


def _route_kernel(lg_ref, ri_ref, rw_ref, cnt_ref, carry_ref, *, tr):
    i = pl.program_id(0)

    @pl.when(i == 0)
    def _():
        carry_ref[...] = jnp.zeros_like(carry_ref)

    lg = lg_ref[...]
    lane_i = lax.broadcasted_iota(jnp.int32, (tr, LANES), 1)
    lane = lane_i.astype(_F32)
    no_lane = float(LANES)
    is_grp = lane_i < N_GROUPS
    glog = jnp.where(is_grp, lg, MASKED)
    gmax = jnp.max(glog, axis=-1, keepdims=True)
    grp = jnp.min(jnp.where(glog == gmax, lane, no_lane), axis=-1, keepdims=True)
    gsum = jnp.sum(jnp.where(is_grp, jnp.exp(glog - gmax), 0.0), axis=-1, keepdims=True)
    pgrp = 1.0 / gsum
    lane_grp = ((lane_i - N_GROUPS) // EXPERTS_PER_GROUP).astype(_F32)
    in_grp = (lane_i >= N_GROUPS) & (lane_i < N_GROUPS + N_EXPERTS) & (lane_grp == grp)
    elog = jnp.where(in_grp, lg, MASKED)
    t1 = jnp.max(elog, axis=-1, keepdims=True)
    i1 = jnp.min(jnp.where(elog == t1, lane, no_lane), axis=-1, keepdims=True)
    elog2 = jnp.where(lane == i1, MASKED, elog)
    t2 = jnp.max(elog2, axis=-1, keepdims=True)
    i2 = jnp.min(jnp.where(elog2 == t2, lane, no_lane), axis=-1, keepdims=True)
    e21 = jnp.exp(t2 - t1)
    w1 = pgrp / (1.0 + e21)
    w2 = pgrp * e21 / (1.0 + e21)
    eid1 = i1 - N_GROUPS
    eid2 = i2 - N_GROUPS
    hot1 = lane == eid1
    hot2 = lane == eid2
    onehot = jnp.where(hot1, 1.0, 0.0) + jnp.where(hot2, 1.0, 0.0)
    r_i = lax.broadcasted_iota(jnp.int32, (tr, tr), 0)
    c_i = lax.broadcasted_iota(jnp.int32, (tr, tr), 1)
    lower = jnp.where(c_i < r_i, 1.0, 0.0).astype(_BF)
    before = jnp.dot(lower, onehot.astype(_BF), preferred_element_type=_F32) + carry_ref[0:1, :]
    rank1 = jnp.sum(jnp.where(hot1, before, 0.0), axis=-1, keepdims=True)
    rank2 = jnp.sum(jnp.where(hot2, before, 0.0), axis=-1, keepdims=True)
    total = carry_ref[0:1, :] + jnp.sum(onehot, axis=0, keepdims=True)
    carry_ref[...] = jnp.broadcast_to(total, carry_ref.shape)
    cnt_ref[...] = jnp.broadcast_to(total, cnt_ref.shape).astype(jnp.int32)
    ri = jnp.where(lane_i == 0, eid1, jnp.where(lane_i == 1, eid2, jnp.where(lane_i == 2, rank1,
                   jnp.where(lane_i == 3, rank2, 0.0))))
    ri_ref[...] = ri.astype(jnp.int32)
    rw_ref[...] = jnp.where(lane_i == 0, w1, jnp.where(lane_i == 1, w2, 0.0))


def _route(logits):
    t = logits.shape[0]
    tr = ROW_TILE
    row = lambda i: (i, 0)
    return pl.pallas_call(
        functools.partial(_route_kernel, tr=tr),
        grid=(t // tr,),
        in_specs=[pl.BlockSpec((tr, LANES), row)],
        out_specs=(pl.BlockSpec((tr, LANES), row), pl.BlockSpec((tr, LANES), row),
                   pl.BlockSpec((8, LANES), lambda i: (0, 0))),
        out_shape=(jax.ShapeDtypeStruct((t, LANES), jnp.int32), jax.ShapeDtypeStruct((t, LANES), _F32),
                   jax.ShapeDtypeStruct((8, LANES), jnp.int32)),
        scratch_shapes=[pltpu.VMEM((8, LANES), _F32)],
        compiler_params=_params("arbitrary"),
        name="route",
    )(logits)


def _dispatch_kernel(dest_ref, pad_start_ref, pad_n_ref, nvalid_ref, hn_ref, xr_ref, zeros, sem, pad_sem, *, td):
    i = pl.program_id(0)

    @pl.when(i == 0)
    def _():
        zeros[...] = jnp.zeros_like(zeros)

        def pad_chunk(e, r):
            row0 = pl.multiple_of(pad_start_ref[e] + r * SUBLANES, SUBLANES)
            return pltpu.make_async_copy(zeros.at[pl.ds(0, SUBLANES), :], xr_ref.at[pl.ds(row0, SUBLANES), :],
                                         pad_sem)

        def per_expert(act):
            def body(e, carry):
                lax.fori_loop(0, pad_n_ref[e], lambda r, c: (act(pad_chunk(e, r)), c)[1], 0)
                return carry
            return body

        def tail_block(b):
            row0 = pl.multiple_of(b * EXPERT_ROWS, EXPERT_ROWS)
            return pltpu.make_async_copy(zeros, xr_ref.at[pl.ds(row0, EXPERT_ROWS), :], pad_sem)

        n_blocks = xr_ref.shape[0] // EXPERT_ROWS
        lax.fori_loop(0, N_EXPERTS, per_expert(lambda cp: cp.start()), 0)
        lax.fori_loop(nvalid_ref[0], n_blocks, lambda b, c: (tail_block(b).start(), c)[1], 0)
        lax.fori_loop(0, N_EXPERTS, per_expert(lambda cp: cp.wait()), 0)
        lax.fori_loop(nvalid_ref[0], n_blocks, lambda b, c: (tail_block(b).wait(), c)[1], 0)

    def start(j, carry):
        for k in range(2):
            dst = dest_ref[2 * (i * td + j) + k]
            pltpu.make_async_copy(hn_ref.at[pl.ds(j, 1), :], xr_ref.at[pl.ds(dst, 1), :], sem).start()
        return carry

    lax.fori_loop(0, td, start, 0, unroll=DMA_UNROLL)
    for _ in range(2):
        pltpu.make_async_copy(hn_ref, xr_ref.at[pl.ds(0, td), :], sem).wait()


def _dispatch(dest, pad_start, pad_n, nvalid, hn, rows):
    t, d = hn.shape
    td = ROW_TILE
    return pl.pallas_call(
        functools.partial(_dispatch_kernel, td=td),
        grid_spec=pltpu.PrefetchScalarGridSpec(
            num_scalar_prefetch=4,
            grid=(t // td,),
            in_specs=[pl.BlockSpec((td, d), lambda i, *_: (i, 0))],
            out_specs=pl.BlockSpec(memory_space=pl.ANY),
            scratch_shapes=[pltpu.VMEM((EXPERT_ROWS, d), hn.dtype), pltpu.SemaphoreType.DMA(()),
                            pltpu.SemaphoreType.DMA(())],
        ),
        out_shape=jax.ShapeDtypeStruct((rows, d), hn.dtype),
        compiler_params=_params("arbitrary"),
        name="dispatch",
    )(dest, pad_start, pad_n, nvalid, hn)


def _expert_kernel(blk_e_ref, nvalid_ref, first_ref, slot_ref, next_e_ref, x_ref, w1_hbm, w3_hbm, w2_hbm, y_ref,
                   w1_buf, w3_buf, w2_buf, sems):
    b = pl.program_id(0)
    valid = b < nvalid_ref[0]

    def fetch(e, slot):
        return (pltpu.make_async_copy(w1_hbm.at[e], w1_buf.at[slot], sems.at[slot, 0]),
                pltpu.make_async_copy(w3_hbm.at[e], w3_buf.at[slot], sems.at[slot, 1]),
                pltpu.make_async_copy(w2_hbm.at[e], w2_buf.at[slot], sems.at[slot, 2]))

    @pl.when(b == 0)
    def _():
        for cp in fetch(blk_e_ref[0], 0):
            cp.start()

    @pl.when(valid & (first_ref[b] == 1))
    def _():
        for cp in fetch(blk_e_ref[b], slot_ref[b]):
            cp.wait()

        @pl.when(next_e_ref[b] >= 0)
        def _():
            for cp in fetch(next_e_ref[b], 1 - slot_ref[b]):
                cp.start()

    @pl.when(valid)
    def _():
        slot = slot_ref[b]
        xb = jnp.concatenate([half.astype(_BF) for half in _unpack_halves(x_ref[...])], axis=1)
        a = jnp.dot(xb, w1_buf[slot].astype(_BF), preferred_element_type=_F32)
        u = jnp.dot(xb, w3_buf[slot].astype(_BF), preferred_element_type=_F32)
        hmid = (a / (1.0 + jnp.exp(-a))) * u
        y_ref[...] = _pack_halves(jnp.dot(hmid.astype(_BF), w2_buf[slot].astype(_BF), preferred_element_type=_F32))

    @pl.when(jnp.logical_not(valid))
    def _():
        y_ref[...] = jnp.zeros_like(y_ref)


def _experts(blk_e, nvalid, first, slot, next_e, xr, w1, w3, w2):
    rows, words = xr.shape
    _, d, f = w1.shape
    assert d == 2 * words
    nblk = rows // EXPERT_ROWS
    any_space = pl.BlockSpec(memory_space=pl.ANY)
    return pl.pallas_call(
        _expert_kernel,
        grid_spec=pltpu.PrefetchScalarGridSpec(
            num_scalar_prefetch=5,
            grid=(nblk,),
            in_specs=[pl.BlockSpec((EXPERT_ROWS, words), lambda b, be, nv, *_: (jnp.minimum(b, nv[0] - 1), 0)),
                      any_space, any_space, any_space],
            out_specs=pl.BlockSpec((EXPERT_ROWS, words), lambda b, *_: (b, 0)),
            scratch_shapes=[pltpu.VMEM((2, d, f), w1.dtype), pltpu.VMEM((2, d, f), w3.dtype),
                            pltpu.VMEM((2, f, d), w2.dtype), pltpu.SemaphoreType.DMA((2, 3))],
        ),
        out_shape=jax.ShapeDtypeStruct((rows, words), jnp.uint32),
        compiler_params=_params("arbitrary"),
        name="experts",
    )(blk_e, nvalid, first, slot, next_e, xr, w1, w3, w2)


def _combine_kernel(dest_ref, h_ref, rw_ref, g_ref, yr_ref, op_ref, os_ref, ybuf, sems, *, tc, n_prompt_tiles):
    i = pl.program_id(0)
    slot = i % 2

    def gather(tile, s):
        def start(j, carry):
            for k in range(2):
                src = dest_ref[2 * (tile * tc + j) + k]
                pltpu.make_async_copy(yr_ref.at[pl.ds(src, 1), :], ybuf.at[s, k, pl.ds(j, 1), :],
                                      sems.at[s]).start()
            return carry

        lax.fori_loop(0, tc, start, 0, unroll=DMA_UNROLL)

    @pl.when(i == 0)
    def _():
        gather(0, 0)

    @pl.when(i + 1 < pl.num_programs(0))
    def _():
        gather(i + 1, 1 - slot)

    for k in range(2):
        pltpu.make_async_copy(yr_ref.at[pl.ds(0, tc), :], ybuf.at[slot, k], sems.at[slot]).wait()

    rw = rw_ref[...]
    y0 = _unpack_halves(ybuf[slot, 0])
    y1 = _unpack_halves(ybuf[slot, 1])
    y = jnp.concatenate([rw[:, 0:1] * a + rw[:, 1:2] * b for a, b in zip(y0, y1)], axis=1)
    z = h_ref[...] + y
    out = z * lax.rsqrt(jnp.mean(z * z, axis=-1, keepdims=True) + EPS) * g_ref[...]

    @pl.when(i < n_prompt_tiles)
    def _():
        op_ref[...] = out

    @pl.when(i >= n_prompt_tiles)
    def _():
        os_ref[...] = out


def _combine(dest, h, rw, g, yr, *, n_p):
    t, d = h.shape
    tc = ROW_TILE
    npt = n_p // tc
    return pl.pallas_call(
        functools.partial(_combine_kernel, tc=tc, n_prompt_tiles=npt),
        grid_spec=pltpu.PrefetchScalarGridSpec(
            num_scalar_prefetch=1,
            grid=(t // tc,),
            in_specs=[
                pl.BlockSpec((tc, d), lambda i, dest: (i, 0)),
                pl.BlockSpec((tc, LANES), lambda i, dest: (i, 0)),
                pl.BlockSpec((1, d), lambda i, dest: (0, 0)),
                pl.BlockSpec(memory_space=pl.ANY),
            ],
            out_specs=(pl.BlockSpec((tc, d), lambda i, dest: (jnp.minimum(i, npt - 1), 0)),
                       pl.BlockSpec((tc, d), lambda i, dest: (jnp.maximum(i - npt, 0), 0))),
            scratch_shapes=[pltpu.VMEM((2, 2, tc, yr.shape[1]), yr.dtype), pltpu.SemaphoreType.DMA((2,))],
        ),
        out_shape=(jax.ShapeDtypeStruct((n_p, d), _F32), jax.ShapeDtypeStruct((t - n_p, d), _F32)),
        compiler_params=_params("arbitrary"),
        name="combine_norm",
    )(dest, h, rw, g, yr)


def _layer(xp, xs, prompt_seq, sample_seq, norm1, w_in, sink, w_proj_a, w_proj_b, w_gate, b_gate, w_out,
           norm2, w_rg, b_rg, w_re, b_re, w1, w3, w2, norm_final):
    n_p, d = xp.shape
    n_s = xs.shape[0]
    t = n_p + n_s
    a_q = A_HEADS * HEAD_DIM
    a_cols = a_q + 2 * A_KV * HEAD_DIM

    proj_a, q0, q1, q2, gates = _project(xp, xs, norm1.reshape(1, d), w_in, w_gate, b_gate.reshape(1, 2 * d),
                                         a_cols=a_cols)

    oa = _band_attention(
        proj_a.reshape(1, t, a_cols), jnp.asarray(_alibi(A_HEADS)), sink.astype(_F32), n_inner=A_KV,
        bq=min(512, prompt_seq, sample_seq), half=A_HALF, step=1, shared_kv=True, has_sink=True,
        prompt_rows=n_p, prompt_seq=prompt_seq, sample_seq=sample_seq, name="attn_window")

    slopes_b = _alibi(B_HEADS)
    obs, lses = [], []
    for gi, ((w, r), qkv) in enumerate(zip(DILATED_GROUPS, (q0, q1, q2))):
        o_g, lse_g = _band_attention(
            qkv, jnp.asarray(slopes_b[gi * B_HPG:(gi + 1) * B_HPG]), jnp.zeros((B_HPG,), _F32), n_inner=r,
            bq=min(512, prompt_seq // r, sample_seq // r), half=w // (2 * r), step=r, shared_kv=False,
            has_sink=False, prompt_rows=n_p // r, prompt_seq=prompt_seq // r, sample_seq=sample_seq // r,
            name=f"attn_dilated_{r}")
        obs.append(o_g)
        lses.append(lse_g)

    n_r = N_GROUPS + N_EXPERTS
    wr = jnp.concatenate([w_rg, jnp.transpose(w_re, (1, 0, 2)).reshape(d, N_EXPERTS)], axis=1)
    wr = jnp.pad(wr, ((0, 0), (0, LANES - n_r)))
    wr_hi = wr.astype(_BF)
    wr_lo = (wr - wr_hi.astype(_F32)).astype(_BF)
    wr2 = jnp.concatenate([wr_hi, wr_lo], axis=1)
    br = jnp.pad(jnp.concatenate([b_rg, b_re.reshape(-1)]), (0, LANES - n_r)).reshape(1, LANES).astype(_F32)

    h, hn, logits = _mix(xp, xs, oa, obs, lses, gates, w_proj_a.astype(_BF), w_proj_b.astype(_BF),
                         w_out.astype(_BF), norm2.reshape(1, d), wr2, br)

    ri, rw, cnt = _route(logits)
    counts = cnt[0, :N_EXPERTS]
    eid = ri[:, 0:2]
    rank = ri[:, 2:4]
    pcounts = (counts + EXPERT_ROWS - 1) // EXPERT_ROWS * EXPERT_ROWS
    pends = jnp.cumsum(pcounts)
    pstarts = pends - pcounts
    expert_ids = jnp.arange(N_EXPERTS, dtype=jnp.int32)
    start_of = jnp.sum(jnp.where(eid[:, :, None] == expert_ids, pstarts.astype(jnp.int32), 0), axis=-1)
    dest = (start_of + rank).reshape(-1).astype(jnp.int32)
    nblk = (2 * t + N_EXPERTS * (EXPERT_ROWS - 1) + EXPERT_ROWS - 1) // EXPERT_ROWS
    nvalid = (pends[-1] // EXPERT_ROWS).astype(jnp.int32)
    blk_start = jnp.minimum(jnp.arange(nblk, dtype=jnp.int32), nvalid - 1) * EXPERT_ROWS
    blk_e = jnp.sum(pends[None, :] <= blk_start[:, None], axis=1).astype(jnp.int32)

    pad_start = (pstarts + counts) // SUBLANES * SUBLANES
    xr = _dispatch(dest, pad_start.astype(jnp.int32), ((pends - pad_start) // SUBLANES).astype(jnp.int32),
                   nvalid.reshape(1), hn, nblk * EXPERT_ROWS)
    blk_ids = jnp.arange(nblk, dtype=jnp.int32)
    first = ((blk_ids == 0) | (blk_e != jnp.roll(blk_e, 1))).astype(jnp.int32)
    slot = ((jnp.cumsum(first) - 1) % 2).astype(jnp.int32)
    later = (expert_ids[None, :] > expert_ids[:, None]) & (counts[None, :] > 0)
    next_of = jnp.min(jnp.where(later, expert_ids[None, :], N_EXPERTS), axis=1)
    next_e = jnp.sum(jnp.where(blk_e[:, None] == expert_ids, next_of, 0), axis=1)
    next_e = jnp.where(next_e < N_EXPERTS, next_e, -1).astype(jnp.int32)
    yr = _experts(blk_e, nvalid.reshape(1), first, slot, next_e, xr, w1, w3, w2)
    return _combine(dest, h, rw, norm_final.reshape(1, d), yr, n_p=n_p)


def kernel(x_prompt, x_sample, norm1, w_in, attn_sink, w_proj_a, w_proj_b, w_gate, b_gate, w_out, norm2,
           w_router_group, b_router_group, w_router_expert, b_router_expert, w_expert_gate, w_expert_up,
           w_expert_down, norm_final):
    assert norm1.shape[0] == 1, "one layer"
    d = x_prompt.shape[-1]
    xp = x_prompt.reshape(-1, d)
    xs = x_sample.reshape(-1, d)
    yp, ys = _layer(xp, xs, x_prompt.shape[1], x_sample.shape[1], norm1[0], w_in[0], attn_sink[0], w_proj_a[0],
                    w_proj_b[0], w_gate[0], b_gate[0], w_out[0], norm2[0], w_router_group[0],
                    b_router_group[0], w_router_expert[0], b_router_expert[0], w_expert_gate[0],
                    w_expert_up[0], w_expert_down[0], norm_final)
    return yp.reshape(x_prompt.shape), ys.reshape(x_sample.shape)
```

```python
import functools
import math

import jax
import jax.numpy as jnp
import numpy as np
from jax import lax
from jax.experimental import pallas as pl
from jax.experimental.pallas import tpu as pltpu

HEAD_DIM = 128
A_HEADS = 16
A_KV = 4
A_HALF = 128
DILATED_GROUPS = ((128, 1), (512, 4), (2048, 16))
N_DIL = len(DILATED_GROUPS)
B_HPG = 4
B_HEADS = N_DIL * B_HPG
N_GROUPS = 8
EXPERTS_PER_GROUP = 8
N_EXPERTS = N_GROUPS * EXPERTS_PER_GROUP
EPS = 1e-6

LANES = 128
SUBLANES = 8
MASKED = -1e30
LOG2_E = math.log2(math.e)
LN_2 = math.log(2.0)
VMEM_LIMIT = 56 * 1024 * 1024
SUBQ = 128
HEADS_PER_STEP = 4
EXPERT_ROWS = 256
DMA_UNROLL = 8
PROJ_TM, PROJ_TN = 2048, 512
MAX_ROW_STRIDE = 4
ROW_TILE = 256

_BF = jnp.bfloat16
_F32 = jnp.float32


def _alibi(n):
    return np.power(2.0, -8.0 * (np.arange(n) + 1) / n).astype(np.float32)


def _params(*sem):
    return pltpu.CompilerParams(dimension_semantics=sem, vmem_limit_bytes=VMEM_LIMIT)


def _pack_halves(x):
    n = x.shape[1] // 2

    def bf16_bits(v):
        return lax.bitcast_convert_type(v.astype(_BF).astype(_F32), jnp.uint32)

    return bf16_bits(x[:, n:]) | (bf16_bits(x[:, :n]) >> 16)


def _unpack_halves(p):
    lo = lax.bitcast_convert_type(p << 16, _F32)
    hi = lax.bitcast_convert_type(p & jnp.uint32(0xFFFF0000), _F32)
    return lo, hi


def _store_row_tiles(ref, packed):
    rows, words = packed.shape
    s_count = words // LANES
    for s in range(s_count):
        ref[pl.ds(s, rows, stride=s_count), :] = packed[:, s * LANES:(s + 1) * LANES]


def _load_row_tiles(ref, rows):
    s_count = ref.shape[0] // rows
    return jnp.concatenate([ref[pl.ds(s, rows, stride=s_count), :] for s in range(s_count)], axis=1)


def _row_tile(ref, row, s_count):
    return ref.at[pl.ds(pl.multiple_of(row * s_count, s_count), s_count), :]


def _proj_kernel(g_ref, b_ref, xp_hbm, xs_hbm, win_hbm, wg_hbm, oa_hbm, q0_hbm, q1_hbm, q2_hbm, gt_hbm,
                 xbuf, xn_ref, wbuf, acc_ref, obuf, x_sem, w_sems, o_sems,
                 *, n_prompt_tiles, n_row_tiles, tm, tn, na, ng):
    i = pl.program_id(0)
    nb = 3
    tiles_per_row = na + N_DIL * nb + ng
    rows = pl.ds(pl.multiple_of(i * tm, tm), tm)

    def cols(jj):
        return pl.ds(pl.multiple_of(jj * tn, tn), tn)

    def x_copy(x_hbm, row_tile):
        return pltpu.make_async_copy(x_hbm.at[pl.ds(pl.multiple_of(row_tile * tm, tm), tm), :], xbuf, x_sem)

    def start_x(row_tile):
        @pl.when(row_tile < n_prompt_tiles)
        def _():
            x_copy(xp_hbm, row_tile).start()

        @pl.when(row_tile >= n_prompt_tiles)
        def _():
            x_copy(xs_hbm, row_tile - n_prompt_tiles).start()

    def w_copy(w_hbm, col, slot):
        return pltpu.make_async_copy(w_hbm.at[:, cols(col)], wbuf.at[slot], w_sems.at[slot])

    def wait_staging(slot):
        pltpu.make_async_copy(obuf.at[slot], oa_hbm.at[pl.ds(0, tm), pl.ds(0, tn)], o_sems.at[slot]).wait()

    def store_plain(out_hbm):
        def store(acc, jj, slot):
            obuf[slot] = acc.astype(_BF)
            pltpu.make_async_copy(obuf.at[slot], out_hbm.at[rows, cols(jj)], o_sems.at[slot]).start()
        return store

    def store_gate(acc, jj, slot):
        z = acc + b_ref[jj]
        obuf[slot] = (0.5 * jnp.tanh(0.5 * z) + 0.5).astype(_BF)
        pltpu.make_async_copy(obuf.at[slot], gt_hbm.at[rows, cols(jj)], o_sems.at[slot]).start()

    def store_phases(q_hbm, r):
        n = tm // r

        def store(acc, jj, slot):
            if r == 1:
                obuf[slot] = acc.astype(_BF)
            else:
                for cb in range(tn // LANES):
                    acc_ref[0, cb] = acc[:, cb * LANES:(cb + 1) * LANES]
                src, stride = 0, r
                if r > MAX_ROW_STRIDE:
                    stride = r // MAX_ROW_STRIDE
                    m = tm // MAX_ROW_STRIDE
                    for c in range(MAX_ROW_STRIDE):
                        for cb in range(tn // LANES):
                            acc_ref[1, cb, c * m:(c + 1) * m, :] = acc_ref[0, cb, pl.ds(c, m, stride=MAX_ROW_STRIDE), :]
                    src = 1
                for p in range(r):
                    start = p if src == 0 else (p % MAX_ROW_STRIDE) * (tm // MAX_ROW_STRIDE) + p // MAX_ROW_STRIDE
                    for cb in range(tn // LANES):
                        obuf[slot, p * n:(p + 1) * n, cb * LANES:(cb + 1) * LANES] = (
                            acc_ref[src, cb, pl.ds(start, n, stride=stride), :].astype(_BF))
            for p in range(r):
                pltpu.make_async_copy(obuf.at[slot, pl.ds(p * n, n), :],
                                      q_hbm.at[p, pl.ds(pl.multiple_of(i * n, n), n), cols(jj)],
                                      o_sems.at[slot]).start()
        return store

    parts = [(na, win_hbm, lambda jj: jj, store_plain(oa_hbm))]
    for gi, (q_hbm, (_, r)) in enumerate(zip((q0_hbm, q1_hbm, q2_hbm), DILATED_GROUPS)):
        parts.append((nb, win_hbm, lambda jj, gi=gi: na + jj * N_DIL + gi, store_phases(q_hbm, r)))
    parts.append((ng, wg_hbm, lambda jj: jj, store_gate))

    @pl.when(i == 0)
    def _():
        start_x(0)
        w_copy(win_hbm, 0, 0).start()

    x_copy(xp_hbm, 0).wait()
    x = xbuf[...]
    y = x * lax.rsqrt(jnp.mean(x * x, axis=-1, keepdims=True) + EPS)
    xn_ref[...] = (y * g_ref[...]).astype(_BF)

    @pl.when(i + 1 < n_row_tiles)
    def _():
        start_x(i + 1)

    base = 0
    for k, (length, w_hbm, w_col, store) in enumerate(parts):
        def tile(jj, carry, base=base, length=length, w_hbm=w_hbm, w_col=w_col, store=store, k=k):
            n = i * tiles_per_row + base + jj
            slot = n % 2

            @pl.when(jj + 1 < length)
            def _():
                w_copy(w_hbm, w_col(jj + 1), 1 - slot).start()

            @pl.when(jj + 1 == length)
            def _():
                if k + 1 < len(parts):
                    w_copy(parts[k + 1][1], parts[k + 1][2](0), 1 - slot).start()
                else:
                    @pl.when(i + 1 < n_row_tiles)
                    def _():
                        w_copy(win_hbm, 0, 1 - slot).start()

            @pl.when(n >= 2)
            def _():
                wait_staging(slot)

            w_copy(w_hbm, w_col(jj), slot).wait()
            acc = jnp.dot(xn_ref[...], wbuf[slot].astype(_BF), preferred_element_type=_F32)
            store(acc, jj, slot)
            return carry

        lax.fori_loop(0, length, tile, 0)
        base += length

    @pl.when(i == n_row_tiles - 1)
    def _():
        wait_staging(0)
        wait_staging(1)


def _project(xp, xs, g, w_in, w_gate, b_gate, *, a_cols):
    n_p, d = xp.shape
    n_s = xs.shape[0]
    t = n_p + n_s
    tm, tn = PROJ_TM, PROJ_TN
    assert tn == B_HPG * HEAD_DIM and w_in.shape[1] == a_cols + 3 * N_DIL * tn
    assert n_p % tm == 0 and n_s % tm == 0
    gate_cols = w_gate.shape[1]
    na, ng = a_cols // tn, gate_cols // tn
    any_space = pl.BlockSpec(memory_space=pl.ANY)
    return pl.pallas_call(
        functools.partial(_proj_kernel, n_prompt_tiles=n_p // tm, n_row_tiles=t // tm, tm=tm, tn=tn, na=na, ng=ng),
        grid=(t // tm,),
        in_specs=[pl.BlockSpec((1, d), lambda i: (0, 0)), pl.BlockSpec((ng, 1, tn), lambda i: (0, 0, 0)),
                  any_space, any_space, any_space, any_space],
        out_specs=[any_space] * 5,
        out_shape=[jax.ShapeDtypeStruct((t, a_cols), _BF)]
                  + [jax.ShapeDtypeStruct((r, t // r, 3 * tn), _BF) for _, r in DILATED_GROUPS]
                  + [jax.ShapeDtypeStruct((t, gate_cols), _BF)],
        scratch_shapes=[pltpu.VMEM((tm, d), xp.dtype), pltpu.VMEM((tm, d), _BF), pltpu.VMEM((2, d, tn), w_in.dtype),
                        pltpu.VMEM((2, tn // LANES, tm, LANES), _F32), pltpu.VMEM((2, tm, tn), _BF),
                        pltpu.SemaphoreType.DMA(()), pltpu.SemaphoreType.DMA((2,)), pltpu.SemaphoreType.DMA((2,))],
        compiler_params=_params("arbitrary"),
        name="norm_proj",
    )(g, b_gate.reshape(ng, 1, tn), xp, xs, w_in, w_gate)


def _band_attn_kernel(slope_ref, sink_ref, q_ref, kl_ref, km_ref, kr_ref, vl_ref, vm_ref, vr_ref,
                      *rest, bq, half, step, shared_kv, has_sink, prompt_rows, prompt_seq, sample_seq):
    if has_sink:
        o_ref, kcat, vcat = rest
        lse_ref = None
    else:
        o_ref, lse_ref, kcat, vcat = rest
    i = pl.program_id(0)
    c = pl.program_id(1)
    win = SUBQ + 2 * half

    kcat[0:half, :] = kl_ref[...]
    kcat[half:half + bq, :] = km_ref[...]
    kcat[half + bq:, :] = kr_ref[...]
    vcat[0:half, :] = vl_ref[...]
    vcat[half:half + bq, :] = vm_ref[...]
    vcat[half + bq:, :] = vr_ref[...]

    u0 = i * bq
    in_prompt = u0 < prompt_rows
    lo = jnp.where(in_prompt, (u0 // prompt_seq) * prompt_seq,
                   prompt_rows + ((u0 - prompt_rows) // sample_seq) * sample_seq)
    hi = lo + jnp.where(in_prompt, prompt_seq, sample_seq)

    qi = lax.broadcasted_iota(jnp.int32, (SUBQ, win), 0)
    kj = lax.broadcasted_iota(jnp.int32, (SUBQ, win), 1)
    rel = kj - half - qi
    absrel = jnp.abs(rel)
    band_bias = jnp.where(absrel <= half, -(absrel * step).astype(_F32), MASKED)
    lane = lax.broadcasted_iota(jnp.int32, (SUBQ, LANES), 1)

    heads = range(HEADS_PER_STEP)
    head_ids = [c * HEADS_PER_STEP + h if shared_kv else h for h in heads]
    scale2 = HEAD_DIM ** -0.5 * LOG2_E
    head_bias = [(slope_ref[hid] * LOG2_E) * band_bias for hid in head_ids]
    n_sub = bq // SUBQ
    for sb in range(n_sub):
        kpos = kj + (u0 + sb * SUBQ - half)
        in_seq = None
        if sb == 0:
            in_seq = kpos >= lo
        if sb == n_sub - 1:
            in_seq = kpos < hi if in_seq is None else in_seq & (kpos < hi)

        def scores2(s, h):
            s = s * scale2 + head_bias[h]
            return s if in_seq is None else jnp.where(in_seq, s, MASKED)

        rows = slice(sb * SUBQ, (sb + 1) * SUBQ)
        if shared_kv:
            k = kcat[sb * SUBQ:sb * SUBQ + win, :]
            v = vcat[sb * SUBQ:sb * SUBQ + win, :]
            q4 = jnp.concatenate([q_ref[rows, h * HEAD_DIM:(h + 1) * HEAD_DIM] for h in heads], axis=0)
            s4 = lax.dot_general(q4, k, (((1,), (1,)), ((), ())), preferred_element_type=_F32)
            ps, ms, ls = [], [], []
            for h in heads:
                s = scores2(s4[h * SUBQ:(h + 1) * SUBQ], h)
                m = jnp.max(s, axis=-1, keepdims=True)
                p = jnp.exp2(s - m)
                ms.append(m)
                ls.append(jnp.sum(p, axis=-1, keepdims=True))
                ps.append(p.astype(_BF))
            pv4 = jnp.dot(jnp.concatenate(ps, axis=0), v, preferred_element_type=_F32)
            for h in heads:
                l = ls[h] + jnp.exp2(sink_ref[head_ids[h]] * LOG2_E - ms[h])
                o_ref[rows, h * HEAD_DIM:(h + 1) * HEAD_DIM] = (
                    pv4[h * SUBQ:(h + 1) * SUBQ] / l).astype(o_ref.dtype)
            continue
        lse_tile = jnp.zeros((SUBQ, LANES), _F32)
        for h in heads:
            cols = slice(h * HEAD_DIM, (h + 1) * HEAD_DIM)
            k = kcat[sb * SUBQ:sb * SUBQ + win, cols]
            v = vcat[sb * SUBQ:sb * SUBQ + win, cols]
            s = lax.dot_general(q_ref[rows, cols], k, (((1,), (1,)), ((), ())), preferred_element_type=_F32)
            s = scores2(s, h)
            m = jnp.max(s, axis=-1, keepdims=True)
            p = jnp.exp2(s - m)
            l = jnp.sum(p, axis=-1, keepdims=True)
            pv = jnp.dot(p.astype(_BF), v, preferred_element_type=_F32)
            lse_tile = jnp.where(lane == h, m * LN_2 + jnp.log(l), lse_tile)
            o_ref[rows, cols] = (pv / l).astype(o_ref.dtype)
        lse_ref[rows, :] = lse_tile


def _band_attention(qkv, slopes, sinks, *, n_inner, bq, half, step, shared_kv, has_sink,
                    prompt_rows, prompt_seq, sample_seq, name):
    lead, rows, _ = qkv.shape
    assert prompt_seq % bq == 0 and sample_seq % bq == 0 and bq % SUBQ == 0 and bq % half == 0
    assert half <= SUBQ
    nq = rows // bq
    hb = bq // half
    last_halo = rows // half - 1
    width = HEADS_PER_STEP * HEAD_DIM
    if shared_kv:
        kv_w = HEAD_DIM
        k_base, v_base = width * n_inner // HEAD_DIM, width * n_inner // HEAD_DIM + n_inner
        lead_of = lambda c: 0
        q_col = lambda c: c
        kv_col = lambda base: (lambda c: base + c)
    else:
        kv_w = width
        k_base, v_base = 1, 2
        lead_of = lambda c: c
        q_col = lambda c: 0
        kv_col = lambda base: (lambda c: base)

    def main_map(col):
        return lambda i, c: (lead_of(c), i, col(c))

    def left_map(col):
        return lambda i, c: (lead_of(c), jnp.maximum(i * hb - 1, 0), col(c))

    def right_map(col):
        return lambda i, c: (lead_of(c), jnp.minimum((i + 1) * hb, last_halo), col(c))

    smem = pl.BlockSpec(memory_space=pltpu.SMEM)
    in_specs = [
        smem, smem,
        pl.BlockSpec((None, bq, width), main_map(q_col)),
        pl.BlockSpec((None, half, kv_w), left_map(kv_col(k_base))),
        pl.BlockSpec((None, bq, kv_w), main_map(kv_col(k_base))),
        pl.BlockSpec((None, half, kv_w), right_map(kv_col(k_base))),
        pl.BlockSpec((None, half, kv_w), left_map(kv_col(v_base))),
        pl.BlockSpec((None, bq, kv_w), main_map(kv_col(v_base))),
        pl.BlockSpec((None, half, kv_w), right_map(kv_col(v_base))),
    ]
    if has_sink:
        out_specs = pl.BlockSpec((bq, width), lambda i, c: (i, c))
        out_shape = jax.ShapeDtypeStruct((rows, n_inner * width), _BF)
    else:
        out_specs = (pl.BlockSpec((None, bq, width), lambda i, c: (c, i, 0)),
                     pl.BlockSpec((None, bq, LANES), lambda i, c: (c, i, 0)))
        out_shape = (jax.ShapeDtypeStruct((lead, rows, width), _BF),
                     jax.ShapeDtypeStruct((lead, rows, LANES), _F32))
    return pl.pallas_call(
        functools.partial(_band_attn_kernel, bq=bq, half=half, step=step, shared_kv=shared_kv,
                          has_sink=has_sink, prompt_rows=prompt_rows, prompt_seq=prompt_seq,
                          sample_seq=sample_seq),
        grid=(nq, n_inner),
        in_specs=in_specs,
        out_specs=out_specs,
        out_shape=out_shape,
        scratch_shapes=[pltpu.VMEM((bq + 2 * half, kv_w), _BF), pltpu.VMEM((bq + 2 * half, kv_w), _BF)],
        compiler_params=_params("parallel", "arbitrary"),
        name=name,
    )(slopes, sinks, qkv, qkv, qkv, qkv, qkv, qkv, qkv)


def _mix_kernel(xp_ref, xs_ref, oa_ref, ob0_ref, ob1_ref, ob2_ref, l0_ref, l1_ref, l2_ref, g_ref,
                wpa_ref, wpb_ref, wout_ref, n2_ref, wr_ref, br_ref, h_ref, hn_ref, lg_ref, o_scr, l_scr,
                *, n_prompt_tiles, d, tm):
    i = pl.program_id(0)

    def token_order(o_ref, l_ref, r, slot):
        if r == 1:
            return [o_ref[0, :, h * HEAD_DIM:(h + 1) * HEAD_DIM].astype(_F32) for h in range(B_HPG)], l_ref[0]
        for p in range(r):
            for h in range(B_HPG):
                o_scr[slot, h, pl.ds(p, tm // r, stride=r), :] = (
                    o_ref[p, :, h * HEAD_DIM:(h + 1) * HEAD_DIM].astype(_F32))
            l_scr[slot, pl.ds(p, tm // r, stride=r), :] = l_ref[p]
        return [o_scr[slot, h] for h in range(B_HPG)], l_scr[slot]

    o0, l0 = token_order(ob0_ref, l0_ref, DILATED_GROUPS[0][1], 0)
    o1, l1 = token_order(ob1_ref, l1_ref, DILATED_GROUPS[1][1], 0)
    o2, l2 = token_order(ob2_ref, l2_ref, DILATED_GROUPS[2][1], 1)
    mx = jnp.maximum(jnp.maximum(l0, l1), l2)
    e0, e1, e2 = jnp.exp(l0 - mx), jnp.exp(l1 - mx), jnp.exp(l2 - mx)
    den = e0 + e1 + e2
    a0, a1, a2 = e0 / den, e1 / den, e2 / den
    parts = []
    for h in range(B_HPG):
        parts.append(a0[:, h:h + 1] * o0[h] + a1[:, h:h + 1] * o1[h] + a2[:, h:h + 1] * o2[h])
    ob = jnp.concatenate(parts, axis=1).astype(_BF)

    ta = jnp.dot(oa_ref[...], wpa_ref[...], preferred_element_type=_F32)
    tb = jnp.dot(ob, wpb_ref[...], preferred_element_type=_F32)
    merged = g_ref[:, :d].astype(_F32) * ta + g_ref[:, d:].astype(_F32) * tb
    x = jnp.where(i < n_prompt_tiles, xp_ref[...], xs_ref[...])
    h_new = x + jnp.dot(merged.astype(_BF), wout_ref[...], preferred_element_type=_F32)
    h_ref[...] = h_new
    hn = h_new * lax.rsqrt(jnp.mean(h_new * h_new, axis=-1, keepdims=True) + EPS) * n2_ref[...]
    _store_row_tiles(hn_ref, _pack_halves(hn))
    hn_hi = hn.astype(_BF)
    hn_lo = (hn - hn_hi.astype(_F32)).astype(_BF)
    r = (jnp.dot(hn_hi, wr_ref[...], preferred_element_type=_F32)
         + jnp.dot(hn_lo, wr_ref[...], preferred_element_type=_F32))
    lg_ref[...] = r[:, :LANES] + r[:, LANES:] + br_ref[...]


def _mix(xp, xs, oa, obs, lses, gates, wpa, wpb, wout, n2, wr, br):
    n_p, d = xp.shape
    n_s = xs.shape[0]
    t = n_p + n_s
    tm = ROW_TILE
    npt = n_p // tm
    row = lambda i: (i, 0)
    const = lambda i: (0, 0)
    width = B_HPG * HEAD_DIM

    def resident(shape):
        return pl.BlockSpec(shape, const, pipeline_mode=pl.Buffered(1))

    def phase_blocks(cols):
        return [pl.BlockSpec((r, tm // r, cols), lambda i: (0, i, 0)) for _, r in DILATED_GROUPS]

    in_specs = [
        pl.BlockSpec((tm, d), lambda i: (jnp.minimum(i, npt - 1), 0)),
        pl.BlockSpec((tm, d), lambda i: (jnp.maximum(i - npt, 0), 0)),
        pl.BlockSpec((tm, oa.shape[1]), row),
        *phase_blocks(width), *phase_blocks(LANES),
        pl.BlockSpec((tm, 2 * d), row),
        resident(wpa.shape), resident(wpb.shape), resident(wout.shape),
        resident((1, d)), resident(wr.shape), resident((1, LANES)),
    ]
    return pl.pallas_call(
        functools.partial(_mix_kernel, n_prompt_tiles=npt, d=d, tm=tm),
        grid=(t // tm,),
        in_specs=in_specs,
        out_specs=(pl.BlockSpec((tm, d), row), pl.BlockSpec((tm * (d // 2 // LANES), LANES), row),
                   pl.BlockSpec((tm, LANES), row)),
        out_shape=(jax.ShapeDtypeStruct((t, d), _F32),
                   jax.ShapeDtypeStruct((t * (d // 2 // LANES), LANES), jnp.uint32),
                   jax.ShapeDtypeStruct((t, LANES), _F32)),
        scratch_shapes=[pltpu.VMEM((2, B_HPG, tm, HEAD_DIM), _F32), pltpu.VMEM((2, tm, LANES), _F32)],
        compiler_params=_params("parallel"),
        name="mix_out_router",
    )(xp, xs, oa, *obs, *lses, gates, wpa, wpb, wout, n2, wr, br)


def _route_kernel(lg_ref, ri_ref, rw_ref, cnt_ref, carry_ref, *, tr):
    i = pl.program_id(0)

    @pl.when(i == 0)
    def _():
        carry_ref[...] = jnp.zeros_like(carry_ref)

    lg = lg_ref[...]
    lane_i = lax.broadcasted_iota(jnp.int32, (tr, LANES), 1)
    lane = lane_i.astype(_F32)
    no_lane = float(LANES)
    is_grp = lane_i < N_GROUPS
    glog = jnp.where(is_grp, lg, MASKED)
    gmax = jnp.max(glog, axis=-1, keepdims=True)
    grp = jnp.min(jnp.where(glog == gmax, lane, no_lane), axis=-1, keepdims=True)
    gsum = jnp.sum(jnp.where(is_grp, jnp.exp(glog - gmax), 0.0), axis=-1, keepdims=True)
    pgrp = 1.0 / gsum
    lane_grp = ((lane_i - N_GROUPS) // EXPERTS_PER_GROUP).astype(_F32)
    in_grp = (lane_i >= N_GROUPS) & (lane_i < N_GROUPS + N_EXPERTS) & (lane_grp == grp)
    elog = jnp.where(in_grp, lg, MASKED)
    t1 = jnp.max(elog, axis=-1, keepdims=True)
    i1 = jnp.min(jnp.where(elog == t1, lane, no_lane), axis=-1, keepdims=True)
    elog2 = jnp.where(lane == i1, MASKED, elog)
    t2 = jnp.max(elog2, axis=-1, keepdims=True)
    i2 = jnp.min(jnp.where(elog2 == t2, lane, no_lane), axis=-1, keepdims=True)
    e21 = jnp.exp(t2 - t1)
    w1 = pgrp / (1.0 + e21)
    w2 = pgrp * e21 / (1.0 + e21)
    eid1 = i1 - N_GROUPS
    eid2 = i2 - N_GROUPS
    hot1 = lane == eid1
    hot2 = lane == eid2
    onehot = jnp.where(hot1, 1.0, 0.0) + jnp.where(hot2, 1.0, 0.0)
    r_i = lax.broadcasted_iota(jnp.int32, (tr, tr), 0)
    c_i = lax.broadcasted_iota(jnp.int32, (tr, tr), 1)
    lower = jnp.where(c_i < r_i, 1.0, 0.0).astype(_BF)
    before = jnp.dot(lower, onehot.astype(_BF), preferred_element_type=_F32) + carry_ref[0:1, :]
    rank1 = jnp.sum(jnp.where(hot1, before, 0.0), axis=-1, keepdims=True)
    rank2 = jnp.sum(jnp.where(hot2, before, 0.0), axis=-1, keepdims=True)
    total = carry_ref[0:1, :] + jnp.sum(onehot, axis=0, keepdims=True)
    carry_ref[...] = jnp.broadcast_to(total, carry_ref.shape)
    cnt_ref[...] = jnp.broadcast_to(total, cnt_ref.shape).astype(jnp.int32)
    ri = jnp.where(lane_i == 0, eid1, jnp.where(lane_i == 1, eid2, jnp.where(lane_i == 2, rank1,
                   jnp.where(lane_i == 3, rank2, 0.0))))
    ri_ref[...] = ri.astype(jnp.int32)
    rw_ref[...] = jnp.where(lane_i == 0, w1, jnp.where(lane_i == 1, w2, 0.0))


def _route(logits):
    t = logits.shape[0]
    tr = ROW_TILE
    row = lambda i: (i, 0)
    return pl.pallas_call(
        functools.partial(_route_kernel, tr=tr),
        grid=(t // tr,),
        in_specs=[pl.BlockSpec((tr, LANES), row)],
        out_specs=(pl.BlockSpec((tr, LANES), row), pl.BlockSpec((tr, LANES), row),
                   pl.BlockSpec((8, LANES), lambda i: (0, 0))),
        out_shape=(jax.ShapeDtypeStruct((t, LANES), jnp.int32), jax.ShapeDtypeStruct((t, LANES), _F32),
                   jax.ShapeDtypeStruct((8, LANES), jnp.int32)),
        scratch_shapes=[pltpu.VMEM((8, LANES), _F32)],
        compiler_params=_params("arbitrary"),
        name="route",
    )(logits)


def _dispatch_kernel(dest_ref, pad_start_ref, pad_n_ref, nvalid_ref, hn_ref, xr_ref, zeros, sem, pad_sem,
                     *, td, sc):
    i = pl.program_id(0)

    def rows_of(ref, row0, n):
        start = row0 * sc if isinstance(row0, int) else pl.multiple_of(row0 * sc, n * sc)
        return ref.at[pl.ds(start, n * sc), :]

    @pl.when(i == 0)
    def _():
        zeros[...] = jnp.zeros_like(zeros)

        def pad_chunk(e, r):
            row0 = pl.multiple_of(pad_start_ref[e] + r * SUBLANES, SUBLANES)
            return pltpu.make_async_copy(rows_of(zeros, 0, SUBLANES), rows_of(xr_ref, row0, SUBLANES), pad_sem)

        def per_expert(act):
            def body(e, carry):
                lax.fori_loop(0, pad_n_ref[e], lambda r, c: (act(pad_chunk(e, r)), c)[1], 0)
                return carry
            return body

        def tail_block(b):
            return pltpu.make_async_copy(zeros, rows_of(xr_ref, b * EXPERT_ROWS, EXPERT_ROWS), pad_sem)

        n_blocks = xr_ref.shape[0] // (EXPERT_ROWS * sc)
        lax.fori_loop(0, N_EXPERTS, per_expert(lambda cp: cp.start()), 0)
        lax.fori_loop(nvalid_ref[0], n_blocks, lambda b, c: (tail_block(b).start(), c)[1], 0)
        lax.fori_loop(0, N_EXPERTS, per_expert(lambda cp: cp.wait()), 0)
        lax.fori_loop(nvalid_ref[0], n_blocks, lambda b, c: (tail_block(b).wait(), c)[1], 0)

    def start(j, carry):
        for k in range(2):
            dst = dest_ref[2 * (i * td + j) + k]
            pltpu.make_async_copy(rows_of(hn_ref, j, 1), rows_of(xr_ref, dst, 1), sem).start()
        return carry

    lax.fori_loop(0, td, start, 0, unroll=DMA_UNROLL)
    for _ in range(2):
        pltpu.make_async_copy(hn_ref, rows_of(xr_ref, 0, td), sem).wait()


def _dispatch(dest, pad_start, pad_n, nvalid, hn, t, rows):
    sc = hn.shape[0] // t
    td = ROW_TILE
    return pl.pallas_call(
        functools.partial(_dispatch_kernel, td=td, sc=sc),
        grid_spec=pltpu.PrefetchScalarGridSpec(
            num_scalar_prefetch=4,
            grid=(t // td,),
            in_specs=[pl.BlockSpec((td * sc, LANES), lambda i, *_: (i, 0))],
            out_specs=pl.BlockSpec(memory_space=pl.ANY),
            scratch_shapes=[pltpu.VMEM((EXPERT_ROWS * sc, LANES), hn.dtype), pltpu.SemaphoreType.DMA(()),
                            pltpu.SemaphoreType.DMA(())],
        ),
        out_shape=jax.ShapeDtypeStruct((rows * sc, LANES), hn.dtype),
        compiler_params=_params("arbitrary"),
        name="dispatch",
    )(dest, pad_start, pad_n, nvalid, hn)


def _expert_kernel(blk_e_ref, nvalid_ref, first_ref, slot_ref, next_e_ref, x_ref, w1_hbm, w3_hbm, w2_hbm, y_ref,
                   w1_buf, w3_buf, w2_buf, sems):
    b = pl.program_id(0)
    valid = b < nvalid_ref[0]

    def fetch(e, slot):
        return (pltpu.make_async_copy(w1_hbm.at[e], w1_buf.at[slot], sems.at[slot, 0]),
                pltpu.make_async_copy(w3_hbm.at[e], w3_buf.at[slot], sems.at[slot, 1]),
                pltpu.make_async_copy(w2_hbm.at[e], w2_buf.at[slot], sems.at[slot, 2]))

    @pl.when(b == 0)
    def _():
        for cp in fetch(blk_e_ref[0], 0):
            cp.start()

    @pl.when(valid & (first_ref[b] == 1))
    def _():
        for cp in fetch(blk_e_ref[b], slot_ref[b]):
            cp.wait()

        @pl.when(next_e_ref[b] >= 0)
        def _():
            for cp in fetch(next_e_ref[b], 1 - slot_ref[b]):
                cp.start()

    @pl.when(valid)
    def _():
        slot = slot_ref[b]
        packed_x = _load_row_tiles(x_ref, EXPERT_ROWS)
        xb = jnp.concatenate([half.astype(_BF) for half in _unpack_halves(packed_x)], axis=1)
        a = jnp.dot(xb, w1_buf[slot].astype(_BF), preferred_element_type=_F32)
        u = jnp.dot(xb, w3_buf[slot].astype(_BF), preferred_element_type=_F32)
        hmid = (a / (1.0 + jnp.exp(-a))) * u
        y = jnp.dot(hmid.astype(_BF), w2_buf[slot].astype(_BF), preferred_element_type=_F32)
        _store_row_tiles(y_ref, _pack_halves(y))

    @pl.when(jnp.logical_not(valid))
    def _():
        y_ref[...] = jnp.zeros_like(y_ref)


def _experts(blk_e, nvalid, first, slot, next_e, xr, w1, w3, w2):
    _, d, f = w1.shape
    sc = d // 2 // LANES
    blk = EXPERT_ROWS * sc
    nblk = xr.shape[0] // blk
    any_space = pl.BlockSpec(memory_space=pl.ANY)
    return pl.pallas_call(
        _expert_kernel,
        grid_spec=pltpu.PrefetchScalarGridSpec(
            num_scalar_prefetch=5,
            grid=(nblk,),
            in_specs=[pl.BlockSpec((blk, LANES), lambda b, be, nv, *_: (jnp.minimum(b, nv[0] - 1), 0)),
                      any_space, any_space, any_space],
            out_specs=pl.BlockSpec((blk, LANES), lambda b, *_: (b, 0)),
            scratch_shapes=[pltpu.VMEM((2, d, f), w1.dtype), pltpu.VMEM((2, d, f), w3.dtype),
                            pltpu.VMEM((2, f, d), w2.dtype), pltpu.SemaphoreType.DMA((2, 3))],
        ),
        out_shape=jax.ShapeDtypeStruct(xr.shape, jnp.uint32),
        compiler_params=_params("arbitrary"),
        name="experts",
    )(blk_e, nvalid, first, slot, next_e, xr, w1, w3, w2)


def _combine_kernel(dest_ref, h_ref, rw_ref, g_ref, yr_ref, op_ref, os_ref, ybuf, sems, *, tc, sc, n_prompt_tiles):
    i = pl.program_id(0)
    slot = i % 2

    def gather(tile, s):
        def start(j, carry):
            for k in range(2):
                src = dest_ref[2 * (tile * tc + j) + k]
                pltpu.make_async_copy(_row_tile(yr_ref, src, sc), _row_tile(ybuf.at[s, k], j, sc),
                                      sems.at[s]).start()
            return carry

        lax.fori_loop(0, tc, start, 0, unroll=DMA_UNROLL)

    @pl.when(i == 0)
    def _():
        gather(0, 0)

    @pl.when(i + 1 < pl.num_programs(0))
    def _():
        gather(i + 1, 1 - slot)

    for k in range(2):
        pltpu.make_async_copy(yr_ref.at[pl.ds(0, tc * sc), :], ybuf.at[slot, k], sems.at[slot]).wait()

    rw = rw_ref[...]
    y0 = _unpack_halves(_load_row_tiles(ybuf.at[slot, 0], tc))
    y1 = _unpack_halves(_load_row_tiles(ybuf.at[slot, 1], tc))
    y = jnp.concatenate([rw[:, 0:1] * a + rw[:, 1:2] * b for a, b in zip(y0, y1)], axis=1)
    z = h_ref[...] + y
    out = z * lax.rsqrt(jnp.mean(z * z, axis=-1, keepdims=True) + EPS) * g_ref[...]

    @pl.when(i < n_prompt_tiles)
    def _():
        op_ref[...] = out

    @pl.when(i >= n_prompt_tiles)
    def _():
        os_ref[...] = out


def _combine(dest, h, rw, g, yr, *, n_p):
    t, d = h.shape
    tc = ROW_TILE
    npt = n_p // tc
    sc = d // 2 // LANES
    return pl.pallas_call(
        functools.partial(_combine_kernel, tc=tc, sc=sc, n_prompt_tiles=npt),
        grid_spec=pltpu.PrefetchScalarGridSpec(
            num_scalar_prefetch=1,
            grid=(t // tc,),
            in_specs=[
                pl.BlockSpec((tc, d), lambda i, dest: (i, 0)),
                pl.BlockSpec((tc, LANES), lambda i, dest: (i, 0)),
                pl.BlockSpec((1, d), lambda i, dest: (0, 0)),
                pl.BlockSpec(memory_space=pl.ANY),
            ],
            out_specs=(pl.BlockSpec((tc, d), lambda i, dest: (jnp.minimum(i, npt - 1), 0)),
                       pl.BlockSpec((tc, d), lambda i, dest: (jnp.maximum(i - npt, 0), 0))),
            scratch_shapes=[pltpu.VMEM((2, 2, tc * sc, LANES), yr.dtype), pltpu.SemaphoreType.DMA((2,))],
        ),
        out_shape=(jax.ShapeDtypeStruct((n_p, d), _F32), jax.ShapeDtypeStruct((t - n_p, d), _F32)),
        compiler_params=_params("arbitrary"),
        name="combine_norm",
    )(dest, h, rw, g, yr)


def _layer(xp, xs, prompt_seq, sample_seq, norm1, w_in, sink, w_proj_a, w_proj_b, w_gate, b_gate, w_out,
           norm2, w_rg, b_rg, w_re, b_re, w1, w3, w2, norm_final):
    n_p, d = xp.shape
    n_s = xs.shape[0]
    t = n_p + n_s
    a_q = A_HEADS * HEAD_DIM
    a_cols = a_q + 2 * A_KV * HEAD_DIM

    proj_a, q0, q1, q2, gates = _project(xp, xs, norm1.reshape(1, d), w_in, w_gate, b_gate.reshape(1, 2 * d),
                                         a_cols=a_cols)

    oa = _band_attention(
        proj_a.reshape(1, t, a_cols), jnp.asarray(_alibi(A_HEADS)), sink.astype(_F32), n_inner=A_KV,
        bq=min(512, prompt_seq, sample_seq), half=A_HALF, step=1, shared_kv=True, has_sink=True,
        prompt_rows=n_p, prompt_seq=prompt_seq, sample_seq=sample_seq, name="attn_window")

    slopes_b = _alibi(B_HEADS)
    obs, lses = [], []
    for gi, ((w, r), qkv) in enumerate(zip(DILATED_GROUPS, (q0, q1, q2))):
        o_g, lse_g = _band_attention(
            qkv, jnp.asarray(slopes_b[gi * B_HPG:(gi + 1) * B_HPG]), jnp.zeros((B_HPG,), _F32), n_inner=r,
            bq=min(512, prompt_seq // r, sample_seq // r), half=w // (2 * r), step=r, shared_kv=False,
            has_sink=False, prompt_rows=n_p // r, prompt_seq=prompt_seq // r, sample_seq=sample_seq // r,
            name=f"attn_dilated_{r}")
        obs.append(o_g)
        lses.append(lse_g)

    n_r = N_GROUPS + N_EXPERTS
    wr = jnp.concatenate([w_rg, jnp.transpose(w_re, (1, 0, 2)).reshape(d, N_EXPERTS)], axis=1)
    wr = jnp.pad(wr, ((0, 0), (0, LANES - n_r)))
    wr_hi = wr.astype(_BF)
    wr_lo = (wr - wr_hi.astype(_F32)).astype(_BF)
    wr2 = jnp.concatenate([wr_hi, wr_lo], axis=1)
    br = jnp.pad(jnp.concatenate([b_rg, b_re.reshape(-1)]), (0, LANES - n_r)).reshape(1, LANES).astype(_F32)

    h, hn, logits = _mix(xp, xs, oa, obs, lses, gates, w_proj_a.astype(_BF), w_proj_b.astype(_BF),
                         w_out.astype(_BF), norm2.reshape(1, d), wr2, br)

    ri, rw, cnt = _route(logits)
    counts = cnt[0, :N_EXPERTS]
    eid = ri[:, 0:2]
    rank = ri[:, 2:4]
    pcounts = (counts + EXPERT_ROWS - 1) // EXPERT_ROWS * EXPERT_ROWS
    pends = jnp.cumsum(pcounts)
    pstarts = pends - pcounts
    expert_ids = jnp.arange(N_EXPERTS, dtype=jnp.int32)
    start_of = jnp.sum(jnp.where(eid[:, :, None] == expert_ids, pstarts.astype(jnp.int32), 0), axis=-1)
    dest = (start_of + rank).reshape(-1).astype(jnp.int32)
    nblk = (2 * t + N_EXPERTS * (EXPERT_ROWS - 1) + EXPERT_ROWS - 1) // EXPERT_ROWS
    nvalid = (pends[-1] // EXPERT_ROWS).astype(jnp.int32)
    blk_start = jnp.minimum(jnp.arange(nblk, dtype=jnp.int32), nvalid - 1) * EXPERT_ROWS
    blk_e = jnp.sum(pends[None, :] <= blk_start[:, None], axis=1).astype(jnp.int32)

    pad_start = (pstarts + counts) // SUBLANES * SUBLANES
    xr = _dispatch(dest, pad_start.astype(jnp.int32), ((pends - pad_start) // SUBLANES).astype(jnp.int32),
                   nvalid.reshape(1), hn, t, nblk * EXPERT_ROWS)
    blk_ids = jnp.arange(nblk, dtype=jnp.int32)
    first = ((blk_ids == 0) | (blk_e != jnp.roll(blk_e, 1))).astype(jnp.int32)
    slot = ((jnp.cumsum(first) - 1) % 2).astype(jnp.int32)
    later = (expert_ids[None, :] > expert_ids[:, None]) & (counts[None, :] > 0)
    next_of = jnp.min(jnp.where(later, expert_ids[None, :], N_EXPERTS), axis=1)
    next_e = jnp.sum(jnp.where(blk_e[:, None] == expert_ids, next_of, 0), axis=1)
    next_e = jnp.where(next_e < N_EXPERTS, next_e, -1).astype(jnp.int32)
    yr = _experts(blk_e, nvalid.reshape(1), first, slot, next_e, xr, w1, w3, w2)
    return _combine(dest, h, rw, norm_final.reshape(1, d), yr, n_p=n_p)


def kernel(x_prompt, x_sample, norm1, w_in, attn_sink, w_proj_a, w_proj_b, w_gate, b_gate, w_out, norm2,
           w_router_group, b_router_group, w_router_expert, b_router_expert, w_expert_gate, w_expert_up,
           w_expert_down, norm_final):
    assert norm1.shape[0] == 1, "one layer"
    d = x_prompt.shape[-1]
    xp = x_prompt.reshape(-1, d)
    xs = x_sample.reshape(-1, d)
    yp, ys = _layer(xp, xs, x_prompt.shape[1], x_sample.shape[1], norm1[0], w_in[0], attn_sink[0], w_proj_a[0],
                    w_proj_b[0], w_gate[0], b_gate[0], w_out[0], norm2[0], w_router_group[0],
                    b_router_group[0], w_router_expert[0], b_router_expert[0], w_expert_gate[0],
                    w_expert_up[0], w_expert_down[0], norm_final)
    return yp.reshape(x_prompt.shape), ys.reshape(x_sample.shape)
```

```python
import functools
import math

import jax
import jax.numpy as jnp
import numpy as np
from jax import lax
from jax.experimental import pallas as pl
from jax.experimental.pallas import tpu as pltpu

HEAD_DIM = 128
A_HEADS = 16
A_KV = 4
A_HALF = 128
DILATED_GROUPS = ((128, 1), (512, 4), (2048, 16))
N_DIL = len(DILATED_GROUPS)
B_HPG = 4
B_HEADS = N_DIL * B_HPG
N_GROUPS = 8
EXPERTS_PER_GROUP = 8
N_EXPERTS = N_GROUPS * EXPERTS_PER_GROUP
EPS = 1e-6

LANES = 128
SUBLANES = 8
MASKED = -1e30
LOG2_E = math.log2(math.e)
LN_2 = math.log(2.0)
VMEM_LIMIT = 56 * 1024 * 1024
SUBQ = 128
HEADS_PER_STEP = 4
EXPERT_ROWS = 256
DMA_UNROLL = 8
PROJ_TM, PROJ_TN = 2048, 512
MAX_ROW_STRIDE = 4
ROW_TILE = 256

_BF = jnp.bfloat16
_F32 = jnp.float32


def _alibi(n):
    return np.power(2.0, -8.0 * (np.arange(n) + 1) / n).astype(np.float32)


def _params(*sem):
    return pltpu.CompilerParams(dimension_semantics=sem, vmem_limit_bytes=VMEM_LIMIT)


def _pack_halves(x):
    n = x.shape[1] // 2

    def bf16_bits(v):
        return lax.bitcast_convert_type(v.astype(_BF).astype(_F32), jnp.uint32)

    return bf16_bits(x[:, n:]) | (bf16_bits(x[:, :n]) >> 16)


def _unpack_halves(p):
    lo = lax.bitcast_convert_type(p << 16, _F32)
    hi = lax.bitcast_convert_type(p & jnp.uint32(0xFFFF0000), _F32)
    return lo, hi


def _store_row_tiles(ref, packed):
    rows, words = packed.shape
    s_count = words // LANES
    for s in range(s_count):
        ref[pl.ds(s, rows, stride=s_count), :] = packed[:, s * LANES:(s + 1) * LANES]


def _load_row_tiles(ref, rows):
    s_count = ref.shape[0] // rows
    return jnp.concatenate([ref[pl.ds(s, rows, stride=s_count), :] for s in range(s_count)], axis=1)


def _row_tile(ref, row, s_count):
    return ref.at[pl.ds(pl.multiple_of(row * s_count, s_count), s_count), :]


def _proj_kernel(g_ref, b_ref, xp_hbm, xs_hbm, win_hbm, wg_hbm, oa_hbm, q0_hbm, q1_hbm, q2_hbm, gt_hbm,
                 xbuf, xn_ref, wbuf, acc_ref, obuf, x_sem, w_sems, o_sems,
                 *, n_prompt_tiles, n_row_tiles, tm, tn, na, ng):
    i = pl.program_id(0)
    nb = 3
    tiles_per_row = na + N_DIL * nb + ng
    rows = pl.ds(pl.multiple_of(i * tm, tm), tm)

    def cols(jj):
        return pl.ds(pl.multiple_of(jj * tn, tn), tn)

    def x_copy(x_hbm, row_tile):
        return pltpu.make_async_copy(x_hbm.at[pl.ds(pl.multiple_of(row_tile * tm, tm), tm), :], xbuf, x_sem)

    def start_x(row_tile):
        @pl.when(row_tile < n_prompt_tiles)
        def _():
            x_copy(xp_hbm, row_tile).start()

        @pl.when(row_tile >= n_prompt_tiles)
        def _():
            x_copy(xs_hbm, row_tile - n_prompt_tiles).start()

    def w_copy(w_hbm, col, slot):
        return pltpu.make_async_copy(w_hbm.at[:, cols(col)], wbuf.at[slot], w_sems.at[slot])

    def wait_staging(slot):
        pltpu.make_async_copy(obuf.at[slot], oa_hbm.at[pl.ds(0, tm), pl.ds(0, tn)], o_sems.at[slot]).wait()

    def store_plain(out_hbm):
        def store(acc, jj, slot):
            obuf[slot] = acc.astype(_BF)
            pltpu.make_async_copy(obuf.at[slot], out_hbm.at[rows, cols(jj)], o_sems.at[slot]).start()
        return store

    def store_gate(acc, jj, slot):
        z = acc + b_ref[jj]
        obuf[slot] = (0.5 * jnp.tanh(0.5 * z) + 0.5).astype(_BF)
        pltpu.make_async_copy(obuf.at[slot], gt_hbm.at[rows, cols(jj)], o_sems.at[slot]).start()

    def store_phases(q_hbm, r):
        n = tm // r

        def store(acc, jj, slot):
            if r == 1:
                obuf[slot] = acc.astype(_BF)
            else:
                for cb in range(tn // LANES):
                    acc_ref[0, cb] = acc[:, cb * LANES:(cb + 1) * LANES]
                src, stride = 0, r
                if r > MAX_ROW_STRIDE:
                    stride = r // MAX_ROW_STRIDE
                    m = tm // MAX_ROW_STRIDE
                    for c in range(MAX_ROW_STRIDE):
                        for cb in range(tn // LANES):
                            acc_ref[1, cb, c * m:(c + 1) * m, :] = acc_ref[0, cb, pl.ds(c, m, stride=MAX_ROW_STRIDE), :]
                    src = 1
                for p in range(r):
                    start = p if src == 0 else (p % MAX_ROW_STRIDE) * (tm // MAX_ROW_STRIDE) + p // MAX_ROW_STRIDE
                    for cb in range(tn // LANES):
                        obuf[slot, p * n:(p + 1) * n, cb * LANES:(cb + 1) * LANES] = (
                            acc_ref[src, cb, pl.ds(start, n, stride=stride), :].astype(_BF))
            for p in range(r):
                pltpu.make_async_copy(obuf.at[slot, pl.ds(p * n, n), :],
                                      q_hbm.at[p, pl.ds(pl.multiple_of(i * n, n), n), cols(jj)],
                                      o_sems.at[slot]).start()
        return store

    parts = [(na, win_hbm, lambda jj: jj, store_plain(oa_hbm))]
    for gi, (q_hbm, (_, r)) in enumerate(zip((q0_hbm, q1_hbm, q2_hbm), DILATED_GROUPS)):
        parts.append((nb, win_hbm, lambda jj, gi=gi: na + jj * N_DIL + gi, store_phases(q_hbm, r)))
    parts.append((ng, wg_hbm, lambda jj: jj, store_gate))

    @pl.when(i == 0)
    def _():
        start_x(0)
        w_copy(win_hbm, 0, 0).start()

    x_copy(xp_hbm, 0).wait()
    x = xbuf[...]
    y = x * lax.rsqrt(jnp.mean(x * x, axis=-1, keepdims=True) + EPS)
    xn_ref[...] = (y * g_ref[...]).astype(_BF)

    @pl.when(i + 1 < n_row_tiles)
    def _():
        start_x(i + 1)

    base = 0
    for k, (length, w_hbm, w_col, store) in enumerate(parts):
        def tile(jj, carry, base=base, length=length, w_hbm=w_hbm, w_col=w_col, store=store, k=k):
            n = i * tiles_per_row + base + jj
            slot = n % 2

            @pl.when(jj + 1 < length)
            def _():
                w_copy(w_hbm, w_col(jj + 1), 1 - slot).start()

            @pl.when(jj + 1 == length)
            def _():
                if k + 1 < len(parts):
                    w_copy(parts[k + 1][1], parts[k + 1][2](0), 1 - slot).start()
                else:
                    @pl.when(i + 1 < n_row_tiles)
                    def _():
                        w_copy(win_hbm, 0, 1 - slot).start()

            @pl.when(n >= 2)
            def _():
                wait_staging(slot)

            w_copy(w_hbm, w_col(jj), slot).wait()
            acc = jnp.dot(xn_ref[...], wbuf[slot].astype(_BF), preferred_element_type=_F32)
            store(acc, jj, slot)
            return carry

        lax.fori_loop(0, length, tile, 0)
        base += length

    @pl.when(i == n_row_tiles - 1)
    def _():
        wait_staging(0)
        wait_staging(1)


def _project(xp, xs, g, w_in, w_gate, b_gate, *, a_cols):
    n_p, d = xp.shape
    n_s = xs.shape[0]
    t = n_p + n_s
    tm, tn = PROJ_TM, PROJ_TN
    assert tn == B_HPG * HEAD_DIM and w_in.shape[1] == a_cols + 3 * N_DIL * tn
    assert n_p % tm == 0 and n_s % tm == 0
    gate_cols = w_gate.shape[1]
    na, ng = a_cols // tn, gate_cols // tn
    any_space = pl.BlockSpec(memory_space=pl.ANY)
    return pl.pallas_call(
        functools.partial(_proj_kernel, n_prompt_tiles=n_p // tm, n_row_tiles=t // tm, tm=tm, tn=tn, na=na, ng=ng),
        grid=(t // tm,),
        in_specs=[pl.BlockSpec((1, d), lambda i: (0, 0)), pl.BlockSpec((ng, 1, tn), lambda i: (0, 0, 0)),
                  any_space, any_space, any_space, any_space],
        out_specs=[any_space] * 5,
        out_shape=[jax.ShapeDtypeStruct((t, a_cols), _BF)]
                  + [jax.ShapeDtypeStruct((r, t // r, 3 * tn), _BF) for _, r in DILATED_GROUPS]
                  + [jax.ShapeDtypeStruct((t, gate_cols), _BF)],
        scratch_shapes=[pltpu.VMEM((tm, d), xp.dtype), pltpu.VMEM((tm, d), _BF), pltpu.VMEM((2, d, tn), w_in.dtype),
                        pltpu.VMEM((2, tn // LANES, tm, LANES), _F32), pltpu.VMEM((2, tm, tn), _BF),
                        pltpu.SemaphoreType.DMA(()), pltpu.SemaphoreType.DMA((2,)), pltpu.SemaphoreType.DMA((2,))],
        compiler_params=_params("arbitrary"),
        name="norm_proj",
    )(g, b_gate.reshape(ng, 1, tn), xp, xs, w_in, w_gate)


def _band_attn_kernel(slope_ref, sink_ref, q_ref, kl_ref, km_ref, kr_ref, vl_ref, vm_ref, vr_ref,
                      *rest, bq, half, step, shared_kv, has_sink, prompt_rows, prompt_seq, sample_seq):
    if has_sink:
        o_ref, kcat, vcat = rest
        lse_ref = None
    else:
        o_ref, lse_ref, kcat, vcat = rest
    i = pl.program_id(0)
    c = pl.program_id(1)
    win = SUBQ + 2 * half

    kcat[0:half, :] = kl_ref[...]
    kcat[half:half + bq, :] = km_ref[...]
    kcat[half + bq:, :] = kr_ref[...]
    vcat[0:half, :] = vl_ref[...]
    vcat[half:half + bq, :] = vm_ref[...]
    vcat[half + bq:, :] = vr_ref[...]

    u0 = i * bq
    in_prompt = u0 < prompt_rows
    lo = jnp.where(in_prompt, (u0 // prompt_seq) * prompt_seq,
                   prompt_rows + ((u0 - prompt_rows) // sample_seq) * sample_seq)
    hi = lo + jnp.where(in_prompt, prompt_seq, sample_seq)

    qi = lax.broadcasted_iota(jnp.int32, (SUBQ, win), 0)
    kj = lax.broadcasted_iota(jnp.int32, (SUBQ, win), 1)
    rel = kj - half - qi
    absrel = jnp.abs(rel)
    band_bias = jnp.where(absrel <= half, -(absrel * step).astype(_F32), MASKED)
    lane = lax.broadcasted_iota(jnp.int32, (SUBQ, LANES), 1)

    heads = range(HEADS_PER_STEP)
    head_ids = [c * HEADS_PER_STEP + h if shared_kv else h for h in heads]
    scale2 = HEAD_DIM ** -0.5 * LOG2_E
    head_bias = [(slope_ref[hid] * LOG2_E) * band_bias for hid in head_ids]
    n_sub = bq // SUBQ
    for sb in range(n_sub):
        kpos = kj + (u0 + sb * SUBQ - half)
        in_seq = None
        if sb == 0:
            in_seq = kpos >= lo
        if sb == n_sub - 1:
            in_seq = kpos < hi if in_seq is None else in_seq & (kpos < hi)

        def scores2(s, h):
            s = s * scale2 + head_bias[h]
            return s if in_seq is None else jnp.where(in_seq, s, MASKED)

        rows = slice(sb * SUBQ, (sb + 1) * SUBQ)
        if shared_kv:
            k = kcat[sb * SUBQ:sb * SUBQ + win, :]
            v = vcat[sb * SUBQ:sb * SUBQ + win, :]
            q4 = jnp.concatenate([q_ref[rows, h * HEAD_DIM:(h + 1) * HEAD_DIM] for h in heads], axis=0)
            s4 = lax.dot_general(q4, k, (((1,), (1,)), ((), ())), preferred_element_type=_F32)
            ps, ms, ls = [], [], []
            for h in heads:
                s = scores2(s4[h * SUBQ:(h + 1) * SUBQ], h)
                m = jnp.max(s, axis=-1, keepdims=True)
                p = jnp.exp2(s - m)
                ms.append(m)
                ls.append(jnp.sum(p, axis=-1, keepdims=True))
                ps.append(p.astype(_BF))
            pv4 = jnp.dot(jnp.concatenate(ps, axis=0), v, preferred_element_type=_F32)
            for h in heads:
                l = ls[h] + jnp.exp2(sink_ref[head_ids[h]] * LOG2_E - ms[h])
                o_ref[rows, h * HEAD_DIM:(h + 1) * HEAD_DIM] = (
                    pv4[h * SUBQ:(h + 1) * SUBQ] / l).astype(o_ref.dtype)
            continue
        lse_tile = jnp.zeros((SUBQ, LANES), _F32)
        for h in heads:
            cols = slice(h * HEAD_DIM, (h + 1) * HEAD_DIM)
            k = kcat[sb * SUBQ:sb * SUBQ + win, cols]
            v = vcat[sb * SUBQ:sb * SUBQ + win, cols]
            s = lax.dot_general(q_ref[rows, cols], k, (((1,), (1,)), ((), ())), preferred_element_type=_F32)
            s = scores2(s, h)
            m = jnp.max(s, axis=-1, keepdims=True)
            p = jnp.exp2(s - m)
            l = jnp.sum(p, axis=-1, keepdims=True)
            pv = jnp.dot(p.astype(_BF), v, preferred_element_type=_F32)
            lse_tile = jnp.where(lane == h, m * LN_2 + jnp.log(l), lse_tile)
            o_ref[rows, cols] = (pv / l).astype(o_ref.dtype)
        lse_ref[rows, :] = lse_tile


def _band_attention(qkv, slopes, sinks, *, n_inner, bq, half, step, shared_kv, has_sink,
                    prompt_rows, prompt_seq, sample_seq, name):
    lead, rows, _ = qkv.shape
    assert prompt_seq % bq == 0 and sample_seq % bq == 0 and bq % SUBQ == 0 and bq % half == 0
    assert half <= SUBQ
    nq = rows // bq
    hb = bq // half
    last_halo = rows // half - 1
    width = HEADS_PER_STEP * HEAD_DIM
    if shared_kv:
        kv_w = HEAD_DIM
        k_base, v_base = width * n_inner // HEAD_DIM, width * n_inner // HEAD_DIM + n_inner
        lead_of = lambda c: 0
        q_col = lambda c: c
        kv_col = lambda base: (lambda c: base + c)
    else:
        kv_w = width
        k_base, v_base = 1, 2
        lead_of = lambda c: c
        q_col = lambda c: 0
        kv_col = lambda base: (lambda c: base)

    def main_map(col):
        return lambda i, c: (lead_of(c), i, col(c))

    def left_map(col):
        return lambda i, c: (lead_of(c), jnp.maximum(i * hb - 1, 0), col(c))

    def right_map(col):
        return lambda i, c: (lead_of(c), jnp.minimum((i + 1) * hb, last_halo), col(c))

    smem = pl.BlockSpec(memory_space=pltpu.SMEM)
    in_specs = [
        smem, smem,
        pl.BlockSpec((None, bq, width), main_map(q_col)),
        pl.BlockSpec((None, half, kv_w), left_map(kv_col(k_base))),
        pl.BlockSpec((None, bq, kv_w), main_map(kv_col(k_base))),
        pl.BlockSpec((None, half, kv_w), right_map(kv_col(k_base))),
        pl.BlockSpec((None, half, kv_w), left_map(kv_col(v_base))),
        pl.BlockSpec((None, bq, kv_w), main_map(kv_col(v_base))),
        pl.BlockSpec((None, half, kv_w), right_map(kv_col(v_base))),
    ]
    if has_sink:
        out_specs = pl.BlockSpec((bq, width), lambda i, c: (i, c))
        out_shape = jax.ShapeDtypeStruct((rows, n_inner * width), _BF)
    else:
        out_specs = (pl.BlockSpec((None, bq, width), lambda i, c: (c, i, 0)),
                     pl.BlockSpec((None, bq, LANES), lambda i, c: (c, i, 0)))
        out_shape = (jax.ShapeDtypeStruct((lead, rows, width), _BF),
                     jax.ShapeDtypeStruct((lead, rows, LANES), _F32))
    return pl.pallas_call(
        functools.partial(_band_attn_kernel, bq=bq, half=half, step=step, shared_kv=shared_kv,
                          has_sink=has_sink, prompt_rows=prompt_rows, prompt_seq=prompt_seq,
                          sample_seq=sample_seq),
        grid=(nq, n_inner),
        in_specs=in_specs,
        out_specs=out_specs,
        out_shape=out_shape,
        scratch_shapes=[pltpu.VMEM((bq + 2 * half, kv_w), _BF), pltpu.VMEM((bq + 2 * half, kv_w), _BF)],
        compiler_params=_params("parallel", "arbitrary"),
        name=name,
    )(slopes, sinks, qkv, qkv, qkv, qkv, qkv, qkv, qkv)


def _mix_kernel(xp_ref, xs_ref, oa_ref, ob0_ref, ob1_ref, ob2_ref, l0_ref, l1_ref, l2_ref, g_ref,
                wpa_ref, wpb_ref, wout_ref, n2_ref, wr_ref, br_ref, h_ref, hn_ref, ri_ref, rw_ref, cnt_ref,
                o_scr, l_scr, carry_ref, lg_scr, *, n_prompt_tiles, d, tm):
    i = pl.program_id(0)

    @pl.when(i == 0)
    def _():
        carry_ref[...] = jnp.zeros_like(carry_ref)
        lg_scr[...] = jnp.zeros_like(lg_scr)

    _route_tile(lg_scr[...], jnp.where(i > 0, 1.0, 0.0), ri_ref, rw_ref, cnt_ref, carry_ref)

    def token_order(o_ref, l_ref, r, slot):
        if r == 1:
            return [o_ref[0, :, h * HEAD_DIM:(h + 1) * HEAD_DIM].astype(_F32) for h in range(B_HPG)], l_ref[0]
        for p in range(r):
            for h in range(B_HPG):
                o_scr[slot, h, pl.ds(p, tm // r, stride=r), :] = (
                    o_ref[p, :, h * HEAD_DIM:(h + 1) * HEAD_DIM].astype(_F32))
            l_scr[slot, pl.ds(p, tm // r, stride=r), :] = l_ref[p]
        return [o_scr[slot, h] for h in range(B_HPG)], l_scr[slot]

    o0, l0 = token_order(ob0_ref, l0_ref, DILATED_GROUPS[0][1], 0)
    o1, l1 = token_order(ob1_ref, l1_ref, DILATED_GROUPS[1][1], 0)
    o2, l2 = token_order(ob2_ref, l2_ref, DILATED_GROUPS[2][1], 1)
    mx = jnp.maximum(jnp.maximum(l0, l1), l2)
    e0, e1, e2 = jnp.exp(l0 - mx), jnp.exp(l1 - mx), jnp.exp(l2 - mx)
    den = e0 + e1 + e2
    a0, a1, a2 = e0 / den, e1 / den, e2 / den
    parts = []
    for h in range(B_HPG):
        parts.append(a0[:, h:h + 1] * o0[h] + a1[:, h:h + 1] * o1[h] + a2[:, h:h + 1] * o2[h])
    ob = jnp.concatenate(parts, axis=1).astype(_BF)

    ta = jnp.dot(oa_ref[...], wpa_ref[...], preferred_element_type=_F32)
    tb = jnp.dot(ob, wpb_ref[...], preferred_element_type=_F32)
    merged = g_ref[:, :d].astype(_F32) * ta + g_ref[:, d:].astype(_F32) * tb
    x = jnp.where(i < n_prompt_tiles, xp_ref[...], xs_ref[...])
    h_new = x + jnp.dot(merged.astype(_BF), wout_ref[...], preferred_element_type=_F32)
    h_ref[...] = h_new
    hn = h_new * lax.rsqrt(jnp.mean(h_new * h_new, axis=-1, keepdims=True) + EPS) * n2_ref[...]
    _store_row_tiles(hn_ref, _pack_halves(hn))
    hn_hi = hn.astype(_BF)
    hn_lo = (hn - hn_hi.astype(_F32)).astype(_BF)
    r = (jnp.dot(hn_hi, wr_ref[...], preferred_element_type=_F32)
         + jnp.dot(hn_lo, wr_ref[...], preferred_element_type=_F32))
    lg_scr[...] = r[:, :LANES] + r[:, LANES:] + br_ref[...]


def _mix(xp, xs, oa, obs, lses, gates, wpa, wpb, wout, n2, wr, br):
    n_p, d = xp.shape
    n_s = xs.shape[0]
    t = n_p + n_s
    tm = ROW_TILE
    npt = n_p // tm
    n_tiles = t // tm
    row = lambda i: (jnp.minimum(i, n_tiles - 1), 0)
    routed = lambda i: (jnp.maximum(i - 1, 0), 0)
    const = lambda i: (0, 0)
    width = B_HPG * HEAD_DIM

    def resident(shape):
        return pl.BlockSpec(shape, const, pipeline_mode=pl.Buffered(1))

    def phase_blocks(cols):
        return [pl.BlockSpec((r, tm // r, cols), lambda i: (0, jnp.minimum(i, n_tiles - 1), 0))
                for _, r in DILATED_GROUPS]

    in_specs = [
        pl.BlockSpec((tm, d), lambda i: (jnp.minimum(i, npt - 1), 0)),
        pl.BlockSpec((tm, d), lambda i: (jnp.clip(i - npt, 0, n_tiles - npt - 1), 0)),
        pl.BlockSpec((tm, oa.shape[1]), row),
        *phase_blocks(width), *phase_blocks(LANES),
        pl.BlockSpec((tm, 2 * d), row),
        resident(wpa.shape), resident(wpb.shape), resident(wout.shape),
        resident((1, d)), resident(wr.shape), resident((1, LANES)),
    ]
    return pl.pallas_call(
        functools.partial(_mix_kernel, n_prompt_tiles=npt, d=d, tm=tm),
        grid=(n_tiles + 1,),
        in_specs=in_specs,
        out_specs=(pl.BlockSpec((tm, d), row), pl.BlockSpec((tm * (d // 2 // LANES), LANES), row),
                   pl.BlockSpec((tm, LANES), routed), pl.BlockSpec((tm, LANES), routed),
                   pl.BlockSpec((SUBLANES, LANES), const)),
        out_shape=(jax.ShapeDtypeStruct((t, d), _F32),
                   jax.ShapeDtypeStruct((t * (d // 2 // LANES), LANES), jnp.uint32),
                   jax.ShapeDtypeStruct((t, LANES), jnp.int32),
                   jax.ShapeDtypeStruct((t, LANES), _F32),
                   jax.ShapeDtypeStruct((SUBLANES, LANES), jnp.int32)),
        scratch_shapes=[pltpu.VMEM((2, B_HPG, tm, HEAD_DIM), _F32), pltpu.VMEM((2, tm, LANES), _F32),
                        pltpu.VMEM((SUBLANES, LANES), _F32), pltpu.VMEM((tm, LANES), _F32)],
        compiler_params=_params("arbitrary"),
        name="mix_out_router",
    )(xp, xs, oa, *obs, *lses, gates, wpa, wpb, wout, n2, wr, br)


def _route_tile(lg, live, ri_ref, rw_ref, cnt_ref, carry_ref):
    tr = lg.shape[0]
    lane_i = lax.broadcasted_iota(jnp.int32, (tr, LANES), 1)
    lane = lane_i.astype(_F32)
    no_lane = float(LANES)
    is_grp = lane_i < N_GROUPS
    glog = jnp.where(is_grp, lg, MASKED)
    gmax = jnp.max(glog, axis=-1, keepdims=True)
    grp = jnp.min(jnp.where(glog == gmax, lane, no_lane), axis=-1, keepdims=True)
    gsum = jnp.sum(jnp.where(is_grp, jnp.exp(glog - gmax), 0.0), axis=-1, keepdims=True)
    pgrp = 1.0 / gsum
    lane_grp = ((lane_i - N_GROUPS) // EXPERTS_PER_GROUP).astype(_F32)
    in_grp = (lane_i >= N_GROUPS) & (lane_i < N_GROUPS + N_EXPERTS) & (lane_grp == grp)
    elog = jnp.where(in_grp, lg, MASKED)
    t1 = jnp.max(elog, axis=-1, keepdims=True)
    i1 = jnp.min(jnp.where(elog == t1, lane, no_lane), axis=-1, keepdims=True)
    elog2 = jnp.where(lane == i1, MASKED, elog)
    t2 = jnp.max(elog2, axis=-1, keepdims=True)
    i2 = jnp.min(jnp.where(elog2 == t2, lane, no_lane), axis=-1, keepdims=True)
    e21 = jnp.exp(t2 - t1)
    w1 = pgrp / (1.0 + e21)
    w2 = pgrp * e21 / (1.0 + e21)
    eid1 = i1 - N_GROUPS
    eid2 = i2 - N_GROUPS
    hot1 = lane == eid1
    hot2 = lane == eid2
    onehot = (jnp.where(hot1, 1.0, 0.0) + jnp.where(hot2, 1.0, 0.0)) * live
    r_i = lax.broadcasted_iota(jnp.int32, (tr, tr), 0)
    c_i = lax.broadcasted_iota(jnp.int32, (tr, tr), 1)
    lower = jnp.where(c_i < r_i, 1.0, 0.0).astype(_BF)
    before = jnp.dot(lower, onehot.astype(_BF), preferred_element_type=_F32) + carry_ref[0:1, :]
    rank1 = jnp.sum(jnp.where(hot1, before, 0.0), axis=-1, keepdims=True)
    rank2 = jnp.sum(jnp.where(hot2, before, 0.0), axis=-1, keepdims=True)
    total = carry_ref[0:1, :] + jnp.sum(onehot, axis=0, keepdims=True)
    carry_ref[...] = jnp.broadcast_to(total, carry_ref.shape)
    cnt_ref[...] = jnp.broadcast_to(total, cnt_ref.shape).astype(jnp.int32)
    ri = jnp.where(lane_i == 0, eid1, jnp.where(lane_i == 1, eid2, jnp.where(lane_i == 2, rank1,
                   jnp.where(lane_i == 3, rank2, 0.0))))
    ri_ref[...] = ri.astype(jnp.int32)
    rw_ref[...] = jnp.where(lane_i == 0, w1, jnp.where(lane_i == 1, w2, 0.0))


def _dispatch_kernel(dest_ref, pad_start_ref, pad_n_ref, nvalid_ref, hn_ref, xr_ref, zeros, sem, pad_sem,
                     *, td, sc):
    i = pl.program_id(0)

    def rows_of(ref, row0, n):
        start = row0 * sc if isinstance(row0, int) else pl.multiple_of(row0 * sc, n * sc)
        return ref.at[pl.ds(start, n * sc), :]

    @pl.when(i == 0)
    def _():
        zeros[...] = jnp.zeros_like(zeros)

        def pad_chunk(e, r):
            row0 = pl.multiple_of(pad_start_ref[e] + r * SUBLANES, SUBLANES)
            return pltpu.make_async_copy(rows_of(zeros, 0, SUBLANES), rows_of(xr_ref, row0, SUBLANES), pad_sem)

        def per_expert(act):
            def body(e, carry):
                lax.fori_loop(0, pad_n_ref[e], lambda r, c: (act(pad_chunk(e, r)), c)[1], 0)
                return carry
            return body

        def tail_block(b):
            return pltpu.make_async_copy(zeros, rows_of(xr_ref, b * EXPERT_ROWS, EXPERT_ROWS), pad_sem)

        n_blocks = xr_ref.shape[0] // (EXPERT_ROWS * sc)
        lax.fori_loop(0, N_EXPERTS, per_expert(lambda cp: cp.start()), 0)
        lax.fori_loop(nvalid_ref[0], n_blocks, lambda b, c: (tail_block(b).start(), c)[1], 0)
        lax.fori_loop(0, N_EXPERTS, per_expert(lambda cp: cp.wait()), 0)
        lax.fori_loop(nvalid_ref[0], n_blocks, lambda b, c: (tail_block(b).wait(), c)[1], 0)

    def start(j, carry):
        for k in range(2):
            dst = dest_ref[2 * (i * td + j) + k]
            pltpu.make_async_copy(rows_of(hn_ref, j, 1), rows_of(xr_ref, dst, 1), sem).start()
        return carry

    lax.fori_loop(0, td, start, 0, unroll=DMA_UNROLL)
    for _ in range(2):
        pltpu.make_async_copy(hn_ref, rows_of(xr_ref, 0, td), sem).wait()


def _dispatch(dest, pad_start, pad_n, nvalid, hn, t, rows):
    sc = hn.shape[0] // t
    td = ROW_TILE
    return pl.pallas_call(
        functools.partial(_dispatch_kernel, td=td, sc=sc),
        grid_spec=pltpu.PrefetchScalarGridSpec(
            num_scalar_prefetch=4,
            grid=(t // td,),
            in_specs=[pl.BlockSpec((td * sc, LANES), lambda i, *_: (i, 0))],
            out_specs=pl.BlockSpec(memory_space=pl.ANY),
            scratch_shapes=[pltpu.VMEM((EXPERT_ROWS * sc, LANES), hn.dtype), pltpu.SemaphoreType.DMA(()),
                            pltpu.SemaphoreType.DMA(())],
        ),
        out_shape=jax.ShapeDtypeStruct((rows * sc, LANES), hn.dtype),
        compiler_params=_params("arbitrary"),
        name="dispatch",
    )(dest, pad_start, pad_n, nvalid, hn)


def _expert_kernel(blk_e_ref, nvalid_ref, first_ref, slot_ref, next_e_ref, x_ref, w1_hbm, w3_hbm, w2_hbm, y_ref,
                   w1_buf, w3_buf, w2_buf, sems):
    b = pl.program_id(0)
    valid = b < nvalid_ref[0]

    def fetch(e, slot):
        return (pltpu.make_async_copy(w1_hbm.at[e], w1_buf.at[slot], sems.at[slot, 0]),
                pltpu.make_async_copy(w3_hbm.at[e], w3_buf.at[slot], sems.at[slot, 1]),
                pltpu.make_async_copy(w2_hbm.at[e], w2_buf.at[slot], sems.at[slot, 2]))

    @pl.when(b == 0)
    def _():
        for cp in fetch(blk_e_ref[0], 0):
            cp.start()

    @pl.when(valid & (first_ref[b] == 1))
    def _():
        for cp in fetch(blk_e_ref[b], slot_ref[b]):
            cp.wait()

        @pl.when(next_e_ref[b] >= 0)
        def _():
            for cp in fetch(next_e_ref[b], 1 - slot_ref[b]):
                cp.start()

    @pl.when(valid)
    def _():
        slot = slot_ref[b]
        packed_x = _load_row_tiles(x_ref, EXPERT_ROWS)
        xb = jnp.concatenate([half.astype(_BF) for half in _unpack_halves(packed_x)], axis=1)
        a = jnp.dot(xb, w1_buf[slot].astype(_BF), preferred_element_type=_F32)
        u = jnp.dot(xb, w3_buf[slot].astype(_BF), preferred_element_type=_F32)
        hmid = (a / (1.0 + jnp.exp(-a))) * u
        y = jnp.dot(hmid.astype(_BF), w2_buf[slot].astype(_BF), preferred_element_type=_F32)
        _store_row_tiles(y_ref, _pack_halves(y))

    @pl.when(jnp.logical_not(valid))
    def _():
        y_ref[...] = jnp.zeros_like(y_ref)


def _experts(blk_e, nvalid, first, slot, next_e, xr, w1, w3, w2):
    _, d, f = w1.shape
    sc = d // 2 // LANES
    blk = EXPERT_ROWS * sc
    nblk = xr.shape[0] // blk
    any_space = pl.BlockSpec(memory_space=pl.ANY)
    return pl.pallas_call(
        _expert_kernel,
        grid_spec=pltpu.PrefetchScalarGridSpec(
            num_scalar_prefetch=5,
            grid=(nblk,),
            in_specs=[pl.BlockSpec((blk, LANES), lambda b, be, nv, *_: (jnp.minimum(b, nv[0] - 1), 0)),
                      any_space, any_space, any_space],
            out_specs=pl.BlockSpec((blk, LANES), lambda b, *_: (b, 0)),
            scratch_shapes=[pltpu.VMEM((2, d, f), w1.dtype), pltpu.VMEM((2, d, f), w3.dtype),
                            pltpu.VMEM((2, f, d), w2.dtype), pltpu.SemaphoreType.DMA((2, 3))],
        ),
        out_shape=jax.ShapeDtypeStruct(xr.shape, jnp.uint32),
        compiler_params=_params("arbitrary"),
        name="experts",
    )(blk_e, nvalid, first, slot, next_e, xr, w1, w3, w2)


def _combine_kernel(dest_ref, h_ref, rw_ref, g_ref, yr_ref, op_ref, os_ref, ybuf, sems, *, tc, sc, n_prompt_tiles):
    i = pl.program_id(0)
    slot = i % 2

    def gather(tile, s):
        def start(j, carry):
            for k in range(2):
                src = dest_ref[2 * (tile * tc + j) + k]
                pltpu.make_async_copy(_row_tile(yr_ref, src, sc), _row_tile(ybuf.at[s, k], j, sc),
                                      sems.at[s]).start()
            return carry

        lax.fori_loop(0, tc, start, 0, unroll=DMA_UNROLL)

    @pl.when(i == 0)
    def _():
        gather(0, 0)

    @pl.when(i + 1 < pl.num_programs(0))
    def _():
        gather(i + 1, 1 - slot)

    for k in range(2):
        pltpu.make_async_copy(yr_ref.at[pl.ds(0, tc * sc), :], ybuf.at[slot, k], sems.at[slot]).wait()

    rw = rw_ref[...]
    y0 = _unpack_halves(_load_row_tiles(ybuf.at[slot, 0], tc))
    y1 = _unpack_halves(_load_row_tiles(ybuf.at[slot, 1], tc))
    y = jnp.concatenate([rw[:, 0:1] * a + rw[:, 1:2] * b for a, b in zip(y0, y1)], axis=1)
    z = h_ref[...] + y
    out = z * lax.rsqrt(jnp.mean(z * z, axis=-1, keepdims=True) + EPS) * g_ref[...]

    @pl.when(i < n_prompt_tiles)
    def _():
        op_ref[...] = out

    @pl.when(i >= n_prompt_tiles)
    def _():
        os_ref[...] = out


def _combine(dest, h, rw, g, yr, *, n_p):
    t, d = h.shape
    tc = ROW_TILE
    npt = n_p // tc
    sc = d // 2 // LANES
    return pl.pallas_call(
        functools.partial(_combine_kernel, tc=tc, sc=sc, n_prompt_tiles=npt),
        grid_spec=pltpu.PrefetchScalarGridSpec(
            num_scalar_prefetch=1,
            grid=(t // tc,),
            in_specs=[
                pl.BlockSpec((tc, d), lambda i, dest: (i, 0)),
                pl.BlockSpec((tc, LANES), lambda i, dest: (i, 0)),
                pl.BlockSpec((1, d), lambda i, dest: (0, 0)),
                pl.BlockSpec(memory_space=pl.ANY),
            ],
            out_specs=(pl.BlockSpec((tc, d), lambda i, dest: (jnp.minimum(i, npt - 1), 0)),
                       pl.BlockSpec((tc, d), lambda i, dest: (jnp.maximum(i - npt, 0), 0))),
            scratch_shapes=[pltpu.VMEM((2, 2, tc * sc, LANES), yr.dtype), pltpu.SemaphoreType.DMA((2,))],
        ),
        out_shape=(jax.ShapeDtypeStruct((n_p, d), _F32), jax.ShapeDtypeStruct((t - n_p, d), _F32)),
        compiler_params=_params("arbitrary"),
        name="combine_norm",
    )(dest, h, rw, g, yr)


def _layer(xp, xs, prompt_seq, sample_seq, norm1, w_in, sink, w_proj_a, w_proj_b, w_gate, b_gate, w_out,
           norm2, w_rg, b_rg, w_re, b_re, w1, w3, w2, norm_final):
    n_p, d = xp.shape
    n_s = xs.shape[0]
    t = n_p + n_s
    a_q = A_HEADS * HEAD_DIM
    a_cols = a_q + 2 * A_KV * HEAD_DIM

    proj_a, q0, q1, q2, gates = _project(xp, xs, norm1.reshape(1, d), w_in, w_gate, b_gate.reshape(1, 2 * d),
                                         a_cols=a_cols)

    oa = _band_attention(
        proj_a.reshape(1, t, a_cols), jnp.asarray(_alibi(A_HEADS)), sink.astype(_F32), n_inner=A_KV,
        bq=min(1024, prompt_seq, sample_seq), half=A_HALF, step=1, shared_kv=True, has_sink=True,
        prompt_rows=n_p, prompt_seq=prompt_seq, sample_seq=sample_seq, name="attn_window")

    slopes_b = _alibi(B_HEADS)
    obs, lses = [], []
    for gi, ((w, r), qkv) in enumerate(zip(DILATED_GROUPS, (q0, q1, q2))):
        o_g, lse_g = _band_attention(
            qkv, jnp.asarray(slopes_b[gi * B_HPG:(gi + 1) * B_HPG]), jnp.zeros((B_HPG,), _F32), n_inner=r,
            bq=min(512, prompt_seq // r, sample_seq // r), half=w // (2 * r), step=r, shared_kv=False,
            has_sink=False, prompt_rows=n_p // r, prompt_seq=prompt_seq // r, sample_seq=sample_seq // r,
            name=f"attn_dilated_{r}")
        obs.append(o_g)
        lses.append(lse_g)

    n_r = N_GROUPS + N_EXPERTS
    wr = jnp.concatenate([w_rg, jnp.transpose(w_re, (1, 0, 2)).reshape(d, N_EXPERTS)], axis=1)
    wr = jnp.pad(wr, ((0, 0), (0, LANES - n_r)))
    wr_hi = wr.astype(_BF)
    wr_lo = (wr - wr_hi.astype(_F32)).astype(_BF)
    wr2 = jnp.concatenate([wr_hi, wr_lo], axis=1)
    br = jnp.pad(jnp.concatenate([b_rg, b_re.reshape(-1)]), (0, LANES - n_r)).reshape(1, LANES).astype(_F32)

    h, hn, ri, rw, cnt = _mix(xp, xs, oa, obs, lses, gates, w_proj_a.astype(_BF), w_proj_b.astype(_BF),
                              w_out.astype(_BF), norm2.reshape(1, d), wr2, br)
    counts = cnt[0, :N_EXPERTS]
    eid = ri[:, 0:2]
    rank = ri[:, 2:4]
    pcounts = (counts + EXPERT_ROWS - 1) // EXPERT_ROWS * EXPERT_ROWS
    pends = jnp.cumsum(pcounts)
    pstarts = pends - pcounts
    expert_ids = jnp.arange(N_EXPERTS, dtype=jnp.int32)
    start_of = jnp.sum(jnp.where(eid[:, :, None] == expert_ids, pstarts.astype(jnp.int32), 0), axis=-1)
    dest = (start_of + rank).reshape(-1).astype(jnp.int32)
    nblk = (2 * t + N_EXPERTS * (EXPERT_ROWS - 1) + EXPERT_ROWS - 1) // EXPERT_ROWS
    nvalid = (pends[-1] // EXPERT_ROWS).astype(jnp.int32)
    blk_start = jnp.minimum(jnp.arange(nblk, dtype=jnp.int32), nvalid - 1) * EXPERT_ROWS
    blk_e = jnp.sum(pends[None, :] <= blk_start[:, None], axis=1).astype(jnp.int32)

    pad_start = (pstarts + counts) // SUBLANES * SUBLANES
    xr = _dispatch(dest, pad_start.astype(jnp.int32), ((pends - pad_start) // SUBLANES).astype(jnp.int32),
                   nvalid.reshape(1), hn, t, nblk * EXPERT_ROWS)
    blk_ids = jnp.arange(nblk, dtype=jnp.int32)
    first = ((blk_ids == 0) | (blk_e != jnp.roll(blk_e, 1))).astype(jnp.int32)
    slot = ((jnp.cumsum(first) - 1) % 2).astype(jnp.int32)
    later = (expert_ids[None, :] > expert_ids[:, None]) & (counts[None, :] > 0)
    next_of = jnp.min(jnp.where(later, expert_ids[None, :], N_EXPERTS), axis=1)
    next_e = jnp.sum(jnp.where(blk_e[:, None] == expert_ids, next_of, 0), axis=1)
    next_e = jnp.where(next_e < N_EXPERTS, next_e, -1).astype(jnp.int32)
    yr = _experts(blk_e, nvalid.reshape(1), first, slot, next_e, xr, w1, w3, w2)
    return _combine(dest, h, rw, norm_final.reshape(1, d), yr, n_p=n_p)


def kernel(x_prompt, x_sample, norm1, w_in, attn_sink, w_proj_a, w_proj_b, w_gate, b_gate, w_out, norm2,
           w_router_group, b_router_group, w_router_expert, b_router_expert, w_expert_gate, w_expert_up,
           w_expert_down, norm_final):
    assert norm1.shape[0] == 1, "one layer"
    d = x_prompt.shape[-1]
    xp = x_prompt.reshape(-1, d)
    xs = x_sample.reshape(-1, d)
    yp, ys = _layer(xp, xs, x_prompt.shape[1], x_sample.shape[1], norm1[0], w_in[0], attn_sink[0], w_proj_a[0],
                    w_proj_b[0], w_gate[0], b_gate[0], w_out[0], norm2[0], w_router_group[0],
                    b_router_group[0], w_router_expert[0], b_router_expert[0], w_expert_gate[0],
                    w_expert_up[0], w_expert_down[0], norm_final)
    return yp.reshape(x_prompt.shape), ys.reshape(x_sample.shape)
```

```python
import functools
import math

import jax
import jax.numpy as jnp
import numpy as np
from jax import lax
from jax.experimental import pallas as pl
from jax.experimental.pallas import tpu as pltpu

HEAD_DIM = 128
A_HEADS = 16
A_KV = 4
A_HALF = 128
DILATED_GROUPS = ((128, 1), (512, 4), (2048, 16))
N_DIL = len(DILATED_GROUPS)
B_HPG = 4
B_HEADS = N_DIL * B_HPG
N_GROUPS = 8
EXPERTS_PER_GROUP = 8
N_EXPERTS = N_GROUPS * EXPERTS_PER_GROUP
EPS = 1e-6

LANES = 128
SUBLANES = 8
MASKED = -1e30
LOG2_E = math.log2(math.e)
LN_2 = math.log(2.0)
VMEM_LIMIT = 56 * 1024 * 1024
SUBQ = 128
HEADS_PER_STEP = 4
EXPERT_ROWS = 256
DMA_UNROLL = 8
PROJ_TM, PROJ_TN = 2048, 512
MAX_ROW_STRIDE = 4
ROW_TILE = 256

_BF = jnp.bfloat16
_F32 = jnp.float32


def _alibi(n):
    return np.power(2.0, -8.0 * (np.arange(n) + 1) / n).astype(np.float32)


def _params(*sem):
    return pltpu.CompilerParams(dimension_semantics=sem, vmem_limit_bytes=VMEM_LIMIT)


def _pack_halves(x):
    n = x.shape[1] // 2

    def bf16_bits(v):
        return lax.bitcast_convert_type(v.astype(_BF).astype(_F32), jnp.uint32)

    return bf16_bits(x[:, n:]) | (bf16_bits(x[:, :n]) >> 16)


def _unpack_halves(p):
    lo = lax.bitcast_convert_type(p << 16, _F32)
    hi = lax.bitcast_convert_type(p & jnp.uint32(0xFFFF0000), _F32)
    return lo, hi


def _store_row_tiles(ref, packed):
    rows, words = packed.shape
    s_count = words // LANES
    for s in range(s_count):
        ref[pl.ds(s, rows, stride=s_count), :] = packed[:, s * LANES:(s + 1) * LANES]


def _load_row_tiles(ref, rows):
    s_count = ref.shape[0] // rows
    return jnp.concatenate([ref[pl.ds(s, rows, stride=s_count), :] for s in range(s_count)], axis=1)


def _row_tile(ref, row, s_count):
    return ref.at[pl.ds(pl.multiple_of(row * s_count, s_count), s_count), :]


def _proj_kernel(g_ref, b_ref, xp_hbm, xs_hbm, win_hbm, wg_hbm, oa_hbm, q0_hbm, q1_hbm, q2_hbm, gt_hbm,
                 xbuf, xn_ref, wbuf, acc_ref, obuf, x_sem, w_sems, o_sems,
                 *, n_prompt_tiles, n_row_tiles, tm, tn, na, ng):
    i = pl.program_id(0)
    nb = 3
    tiles_per_row = na + N_DIL * nb + ng
    rows = pl.ds(pl.multiple_of(i * tm, tm), tm)

    def cols(jj):
        return pl.ds(pl.multiple_of(jj * tn, tn), tn)

    def x_copy(x_hbm, row_tile):
        return pltpu.make_async_copy(x_hbm.at[pl.ds(pl.multiple_of(row_tile * tm, tm), tm), :], xbuf, x_sem)

    def start_x(row_tile):
        @pl.when(row_tile < n_prompt_tiles)
        def _():
            x_copy(xp_hbm, row_tile).start()

        @pl.when(row_tile >= n_prompt_tiles)
        def _():
            x_copy(xs_hbm, row_tile - n_prompt_tiles).start()

    def w_copy(w_hbm, col, slot):
        return pltpu.make_async_copy(w_hbm.at[:, cols(col)], wbuf.at[slot], w_sems.at[slot])

    def wait_staging(slot):
        pltpu.make_async_copy(obuf.at[slot], oa_hbm.at[pl.ds(0, tm), pl.ds(0, tn)], o_sems.at[slot]).wait()

    def store_plain(out_hbm):
        def store(acc, jj, slot):
            obuf[slot] = acc.astype(_BF)
            pltpu.make_async_copy(obuf.at[slot], out_hbm.at[rows, cols(jj)], o_sems.at[slot]).start()
        return store

    def store_gate(acc, jj, slot):
        z = acc + b_ref[jj]
        obuf[slot] = (0.5 * jnp.tanh(0.5 * z) + 0.5).astype(_BF)
        pltpu.make_async_copy(obuf.at[slot], gt_hbm.at[rows, cols(jj)], o_sems.at[slot]).start()

    def store_phases(q_hbm, r):
        n = tm // r

        def store(acc, jj, slot):
            if r == 1:
                obuf[slot] = acc.astype(_BF)
            else:
                for cb in range(tn // LANES):
                    acc_ref[0, cb] = acc[:, cb * LANES:(cb + 1) * LANES]
                src, stride = 0, r
                if r > MAX_ROW_STRIDE:
                    stride = r // MAX_ROW_STRIDE
                    m = tm // MAX_ROW_STRIDE
                    for c in range(MAX_ROW_STRIDE):
                        for cb in range(tn // LANES):
                            acc_ref[1, cb, c * m:(c + 1) * m, :] = acc_ref[0, cb, pl.ds(c, m, stride=MAX_ROW_STRIDE), :]
                    src = 1
                for p in range(r):
                    start = p if src == 0 else (p % MAX_ROW_STRIDE) * (tm // MAX_ROW_STRIDE) + p // MAX_ROW_STRIDE
                    for cb in range(tn // LANES):
                        obuf[slot, p * n:(p + 1) * n, cb * LANES:(cb + 1) * LANES] = (
                            acc_ref[src, cb, pl.ds(start, n, stride=stride), :].astype(_BF))
            for p in range(r):
                pltpu.make_async_copy(obuf.at[slot, pl.ds(p * n, n), :],
                                      q_hbm.at[p, pl.ds(pl.multiple_of(i * n, n), n), cols(jj)],
                                      o_sems.at[slot]).start()
        return store

    parts = [(na, win_hbm, lambda jj: jj, store_plain(oa_hbm))]
    for gi, (q_hbm, (_, r)) in enumerate(zip((q0_hbm, q1_hbm, q2_hbm), DILATED_GROUPS)):
        parts.append((nb, win_hbm, lambda jj, gi=gi: na + jj * N_DIL + gi, store_phases(q_hbm, r)))
    parts.append((ng, wg_hbm, lambda jj: jj, store_gate))

    @pl.when(i == 0)
    def _():
        start_x(0)
        w_copy(win_hbm, 0, 0).start()

    x_copy(xp_hbm, 0).wait()
    x = xbuf[...]
    y = x * lax.rsqrt(jnp.mean(x * x, axis=-1, keepdims=True) + EPS)
    xn_ref[...] = (y * g_ref[...]).astype(_BF)

    @pl.when(i + 1 < n_row_tiles)
    def _():
        start_x(i + 1)

    base = 0
    for k, (length, w_hbm, w_col, store) in enumerate(parts):
        def tile(jj, carry, base=base, length=length, w_hbm=w_hbm, w_col=w_col, store=store, k=k):
            n = i * tiles_per_row + base + jj
            slot = n % 2

            @pl.when(jj + 1 < length)
            def _():
                w_copy(w_hbm, w_col(jj + 1), 1 - slot).start()

            @pl.when(jj + 1 == length)
            def _():
                if k + 1 < len(parts):
                    w_copy(parts[k + 1][1], parts[k + 1][2](0), 1 - slot).start()
                else:
                    @pl.when(i + 1 < n_row_tiles)
                    def _():
                        w_copy(win_hbm, 0, 1 - slot).start()

            @pl.when(n >= 2)
            def _():
                wait_staging(slot)

            w_copy(w_hbm, w_col(jj), slot).wait()
            acc = jnp.dot(xn_ref[...], wbuf[slot].astype(_BF), preferred_element_type=_F32)
            store(acc, jj, slot)
            return carry

        lax.fori_loop(0, length, tile, 0)
        base += length

    @pl.when(i == n_row_tiles - 1)
    def _():
        wait_staging(0)
        wait_staging(1)


def _project(xp, xs, g, w_in, w_gate, b_gate, *, a_cols):
    n_p, d = xp.shape
    n_s = xs.shape[0]
    t = n_p + n_s
    tm, tn = PROJ_TM, PROJ_TN
    assert tn == B_HPG * HEAD_DIM and w_in.shape[1] == a_cols + 3 * N_DIL * tn
    assert n_p % tm == 0 and n_s % tm == 0
    gate_cols = w_gate.shape[1]
    na, ng = a_cols // tn, gate_cols // tn
    any_space = pl.BlockSpec(memory_space=pl.ANY)
    return pl.pallas_call(
        functools.partial(_proj_kernel, n_prompt_tiles=n_p // tm, n_row_tiles=t // tm, tm=tm, tn=tn, na=na, ng=ng),
        grid=(t // tm,),
        in_specs=[pl.BlockSpec((1, d), lambda i: (0, 0)), pl.BlockSpec((ng, 1, tn), lambda i: (0, 0, 0)),
                  any_space, any_space, any_space, any_space],
        out_specs=[any_space] * 5,
        out_shape=[jax.ShapeDtypeStruct((t, a_cols), _BF)]
                  + [jax.ShapeDtypeStruct((r, t // r, 3 * tn), _BF) for _, r in DILATED_GROUPS]
                  + [jax.ShapeDtypeStruct((t, gate_cols), _BF)],
        scratch_shapes=[pltpu.VMEM((tm, d), xp.dtype), pltpu.VMEM((tm, d), _BF), pltpu.VMEM((2, d, tn), w_in.dtype),
                        pltpu.VMEM((2, tn // LANES, tm, LANES), _F32), pltpu.VMEM((2, tm, tn), _BF),
                        pltpu.SemaphoreType.DMA(()), pltpu.SemaphoreType.DMA((2,)), pltpu.SemaphoreType.DMA((2,))],
        compiler_params=_params("arbitrary"),
        name="norm_proj",
    )(g, b_gate.reshape(ng, 1, tn), xp, xs, w_in, w_gate)


def _band_attn_kernel(slope_ref, sink_ref, q_ref, kl_ref, km_ref, kr_ref, vl_ref, vm_ref, vr_ref,
                      *rest, bq, half, step, shared_kv, has_sink, prompt_rows, prompt_seq, sample_seq):
    if has_sink:
        o_ref, kcat, vcat = rest
        lse_ref = None
    else:
        o_ref, lse_ref, kcat, vcat = rest
    i = pl.program_id(0)
    c = pl.program_id(1)
    win = SUBQ + 2 * half

    kcat[0:half, :] = kl_ref[...]
    kcat[half:half + bq, :] = km_ref[...]
    kcat[half + bq:, :] = kr_ref[...]
    vcat[0:half, :] = vl_ref[...]
    vcat[half:half + bq, :] = vm_ref[...]
    vcat[half + bq:, :] = vr_ref[...]

    u0 = i * bq
    in_prompt = u0 < prompt_rows
    lo = jnp.where(in_prompt, (u0 // prompt_seq) * prompt_seq,
                   prompt_rows + ((u0 - prompt_rows) // sample_seq) * sample_seq)
    hi = lo + jnp.where(in_prompt, prompt_seq, sample_seq)

    qi = lax.broadcasted_iota(jnp.int32, (SUBQ, win), 0)
    kj = lax.broadcasted_iota(jnp.int32, (SUBQ, win), 1)
    rel = kj - half - qi
    absrel = jnp.abs(rel)
    band_bias = jnp.where(absrel <= half, -(absrel * step).astype(_F32), MASKED)
    lane = lax.broadcasted_iota(jnp.int32, (SUBQ, LANES), 1)

    heads = range(HEADS_PER_STEP)
    head_ids = [c * HEADS_PER_STEP + h if shared_kv else h for h in heads]
    scale2 = HEAD_DIM ** -0.5 * LOG2_E
    head_bias = [(slope_ref[hid] * LOG2_E) * band_bias for hid in head_ids]
    n_sub = bq // SUBQ
    for sb in range(n_sub):
        kpos = kj + (u0 + sb * SUBQ - half)
        in_seq = None
        if sb == 0:
            in_seq = kpos >= lo
        if sb == n_sub - 1:
            in_seq = kpos < hi if in_seq is None else in_seq & (kpos < hi)

        def scores2(s, h):
            s = s * scale2 + head_bias[h]
            return s if in_seq is None else jnp.where(in_seq, s, MASKED)

        rows = slice(sb * SUBQ, (sb + 1) * SUBQ)
        if shared_kv:
            k = kcat[sb * SUBQ:sb * SUBQ + win, :]
            v = vcat[sb * SUBQ:sb * SUBQ + win, :]
            q4 = jnp.concatenate([q_ref[rows, h * HEAD_DIM:(h + 1) * HEAD_DIM] for h in heads], axis=0)
            s4 = lax.dot_general(q4, k, (((1,), (1,)), ((), ())), preferred_element_type=_F32)
            ps, ms, ls = [], [], []
            for h in heads:
                s = scores2(s4[h * SUBQ:(h + 1) * SUBQ], h)
                m = jnp.max(s, axis=-1, keepdims=True)
                p = jnp.exp2(s - m)
                ms.append(m)
                ls.append(jnp.sum(p, axis=-1, keepdims=True))
                ps.append(p.astype(_BF))
            pv4 = jnp.dot(jnp.concatenate(ps, axis=0), v, preferred_element_type=_F32)
            for h in heads:
                l = ls[h] + jnp.exp2(sink_ref[head_ids[h]] * LOG2_E - ms[h])
                o_ref[rows, h * HEAD_DIM:(h + 1) * HEAD_DIM] = (
                    pv4[h * SUBQ:(h + 1) * SUBQ] / l).astype(o_ref.dtype)
            continue
        lse_tile = jnp.zeros((SUBQ, LANES), _F32)
        for h in heads:
            cols = slice(h * HEAD_DIM, (h + 1) * HEAD_DIM)
            k = kcat[sb * SUBQ:sb * SUBQ + win, cols]
            v = vcat[sb * SUBQ:sb * SUBQ + win, cols]
            s = lax.dot_general(q_ref[rows, cols], k, (((1,), (1,)), ((), ())), preferred_element_type=_F32)
            s = scores2(s, h)
            m = jnp.max(s, axis=-1, keepdims=True)
            p = jnp.exp2(s - m)
            l = jnp.sum(p, axis=-1, keepdims=True)
            pv = jnp.dot(p.astype(_BF), v, preferred_element_type=_F32)
            lse_tile = jnp.where(lane == h, m * LN_2 + jnp.log(l), lse_tile)
            o_ref[rows, cols] = (pv / l).astype(o_ref.dtype)
        lse_ref[rows, :] = lse_tile


def _band_attention(qkv, slopes, sinks, *, n_inner, bq, half, step, shared_kv, has_sink,
                    prompt_rows, prompt_seq, sample_seq, name):
    lead, rows, _ = qkv.shape
    assert prompt_seq % bq == 0 and sample_seq % bq == 0 and bq % SUBQ == 0 and bq % half == 0
    assert half <= SUBQ
    nq = rows // bq
    hb = bq // half
    last_halo = rows // half - 1
    width = HEADS_PER_STEP * HEAD_DIM
    if shared_kv:
        kv_w = HEAD_DIM
        k_base, v_base = width * n_inner // HEAD_DIM, width * n_inner // HEAD_DIM + n_inner
        lead_of = lambda c: 0
        q_col = lambda c: c
        kv_col = lambda base: (lambda c: base + c)
    else:
        kv_w = width
        k_base, v_base = 1, 2
        lead_of = lambda c: c
        q_col = lambda c: 0
        kv_col = lambda base: (lambda c: base)

    def main_map(col):
        return lambda i, c: (lead_of(c), i, col(c))

    def left_map(col):
        return lambda i, c: (lead_of(c), jnp.maximum(i * hb - 1, 0), col(c))

    def right_map(col):
        return lambda i, c: (lead_of(c), jnp.minimum((i + 1) * hb, last_halo), col(c))

    smem = pl.BlockSpec(memory_space=pltpu.SMEM)
    in_specs = [
        smem, smem,
        pl.BlockSpec((None, bq, width), main_map(q_col)),
        pl.BlockSpec((None, half, kv_w), left_map(kv_col(k_base))),
        pl.BlockSpec((None, bq, kv_w), main_map(kv_col(k_base))),
        pl.BlockSpec((None, half, kv_w), right_map(kv_col(k_base))),
        pl.BlockSpec((None, half, kv_w), left_map(kv_col(v_base))),
        pl.BlockSpec((None, bq, kv_w), main_map(kv_col(v_base))),
        pl.BlockSpec((None, half, kv_w), right_map(kv_col(v_base))),
    ]
    if has_sink:
        out_specs = pl.BlockSpec((bq, width), lambda i, c: (i, c))
        out_shape = jax.ShapeDtypeStruct((rows, n_inner * width), _BF)
    else:
        out_specs = (pl.BlockSpec((None, bq, width), lambda i, c: (c, i, 0)),
                     pl.BlockSpec((None, bq, LANES), lambda i, c: (c, i, 0)))
        out_shape = (jax.ShapeDtypeStruct((lead, rows, width), _BF),
                     jax.ShapeDtypeStruct((lead, rows, LANES), _F32))
    return pl.pallas_call(
        functools.partial(_band_attn_kernel, bq=bq, half=half, step=step, shared_kv=shared_kv,
                          has_sink=has_sink, prompt_rows=prompt_rows, prompt_seq=prompt_seq,
                          sample_seq=sample_seq),
        grid=(nq, n_inner),
        in_specs=in_specs,
        out_specs=out_specs,
        out_shape=out_shape,
        scratch_shapes=[pltpu.VMEM((bq + 2 * half, kv_w), _BF), pltpu.VMEM((bq + 2 * half, kv_w), _BF)],
        compiler_params=_params("parallel", "arbitrary"),
        name=name,
    )(slopes, sinks, qkv, qkv, qkv, qkv, qkv, qkv, qkv)


def _mix_kernel(xp_ref, xs_ref, oa_ref, ob0_ref, ob1_ref, ob2_ref, l0_ref, l1_ref, l2_ref, g_ref,
                wpa_ref, wpb_ref, wout_ref, n2_ref, wr_ref, br_ref, h_ref, hn_ref, ri_ref, rw_ref, cnt_ref,
                o_scr, l_scr, carry_ref, lg_scr, mg_scr, *, n_tiles, n_prompt_tiles, d, tm):
    i = pl.program_id(0)

    @pl.when(i == 0)
    def _():
        carry_ref[...] = jnp.zeros_like(carry_ref)
        lg_scr[...] = jnp.zeros_like(lg_scr)
        mg_scr[...] = jnp.zeros_like(mg_scr)

    _route_tile(lg_scr[...], jnp.where(i > 1, 1.0, 0.0), ri_ref, rw_ref, cnt_ref, carry_ref)

    tile2 = jnp.clip(i - 1, 0, n_tiles - 1)
    x = jnp.where(tile2 < n_prompt_tiles, xp_ref[...], xs_ref[...])
    h_new = x + jnp.dot(mg_scr[(i + 1) % 2], wout_ref[...], preferred_element_type=_F32)
    h_ref[...] = h_new
    hn = h_new * lax.rsqrt(jnp.mean(h_new * h_new, axis=-1, keepdims=True) + EPS) * n2_ref[...]
    _store_row_tiles(hn_ref, _pack_halves(hn))
    hn_hi = hn.astype(_BF)
    hn_lo = (hn - hn_hi.astype(_F32)).astype(_BF)
    r = (jnp.dot(hn_hi, wr_ref[...], preferred_element_type=_F32)
         + jnp.dot(hn_lo, wr_ref[...], preferred_element_type=_F32))
    lg_scr[...] = r[:, :LANES] + r[:, LANES:] + br_ref[...]


    def token_order(o_ref, l_ref, r, slot):
        if r == 1:
            return [o_ref[0, :, h * HEAD_DIM:(h + 1) * HEAD_DIM].astype(_F32) for h in range(B_HPG)], l_ref[0]
        for p in range(r):
            for h in range(B_HPG):
                o_scr[slot, h, pl.ds(p, tm // r, stride=r), :] = (
                    o_ref[p, :, h * HEAD_DIM:(h + 1) * HEAD_DIM].astype(_F32))
            l_scr[slot, pl.ds(p, tm // r, stride=r), :] = l_ref[p]
        return [o_scr[slot, h] for h in range(B_HPG)], l_scr[slot]

    o0, l0 = token_order(ob0_ref, l0_ref, DILATED_GROUPS[0][1], 0)
    o1, l1 = token_order(ob1_ref, l1_ref, DILATED_GROUPS[1][1], 0)
    o2, l2 = token_order(ob2_ref, l2_ref, DILATED_GROUPS[2][1], 1)
    mx = jnp.maximum(jnp.maximum(l0, l1), l2)
    e0, e1, e2 = jnp.exp(l0 - mx), jnp.exp(l1 - mx), jnp.exp(l2 - mx)
    den = e0 + e1 + e2
    a0, a1, a2 = e0 / den, e1 / den, e2 / den
    parts = []
    for h in range(B_HPG):
        parts.append(a0[:, h:h + 1] * o0[h] + a1[:, h:h + 1] * o1[h] + a2[:, h:h + 1] * o2[h])
    ob = jnp.concatenate(parts, axis=1).astype(_BF)

    ta = jnp.dot(oa_ref[...], wpa_ref[...], preferred_element_type=_F32)
    tb = jnp.dot(ob, wpb_ref[...], preferred_element_type=_F32)
    merged = g_ref[:, :d].astype(_F32) * ta + g_ref[:, d:].astype(_F32) * tb
    mg_scr[i % 2] = merged.astype(_BF)


def _mix(xp, xs, oa, obs, lses, gates, wpa, wpb, wout, n2, wr, br):
    n_p, d = xp.shape
    n_s = xs.shape[0]
    t = n_p + n_s
    tm = ROW_TILE
    npt = n_p // tm
    n_tiles = t // tm
    def stage_tile(lag):
        return lambda i: jnp.clip(i - lag, 0, n_tiles - 1)

    row = lambda i: (stage_tile(0)(i), 0)
    row2 = lambda i: (stage_tile(1)(i), 0)
    routed = lambda i: (stage_tile(2)(i), 0)
    const = lambda i: (0, 0)
    width = B_HPG * HEAD_DIM

    def resident(shape):
        return pl.BlockSpec(shape, const, pipeline_mode=pl.Buffered(1))

    def phase_blocks(cols):
        return [pl.BlockSpec((r, tm // r, cols), lambda i: (0, stage_tile(0)(i), 0)) for _, r in DILATED_GROUPS]

    in_specs = [
        pl.BlockSpec((tm, d), lambda i: (jnp.minimum(stage_tile(1)(i), npt - 1), 0)),
        pl.BlockSpec((tm, d), lambda i: (jnp.maximum(stage_tile(1)(i) - npt, 0), 0)),
        pl.BlockSpec((tm, oa.shape[1]), row),
        *phase_blocks(width), *phase_blocks(LANES),
        pl.BlockSpec((tm, 2 * d), row),
        resident(wpa.shape), resident(wpb.shape), resident(wout.shape),
        resident((1, d)), resident(wr.shape), resident((1, LANES)),
    ]
    return pl.pallas_call(
        functools.partial(_mix_kernel, n_tiles=n_tiles, n_prompt_tiles=npt, d=d, tm=tm),
        grid=(n_tiles + 2,),
        in_specs=in_specs,
        out_specs=(pl.BlockSpec((tm, d), row2), pl.BlockSpec((tm * (d // 2 // LANES), LANES), row2),
                   pl.BlockSpec((tm, LANES), routed), pl.BlockSpec((tm, LANES), routed),
                   pl.BlockSpec((SUBLANES, LANES), const)),
        out_shape=(jax.ShapeDtypeStruct((t, d), _F32),
                   jax.ShapeDtypeStruct((t * (d // 2 // LANES), LANES), jnp.uint32),
                   jax.ShapeDtypeStruct((t, LANES), jnp.int32),
                   jax.ShapeDtypeStruct((t, LANES), _F32),
                   jax.ShapeDtypeStruct((SUBLANES, LANES), jnp.int32)),
        scratch_shapes=[pltpu.VMEM((2, B_HPG, tm, HEAD_DIM), _F32), pltpu.VMEM((2, tm, LANES), _F32),
                        pltpu.VMEM((SUBLANES, LANES), _F32), pltpu.VMEM((tm, LANES), _F32),
                        pltpu.VMEM((2, tm, d), _BF)],
        compiler_params=_params("arbitrary"),
        name="mix_out_router",
    )(xp, xs, oa, *obs, *lses, gates, wpa, wpb, wout, n2, wr, br)


def _route_tile(lg, live, ri_ref, rw_ref, cnt_ref, carry_ref):
    tr = lg.shape[0]
    lane_i = lax.broadcasted_iota(jnp.int32, (tr, LANES), 1)
    lane = lane_i.astype(_F32)
    no_lane = float(LANES)
    is_grp = lane_i < N_GROUPS
    glog = jnp.where(is_grp, lg, MASKED)
    gmax = jnp.max(glog, axis=-1, keepdims=True)
    grp = jnp.min(jnp.where(glog == gmax, lane, no_lane), axis=-1, keepdims=True)
    gsum = jnp.sum(jnp.where(is_grp, jnp.exp(glog - gmax), 0.0), axis=-1, keepdims=True)
    pgrp = 1.0 / gsum
    lane_grp = ((lane_i - N_GROUPS) // EXPERTS_PER_GROUP).astype(_F32)
    in_grp = (lane_i >= N_GROUPS) & (lane_i < N_GROUPS + N_EXPERTS) & (lane_grp == grp)
    elog = jnp.where(in_grp, lg, MASKED)
    t1 = jnp.max(elog, axis=-1, keepdims=True)
    i1 = jnp.min(jnp.where(elog == t1, lane, no_lane), axis=-1, keepdims=True)
    elog2 = jnp.where(lane == i1, MASKED, elog)
    t2 = jnp.max(elog2, axis=-1, keepdims=True)
    i2 = jnp.min(jnp.where(elog2 == t2, lane, no_lane), axis=-1, keepdims=True)
    e21 = jnp.exp(t2 - t1)
    w1 = pgrp / (1.0 + e21)
    w2 = pgrp * e21 / (1.0 + e21)
    eid1 = i1 - N_GROUPS
    eid2 = i2 - N_GROUPS
    hot1 = lane == eid1
    hot2 = lane == eid2
    onehot = (jnp.where(hot1, 1.0, 0.0) + jnp.where(hot2, 1.0, 0.0)) * live
    r_i = lax.broadcasted_iota(jnp.int32, (tr, tr), 0)
    c_i = lax.broadcasted_iota(jnp.int32, (tr, tr), 1)
    lower = jnp.where(c_i < r_i, 1.0, 0.0).astype(_BF)
    before = jnp.dot(lower, onehot.astype(_BF), preferred_element_type=_F32) + carry_ref[0:1, :]
    rank1 = jnp.sum(jnp.where(hot1, before, 0.0), axis=-1, keepdims=True)
    rank2 = jnp.sum(jnp.where(hot2, before, 0.0), axis=-1, keepdims=True)
    total = carry_ref[0:1, :] + jnp.sum(onehot, axis=0, keepdims=True)
    carry_ref[...] = jnp.broadcast_to(total, carry_ref.shape)
    cnt_ref[...] = jnp.broadcast_to(total, cnt_ref.shape).astype(jnp.int32)
    ri = jnp.where(lane_i == 0, eid1, jnp.where(lane_i == 1, eid2, jnp.where(lane_i == 2, rank1,
                   jnp.where(lane_i == 3, rank2, 0.0))))
    ri_ref[...] = ri.astype(jnp.int32)
    rw_ref[...] = jnp.where(lane_i == 0, w1, jnp.where(lane_i == 1, w2, 0.0))


def _dispatch_kernel(dest_ref, pad_start_ref, pad_n_ref, nvalid_ref, hn_ref, xr_ref, zeros, sem, pad_sem,
                     *, td, sc):
    i = pl.program_id(0)

    def rows_of(ref, row0, n):
        start = row0 * sc if isinstance(row0, int) else pl.multiple_of(row0 * sc, n * sc)
        return ref.at[pl.ds(start, n * sc), :]

    @pl.when(i == 0)
    def _():
        zeros[...] = jnp.zeros_like(zeros)

        def pad_chunk(e, r):
            row0 = pl.multiple_of(pad_start_ref[e] + r * SUBLANES, SUBLANES)
            return pltpu.make_async_copy(rows_of(zeros, 0, SUBLANES), rows_of(xr_ref, row0, SUBLANES), pad_sem)

        def per_expert(act):
            def body(e, carry):
                lax.fori_loop(0, pad_n_ref[e], lambda r, c: (act(pad_chunk(e, r)), c)[1], 0)
                return carry
            return body

        def tail_block(b):
            return pltpu.make_async_copy(zeros, rows_of(xr_ref, b * EXPERT_ROWS, EXPERT_ROWS), pad_sem)

        n_blocks = xr_ref.shape[0] // (EXPERT_ROWS * sc)
        lax.fori_loop(0, N_EXPERTS, per_expert(lambda cp: cp.start()), 0)
        lax.fori_loop(nvalid_ref[0], n_blocks, lambda b, c: (tail_block(b).start(), c)[1], 0)
        lax.fori_loop(0, N_EXPERTS, per_expert(lambda cp: cp.wait()), 0)
        lax.fori_loop(nvalid_ref[0], n_blocks, lambda b, c: (tail_block(b).wait(), c)[1], 0)

    def start(j, carry):
        for k in range(2):
            dst = dest_ref[2 * (i * td + j) + k]
            pltpu.make_async_copy(rows_of(hn_ref, j, 1), rows_of(xr_ref, dst, 1), sem).start()
        return carry

    lax.fori_loop(0, td, start, 0, unroll=DMA_UNROLL)
    for _ in range(2):
        pltpu.make_async_copy(hn_ref, rows_of(xr_ref, 0, td), sem).wait()


def _dispatch(dest, pad_start, pad_n, nvalid, hn, t, rows):
    sc = hn.shape[0] // t
    td = ROW_TILE
    return pl.pallas_call(
        functools.partial(_dispatch_kernel, td=td, sc=sc),
        grid_spec=pltpu.PrefetchScalarGridSpec(
            num_scalar_prefetch=4,
            grid=(t // td,),
            in_specs=[pl.BlockSpec((td * sc, LANES), lambda i, *_: (i, 0))],
            out_specs=pl.BlockSpec(memory_space=pl.ANY),
            scratch_shapes=[pltpu.VMEM((EXPERT_ROWS * sc, LANES), hn.dtype), pltpu.SemaphoreType.DMA(()),
                            pltpu.SemaphoreType.DMA(())],
        ),
        out_shape=jax.ShapeDtypeStruct((rows * sc, LANES), hn.dtype),
        compiler_params=_params("arbitrary"),
        name="dispatch",
    )(dest, pad_start, pad_n, nvalid, hn)


def _expert_kernel(blk_e_ref, nvalid_ref, first_ref, slot_ref, next_e_ref, x_ref, w1_hbm, w3_hbm, w2_hbm, y_ref,
                   w1_buf, w3_buf, w2_buf, sems):
    b = pl.program_id(0)
    valid = b < nvalid_ref[0]

    def fetch(e, slot):
        return (pltpu.make_async_copy(w1_hbm.at[e], w1_buf.at[slot], sems.at[slot, 0]),
                pltpu.make_async_copy(w3_hbm.at[e], w3_buf.at[slot], sems.at[slot, 1]),
                pltpu.make_async_copy(w2_hbm.at[e], w2_buf.at[slot], sems.at[slot, 2]))

    @pl.when(b == 0)
    def _():
        for cp in fetch(blk_e_ref[0], 0):
            cp.start()

    @pl.when(valid & (first_ref[b] == 1))
    def _():
        for cp in fetch(blk_e_ref[b], slot_ref[b]):
            cp.wait()

        @pl.when(next_e_ref[b] >= 0)
        def _():
            for cp in fetch(next_e_ref[b], 1 - slot_ref[b]):
                cp.start()

    @pl.when(valid)
    def _():
        slot = slot_ref[b]
        packed_x = _load_row_tiles(x_ref, EXPERT_ROWS)
        xb = jnp.concatenate([half.astype(_BF) for half in _unpack_halves(packed_x)], axis=1)
        a = jnp.dot(xb, w1_buf[slot].astype(_BF), preferred_element_type=_F32)
        u = jnp.dot(xb, w3_buf[slot].astype(_BF), preferred_element_type=_F32)
        hmid = (a / (1.0 + jnp.exp(-a))) * u
        y = jnp.dot(hmid.astype(_BF), w2_buf[slot].astype(_BF), preferred_element_type=_F32)
        _store_row_tiles(y_ref, _pack_halves(y))

    @pl.when(jnp.logical_not(valid))
    def _():
        y_ref[...] = jnp.zeros_like(y_ref)


def _experts(blk_e, nvalid, first, slot, next_e, xr, w1, w3, w2):
    _, d, f = w1.shape
    sc = d // 2 // LANES
    blk = EXPERT_ROWS * sc
    nblk = xr.shape[0] // blk
    any_space = pl.BlockSpec(memory_space=pl.ANY)
    return pl.pallas_call(
        _expert_kernel,
        grid_spec=pltpu.PrefetchScalarGridSpec(
            num_scalar_prefetch=5,
            grid=(nblk,),
            in_specs=[pl.BlockSpec((blk, LANES), lambda b, be, nv, *_: (jnp.minimum(b, nv[0] - 1), 0)),
                      any_space, any_space, any_space],
            out_specs=pl.BlockSpec((blk, LANES), lambda b, *_: (b, 0)),
            scratch_shapes=[pltpu.VMEM((2, d, f), w1.dtype), pltpu.VMEM((2, d, f), w3.dtype),
                            pltpu.VMEM((2, f, d), w2.dtype), pltpu.SemaphoreType.DMA((2, 3))],
        ),
        out_shape=jax.ShapeDtypeStruct(xr.shape, jnp.uint32),
        compiler_params=_params("arbitrary"),
        name="experts",
    )(blk_e, nvalid, first, slot, next_e, xr, w1, w3, w2)


def _combine_kernel(dest_ref, h_ref, rw_ref, g_ref, yr_ref, op_ref, os_ref, ybuf, sems, *, tc, sc, n_prompt_tiles):
    i = pl.program_id(0)
    slot = i % 2

    def gather(tile, s):
        def start(j, carry):
            for k in range(2):
                src = dest_ref[2 * (tile * tc + j) + k]
                pltpu.make_async_copy(_row_tile(yr_ref, src, sc), _row_tile(ybuf.at[s, k], j, sc),
                                      sems.at[s]).start()
            return carry

        lax.fori_loop(0, tc, start, 0, unroll=DMA_UNROLL)

    @pl.when(i == 0)
    def _():
        gather(0, 0)

    @pl.when(i + 1 < pl.num_programs(0))
    def _():
        gather(i + 1, 1 - slot)

    for k in range(2):
        pltpu.make_async_copy(yr_ref.at[pl.ds(0, tc * sc), :], ybuf.at[slot, k], sems.at[slot]).wait()

    rw = rw_ref[...]
    y0 = _unpack_halves(_load_row_tiles(ybuf.at[slot, 0], tc))
    y1 = _unpack_halves(_load_row_tiles(ybuf.at[slot, 1], tc))
    y = jnp.concatenate([rw[:, 0:1] * a + rw[:, 1:2] * b for a, b in zip(y0, y1)], axis=1)
    z = h_ref[...] + y
    out = z * lax.rsqrt(jnp.mean(z * z, axis=-1, keepdims=True) + EPS) * g_ref[...]

    @pl.when(i < n_prompt_tiles)
    def _():
        op_ref[...] = out

    @pl.when(i >= n_prompt_tiles)
    def _():
        os_ref[...] = out


def _combine(dest, h, rw, g, yr, *, n_p):
    t, d = h.shape
    tc = ROW_TILE
    npt = n_p // tc
    sc = d // 2 // LANES
    return pl.pallas_call(
        functools.partial(_combine_kernel, tc=tc, sc=sc, n_prompt_tiles=npt),
        grid_spec=pltpu.PrefetchScalarGridSpec(
            num_scalar_prefetch=1,
            grid=(t // tc,),
            in_specs=[
                pl.BlockSpec((tc, d), lambda i, dest: (i, 0)),
                pl.BlockSpec((tc, LANES), lambda i, dest: (i, 0)),
                pl.BlockSpec((1, d), lambda i, dest: (0, 0)),
                pl.BlockSpec(memory_space=pl.ANY),
            ],
            out_specs=(pl.BlockSpec((tc, d), lambda i, dest: (jnp.minimum(i, npt - 1), 0)),
                       pl.BlockSpec((tc, d), lambda i, dest: (jnp.maximum(i - npt, 0), 0))),
            scratch_shapes=[pltpu.VMEM((2, 2, tc * sc, LANES), yr.dtype), pltpu.SemaphoreType.DMA((2,))],
        ),
        out_shape=(jax.ShapeDtypeStruct((n_p, d), _F32), jax.ShapeDtypeStruct((t - n_p, d), _F32)),
        compiler_params=_params("arbitrary"),
        name="combine_norm",
    )(dest, h, rw, g, yr)


def _layer(xp, xs, prompt_seq, sample_seq, norm1, w_in, sink, w_proj_a, w_proj_b, w_gate, b_gate, w_out,
           norm2, w_rg, b_rg, w_re, b_re, w1, w3, w2, norm_final):
    n_p, d = xp.shape
    n_s = xs.shape[0]
    t = n_p + n_s
    a_q = A_HEADS * HEAD_DIM
    a_cols = a_q + 2 * A_KV * HEAD_DIM

    proj_a, q0, q1, q2, gates = _project(xp, xs, norm1.reshape(1, d), w_in, w_gate, b_gate.reshape(1, 2 * d),
                                         a_cols=a_cols)

    oa = _band_attention(
        proj_a.reshape(1, t, a_cols), jnp.asarray(_alibi(A_HEADS)), sink.astype(_F32), n_inner=A_KV,
        bq=min(1024, prompt_seq, sample_seq), half=A_HALF, step=1, shared_kv=True, has_sink=True,
        prompt_rows=n_p, prompt_seq=prompt_seq, sample_seq=sample_seq, name="attn_window")

    slopes_b = _alibi(B_HEADS)
    obs, lses = [], []
    for gi, ((w, r), qkv) in enumerate(zip(DILATED_GROUPS, (q0, q1, q2))):
        o_g, lse_g = _band_attention(
            qkv, jnp.asarray(slopes_b[gi * B_HPG:(gi + 1) * B_HPG]), jnp.zeros((B_HPG,), _F32), n_inner=r,
            bq=min(512, prompt_seq // r, sample_seq // r), half=w // (2 * r), step=r, shared_kv=False,
            has_sink=False, prompt_rows=n_p // r, prompt_seq=prompt_seq // r, sample_seq=sample_seq // r,
            name=f"attn_dilated_{r}")
        obs.append(o_g)
        lses.append(lse_g)

    n_r = N_GROUPS + N_EXPERTS
    wr = jnp.concatenate([w_rg, jnp.transpose(w_re, (1, 0, 2)).reshape(d, N_EXPERTS)], axis=1)
    wr = jnp.pad(wr, ((0, 0), (0, LANES - n_r)))
    wr_hi = wr.astype(_BF)
    wr_lo = (wr - wr_hi.astype(_F32)).astype(_BF)
    wr2 = jnp.concatenate([wr_hi, wr_lo], axis=1)
    br = jnp.pad(jnp.concatenate([b_rg, b_re.reshape(-1)]), (0, LANES - n_r)).reshape(1, LANES).astype(_F32)

    h, hn, ri, rw, cnt = _mix(xp, xs, oa, obs, lses, gates, w_proj_a.astype(_BF), w_proj_b.astype(_BF),
                              w_out.astype(_BF), norm2.reshape(1, d), wr2, br)
    counts = cnt[0, :N_EXPERTS]
    eid = ri[:, 0:2]
    rank = ri[:, 2:4]
    pcounts = (counts + EXPERT_ROWS - 1) // EXPERT_ROWS * EXPERT_ROWS
    pends = jnp.cumsum(pcounts)
    pstarts = pends - pcounts
    expert_ids = jnp.arange(N_EXPERTS, dtype=jnp.int32)
    start_of = jnp.sum(jnp.where(eid[:, :, None] == expert_ids, pstarts.astype(jnp.int32), 0), axis=-1)
    dest = (start_of + rank).reshape(-1).astype(jnp.int32)
    nblk = (2 * t + N_EXPERTS * (EXPERT_ROWS - 1) + EXPERT_ROWS - 1) // EXPERT_ROWS
    nvalid = (pends[-1] // EXPERT_ROWS).astype(jnp.int32)
    blk_start = jnp.minimum(jnp.arange(nblk, dtype=jnp.int32), nvalid - 1) * EXPERT_ROWS
    blk_e = jnp.sum(pends[None, :] <= blk_start[:, None], axis=1).astype(jnp.int32)

    pad_start = (pstarts + counts) // SUBLANES * SUBLANES
    xr = _dispatch(dest, pad_start.astype(jnp.int32), ((pends - pad_start) // SUBLANES).astype(jnp.int32),
                   nvalid.reshape(1), hn, t, nblk * EXPERT_ROWS)
    blk_ids = jnp.arange(nblk, dtype=jnp.int32)
    first = ((blk_ids == 0) | (blk_e != jnp.roll(blk_e, 1))).astype(jnp.int32)
    slot = ((jnp.cumsum(first) - 1) % 2).astype(jnp.int32)
    later = (expert_ids[None, :] > expert_ids[:, None]) & (counts[None, :] > 0)
    next_of = jnp.min(jnp.where(later, expert_ids[None, :], N_EXPERTS), axis=1)
    next_e = jnp.sum(jnp.where(blk_e[:, None] == expert_ids, next_of, 0), axis=1)
    next_e = jnp.where(next_e < N_EXPERTS, next_e, -1).astype(jnp.int32)
    yr = _experts(blk_e, nvalid.reshape(1), first, slot, next_e, xr, w1, w3, w2)
    return _combine(dest, h, rw, norm_final.reshape(1, d), yr, n_p=n_p)


def kernel(x_prompt, x_sample, norm1, w_in, attn_sink, w_proj_a, w_proj_b, w_gate, b_gate, w_out, norm2,
           w_router_group, b_router_group, w_router_expert, b_router_expert, w_expert_gate, w_expert_up,
           w_expert_down, norm_final):
    assert norm1.shape[0] == 1, "one layer"
    d = x_prompt.shape[-1]
    xp = x_prompt.reshape(-1, d)
    xs = x_sample.reshape(-1, d)
    yp, ys = _layer(xp, xs, x_prompt.shape[1], x_sample.shape[1], norm1[0], w_in[0], attn_sink[0], w_proj_a[0],
                    w_proj_b[0], w_gate[0], b_gate[0], w_out[0], norm2[0], w_router_group[0],
                    b_router_group[0], w_router_expert[0], b_router_expert[0], w_expert_gate[0],
                    w_expert_up[0], w_expert_down[0], norm_final)
    return yp.reshape(x_prompt.shape), ys.reshape(x_sample.shape)
```

```python
import functools
import math

import jax
import jax.numpy as jnp
import numpy as np
from jax import lax
from jax.experimental import pallas as pl
from jax.experimental.pallas import tpu as pltpu

HEAD_DIM = 128
A_HEADS = 16
A_KV = 4
A_HALF = 128
DILATED_GROUPS = ((128, 1), (512, 4), (2048, 16))
N_DIL = len(DILATED_GROUPS)
B_HPG = 4
B_HEADS = N_DIL * B_HPG
N_GROUPS = 8
EXPERTS_PER_GROUP = 8
N_EXPERTS = N_GROUPS * EXPERTS_PER_GROUP
EPS = 1e-6

LANES = 128
SUBLANES = 8
MASKED = -1e30
LOG2_E = math.log2(math.e)
LN_2 = math.log(2.0)
VMEM_LIMIT = 56 * 1024 * 1024
SUBQ = 128
HEADS_PER_STEP = 4
EXPERT_ROWS = 256
DMA_UNROLL = 8
PROJ_TM, PROJ_TN = 2048, 512
MAX_ROW_STRIDE = 4
ROW_TILE = 256
MOVE_TILE = 512

_BF = jnp.bfloat16
_F32 = jnp.float32


def _alibi(n):
    return np.power(2.0, -8.0 * (np.arange(n) + 1) / n).astype(np.float32)


def _params(*sem):
    return pltpu.CompilerParams(dimension_semantics=sem, vmem_limit_bytes=VMEM_LIMIT)


def _pack_halves(x):
    n = x.shape[1] // 2

    def bf16_bits(v):
        return lax.bitcast_convert_type(v.astype(_BF).astype(_F32), jnp.uint32)

    return bf16_bits(x[:, n:]) | (bf16_bits(x[:, :n]) >> 16)


def _unpack_halves(p):
    lo = lax.bitcast_convert_type(p << 16, _F32)
    hi = lax.bitcast_convert_type(p & jnp.uint32(0xFFFF0000), _F32)
    return lo, hi


def _store_row_tiles(ref, packed):
    rows, words = packed.shape
    s_count = words // LANES
    for s in range(s_count):
        ref[pl.ds(s, rows, stride=s_count), :] = packed[:, s * LANES:(s + 1) * LANES]


def _load_row_tiles(ref, rows):
    s_count = ref.shape[0] // rows
    return jnp.concatenate([ref[pl.ds(s, rows, stride=s_count), :] for s in range(s_count)], axis=1)


def _row_tile(ref, row, s_count):
    return ref.at[pl.ds(pl.multiple_of(row * s_count, s_count), s_count), :]


def _proj_kernel(g_ref, b_ref, xp_hbm, xs_hbm, win_hbm, wg_hbm, oa_hbm, q0_hbm, q1_hbm, q2_hbm, gt_hbm,
                 xbuf, xn_ref, wbuf, acc_ref, obuf, x_sem, w_sems, o_sems,
                 *, n_prompt_tiles, n_row_tiles, tm, tn, na, ng):
    i = pl.program_id(0)
    nb = 3
    tiles_per_row = na + N_DIL * nb + ng
    rows = pl.ds(pl.multiple_of(i * tm, tm), tm)

    def cols(jj):
        return pl.ds(pl.multiple_of(jj * tn, tn), tn)

    def x_copy(x_hbm, row_tile):
        return pltpu.make_async_copy(x_hbm.at[pl.ds(pl.multiple_of(row_tile * tm, tm), tm), :], xbuf, x_sem)

    def start_x(row_tile):
        @pl.when(row_tile < n_prompt_tiles)
        def _():
            x_copy(xp_hbm, row_tile).start()

        @pl.when(row_tile >= n_prompt_tiles)
        def _():
            x_copy(xs_hbm, row_tile - n_prompt_tiles).start()

    def w_copy(w_hbm, col, slot):
        return pltpu.make_async_copy(w_hbm.at[:, cols(col)], wbuf.at[slot], w_sems.at[slot])

    def wait_staging(slot):
        pltpu.make_async_copy(obuf.at[slot], oa_hbm.at[pl.ds(0, tm), pl.ds(0, tn)], o_sems.at[slot]).wait()

    def store_plain(out_hbm):
        def store(acc, jj, slot):
            obuf[slot] = acc.astype(_BF)
            pltpu.make_async_copy(obuf.at[slot], out_hbm.at[rows, cols(jj)], o_sems.at[slot]).start()
        return store

    def store_gate(acc, jj, slot):
        z = acc + b_ref[jj]
        obuf[slot] = (0.5 * jnp.tanh(0.5 * z) + 0.5).astype(_BF)
        pltpu.make_async_copy(obuf.at[slot], gt_hbm.at[rows, cols(jj)], o_sems.at[slot]).start()

    def store_phases(q_hbm, r):
        n = tm // r

        def store(acc, jj, slot):
            if r == 1:
                obuf[slot] = acc.astype(_BF)
            else:
                for cb in range(tn // LANES):
                    acc_ref[0, cb] = acc[:, cb * LANES:(cb + 1) * LANES]
                src, stride = 0, r
                if r > MAX_ROW_STRIDE:
                    stride = r // MAX_ROW_STRIDE
                    m = tm // MAX_ROW_STRIDE
                    for c in range(MAX_ROW_STRIDE):
                        for cb in range(tn // LANES):
                            acc_ref[1, cb, c * m:(c + 1) * m, :] = acc_ref[0, cb, pl.ds(c, m, stride=MAX_ROW_STRIDE), :]
                    src = 1
                for p in range(r):
                    start = p if src == 0 else (p % MAX_ROW_STRIDE) * (tm // MAX_ROW_STRIDE) + p // MAX_ROW_STRIDE
                    for cb in range(tn // LANES):
                        obuf[slot, p * n:(p + 1) * n, cb * LANES:(cb + 1) * LANES] = (
                            acc_ref[src, cb, pl.ds(start, n, stride=stride), :].astype(_BF))
            for p in range(r):
                pltpu.make_async_copy(obuf.at[slot, pl.ds(p * n, n), :],
                                      q_hbm.at[p, pl.ds(pl.multiple_of(i * n, n), n), cols(jj)],
                                      o_sems.at[slot]).start()
        return store

    parts = [(na, win_hbm, lambda jj: jj, store_plain(oa_hbm))]
    for gi, (q_hbm, (_, r)) in enumerate(zip((q0_hbm, q1_hbm, q2_hbm), DILATED_GROUPS)):
        parts.append((nb, win_hbm, lambda jj, gi=gi: na + jj * N_DIL + gi, store_phases(q_hbm, r)))
    parts.append((ng, wg_hbm, lambda jj: jj, store_gate))

    @pl.when(i == 0)
    def _():
        start_x(0)
        w_copy(win_hbm, 0, 0).start()

    x_copy(xp_hbm, 0).wait()
    x = xbuf[...]
    y = x * lax.rsqrt(jnp.mean(x * x, axis=-1, keepdims=True) + EPS)
    xn_ref[...] = (y * g_ref[...]).astype(_BF)

    @pl.when(i + 1 < n_row_tiles)
    def _():
        start_x(i + 1)

    base = 0
    for k, (length, w_hbm, w_col, store) in enumerate(parts):
        def tile(jj, carry, base=base, length=length, w_hbm=w_hbm, w_col=w_col, store=store, k=k):
            n = i * tiles_per_row + base + jj
            slot = n % 2

            @pl.when(jj + 1 < length)
            def _():
                w_copy(w_hbm, w_col(jj + 1), 1 - slot).start()

            @pl.when(jj + 1 == length)
            def _():
                if k + 1 < len(parts):
                    w_copy(parts[k + 1][1], parts[k + 1][2](0), 1 - slot).start()
                else:
                    @pl.when(i + 1 < n_row_tiles)
                    def _():
                        w_copy(win_hbm, 0, 1 - slot).start()

            @pl.when(n >= 2)
            def _():
                wait_staging(slot)

            w_copy(w_hbm, w_col(jj), slot).wait()
            acc = jnp.dot(xn_ref[...], wbuf[slot].astype(_BF), preferred_element_type=_F32)
            store(acc, jj, slot)
            return carry

        lax.fori_loop(0, length, tile, 0)
        base += length

    @pl.when(i == n_row_tiles - 1)
    def _():
        wait_staging(0)
        wait_staging(1)


def _project(xp, xs, g, w_in, w_gate, b_gate, *, a_cols):
    n_p, d = xp.shape
    n_s = xs.shape[0]
    t = n_p + n_s
    tm, tn = PROJ_TM, PROJ_TN
    assert tn == B_HPG * HEAD_DIM and w_in.shape[1] == a_cols + 3 * N_DIL * tn
    assert n_p % tm == 0 and n_s % tm == 0
    gate_cols = w_gate.shape[1]
    na, ng = a_cols // tn, gate_cols // tn
    any_space = pl.BlockSpec(memory_space=pl.ANY)
    return pl.pallas_call(
        functools.partial(_proj_kernel, n_prompt_tiles=n_p // tm, n_row_tiles=t // tm, tm=tm, tn=tn, na=na, ng=ng),
        grid=(t // tm,),
        in_specs=[pl.BlockSpec((1, d), lambda i: (0, 0)), pl.BlockSpec((ng, 1, tn), lambda i: (0, 0, 0)),
                  any_space, any_space, any_space, any_space],
        out_specs=[any_space] * 5,
        out_shape=[jax.ShapeDtypeStruct((t, a_cols), _BF)]
                  + [jax.ShapeDtypeStruct((r, t // r, 3 * tn), _BF) for _, r in DILATED_GROUPS]
                  + [jax.ShapeDtypeStruct((t, gate_cols), _BF)],
        scratch_shapes=[pltpu.VMEM((tm, d), xp.dtype), pltpu.VMEM((tm, d), _BF), pltpu.VMEM((2, d, tn), w_in.dtype),
                        pltpu.VMEM((2, tn // LANES, tm, LANES), _F32), pltpu.VMEM((2, tm, tn), _BF),
                        pltpu.SemaphoreType.DMA(()), pltpu.SemaphoreType.DMA((2,)), pltpu.SemaphoreType.DMA((2,))],
        compiler_params=_params("arbitrary"),
        name="norm_proj",
    )(g, b_gate.reshape(ng, 1, tn), xp, xs, w_in, w_gate)


def _band_attn_kernel(slope_ref, sink_ref, q_ref, kl_ref, km_ref, kr_ref, vl_ref, vm_ref, vr_ref,
                      *rest, bq, half, step, shared_kv, has_sink, prompt_rows, prompt_seq, sample_seq):
    if has_sink:
        o_ref, kcat, vcat = rest
        lse_ref = None
    else:
        o_ref, lse_ref, kcat, vcat = rest
    i = pl.program_id(0)
    c = pl.program_id(1)
    win = SUBQ + 2 * half

    kcat[0:half, :] = kl_ref[...]
    kcat[half:half + bq, :] = km_ref[...]
    kcat[half + bq:, :] = kr_ref[...]
    vcat[0:half, :] = vl_ref[...]
    vcat[half:half + bq, :] = vm_ref[...]
    vcat[half + bq:, :] = vr_ref[...]

    u0 = i * bq
    in_prompt = u0 < prompt_rows
    lo = jnp.where(in_prompt, (u0 // prompt_seq) * prompt_seq,
                   prompt_rows + ((u0 - prompt_rows) // sample_seq) * sample_seq)
    hi = lo + jnp.where(in_prompt, prompt_seq, sample_seq)

    qi = lax.broadcasted_iota(jnp.int32, (SUBQ, win), 0)
    kj = lax.broadcasted_iota(jnp.int32, (SUBQ, win), 1)
    rel = kj - half - qi
    absrel = jnp.abs(rel)
    band_bias = jnp.where(absrel <= half, -(absrel * step).astype(_F32), MASKED)
    lane = lax.broadcasted_iota(jnp.int32, (SUBQ, LANES), 1)

    heads = range(HEADS_PER_STEP)
    head_ids = [c * HEADS_PER_STEP + h if shared_kv else h for h in heads]
    scale2 = HEAD_DIM ** -0.5 * LOG2_E
    head_bias = [(slope_ref[hid] * LOG2_E) * band_bias for hid in head_ids]
    n_sub = bq // SUBQ
    for sb in range(n_sub):
        kpos = kj + (u0 + sb * SUBQ - half)
        in_seq = None
        if sb == 0:
            in_seq = kpos >= lo
        if sb == n_sub - 1:
            in_seq = kpos < hi if in_seq is None else in_seq & (kpos < hi)

        def scores2(s, h):
            s = s * scale2 + head_bias[h]
            return s if in_seq is None else jnp.where(in_seq, s, MASKED)

        rows = slice(sb * SUBQ, (sb + 1) * SUBQ)
        if shared_kv:
            k = kcat[sb * SUBQ:sb * SUBQ + win, :]
            v = vcat[sb * SUBQ:sb * SUBQ + win, :]
            q4 = jnp.concatenate([q_ref[rows, h * HEAD_DIM:(h + 1) * HEAD_DIM] for h in heads], axis=0)
            s4 = lax.dot_general(q4, k, (((1,), (1,)), ((), ())), preferred_element_type=_F32)
            ps, ms, ls = [], [], []
            for h in heads:
                s = scores2(s4[h * SUBQ:(h + 1) * SUBQ], h)
                m = jnp.max(s, axis=-1, keepdims=True)
                p = jnp.exp2(s - m)
                ms.append(m)
                ls.append(jnp.sum(p, axis=-1, keepdims=True))
                ps.append(p.astype(_BF))
            pv4 = jnp.dot(jnp.concatenate(ps, axis=0), v, preferred_element_type=_F32)
            for h in heads:
                l = ls[h] + jnp.exp2(sink_ref[head_ids[h]] * LOG2_E - ms[h])
                o_ref[rows, h * HEAD_DIM:(h + 1) * HEAD_DIM] = (
                    pv4[h * SUBQ:(h + 1) * SUBQ] / l).astype(o_ref.dtype)
            continue
        lse_tile = jnp.zeros((SUBQ, LANES), _F32)
        for h in heads:
            cols = slice(h * HEAD_DIM, (h + 1) * HEAD_DIM)
            k = kcat[sb * SUBQ:sb * SUBQ + win, cols]
            v = vcat[sb * SUBQ:sb * SUBQ + win, cols]
            s = lax.dot_general(q_ref[rows, cols], k, (((1,), (1,)), ((), ())), preferred_element_type=_F32)
            s = scores2(s, h)
            m = jnp.max(s, axis=-1, keepdims=True)
            p = jnp.exp2(s - m)
            l = jnp.sum(p, axis=-1, keepdims=True)
            pv = jnp.dot(p.astype(_BF), v, preferred_element_type=_F32)
            lse_tile = jnp.where(lane == h, m * LN_2 + jnp.log(l), lse_tile)
            o_ref[rows, cols] = (pv / l).astype(o_ref.dtype)
        lse_ref[rows, :] = lse_tile


def _band_attention(qkv, slopes, sinks, *, n_inner, bq, half, step, shared_kv, has_sink,
                    prompt_rows, prompt_seq, sample_seq, name):
    lead, rows, _ = qkv.shape
    assert prompt_seq % bq == 0 and sample_seq % bq == 0 and bq % SUBQ == 0 and bq % half == 0
    assert half <= SUBQ
    nq = rows // bq
    hb = bq // half
    last_halo = rows // half - 1
    width = HEADS_PER_STEP * HEAD_DIM
    if shared_kv:
        kv_w = HEAD_DIM
        k_base, v_base = width * n_inner // HEAD_DIM, width * n_inner // HEAD_DIM + n_inner
        lead_of = lambda c: 0
        q_col = lambda c: c
        kv_col = lambda base: (lambda c: base + c)
    else:
        kv_w = width
        k_base, v_base = 1, 2
        lead_of = lambda c: c
        q_col = lambda c: 0
        kv_col = lambda base: (lambda c: base)

    def main_map(col):
        return lambda i, c: (lead_of(c), i, col(c))

    def left_map(col):
        return lambda i, c: (lead_of(c), jnp.maximum(i * hb - 1, 0), col(c))

    def right_map(col):
        return lambda i, c: (lead_of(c), jnp.minimum((i + 1) * hb, last_halo), col(c))

    smem = pl.BlockSpec(memory_space=pltpu.SMEM)
    in_specs = [
        smem, smem,
        pl.BlockSpec((None, bq, width), main_map(q_col)),
        pl.BlockSpec((None, half, kv_w), left_map(kv_col(k_base))),
        pl.BlockSpec((None, bq, kv_w), main_map(kv_col(k_base))),
        pl.BlockSpec((None, half, kv_w), right_map(kv_col(k_base))),
        pl.BlockSpec((None, half, kv_w), left_map(kv_col(v_base))),
        pl.BlockSpec((None, bq, kv_w), main_map(kv_col(v_base))),
        pl.BlockSpec((None, half, kv_w), right_map(kv_col(v_base))),
    ]
    if has_sink:
        out_specs = pl.BlockSpec((bq, width), lambda i, c: (i, c))
        out_shape = jax.ShapeDtypeStruct((rows, n_inner * width), _BF)
    else:
        out_specs = (pl.BlockSpec((None, bq, width), lambda i, c: (c, i, 0)),
                     pl.BlockSpec((None, bq, LANES), lambda i, c: (c, i, 0)))
        out_shape = (jax.ShapeDtypeStruct((lead, rows, width), _BF),
                     jax.ShapeDtypeStruct((lead, rows, LANES), _F32))
    return pl.pallas_call(
        functools.partial(_band_attn_kernel, bq=bq, half=half, step=step, shared_kv=shared_kv,
                          has_sink=has_sink, prompt_rows=prompt_rows, prompt_seq=prompt_seq,
                          sample_seq=sample_seq),
        grid=(nq, n_inner),
        in_specs=in_specs,
        out_specs=out_specs,
        out_shape=out_shape,
        scratch_shapes=[pltpu.VMEM((bq + 2 * half, kv_w), _BF), pltpu.VMEM((bq + 2 * half, kv_w), _BF)],
        compiler_params=_params("parallel", "arbitrary"),
        name=name,
    )(slopes, sinks, qkv, qkv, qkv, qkv, qkv, qkv, qkv)


def _mix_kernel(xp_ref, xs_ref, oa_ref, ob0_ref, ob1_ref, ob2_ref, l0_ref, l1_ref, l2_ref, g_ref,
                wpa_ref, wpb_ref, wout_ref, n2_ref, wr_ref, br_ref, h_ref, hn_ref, ri_ref, rw_ref, cnt_ref,
                o_scr, l_scr, carry_ref, lg_scr, mg_scr, *, n_tiles, n_prompt_tiles, d, tm):
    i = pl.program_id(0)

    @pl.when(i == 0)
    def _():
        carry_ref[...] = jnp.zeros_like(carry_ref)
        lg_scr[...] = jnp.zeros_like(lg_scr)
        mg_scr[...] = jnp.zeros_like(mg_scr)

    _route_tile(lg_scr[...], jnp.where(i > 1, 1.0, 0.0), ri_ref, rw_ref, cnt_ref, carry_ref)

    tile2 = jnp.clip(i - 1, 0, n_tiles - 1)
    x = jnp.where(tile2 < n_prompt_tiles, xp_ref[...], xs_ref[...])
    h_new = x + jnp.dot(mg_scr[(i + 1) % 2], wout_ref[...], preferred_element_type=_F32)
    h_ref[...] = h_new
    hn = h_new * lax.rsqrt(jnp.mean(h_new * h_new, axis=-1, keepdims=True) + EPS) * n2_ref[...]
    _store_row_tiles(hn_ref, _pack_halves(hn))
    hn_hi = hn.astype(_BF)
    hn_lo = (hn - hn_hi.astype(_F32)).astype(_BF)
    r = (jnp.dot(hn_hi, wr_ref[...], preferred_element_type=_F32)
         + jnp.dot(hn_lo, wr_ref[...], preferred_element_type=_F32))
    lg_scr[...] = r[:, :LANES] + r[:, LANES:] + br_ref[...]


    def token_order(o_ref, l_ref, r, slot):
        if r == 1:
            return [o_ref[0, :, h * HEAD_DIM:(h + 1) * HEAD_DIM].astype(_F32) for h in range(B_HPG)], l_ref[0]
        for p in range(r):
            for h in range(B_HPG):
                o_scr[slot, h, pl.ds(p, tm // r, stride=r), :] = (
                    o_ref[p, :, h * HEAD_DIM:(h + 1) * HEAD_DIM].astype(_F32))
            l_scr[slot, pl.ds(p, tm // r, stride=r), :] = l_ref[p]
        return [o_scr[slot, h] for h in range(B_HPG)], l_scr[slot]

    o0, l0 = token_order(ob0_ref, l0_ref, DILATED_GROUPS[0][1], 0)
    o1, l1 = token_order(ob1_ref, l1_ref, DILATED_GROUPS[1][1], 0)
    o2, l2 = token_order(ob2_ref, l2_ref, DILATED_GROUPS[2][1], 1)
    mx = jnp.maximum(jnp.maximum(l0, l1), l2)
    e0, e1, e2 = jnp.exp(l0 - mx), jnp.exp(l1 - mx), jnp.exp(l2 - mx)
    den = e0 + e1 + e2
    a0, a1, a2 = e0 / den, e1 / den, e2 / den
    parts = []
    for h in range(B_HPG):
        parts.append(a0[:, h:h + 1] * o0[h] + a1[:, h:h + 1] * o1[h] + a2[:, h:h + 1] * o2[h])
    ob = jnp.concatenate(parts, axis=1).astype(_BF)

    ta = jnp.dot(oa_ref[...], wpa_ref[...], preferred_element_type=_F32)
    tb = jnp.dot(ob, wpb_ref[...], preferred_element_type=_F32)
    merged = g_ref[:, :d].astype(_F32) * ta + g_ref[:, d:].astype(_F32) * tb
    mg_scr[i % 2] = merged.astype(_BF)


def _mix(xp, xs, oa, obs, lses, gates, wpa, wpb, wout, n2, wr, br):
    n_p, d = xp.shape
    n_s = xs.shape[0]
    t = n_p + n_s
    tm = ROW_TILE
    npt = n_p // tm
    n_tiles = t // tm
    def stage_tile(lag):
        return lambda i: jnp.clip(i - lag, 0, n_tiles - 1)

    row = lambda i: (stage_tile(0)(i), 0)
    row2 = lambda i: (stage_tile(1)(i), 0)
    routed = lambda i: (stage_tile(2)(i), 0)
    const = lambda i: (0, 0)
    width = B_HPG * HEAD_DIM

    def resident(shape):
        return pl.BlockSpec(shape, const, pipeline_mode=pl.Buffered(1))

    def phase_blocks(cols):
        return [pl.BlockSpec((r, tm // r, cols), lambda i: (0, stage_tile(0)(i), 0)) for _, r in DILATED_GROUPS]

    in_specs = [
        pl.BlockSpec((tm, d), lambda i: (jnp.minimum(stage_tile(1)(i), npt - 1), 0)),
        pl.BlockSpec((tm, d), lambda i: (jnp.maximum(stage_tile(1)(i) - npt, 0), 0)),
        pl.BlockSpec((tm, oa.shape[1]), row),
        *phase_blocks(width), *phase_blocks(LANES),
        pl.BlockSpec((tm, 2 * d), row),
        resident(wpa.shape), resident(wpb.shape), resident(wout.shape),
        resident((1, d)), resident(wr.shape), resident((1, LANES)),
    ]
    return pl.pallas_call(
        functools.partial(_mix_kernel, n_tiles=n_tiles, n_prompt_tiles=npt, d=d, tm=tm),
        grid=(n_tiles + 2,),
        in_specs=in_specs,
        out_specs=(pl.BlockSpec((tm, d), row2), pl.BlockSpec((tm * (d // 2 // LANES), LANES), row2),
                   pl.BlockSpec((tm, LANES), routed), pl.BlockSpec((tm, LANES), routed),
                   pl.BlockSpec((SUBLANES, LANES), const)),
        out_shape=(jax.ShapeDtypeStruct((t, d), _F32),
                   jax.ShapeDtypeStruct((t * (d // 2 // LANES), LANES), jnp.uint32),
                   jax.ShapeDtypeStruct((t, LANES), jnp.int32),
                   jax.ShapeDtypeStruct((t, LANES), _F32),
                   jax.ShapeDtypeStruct((SUBLANES, LANES), jnp.int32)),
        scratch_shapes=[pltpu.VMEM((2, B_HPG, tm, HEAD_DIM), _F32), pltpu.VMEM((2, tm, LANES), _F32),
                        pltpu.VMEM((SUBLANES, LANES), _F32), pltpu.VMEM((tm, LANES), _F32),
                        pltpu.VMEM((2, tm, d), _BF)],
        compiler_params=_params("arbitrary"),
        name="mix_out_router",
    )(xp, xs, oa, *obs, *lses, gates, wpa, wpb, wout, n2, wr, br)


def _route_tile(lg, live, ri_ref, rw_ref, cnt_ref, carry_ref):
    tr = lg.shape[0]
    lane_i = lax.broadcasted_iota(jnp.int32, (tr, LANES), 1)
    lane = lane_i.astype(_F32)
    no_lane = float(LANES)
    is_grp = lane_i < N_GROUPS
    glog = jnp.where(is_grp, lg, MASKED)
    gmax = jnp.max(glog, axis=-1, keepdims=True)
    grp = jnp.min(jnp.where(glog == gmax, lane, no_lane), axis=-1, keepdims=True)
    gsum = jnp.sum(jnp.where(is_grp, jnp.exp(glog - gmax), 0.0), axis=-1, keepdims=True)
    pgrp = 1.0 / gsum
    lane_grp = ((lane_i - N_GROUPS) // EXPERTS_PER_GROUP).astype(_F32)
    in_grp = (lane_i >= N_GROUPS) & (lane_i < N_GROUPS + N_EXPERTS) & (lane_grp == grp)
    elog = jnp.where(in_grp, lg, MASKED)
    t1 = jnp.max(elog, axis=-1, keepdims=True)
    i1 = jnp.min(jnp.where(elog == t1, lane, no_lane), axis=-1, keepdims=True)
    elog2 = jnp.where(lane == i1, MASKED, elog)
    t2 = jnp.max(elog2, axis=-1, keepdims=True)
    i2 = jnp.min(jnp.where(elog2 == t2, lane, no_lane), axis=-1, keepdims=True)
    e21 = jnp.exp(t2 - t1)
    w1 = pgrp / (1.0 + e21)
    w2 = pgrp * e21 / (1.0 + e21)
    eid1 = i1 - N_GROUPS
    eid2 = i2 - N_GROUPS
    hot1 = lane == eid1
    hot2 = lane == eid2
    onehot = (jnp.where(hot1, 1.0, 0.0) + jnp.where(hot2, 1.0, 0.0)) * live
    r_i = lax.broadcasted_iota(jnp.int32, (tr, tr), 0)
    c_i = lax.broadcasted_iota(jnp.int32, (tr, tr), 1)
    lower = jnp.where(c_i < r_i, 1.0, 0.0).astype(_BF)
    before = jnp.dot(lower, onehot.astype(_BF), preferred_element_type=_F32) + carry_ref[0:1, :]
    rank1 = jnp.sum(jnp.where(hot1, before, 0.0), axis=-1, keepdims=True)
    rank2 = jnp.sum(jnp.where(hot2, before, 0.0), axis=-1, keepdims=True)
    total = carry_ref[0:1, :] + jnp.sum(onehot, axis=0, keepdims=True)
    carry_ref[...] = jnp.broadcast_to(total, carry_ref.shape)
    cnt_ref[...] = jnp.broadcast_to(total, cnt_ref.shape).astype(jnp.int32)
    ri = jnp.where(lane_i == 0, eid1, jnp.where(lane_i == 1, eid2, jnp.where(lane_i == 2, rank1,
                   jnp.where(lane_i == 3, rank2, 0.0))))
    ri_ref[...] = ri.astype(jnp.int32)
    rw_ref[...] = jnp.where(lane_i == 0, w1, jnp.where(lane_i == 1, w2, 0.0))


def _dispatch_kernel(dest_ref, pad_start_ref, pad_n_ref, nvalid_ref, hn_ref, xr_ref, zeros, sem, pad_sem,
                     *, td, sc):
    i = pl.program_id(0)

    def rows_of(ref, row0, n):
        start = row0 * sc if isinstance(row0, int) else pl.multiple_of(row0 * sc, n * sc)
        return ref.at[pl.ds(start, n * sc), :]

    @pl.when(i == 0)
    def _():
        zeros[...] = jnp.zeros_like(zeros)

        def pad_chunk(e, r):
            row0 = pl.multiple_of(pad_start_ref[e] + r * SUBLANES, SUBLANES)
            return pltpu.make_async_copy(rows_of(zeros, 0, SUBLANES), rows_of(xr_ref, row0, SUBLANES), pad_sem)

        def per_expert(act):
            def body(e, carry):
                lax.fori_loop(0, pad_n_ref[e], lambda r, c: (act(pad_chunk(e, r)), c)[1], 0)
                return carry
            return body

        def tail_block(b):
            return pltpu.make_async_copy(zeros, rows_of(xr_ref, b * EXPERT_ROWS, EXPERT_ROWS), pad_sem)

        n_blocks = xr_ref.shape[0] // (EXPERT_ROWS * sc)
        lax.fori_loop(0, N_EXPERTS, per_expert(lambda cp: cp.start()), 0)
        lax.fori_loop(nvalid_ref[0], n_blocks, lambda b, c: (tail_block(b).start(), c)[1], 0)
        lax.fori_loop(0, N_EXPERTS, per_expert(lambda cp: cp.wait()), 0)
        lax.fori_loop(nvalid_ref[0], n_blocks, lambda b, c: (tail_block(b).wait(), c)[1], 0)

    def start(j, carry):
        for k in range(2):
            dst = dest_ref[2 * (i * td + j) + k]
            pltpu.make_async_copy(rows_of(hn_ref, j, 1), rows_of(xr_ref, dst, 1), sem).start()
        return carry

    lax.fori_loop(0, td, start, 0, unroll=DMA_UNROLL)
    for _ in range(2):
        pltpu.make_async_copy(hn_ref, rows_of(xr_ref, 0, td), sem).wait()


def _dispatch(dest, pad_start, pad_n, nvalid, hn, t, rows):
    sc = hn.shape[0] // t
    td = MOVE_TILE
    return pl.pallas_call(
        functools.partial(_dispatch_kernel, td=td, sc=sc),
        grid_spec=pltpu.PrefetchScalarGridSpec(
            num_scalar_prefetch=4,
            grid=(t // td,),
            in_specs=[pl.BlockSpec((td * sc, LANES), lambda i, *_: (i, 0))],
            out_specs=pl.BlockSpec(memory_space=pl.ANY),
            scratch_shapes=[pltpu.VMEM((EXPERT_ROWS * sc, LANES), hn.dtype), pltpu.SemaphoreType.DMA(()),
                            pltpu.SemaphoreType.DMA(())],
        ),
        out_shape=jax.ShapeDtypeStruct((rows * sc, LANES), hn.dtype),
        compiler_params=_params("arbitrary"),
        name="dispatch",
    )(dest, pad_start, pad_n, nvalid, hn)


def _expert_kernel(blk_e_ref, nvalid_ref, first_ref, slot_ref, next_e_ref, x_ref, w1_hbm, w3_hbm, w2_hbm, y_ref,
                   w1_buf, w3_buf, w2_buf, sems):
    b = pl.program_id(0)
    valid = b < nvalid_ref[0]

    def fetch(e, slot):
        return (pltpu.make_async_copy(w1_hbm.at[e], w1_buf.at[slot], sems.at[slot, 0]),
                pltpu.make_async_copy(w3_hbm.at[e], w3_buf.at[slot], sems.at[slot, 1]),
                pltpu.make_async_copy(w2_hbm.at[e], w2_buf.at[slot], sems.at[slot, 2]))

    @pl.when(b == 0)
    def _():
        for cp in fetch(blk_e_ref[0], 0):
            cp.start()

    @pl.when(valid & (first_ref[b] == 1))
    def _():
        for cp in fetch(blk_e_ref[b], slot_ref[b]):
            cp.wait()

        @pl.when(next_e_ref[b] >= 0)
        def _():
            for cp in fetch(next_e_ref[b], 1 - slot_ref[b]):
                cp.start()

    @pl.when(valid)
    def _():
        slot = slot_ref[b]
        packed_x = _load_row_tiles(x_ref, EXPERT_ROWS)
        xb = jnp.concatenate([half.astype(_BF) for half in _unpack_halves(packed_x)], axis=1)
        a = jnp.dot(xb, w1_buf[slot].astype(_BF), preferred_element_type=_F32)
        u = jnp.dot(xb, w3_buf[slot].astype(_BF), preferred_element_type=_F32)
        hmid = (a / (1.0 + jnp.exp(-a))) * u
        y = jnp.dot(hmid.astype(_BF), w2_buf[slot].astype(_BF), preferred_element_type=_F32)
        _store_row_tiles(y_ref, _pack_halves(y))

    @pl.when(jnp.logical_not(valid))
    def _():
        y_ref[...] = jnp.zeros_like(y_ref)


def _experts(blk_e, nvalid, first, slot, next_e, xr, w1, w3, w2):
    _, d, f = w1.shape
    sc = d // 2 // LANES
    blk = EXPERT_ROWS * sc
    nblk = xr.shape[0] // blk
    any_space = pl.BlockSpec(memory_space=pl.ANY)
    return pl.pallas_call(
        _expert_kernel,
        grid_spec=pltpu.PrefetchScalarGridSpec(
            num_scalar_prefetch=5,
            grid=(nblk,),
            in_specs=[pl.BlockSpec((blk, LANES), lambda b, be, nv, *_: (jnp.minimum(b, nv[0] - 1), 0)),
                      any_space, any_space, any_space],
            out_specs=pl.BlockSpec((blk, LANES), lambda b, *_: (b, 0)),
            scratch_shapes=[pltpu.VMEM((2, d, f), w1.dtype), pltpu.VMEM((2, d, f), w3.dtype),
                            pltpu.VMEM((2, f, d), w2.dtype), pltpu.SemaphoreType.DMA((2, 3))],
        ),
        out_shape=jax.ShapeDtypeStruct(xr.shape, jnp.uint32),
        compiler_params=_params("arbitrary"),
        name="experts",
    )(blk_e, nvalid, first, slot, next_e, xr, w1, w3, w2)


def _combine_kernel(dest_ref, h_ref, rw_ref, g_ref, yr_ref, op_ref, os_ref, ybuf, sems, *, tc, sc, n_prompt_tiles):
    i = pl.program_id(0)
    slot = i % 2

    def gather(tile, s):
        def start(j, carry):
            for k in range(2):
                src = dest_ref[2 * (tile * tc + j) + k]
                pltpu.make_async_copy(_row_tile(yr_ref, src, sc), _row_tile(ybuf.at[s, k], j, sc),
                                      sems.at[s]).start()
            return carry

        lax.fori_loop(0, tc, start, 0, unroll=DMA_UNROLL)

    @pl.when(i == 0)
    def _():
        gather(0, 0)

    @pl.when(i + 1 < pl.num_programs(0))
    def _():
        gather(i + 1, 1 - slot)

    for k in range(2):
        pltpu.make_async_copy(yr_ref.at[pl.ds(0, tc * sc), :], ybuf.at[slot, k], sems.at[slot]).wait()

    rw = rw_ref[...]
    y0 = _unpack_halves(_load_row_tiles(ybuf.at[slot, 0], tc))
    y1 = _unpack_halves(_load_row_tiles(ybuf.at[slot, 1], tc))
    y = jnp.concatenate([rw[:, 0:1] * a + rw[:, 1:2] * b for a, b in zip(y0, y1)], axis=1)
    z = h_ref[...] + y
    out = z * lax.rsqrt(jnp.mean(z * z, axis=-1, keepdims=True) + EPS) * g_ref[...]

    @pl.when(i < n_prompt_tiles)
    def _():
        op_ref[...] = out

    @pl.when(i >= n_prompt_tiles)
    def _():
        os_ref[...] = out


def _combine(dest, h, rw, g, yr, *, n_p):
    t, d = h.shape
    tc = MOVE_TILE
    npt = n_p // tc
    sc = d // 2 // LANES
    return pl.pallas_call(
        functools.partial(_combine_kernel, tc=tc, sc=sc, n_prompt_tiles=npt),
        grid_spec=pltpu.PrefetchScalarGridSpec(
            num_scalar_prefetch=1,
            grid=(t // tc,),
            in_specs=[
                pl.BlockSpec((tc, d), lambda i, dest: (i, 0)),
                pl.BlockSpec((tc, LANES), lambda i, dest: (i, 0)),
                pl.BlockSpec((1, d), lambda i, dest: (0, 0)),
                pl.BlockSpec(memory_space=pl.ANY),
            ],
            out_specs=(pl.BlockSpec((tc, d), lambda i, dest: (jnp.minimum(i, npt - 1), 0)),
                       pl.BlockSpec((tc, d), lambda i, dest: (jnp.maximum(i - npt, 0), 0))),
            scratch_shapes=[pltpu.VMEM((2, 2, tc * sc, LANES), yr.dtype), pltpu.SemaphoreType.DMA((2,))],
        ),
        out_shape=(jax.ShapeDtypeStruct((n_p, d), _F32), jax.ShapeDtypeStruct((t - n_p, d), _F32)),
        compiler_params=_params("arbitrary"),
        name="combine_norm",
    )(dest, h, rw, g, yr)


def _layer(xp, xs, prompt_seq, sample_seq, norm1, w_in, sink, w_proj_a, w_proj_b, w_gate, b_gate, w_out,
           norm2, w_rg, b_rg, w_re, b_re, w1, w3, w2, norm_final):
    n_p, d = xp.shape
    n_s = xs.shape[0]
    t = n_p + n_s
    a_q = A_HEADS * HEAD_DIM
    a_cols = a_q + 2 * A_KV * HEAD_DIM

    proj_a, q0, q1, q2, gates = _project(xp, xs, norm1.reshape(1, d), w_in, w_gate, b_gate.reshape(1, 2 * d),
                                         a_cols=a_cols)

    oa = _band_attention(
        proj_a.reshape(1, t, a_cols), jnp.asarray(_alibi(A_HEADS)), sink.astype(_F32), n_inner=A_KV,
        bq=min(1024, prompt_seq, sample_seq), half=A_HALF, step=1, shared_kv=True, has_sink=True,
        prompt_rows=n_p, prompt_seq=prompt_seq, sample_seq=sample_seq, name="attn_window")

    slopes_b = _alibi(B_HEADS)
    obs, lses = [], []
    for gi, ((w, r), qkv) in enumerate(zip(DILATED_GROUPS, (q0, q1, q2))):
        o_g, lse_g = _band_attention(
            qkv, jnp.asarray(slopes_b[gi * B_HPG:(gi + 1) * B_HPG]), jnp.zeros((B_HPG,), _F32), n_inner=r,
            bq=min(1024, prompt_seq // r, sample_seq // r), half=w // (2 * r), step=r, shared_kv=False,
            has_sink=False, prompt_rows=n_p // r, prompt_seq=prompt_seq // r, sample_seq=sample_seq // r,
            name=f"attn_dilated_{r}")
        obs.append(o_g)
        lses.append(lse_g)

    n_r = N_GROUPS + N_EXPERTS
    wr = jnp.concatenate([w_rg, jnp.transpose(w_re, (1, 0, 2)).reshape(d, N_EXPERTS)], axis=1)
    wr = jnp.pad(wr, ((0, 0), (0, LANES - n_r)))
    wr_hi = wr.astype(_BF)
    wr_lo = (wr - wr_hi.astype(_F32)).astype(_BF)
    wr2 = jnp.concatenate([wr_hi, wr_lo], axis=1)
    br = jnp.pad(jnp.concatenate([b_rg, b_re.reshape(-1)]), (0, LANES - n_r)).reshape(1, LANES).astype(_F32)

    h, hn, ri, rw, cnt = _mix(xp, xs, oa, obs, lses, gates, w_proj_a.astype(_BF), w_proj_b.astype(_BF),
                              w_out.astype(_BF), norm2.reshape(1, d), wr2, br)
    counts = cnt[0, :N_EXPERTS]
    eid = ri[:, 0:2]
    rank = ri[:, 2:4]
    pcounts = (counts + EXPERT_ROWS - 1) // EXPERT_ROWS * EXPERT_ROWS
    pends = jnp.cumsum(pcounts)
    pstarts = pends - pcounts
    expert_ids = jnp.arange(N_EXPERTS, dtype=jnp.int32)
    start_of = jnp.sum(jnp.where(eid[:, :, None] == expert_ids, pstarts.astype(jnp.int32), 0), axis=-1)
    dest = (start_of + rank).reshape(-1).astype(jnp.int32)
    nblk = (2 * t + N_EXPERTS * (EXPERT_ROWS - 1) + EXPERT_ROWS - 1) // EXPERT_ROWS
    nvalid = (pends[-1] // EXPERT_ROWS).astype(jnp.int32)
    blk_start = jnp.minimum(jnp.arange(nblk, dtype=jnp.int32), nvalid - 1) * EXPERT_ROWS
    blk_e = jnp.sum(pends[None, :] <= blk_start[:, None], axis=1).astype(jnp.int32)

    pad_start = (pstarts + counts) // SUBLANES * SUBLANES
    xr = _dispatch(dest, pad_start.astype(jnp.int32), ((pends - pad_start) // SUBLANES).astype(jnp.int32),
                   nvalid.reshape(1), hn, t, nblk * EXPERT_ROWS)
    blk_ids = jnp.arange(nblk, dtype=jnp.int32)
    first = ((blk_ids == 0) | (blk_e != jnp.roll(blk_e, 1))).astype(jnp.int32)
    slot = ((jnp.cumsum(first) - 1) % 2).astype(jnp.int32)
    later = (expert_ids[None, :] > expert_ids[:, None]) & (counts[None, :] > 0)
    next_of = jnp.min(jnp.where(later, expert_ids[None, :], N_EXPERTS), axis=1)
    next_e = jnp.sum(jnp.where(blk_e[:, None] == expert_ids, next_of, 0), axis=1)
    next_e = jnp.where(next_e < N_EXPERTS, next_e, -1).astype(jnp.int32)
    yr = _experts(blk_e, nvalid.reshape(1), first, slot, next_e, xr, w1, w3, w2)
    return _combine(dest, h, rw, norm_final.reshape(1, d), yr, n_p=n_p)


def kernel(x_prompt, x_sample, norm1, w_in, attn_sink, w_proj_a, w_proj_b, w_gate, b_gate, w_out, norm2,
           w_router_group, b_router_group, w_router_expert, b_router_expert, w_expert_gate, w_expert_up,
           w_expert_down, norm_final):
    assert norm1.shape[0] == 1, "one layer"
    d = x_prompt.shape[-1]
    xp = x_prompt.reshape(-1, d)
    xs = x_sample.reshape(-1, d)
    yp, ys = _layer(xp, xs, x_prompt.shape[1], x_sample.shape[1], norm1[0], w_in[0], attn_sink[0], w_proj_a[0],
                    w_proj_b[0], w_gate[0], b_gate[0], w_out[0], norm2[0], w_router_group[0],
                    b_router_group[0], w_router_expert[0], b_router_expert[0], w_expert_gate[0],
                    w_expert_up[0], w_expert_down[0], norm_final)
    return yp.reshape(x_prompt.shape), ys.reshape(x_sample.shape)
```

```python
import functools
import math

import jax
import jax.numpy as jnp
import numpy as np
from jax import lax
from jax.experimental import pallas as pl
from jax.experimental.pallas import tpu as pltpu

HEAD_DIM = 128
A_HEADS = 16
A_KV = 4
A_HALF = 128
DILATED_GROUPS = ((128, 1), (512, 4), (2048, 16))
N_DIL = len(DILATED_GROUPS)
B_HPG = 4
B_HEADS = N_DIL * B_HPG
N_GROUPS = 8
EXPERTS_PER_GROUP = 8
N_EXPERTS = N_GROUPS * EXPERTS_PER_GROUP
EPS = 1e-6

LANES = 128
SUBLANES = 8
MASKED = -1e30
LOG2_E = math.log2(math.e)
LN_2 = math.log(2.0)
VMEM_LIMIT = 56 * 1024 * 1024
SUBQ = 128
HEADS_PER_STEP = 4
EXPERT_ROWS = 256
DMA_UNROLL = 8
PROJ_TM, PROJ_TN = 2048, 512
MAX_ROW_STRIDE = 4
ROW_TILE = 256
DISPATCH_TILE = 512
ATTN_ROWS_PER_STEP = 1024

_BF = jnp.bfloat16
_F32 = jnp.float32


def _alibi(n):
    return np.power(2.0, -8.0 * (np.arange(n) + 1) / n).astype(np.float32)


def _params(*sem):
    return pltpu.CompilerParams(dimension_semantics=sem, vmem_limit_bytes=VMEM_LIMIT)


def _pack_halves(x):
    n = x.shape[1] // 2

    def bf16_bits(v):
        return lax.bitcast_convert_type(v.astype(_BF).astype(_F32), jnp.uint32)

    return bf16_bits(x[:, n:]) | (bf16_bits(x[:, :n]) >> 16)


def _unpack_halves(p):
    lo = lax.bitcast_convert_type(p << 16, _F32)
    hi = lax.bitcast_convert_type(p & jnp.uint32(0xFFFF0000), _F32)
    return lo, hi


def _store_row_tiles(ref, packed):
    rows, words = packed.shape
    s_count = words // LANES
    for s in range(s_count):
        ref[pl.ds(s, rows, stride=s_count), :] = packed[:, s * LANES:(s + 1) * LANES]


def _load_row_tiles(ref, rows):
    s_count = ref.shape[0] // rows
    return jnp.concatenate([ref[pl.ds(s, rows, stride=s_count), :] for s in range(s_count)], axis=1)


def _row_tile(ref, row, s_count):
    return ref.at[pl.ds(pl.multiple_of(row * s_count, s_count), s_count), :]


def _proj_kernel(g_ref, b_ref, xp_hbm, xs_hbm, win_hbm, wg_hbm, oa_hbm, q0_hbm, q1_hbm, q2_hbm, gt_hbm,
                 xbuf, xn_ref, wbuf, acc_ref, obuf, x_sem, w_sems, o_sems,
                 *, n_prompt_tiles, n_row_tiles, tm, tn, na, ng):
    i = pl.program_id(0)
    nb = 3
    tiles_per_row = na + N_DIL * nb + ng
    rows = pl.ds(pl.multiple_of(i * tm, tm), tm)

    def cols(jj):
        return pl.ds(pl.multiple_of(jj * tn, tn), tn)

    def x_copy(x_hbm, row_tile):
        return pltpu.make_async_copy(x_hbm.at[pl.ds(pl.multiple_of(row_tile * tm, tm), tm), :], xbuf, x_sem)

    def start_x(row_tile):
        @pl.when(row_tile < n_prompt_tiles)
        def _():
            x_copy(xp_hbm, row_tile).start()

        @pl.when(row_tile >= n_prompt_tiles)
        def _():
            x_copy(xs_hbm, row_tile - n_prompt_tiles).start()

    def w_copy(w_hbm, col, slot):
        return pltpu.make_async_copy(w_hbm.at[:, cols(col)], wbuf.at[slot], w_sems.at[slot])

    def wait_staging(slot):
        pltpu.make_async_copy(obuf.at[slot], oa_hbm.at[pl.ds(0, tm), pl.ds(0, tn)], o_sems.at[slot]).wait()

    def store_plain(out_hbm):
        def store(acc, jj, slot):
            obuf[slot] = acc.astype(_BF)
            pltpu.make_async_copy(obuf.at[slot], out_hbm.at[rows, cols(jj)], o_sems.at[slot]).start()
        return store

    def store_gate(acc, jj, slot):
        z = acc + b_ref[jj]
        obuf[slot] = (0.5 * jnp.tanh(0.5 * z) + 0.5).astype(_BF)
        pltpu.make_async_copy(obuf.at[slot], gt_hbm.at[rows, cols(jj)], o_sems.at[slot]).start()

    def store_phases(q_hbm, r):
        n = tm // r

        def store(acc, jj, slot):
            if r == 1:
                obuf[slot] = acc.astype(_BF)
            else:
                for cb in range(tn // LANES):
                    acc_ref[0, cb] = acc[:, cb * LANES:(cb + 1) * LANES]
                src, stride = 0, r
                if r > MAX_ROW_STRIDE:
                    stride = r // MAX_ROW_STRIDE
                    m = tm // MAX_ROW_STRIDE
                    for c in range(MAX_ROW_STRIDE):
                        for cb in range(tn // LANES):
                            acc_ref[1, cb, c * m:(c + 1) * m, :] = acc_ref[0, cb, pl.ds(c, m, stride=MAX_ROW_STRIDE), :]
                    src = 1
                for p in range(r):
                    start = p if src == 0 else (p % MAX_ROW_STRIDE) * (tm // MAX_ROW_STRIDE) + p // MAX_ROW_STRIDE
                    for cb in range(tn // LANES):
                        obuf[slot, p * n:(p + 1) * n, cb * LANES:(cb + 1) * LANES] = (
                            acc_ref[src, cb, pl.ds(start, n, stride=stride), :].astype(_BF))
            for p in range(r):
                pltpu.make_async_copy(obuf.at[slot, pl.ds(p * n, n), :],
                                      q_hbm.at[p, pl.ds(pl.multiple_of(i * n, n), n), cols(jj)],
                                      o_sems.at[slot]).start()
        return store

    parts = [(na, win_hbm, lambda jj: jj, store_plain(oa_hbm))]
    for gi, (q_hbm, (_, r)) in enumerate(zip((q0_hbm, q1_hbm, q2_hbm), DILATED_GROUPS)):
        parts.append((nb, win_hbm, lambda jj, gi=gi: na + jj * N_DIL + gi, store_phases(q_hbm, r)))
    parts.append((ng, wg_hbm, lambda jj: jj, store_gate))

    @pl.when(i == 0)
    def _():
        start_x(0)
        w_copy(win_hbm, 0, 0).start()

    x_copy(xp_hbm, 0).wait()
    x = xbuf[...]
    y = x * lax.rsqrt(jnp.mean(x * x, axis=-1, keepdims=True) + EPS)
    xn_ref[...] = (y * g_ref[...]).astype(_BF)

    @pl.when(i + 1 < n_row_tiles)
    def _():
        start_x(i + 1)

    base = 0
    for k, (length, w_hbm, w_col, store) in enumerate(parts):
        def tile(jj, carry, base=base, length=length, w_hbm=w_hbm, w_col=w_col, store=store, k=k):
            n = i * tiles_per_row + base + jj
            slot = n % 2

            @pl.when(jj + 1 < length)
            def _():
                w_copy(w_hbm, w_col(jj + 1), 1 - slot).start()

            @pl.when(jj + 1 == length)
            def _():
                if k + 1 < len(parts):
                    w_copy(parts[k + 1][1], parts[k + 1][2](0), 1 - slot).start()
                else:
                    @pl.when(i + 1 < n_row_tiles)
                    def _():
                        w_copy(win_hbm, 0, 1 - slot).start()

            @pl.when(n >= 2)
            def _():
                wait_staging(slot)

            w_copy(w_hbm, w_col(jj), slot).wait()
            acc = jnp.dot(xn_ref[...], wbuf[slot].astype(_BF), preferred_element_type=_F32)
            store(acc, jj, slot)
            return carry

        lax.fori_loop(0, length, tile, 0)
        base += length

    @pl.when(i == n_row_tiles - 1)
    def _():
        wait_staging(0)
        wait_staging(1)


def _project(xp, xs, g, w_in, w_gate, b_gate, *, a_cols):
    n_p, d = xp.shape
    n_s = xs.shape[0]
    t = n_p + n_s
    tm, tn = PROJ_TM, PROJ_TN
    assert tn == B_HPG * HEAD_DIM and w_in.shape[1] == a_cols + 3 * N_DIL * tn
    assert n_p % tm == 0 and n_s % tm == 0
    gate_cols = w_gate.shape[1]
    na, ng = a_cols // tn, gate_cols // tn
    any_space = pl.BlockSpec(memory_space=pl.ANY)
    return pl.pallas_call(
        functools.partial(_proj_kernel, n_prompt_tiles=n_p // tm, n_row_tiles=t // tm, tm=tm, tn=tn, na=na, ng=ng),
        grid=(t // tm,),
        in_specs=[pl.BlockSpec((1, d), lambda i: (0, 0)), pl.BlockSpec((ng, 1, tn), lambda i: (0, 0, 0)),
                  any_space, any_space, any_space, any_space],
        out_specs=[any_space] * 5,
        out_shape=[jax.ShapeDtypeStruct((t, a_cols), _BF)]
                  + [jax.ShapeDtypeStruct((r, t // r, 3 * tn), _BF) for _, r in DILATED_GROUPS]
                  + [jax.ShapeDtypeStruct((t, gate_cols), _BF)],
        scratch_shapes=[pltpu.VMEM((tm, d), xp.dtype), pltpu.VMEM((tm, d), _BF), pltpu.VMEM((2, d, tn), w_in.dtype),
                        pltpu.VMEM((2, tn // LANES, tm, LANES), _F32), pltpu.VMEM((2, tm, tn), _BF),
                        pltpu.SemaphoreType.DMA(()), pltpu.SemaphoreType.DMA((2,)), pltpu.SemaphoreType.DMA((2,))],
        compiler_params=_params("arbitrary"),
        name="norm_proj",
    )(g, b_gate.reshape(ng, 1, tn), xp, xs, w_in, w_gate)


def _band_attn_kernel(slope_ref, sink_ref, q_ref, kl_ref, km_ref, kr_ref, vl_ref, vm_ref, vr_ref, *rest,
                      phases, has_sink, **static):
    if has_sink:
        o_ref, kcat, vcat = rest
    else:
        o_ref, lse_ref, kcat, vcat = rest
    for ph in range(phases):
        _band_attn_phase(slope_ref, sink_ref, *(r.at[ph] for r in (q_ref, kl_ref, km_ref, kr_ref, vl_ref, vm_ref,
                                                                  vr_ref)),
                         o_ref if has_sink else o_ref.at[ph], None if has_sink else lse_ref.at[ph],
                         kcat.at[ph], vcat.at[ph], has_sink=has_sink, **static)


def _band_attn_phase(slope_ref, sink_ref, q_ref, kl_ref, km_ref, kr_ref, vl_ref, vm_ref, vr_ref,
                     o_ref, lse_ref, kcat, vcat, *, bq, half, step, shared_kv, has_sink, prompt_rows,
                     prompt_seq, sample_seq):
    i = pl.program_id(0)
    c = pl.program_id(1)
    win = SUBQ + 2 * half

    kcat[0:half, :] = kl_ref[...]
    kcat[half:half + bq, :] = km_ref[...]
    kcat[half + bq:, :] = kr_ref[...]
    vcat[0:half, :] = vl_ref[...]
    vcat[half:half + bq, :] = vm_ref[...]
    vcat[half + bq:, :] = vr_ref[...]

    u0 = i * bq
    in_prompt = u0 < prompt_rows
    lo = jnp.where(in_prompt, (u0 // prompt_seq) * prompt_seq,
                   prompt_rows + ((u0 - prompt_rows) // sample_seq) * sample_seq)
    hi = lo + jnp.where(in_prompt, prompt_seq, sample_seq)

    qi = lax.broadcasted_iota(jnp.int32, (SUBQ, win), 0)
    kj = lax.broadcasted_iota(jnp.int32, (SUBQ, win), 1)
    rel = kj - half - qi
    absrel = jnp.abs(rel)
    band_bias = jnp.where(absrel <= half, -(absrel * step).astype(_F32), MASKED)
    lane = lax.broadcasted_iota(jnp.int32, (SUBQ, LANES), 1)

    heads = range(HEADS_PER_STEP)
    head_ids = [c * HEADS_PER_STEP + h if shared_kv else h for h in heads]
    scale2 = HEAD_DIM ** -0.5 * LOG2_E
    head_bias = [(slope_ref[hid] * LOG2_E) * band_bias for hid in head_ids]
    n_sub = bq // SUBQ
    for sb in range(n_sub):
        kpos = kj + (u0 + sb * SUBQ - half)
        in_seq = None
        if sb == 0:
            in_seq = kpos >= lo
        if sb == n_sub - 1:
            in_seq = kpos < hi if in_seq is None else in_seq & (kpos < hi)

        def scores2(s, h):
            s = s * scale2 + head_bias[h]
            return s if in_seq is None else jnp.where(in_seq, s, MASKED)

        rows = slice(sb * SUBQ, (sb + 1) * SUBQ)
        if shared_kv:
            k = kcat[sb * SUBQ:sb * SUBQ + win, :]
            v = vcat[sb * SUBQ:sb * SUBQ + win, :]
            q4 = jnp.concatenate([q_ref[rows, h * HEAD_DIM:(h + 1) * HEAD_DIM] for h in heads], axis=0)
            s4 = lax.dot_general(q4, k, (((1,), (1,)), ((), ())), preferred_element_type=_F32)
            ps, ms, ls = [], [], []
            for h in heads:
                s = scores2(s4[h * SUBQ:(h + 1) * SUBQ], h)
                m = jnp.max(s, axis=-1, keepdims=True)
                p = jnp.exp2(s - m)
                ms.append(m)
                ls.append(jnp.sum(p, axis=-1, keepdims=True))
                ps.append(p.astype(_BF))
            pv4 = jnp.dot(jnp.concatenate(ps, axis=0), v, preferred_element_type=_F32)
            for h in heads:
                l = ls[h] + jnp.exp2(sink_ref[head_ids[h]] * LOG2_E - ms[h])
                o_ref[rows, h * HEAD_DIM:(h + 1) * HEAD_DIM] = (
                    pv4[h * SUBQ:(h + 1) * SUBQ] / l).astype(o_ref.dtype)
            continue
        lse_tile = jnp.zeros((SUBQ, LANES), _F32)
        for h in heads:
            cols = slice(h * HEAD_DIM, (h + 1) * HEAD_DIM)
            k = kcat[sb * SUBQ:sb * SUBQ + win, cols]
            v = vcat[sb * SUBQ:sb * SUBQ + win, cols]
            s = lax.dot_general(q_ref[rows, cols], k, (((1,), (1,)), ((), ())), preferred_element_type=_F32)
            s = scores2(s, h)
            m = jnp.max(s, axis=-1, keepdims=True)
            p = jnp.exp2(s - m)
            l = jnp.sum(p, axis=-1, keepdims=True)
            pv = jnp.dot(p.astype(_BF), v, preferred_element_type=_F32)
            lse_tile = jnp.where(lane == h, m * LN_2 + jnp.log(l), lse_tile)
            o_ref[rows, cols] = (pv / l).astype(o_ref.dtype)
        lse_ref[rows, :] = lse_tile


def _band_attention(qkv, slopes, sinks, *, n_inner, bq, half, step, shared_kv, has_sink,
                    prompt_rows, prompt_seq, sample_seq, name):
    lead, rows, _ = qkv.shape
    assert prompt_seq % bq == 0 and sample_seq % bq == 0 and bq % SUBQ == 0 and bq % half == 0
    assert half <= SUBQ
    nq = rows // bq
    hb = bq // half
    last_halo = rows // half - 1
    width = HEADS_PER_STEP * HEAD_DIM
    if shared_kv:
        kv_w = HEAD_DIM
        k_base, v_base = width * n_inner // HEAD_DIM, width * n_inner // HEAD_DIM + n_inner
        pp = 1
        lead_of = lambda c: 0
        q_col = lambda c: c
        kv_col = lambda base: (lambda c: base + c)
    else:
        kv_w = width
        k_base, v_base = 1, 2
        pp = min(n_inner, max(1, ATTN_ROWS_PER_STEP // bq))
        n_inner //= pp
        lead_of = lambda c: c
        q_col = lambda c: 0
        kv_col = lambda base: (lambda c: base)

    def main_map(col):
        return lambda i, c: (lead_of(c), i, col(c))

    def left_map(col):
        return lambda i, c: (lead_of(c), jnp.maximum(i * hb - 1, 0), col(c))

    def right_map(col):
        return lambda i, c: (lead_of(c), jnp.minimum((i + 1) * hb, last_halo), col(c))

    smem = pl.BlockSpec(memory_space=pltpu.SMEM)
    in_specs = [
        smem, smem,
        pl.BlockSpec((pp, bq, width), main_map(q_col)),
        pl.BlockSpec((pp, half, kv_w), left_map(kv_col(k_base))),
        pl.BlockSpec((pp, bq, kv_w), main_map(kv_col(k_base))),
        pl.BlockSpec((pp, half, kv_w), right_map(kv_col(k_base))),
        pl.BlockSpec((pp, half, kv_w), left_map(kv_col(v_base))),
        pl.BlockSpec((pp, bq, kv_w), main_map(kv_col(v_base))),
        pl.BlockSpec((pp, half, kv_w), right_map(kv_col(v_base))),
    ]
    if has_sink:
        out_specs = pl.BlockSpec((bq, width), lambda i, c: (i, c))
        out_shape = jax.ShapeDtypeStruct((rows, n_inner * width), _BF)
    else:
        out_specs = (pl.BlockSpec((pp, bq, width), lambda i, c: (c, i, 0)),
                     pl.BlockSpec((pp, bq, LANES), lambda i, c: (c, i, 0)))
        out_shape = (jax.ShapeDtypeStruct((lead, rows, width), _BF),
                     jax.ShapeDtypeStruct((lead, rows, LANES), _F32))
    return pl.pallas_call(
        functools.partial(_band_attn_kernel, phases=pp, bq=bq, half=half, step=step, shared_kv=shared_kv,
                          has_sink=has_sink, prompt_rows=prompt_rows, prompt_seq=prompt_seq,
                          sample_seq=sample_seq),
        grid=(nq, n_inner),
        in_specs=in_specs,
        out_specs=out_specs,
        out_shape=out_shape,
        scratch_shapes=[pltpu.VMEM((pp, bq + 2 * half, kv_w), _BF), pltpu.VMEM((pp, bq + 2 * half, kv_w), _BF)],
        compiler_params=_params("parallel", "arbitrary"),
        name=name,
    )(slopes, sinks, qkv, qkv, qkv, qkv, qkv, qkv, qkv)


def _mix_kernel(xp_ref, xs_ref, oa_ref, ob0_ref, ob1_ref, ob2_ref, l0_ref, l1_ref, l2_ref, g_ref,
                wpa_ref, wpb_ref, wout_ref, n2_ref, wr_ref, br_ref, h_ref, hn_ref, ri_ref, rw_ref, cnt_ref,
                o_scr, l_scr, carry_ref, lg_scr, mg_scr, *, n_tiles, n_prompt_tiles, d, tm):
    i = pl.program_id(0)

    @pl.when(i == 0)
    def _():
        carry_ref[...] = jnp.zeros_like(carry_ref)
        lg_scr[...] = jnp.zeros_like(lg_scr)
        mg_scr[...] = jnp.zeros_like(mg_scr)

    _route_tile(lg_scr[...], jnp.where(i > 1, 1.0, 0.0), ri_ref, rw_ref, cnt_ref, carry_ref)

    tile2 = jnp.clip(i - 1, 0, n_tiles - 1)
    x = jnp.where(tile2 < n_prompt_tiles, xp_ref[...], xs_ref[...])
    h_new = x + jnp.dot(mg_scr[(i + 1) % 2], wout_ref[...], preferred_element_type=_F32)
    h_ref[...] = h_new
    hn = h_new * lax.rsqrt(jnp.mean(h_new * h_new, axis=-1, keepdims=True) + EPS) * n2_ref[...]
    _store_row_tiles(hn_ref, _pack_halves(hn))
    hn_hi = hn.astype(_BF)
    hn_lo = (hn - hn_hi.astype(_F32)).astype(_BF)
    r = (jnp.dot(hn_hi, wr_ref[...], preferred_element_type=_F32)
         + jnp.dot(hn_lo, wr_ref[...], preferred_element_type=_F32))
    lg_scr[...] = r[:, :LANES] + r[:, LANES:] + br_ref[...]


    def token_order(o_ref, l_ref, r, slot):
        if r == 1:
            return [o_ref[0, :, h * HEAD_DIM:(h + 1) * HEAD_DIM].astype(_F32) for h in range(B_HPG)], l_ref[0]
        for p in range(r):
            for h in range(B_HPG):
                o_scr[slot, h, pl.ds(p, tm // r, stride=r), :] = (
                    o_ref[p, :, h * HEAD_DIM:(h + 1) * HEAD_DIM].astype(_F32))
            l_scr[slot, pl.ds(p, tm // r, stride=r), :] = l_ref[p]
        return [o_scr[slot, h] for h in range(B_HPG)], l_scr[slot]

    o0, l0 = token_order(ob0_ref, l0_ref, DILATED_GROUPS[0][1], 0)
    o1, l1 = token_order(ob1_ref, l1_ref, DILATED_GROUPS[1][1], 0)
    o2, l2 = token_order(ob2_ref, l2_ref, DILATED_GROUPS[2][1], 1)
    mx = jnp.maximum(jnp.maximum(l0, l1), l2)
    e0, e1, e2 = jnp.exp(l0 - mx), jnp.exp(l1 - mx), jnp.exp(l2 - mx)
    den = e0 + e1 + e2
    a0, a1, a2 = e0 / den, e1 / den, e2 / den
    parts = []
    for h in range(B_HPG):
        parts.append(a0[:, h:h + 1] * o0[h] + a1[:, h:h + 1] * o1[h] + a2[:, h:h + 1] * o2[h])
    ob = jnp.concatenate(parts, axis=1).astype(_BF)

    ta = jnp.dot(oa_ref[...], wpa_ref[...], preferred_element_type=_F32)
    tb = jnp.dot(ob, wpb_ref[...], preferred_element_type=_F32)
    merged = g_ref[:, :d].astype(_F32) * ta + g_ref[:, d:].astype(_F32) * tb
    mg_scr[i % 2] = merged.astype(_BF)


def _mix(xp, xs, oa, obs, lses, gates, wpa, wpb, wout, n2, wr, br):
    n_p, d = xp.shape
    n_s = xs.shape[0]
    t = n_p + n_s
    tm = ROW_TILE
    npt = n_p // tm
    n_tiles = t // tm
    def stage_tile(lag):
        return lambda i: jnp.clip(i - lag, 0, n_tiles - 1)

    row = lambda i: (stage_tile(0)(i), 0)
    row2 = lambda i: (stage_tile(1)(i), 0)
    routed = lambda i: (stage_tile(2)(i), 0)
    const = lambda i: (0, 0)
    width = B_HPG * HEAD_DIM

    def resident(shape):
        return pl.BlockSpec(shape, const, pipeline_mode=pl.Buffered(1))

    def phase_blocks(cols):
        return [pl.BlockSpec((r, tm // r, cols), lambda i: (0, stage_tile(0)(i), 0)) for _, r in DILATED_GROUPS]

    in_specs = [
        pl.BlockSpec((tm, d), lambda i: (jnp.minimum(stage_tile(1)(i), npt - 1), 0)),
        pl.BlockSpec((tm, d), lambda i: (jnp.maximum(stage_tile(1)(i) - npt, 0), 0)),
        pl.BlockSpec((tm, oa.shape[1]), row),
        *phase_blocks(width), *phase_blocks(LANES),
        pl.BlockSpec((tm, 2 * d), row),
        resident(wpa.shape), resident(wpb.shape), resident(wout.shape),
        resident((1, d)), resident(wr.shape), resident((1, LANES)),
    ]
    return pl.pallas_call(
        functools.partial(_mix_kernel, n_tiles=n_tiles, n_prompt_tiles=npt, d=d, tm=tm),
        grid=(n_tiles + 2,),
        in_specs=in_specs,
        out_specs=(pl.BlockSpec((tm, d), row2), pl.BlockSpec((tm * (d // 2 // LANES), LANES), row2),
                   pl.BlockSpec((tm, LANES), routed), pl.BlockSpec((tm, LANES), routed),
                   pl.BlockSpec((SUBLANES, LANES), const)),
        out_shape=(jax.ShapeDtypeStruct((t, d), _F32),
                   jax.ShapeDtypeStruct((t * (d // 2 // LANES), LANES), jnp.uint32),
                   jax.ShapeDtypeStruct((t, LANES), jnp.int32),
                   jax.ShapeDtypeStruct((t, LANES), _F32),
                   jax.ShapeDtypeStruct((SUBLANES, LANES), jnp.int32)),
        scratch_shapes=[pltpu.VMEM((2, B_HPG, tm, HEAD_DIM), _F32), pltpu.VMEM((2, tm, LANES), _F32),
                        pltpu.VMEM((SUBLANES, LANES), _F32), pltpu.VMEM((tm, LANES), _F32),
                        pltpu.VMEM((2, tm, d), _BF)],
        compiler_params=_params("arbitrary"),
        name="mix_out_router",
    )(xp, xs, oa, *obs, *lses, gates, wpa, wpb, wout, n2, wr, br)


def _route_tile(lg, live, ri_ref, rw_ref, cnt_ref, carry_ref):
    tr = lg.shape[0]
    lane_i = lax.broadcasted_iota(jnp.int32, (tr, LANES), 1)
    lane = lane_i.astype(_F32)
    no_lane = float(LANES)
    is_grp = lane_i < N_GROUPS
    glog = jnp.where(is_grp, lg, MASKED)
    gmax = jnp.max(glog, axis=-1, keepdims=True)
    grp = jnp.min(jnp.where(glog == gmax, lane, no_lane), axis=-1, keepdims=True)
    gsum = jnp.sum(jnp.where(is_grp, jnp.exp(glog - gmax), 0.0), axis=-1, keepdims=True)
    pgrp = 1.0 / gsum
    lane_grp = ((lane_i - N_GROUPS) // EXPERTS_PER_GROUP).astype(_F32)
    in_grp = (lane_i >= N_GROUPS) & (lane_i < N_GROUPS + N_EXPERTS) & (lane_grp == grp)
    elog = jnp.where(in_grp, lg, MASKED)
    t1 = jnp.max(elog, axis=-1, keepdims=True)
    i1 = jnp.min(jnp.where(elog == t1, lane, no_lane), axis=-1, keepdims=True)
    elog2 = jnp.where(lane == i1, MASKED, elog)
    t2 = jnp.max(elog2, axis=-1, keepdims=True)
    i2 = jnp.min(jnp.where(elog2 == t2, lane, no_lane), axis=-1, keepdims=True)
    e21 = jnp.exp(t2 - t1)
    w1 = pgrp / (1.0 + e21)
    w2 = pgrp * e21 / (1.0 + e21)
    eid1 = i1 - N_GROUPS
    eid2 = i2 - N_GROUPS
    hot1 = lane == eid1
    hot2 = lane == eid2
    onehot = (jnp.where(hot1, 1.0, 0.0) + jnp.where(hot2, 1.0, 0.0)) * live
    r_i = lax.broadcasted_iota(jnp.int32, (tr, tr), 0)
    c_i = lax.broadcasted_iota(jnp.int32, (tr, tr), 1)
    lower = jnp.where(c_i < r_i, 1.0, 0.0).astype(_BF)
    before = jnp.dot(lower, onehot.astype(_BF), preferred_element_type=_F32) + carry_ref[0:1, :]
    rank1 = jnp.sum(jnp.where(hot1, before, 0.0), axis=-1, keepdims=True)
    rank2 = jnp.sum(jnp.where(hot2, before, 0.0), axis=-1, keepdims=True)
    total = carry_ref[0:1, :] + jnp.sum(onehot, axis=0, keepdims=True)
    carry_ref[...] = jnp.broadcast_to(total, carry_ref.shape)
    cnt_ref[...] = jnp.broadcast_to(total, cnt_ref.shape).astype(jnp.int32)
    ri = jnp.where(lane_i == 0, eid1, jnp.where(lane_i == 1, eid2, jnp.where(lane_i == 2, rank1,
                   jnp.where(lane_i == 3, rank2, 0.0))))
    ri_ref[...] = ri.astype(jnp.int32)
    rw_ref[...] = jnp.where(lane_i == 0, w1, jnp.where(lane_i == 1, w2, 0.0))


def _dispatch_kernel(dest_ref, pad_start_ref, pad_n_ref, nvalid_ref, hn_ref, xr_ref, zeros, sem, pad_sem,
                     *, td, sc):
    i = pl.program_id(0)

    def rows_of(ref, row0, n):
        start = row0 * sc if isinstance(row0, int) else pl.multiple_of(row0 * sc, n * sc)
        return ref.at[pl.ds(start, n * sc), :]

    @pl.when(i == 0)
    def _():
        zeros[...] = jnp.zeros_like(zeros)

        def pad_chunk(e, r):
            row0 = pl.multiple_of(pad_start_ref[e] + r * SUBLANES, SUBLANES)
            return pltpu.make_async_copy(rows_of(zeros, 0, SUBLANES), rows_of(xr_ref, row0, SUBLANES), pad_sem)

        def per_expert(act):
            def body(e, carry):
                lax.fori_loop(0, pad_n_ref[e], lambda r, c: (act(pad_chunk(e, r)), c)[1], 0)
                return carry
            return body

        def tail_block(b):
            return pltpu.make_async_copy(zeros, rows_of(xr_ref, b * EXPERT_ROWS, EXPERT_ROWS), pad_sem)

        n_blocks = xr_ref.shape[0] // (EXPERT_ROWS * sc)
        lax.fori_loop(0, N_EXPERTS, per_expert(lambda cp: cp.start()), 0)
        lax.fori_loop(nvalid_ref[0], n_blocks, lambda b, c: (tail_block(b).start(), c)[1], 0)
        lax.fori_loop(0, N_EXPERTS, per_expert(lambda cp: cp.wait()), 0)
        lax.fori_loop(nvalid_ref[0], n_blocks, lambda b, c: (tail_block(b).wait(), c)[1], 0)

    def start(j, carry):
        for k in range(2):
            dst = dest_ref[2 * (i * td + j) + k]
            pltpu.make_async_copy(rows_of(hn_ref, j, 1), rows_of(xr_ref, dst, 1), sem).start()
        return carry

    lax.fori_loop(0, td, start, 0, unroll=DMA_UNROLL)
    for _ in range(2):
        pltpu.make_async_copy(hn_ref, rows_of(xr_ref, 0, td), sem).wait()


def _dispatch(dest, pad_start, pad_n, nvalid, hn, t, rows):
    sc = hn.shape[0] // t
    td = DISPATCH_TILE
    return pl.pallas_call(
        functools.partial(_dispatch_kernel, td=td, sc=sc),
        grid_spec=pltpu.PrefetchScalarGridSpec(
            num_scalar_prefetch=4,
            grid=(t // td,),
            in_specs=[pl.BlockSpec((td * sc, LANES), lambda i, *_: (i, 0))],
            out_specs=pl.BlockSpec(memory_space=pl.ANY),
            scratch_shapes=[pltpu.VMEM((EXPERT_ROWS * sc, LANES), hn.dtype), pltpu.SemaphoreType.DMA(()),
                            pltpu.SemaphoreType.DMA(())],
        ),
        out_shape=jax.ShapeDtypeStruct((rows * sc, LANES), hn.dtype),
        compiler_params=_params("arbitrary"),
        name="dispatch",
    )(dest, pad_start, pad_n, nvalid, hn)


def _expert_kernel(blk_e_ref, nvalid_ref, first_ref, slot_ref, next_e_ref, x_ref, w1_hbm, w3_hbm, w2_hbm, y_ref,
                   w1_buf, w3_buf, w2_buf, sems):
    b = pl.program_id(0)
    valid = b < nvalid_ref[0]

    def fetch(e, slot):
        return (pltpu.make_async_copy(w1_hbm.at[e], w1_buf.at[slot], sems.at[slot, 0]),
                pltpu.make_async_copy(w3_hbm.at[e], w3_buf.at[slot], sems.at[slot, 1]),
                pltpu.make_async_copy(w2_hbm.at[e], w2_buf.at[slot], sems.at[slot, 2]))

    @pl.when(b == 0)
    def _():
        for cp in fetch(blk_e_ref[0], 0):
            cp.start()

    @pl.when(valid & (first_ref[b] == 1))
    def _():
        for cp in fetch(blk_e_ref[b], slot_ref[b]):
            cp.wait()

        @pl.when(next_e_ref[b] >= 0)
        def _():
            for cp in fetch(next_e_ref[b], 1 - slot_ref[b]):
                cp.start()

    @pl.when(valid)
    def _():
        slot = slot_ref[b]
        packed_x = _load_row_tiles(x_ref, EXPERT_ROWS)
        xb = jnp.concatenate([half.astype(_BF) for half in _unpack_halves(packed_x)], axis=1)
        a = jnp.dot(xb, w1_buf[slot].astype(_BF), preferred_element_type=_F32)
        u = jnp.dot(xb, w3_buf[slot].astype(_BF), preferred_element_type=_F32)
        hmid = (a / (1.0 + jnp.exp(-a))) * u
        y = jnp.dot(hmid.astype(_BF), w2_buf[slot].astype(_BF), preferred_element_type=_F32)
        _store_row_tiles(y_ref, _pack_halves(y))

    @pl.when(jnp.logical_not(valid))
    def _():
        y_ref[...] = jnp.zeros_like(y_ref)


def _experts(blk_e, nvalid, first, slot, next_e, xr, w1, w3, w2):
    _, d, f = w1.shape
    sc = d // 2 // LANES
    blk = EXPERT_ROWS * sc
    nblk = xr.shape[0] // blk
    any_space = pl.BlockSpec(memory_space=pl.ANY)
    return pl.pallas_call(
        _expert_kernel,
        grid_spec=pltpu.PrefetchScalarGridSpec(
            num_scalar_prefetch=5,
            grid=(nblk,),
            in_specs=[pl.BlockSpec((blk, LANES), lambda b, be, nv, *_: (jnp.minimum(b, nv[0] - 1), 0)),
                      any_space, any_space, any_space],
            out_specs=pl.BlockSpec((blk, LANES), lambda b, *_: (b, 0)),
            scratch_shapes=[pltpu.VMEM((2, d, f), w1.dtype), pltpu.VMEM((2, d, f), w3.dtype),
                            pltpu.VMEM((2, f, d), w2.dtype), pltpu.SemaphoreType.DMA((2, 3))],
        ),
        out_shape=jax.ShapeDtypeStruct(xr.shape, jnp.uint32),
        compiler_params=_params("arbitrary"),
        name="experts",
    )(blk_e, nvalid, first, slot, next_e, xr, w1, w3, w2)


def _combine_kernel(dest_ref, h_ref, rw_ref, g_ref, yr_ref, op_ref, os_ref, ybuf, sems, *, tc, sc, n_prompt_tiles):
    i = pl.program_id(0)
    slot = i % 2

    def gather(tile, s):
        def start(j, carry):
            for k in range(2):
                src = dest_ref[2 * (tile * tc + j) + k]
                pltpu.make_async_copy(_row_tile(yr_ref, src, sc), _row_tile(ybuf.at[s, k], j, sc),
                                      sems.at[s]).start()
            return carry

        lax.fori_loop(0, tc, start, 0, unroll=DMA_UNROLL)

    @pl.when(i == 0)
    def _():
        gather(0, 0)

    @pl.when(i + 1 < pl.num_programs(0))
    def _():
        gather(i + 1, 1 - slot)

    for k in range(2):
        pltpu.make_async_copy(yr_ref.at[pl.ds(0, tc * sc), :], ybuf.at[slot, k], sems.at[slot]).wait()

    rw = rw_ref[...]
    y0 = _unpack_halves(_load_row_tiles(ybuf.at[slot, 0], tc))
    y1 = _unpack_halves(_load_row_tiles(ybuf.at[slot, 1], tc))
    y = jnp.concatenate([rw[:, 0:1] * a + rw[:, 1:2] * b for a, b in zip(y0, y1)], axis=1)
    z = h_ref[...] + y
    out = z * lax.rsqrt(jnp.mean(z * z, axis=-1, keepdims=True) + EPS) * g_ref[...]

    @pl.when(i < n_prompt_tiles)
    def _():
        op_ref[...] = out

    @pl.when(i >= n_prompt_tiles)
    def _():
        os_ref[...] = out


def _combine(dest, h, rw, g, yr, *, n_p):
    t, d = h.shape
    tc = ROW_TILE
    npt = n_p // tc
    sc = d // 2 // LANES
    return pl.pallas_call(
        functools.partial(_combine_kernel, tc=tc, sc=sc, n_prompt_tiles=npt),
        grid_spec=pltpu.PrefetchScalarGridSpec(
            num_scalar_prefetch=1,
            grid=(t // tc,),
            in_specs=[
                pl.BlockSpec((tc, d), lambda i, dest: (i, 0)),
                pl.BlockSpec((tc, LANES), lambda i, dest: (i, 0)),
                pl.BlockSpec((1, d), lambda i, dest: (0, 0)),
                pl.BlockSpec(memory_space=pl.ANY),
            ],
            out_specs=(pl.BlockSpec((tc, d), lambda i, dest: (jnp.minimum(i, npt - 1), 0)),
                       pl.BlockSpec((tc, d), lambda i, dest: (jnp.maximum(i - npt, 0), 0))),
            scratch_shapes=[pltpu.VMEM((2, 2, tc * sc, LANES), yr.dtype), pltpu.SemaphoreType.DMA((2,))],
        ),
        out_shape=(jax.ShapeDtypeStruct((n_p, d), _F32), jax.ShapeDtypeStruct((t - n_p, d), _F32)),
        compiler_params=_params("arbitrary"),
        name="combine_norm",
    )(dest, h, rw, g, yr)


def _layer(xp, xs, prompt_seq, sample_seq, norm1, w_in, sink, w_proj_a, w_proj_b, w_gate, b_gate, w_out,
           norm2, w_rg, b_rg, w_re, b_re, w1, w3, w2, norm_final):
    n_p, d = xp.shape
    n_s = xs.shape[0]
    t = n_p + n_s
    a_q = A_HEADS * HEAD_DIM
    a_cols = a_q + 2 * A_KV * HEAD_DIM

    proj_a, q0, q1, q2, gates = _project(xp, xs, norm1.reshape(1, d), w_in, w_gate, b_gate.reshape(1, 2 * d),
                                         a_cols=a_cols)

    oa = _band_attention(
        proj_a.reshape(1, t, a_cols), jnp.asarray(_alibi(A_HEADS)), sink.astype(_F32), n_inner=A_KV,
        bq=min(1024, prompt_seq, sample_seq), half=A_HALF, step=1, shared_kv=True, has_sink=True,
        prompt_rows=n_p, prompt_seq=prompt_seq, sample_seq=sample_seq, name="attn_window")

    slopes_b = _alibi(B_HEADS)
    obs, lses = [], []
    for gi, ((w, r), qkv) in enumerate(zip(DILATED_GROUPS, (q0, q1, q2))):
        o_g, lse_g = _band_attention(
            qkv, jnp.asarray(slopes_b[gi * B_HPG:(gi + 1) * B_HPG]), jnp.zeros((B_HPG,), _F32), n_inner=r,
            bq=min(1024, prompt_seq // r, sample_seq // r), half=w // (2 * r), step=r, shared_kv=False,
            has_sink=False, prompt_rows=n_p // r, prompt_seq=prompt_seq // r, sample_seq=sample_seq // r,
            name=f"attn_dilated_{r}")
        obs.append(o_g)
        lses.append(lse_g)

    n_r = N_GROUPS + N_EXPERTS
    wr = jnp.concatenate([w_rg, jnp.transpose(w_re, (1, 0, 2)).reshape(d, N_EXPERTS)], axis=1)
    wr = jnp.pad(wr, ((0, 0), (0, LANES - n_r)))
    wr_hi = wr.astype(_BF)
    wr_lo = (wr - wr_hi.astype(_F32)).astype(_BF)
    wr2 = jnp.concatenate([wr_hi, wr_lo], axis=1)
    br = jnp.pad(jnp.concatenate([b_rg, b_re.reshape(-1)]), (0, LANES - n_r)).reshape(1, LANES).astype(_F32)

    h, hn, ri, rw, cnt = _mix(xp, xs, oa, obs, lses, gates, w_proj_a.astype(_BF), w_proj_b.astype(_BF),
                              w_out.astype(_BF), norm2.reshape(1, d), wr2, br)
    counts = cnt[0, :N_EXPERTS]
    eid = ri[:, 0:2]
    rank = ri[:, 2:4]
    pcounts = (counts + EXPERT_ROWS - 1) // EXPERT_ROWS * EXPERT_ROWS
    pends = jnp.cumsum(pcounts)
    pstarts = pends - pcounts
    expert_ids = jnp.arange(N_EXPERTS, dtype=jnp.int32)
    start_of = jnp.sum(jnp.where(eid[:, :, None] == expert_ids, pstarts.astype(jnp.int32), 0), axis=-1)
    dest = (start_of + rank).reshape(-1).astype(jnp.int32)
    nblk = (2 * t + N_EXPERTS * (EXPERT_ROWS - 1) + EXPERT_ROWS - 1) // EXPERT_ROWS
    nvalid = (pends[-1] // EXPERT_ROWS).astype(jnp.int32)
    blk_start = jnp.minimum(jnp.arange(nblk, dtype=jnp.int32), nvalid - 1) * EXPERT_ROWS
    blk_e = jnp.sum(pends[None, :] <= blk_start[:, None], axis=1).astype(jnp.int32)

    pad_start = (pstarts + counts) // SUBLANES * SUBLANES
    xr = _dispatch(dest, pad_start.astype(jnp.int32), ((pends - pad_start) // SUBLANES).astype(jnp.int32),
                   nvalid.reshape(1), hn, t, nblk * EXPERT_ROWS)
    blk_ids = jnp.arange(nblk, dtype=jnp.int32)
    first = ((blk_ids == 0) | (blk_e != jnp.roll(blk_e, 1))).astype(jnp.int32)
    slot = ((jnp.cumsum(first) - 1) % 2).astype(jnp.int32)
    later = (expert_ids[None, :] > expert_ids[:, None]) & (counts[None, :] > 0)
    next_of = jnp.min(jnp.where(later, expert_ids[None, :], N_EXPERTS), axis=1)
    next_e = jnp.sum(jnp.where(blk_e[:, None] == expert_ids, next_of, 0), axis=1)
    next_e = jnp.where(next_e < N_EXPERTS, next_e, -1).astype(jnp.int32)
    yr = _experts(blk_e, nvalid.reshape(1), first, slot, next_e, xr, w1, w3, w2)
    return _combine(dest, h, rw, norm_final.reshape(1, d), yr, n_p=n_p)


def kernel(x_prompt, x_sample, norm1, w_in, attn_sink, w_proj_a, w_proj_b, w_gate, b_gate, w_out, norm2,
           w_router_group, b_router_group, w_router_expert, b_router_expert, w_expert_gate, w_expert_up,
           w_expert_down, norm_final):
    assert norm1.shape[0] == 1, "one layer"
    d = x_prompt.shape[-1]
    xp = x_prompt.reshape(-1, d)
    xs = x_sample.reshape(-1, d)
    yp, ys = _layer(xp, xs, x_prompt.shape[1], x_sample.shape[1], norm1[0], w_in[0], attn_sink[0], w_proj_a[0],
                    w_proj_b[0], w_gate[0], b_gate[0], w_out[0], norm2[0], w_router_group[0],
                    b_router_group[0], w_router_expert[0], b_router_expert[0], w_expert_gate[0],
                    w_expert_up[0], w_expert_down[0], norm_final)
    return yp.reshape(x_prompt.shape), ys.reshape(x_sample.shape)
```

```python
import functools
import math

import jax
import jax.numpy as jnp
import numpy as np
from jax import lax
from jax.experimental import pallas as pl
from jax.experimental.pallas import tpu as pltpu

HEAD_DIM = 128
A_HEADS = 16
A_KV = 4
A_HALF = 128
DILATED_GROUPS = ((128, 1), (512, 4), (2048, 16))
N_DIL = len(DILATED_GROUPS)
B_HPG = 4
B_HEADS = N_DIL * B_HPG
N_GROUPS = 8
EXPERTS_PER_GROUP = 8
N_EXPERTS = N_GROUPS * EXPERTS_PER_GROUP
EPS = 1e-6

LANES = 128
SUBLANES = 8
MASKED = -1e30
LOG2_E = math.log2(math.e)
LN_2 = math.log(2.0)
VMEM_LIMIT = 56 * 1024 * 1024
SUBQ = 128
HEADS_PER_STEP = 4
EXPERT_ROWS = 256
DMA_UNROLL = 8
PROJ_TM, PROJ_TN = 2048, 512
MAX_ROW_STRIDE = 4
ROW_TILE = 256
DISPATCH_TILE = 512
ATTN_ROWS_PER_STEP = 1024

_BF = jnp.bfloat16
_F32 = jnp.float32


def _alibi(n):
    return np.power(2.0, -8.0 * (np.arange(n) + 1) / n).astype(np.float32)


def _params(*sem):
    return pltpu.CompilerParams(dimension_semantics=sem, vmem_limit_bytes=VMEM_LIMIT)


def _pack_halves(x):
    n = x.shape[1] // 2

    def bf16_bits(v):
        return lax.bitcast_convert_type(v.astype(_BF).astype(_F32), jnp.uint32)

    return bf16_bits(x[:, n:]) | (bf16_bits(x[:, :n]) >> 16)


def _unpack_halves(p):
    lo = lax.bitcast_convert_type(p << 16, _F32)
    hi = lax.bitcast_convert_type(p & jnp.uint32(0xFFFF0000), _F32)
    return lo, hi


def _store_row_tiles(ref, packed):
    rows, words = packed.shape
    s_count = words // LANES
    for s in range(s_count):
        ref[pl.ds(s, rows, stride=s_count), :] = packed[:, s * LANES:(s + 1) * LANES]


def _load_row_tiles(ref, rows):
    s_count = ref.shape[0] // rows
    return jnp.concatenate([ref[pl.ds(s, rows, stride=s_count), :] for s in range(s_count)], axis=1)


def _row_tile(ref, row, s_count):
    return ref.at[pl.ds(pl.multiple_of(row * s_count, s_count), s_count), :]


def _proj_kernel(g_ref, b_ref, xp_hbm, xs_hbm, win_hbm, wg_hbm, oa_hbm, q0_hbm, q1_hbm, q2_hbm, gt_hbm,
                 xbuf, xn_ref, wbuf, acc_ref, obuf, x_sem, w_sems, o_sems,
                 *, n_prompt_tiles, n_row_tiles, tm, tn, na, ng):
    i = pl.program_id(0)
    nb = 3
    tiles_per_row = na + N_DIL * nb + ng
    rows = pl.ds(pl.multiple_of(i * tm, tm), tm)

    def cols(jj):
        return pl.ds(pl.multiple_of(jj * tn, tn), tn)

    def x_copy(x_hbm, row_tile):
        return pltpu.make_async_copy(x_hbm.at[pl.ds(pl.multiple_of(row_tile * tm, tm), tm), :], xbuf, x_sem)

    def start_x(row_tile):
        @pl.when(row_tile < n_prompt_tiles)
        def _():
            x_copy(xp_hbm, row_tile).start()

        @pl.when(row_tile >= n_prompt_tiles)
        def _():
            x_copy(xs_hbm, row_tile - n_prompt_tiles).start()

    def w_copy(w_hbm, col, slot):
        return pltpu.make_async_copy(w_hbm.at[:, cols(col)], wbuf.at[slot], w_sems.at[slot])

    def wait_staging(slot):
        pltpu.make_async_copy(obuf.at[slot], oa_hbm.at[pl.ds(0, tm), pl.ds(0, tn)], o_sems.at[slot]).wait()

    def store_plain(out_hbm):
        def store(acc, jj, slot):
            obuf[slot] = acc.astype(_BF)
            pltpu.make_async_copy(obuf.at[slot], out_hbm.at[rows, cols(jj)], o_sems.at[slot]).start()
        return store

    def store_gate(acc, jj, slot):
        z = acc + b_ref[jj]
        obuf[slot] = (0.5 * jnp.tanh(0.5 * z) + 0.5).astype(_BF)
        pltpu.make_async_copy(obuf.at[slot], gt_hbm.at[rows, cols(jj)], o_sems.at[slot]).start()

    def store_phases(q_hbm, r):
        n = tm // r

        def store(acc, jj, slot):
            if r == 1:
                obuf[slot] = acc.astype(_BF)
            else:
                for cb in range(tn // LANES):
                    acc_ref[0, cb] = acc[:, cb * LANES:(cb + 1) * LANES]
                src, stride = 0, r
                if r > MAX_ROW_STRIDE:
                    stride = r // MAX_ROW_STRIDE
                    m = tm // MAX_ROW_STRIDE
                    for c in range(MAX_ROW_STRIDE):
                        for cb in range(tn // LANES):
                            acc_ref[1, cb, c * m:(c + 1) * m, :] = acc_ref[0, cb, pl.ds(c, m, stride=MAX_ROW_STRIDE), :]
                    src = 1
                for p in range(r):
                    start = p if src == 0 else (p % MAX_ROW_STRIDE) * (tm // MAX_ROW_STRIDE) + p // MAX_ROW_STRIDE
                    for cb in range(tn // LANES):
                        obuf[slot, p * n:(p + 1) * n, cb * LANES:(cb + 1) * LANES] = (
                            acc_ref[src, cb, pl.ds(start, n, stride=stride), :].astype(_BF))
            for p in range(r):
                pltpu.make_async_copy(obuf.at[slot, pl.ds(p * n, n), :],
                                      q_hbm.at[p, pl.ds(pl.multiple_of(i * n, n), n), cols(jj)],
                                      o_sems.at[slot]).start()
        return store

    parts = [(na, win_hbm, lambda jj: jj, store_plain(oa_hbm))]
    for gi, (q_hbm, (_, r)) in enumerate(zip((q0_hbm, q1_hbm, q2_hbm), DILATED_GROUPS)):
        parts.append((nb, win_hbm, lambda jj, gi=gi: na + jj * N_DIL + gi, store_phases(q_hbm, r)))
    parts.append((ng, wg_hbm, lambda jj: jj, store_gate))

    @pl.when(i == 0)
    def _():
        start_x(0)
        w_copy(win_hbm, 0, 0).start()

    x_copy(xp_hbm, 0).wait()
    x = xbuf[...]
    y = x * lax.rsqrt(jnp.mean(x * x, axis=-1, keepdims=True) + EPS)
    xn_ref[...] = (y * g_ref[...]).astype(_BF)

    @pl.when(i + 1 < n_row_tiles)
    def _():
        start_x(i + 1)

    base = 0
    for k, (length, w_hbm, w_col, store) in enumerate(parts):
        def tile(jj, carry, base=base, length=length, w_hbm=w_hbm, w_col=w_col, store=store, k=k):
            n = i * tiles_per_row + base + jj
            slot = n % 2

            @pl.when(jj + 1 < length)
            def _():
                w_copy(w_hbm, w_col(jj + 1), 1 - slot).start()

            @pl.when(jj + 1 == length)
            def _():
                if k + 1 < len(parts):
                    w_copy(parts[k + 1][1], parts[k + 1][2](0), 1 - slot).start()
                else:
                    @pl.when(i + 1 < n_row_tiles)
                    def _():
                        w_copy(win_hbm, 0, 1 - slot).start()

            @pl.when(n >= 2)
            def _():
                wait_staging(slot)

            w_copy(w_hbm, w_col(jj), slot).wait()
            acc = jnp.dot(xn_ref[...], wbuf[slot].astype(_BF), preferred_element_type=_F32)
            store(acc, jj, slot)
            return carry

        lax.fori_loop(0, length, tile, 0)
        base += length

    @pl.when(i == n_row_tiles - 1)
    def _():
        wait_staging(0)
        wait_staging(1)


def _project(xp, xs, g, w_in, w_gate, b_gate, *, a_cols):
    n_p, d = xp.shape
    n_s = xs.shape[0]
    t = n_p + n_s
    tm, tn = PROJ_TM, PROJ_TN
    assert tn == B_HPG * HEAD_DIM and w_in.shape[1] == a_cols + 3 * N_DIL * tn
    assert n_p % tm == 0 and n_s % tm == 0
    gate_cols = w_gate.shape[1]
    na, ng = a_cols // tn, gate_cols // tn
    any_space = pl.BlockSpec(memory_space=pl.ANY)
    return pl.pallas_call(
        functools.partial(_proj_kernel, n_prompt_tiles=n_p // tm, n_row_tiles=t // tm, tm=tm, tn=tn, na=na, ng=ng),
        grid=(t // tm,),
        in_specs=[pl.BlockSpec((1, d), lambda i: (0, 0)), pl.BlockSpec((ng, 1, tn), lambda i: (0, 0, 0)),
                  any_space, any_space, any_space, any_space],
        out_specs=[any_space] * 5,
        out_shape=[jax.ShapeDtypeStruct((t, a_cols), _BF)]
                  + [jax.ShapeDtypeStruct((r, t // r, 3 * tn), _BF) for _, r in DILATED_GROUPS]
                  + [jax.ShapeDtypeStruct((t, gate_cols), _BF)],
        scratch_shapes=[pltpu.VMEM((tm, d), xp.dtype), pltpu.VMEM((tm, d), _BF), pltpu.VMEM((2, d, tn), w_in.dtype),
                        pltpu.VMEM((2, tn // LANES, tm, LANES), _F32), pltpu.VMEM((2, tm, tn), _BF),
                        pltpu.SemaphoreType.DMA(()), pltpu.SemaphoreType.DMA((2,)), pltpu.SemaphoreType.DMA((2,))],
        compiler_params=_params("arbitrary"),
        name="norm_proj",
    )(g, b_gate.reshape(ng, 1, tn), xp, xs, w_in, w_gate)


def _band_attn_kernel(slope_ref, sink_ref, q_ref, kl_ref, km_ref, kr_ref, vl_ref, vm_ref, vr_ref, *rest,
                      phases, has_sink, **static):
    if has_sink:
        o_ref, kcat, vcat = rest
    else:
        o_ref, lse_ref, kcat, vcat = rest
    for ph in range(phases):
        _band_attn_phase(slope_ref, sink_ref, *(r.at[ph] for r in (q_ref, kl_ref, km_ref, kr_ref, vl_ref, vm_ref,
                                                                  vr_ref)),
                         o_ref if has_sink else o_ref.at[ph], None if has_sink else lse_ref.at[ph],
                         kcat.at[ph], vcat.at[ph], has_sink=has_sink, **static)


def _band_attn_phase(slope_ref, sink_ref, q_ref, kl_ref, km_ref, kr_ref, vl_ref, vm_ref, vr_ref,
                     o_ref, lse_ref, kcat, vcat, *, bq, half, step, shared_kv, has_sink, prompt_rows,
                     prompt_seq, sample_seq):
    i = pl.program_id(0)
    c = pl.program_id(1)
    win = SUBQ + 2 * half

    kcat[0:half, :] = kl_ref[...]
    kcat[half:half + bq, :] = km_ref[...]
    kcat[half + bq:, :] = kr_ref[...]
    vcat[0:half, :] = vl_ref[...]
    vcat[half:half + bq, :] = vm_ref[...]
    vcat[half + bq:, :] = vr_ref[...]

    u0 = i * bq
    in_prompt = u0 < prompt_rows
    lo = jnp.where(in_prompt, (u0 // prompt_seq) * prompt_seq,
                   prompt_rows + ((u0 - prompt_rows) // sample_seq) * sample_seq)
    hi = lo + jnp.where(in_prompt, prompt_seq, sample_seq)

    qi = lax.broadcasted_iota(jnp.int32, (SUBQ, win), 0)
    kj = lax.broadcasted_iota(jnp.int32, (SUBQ, win), 1)
    rel = kj - half - qi
    absrel = jnp.abs(rel)
    band_bias = jnp.where(absrel <= half, -(absrel * step).astype(_F32), MASKED)
    lane = lax.broadcasted_iota(jnp.int32, (SUBQ, LANES), 1)

    heads = range(HEADS_PER_STEP)
    head_ids = [c * HEADS_PER_STEP + h if shared_kv else h for h in heads]
    scale2 = HEAD_DIM ** -0.5 * LOG2_E
    head_bias = [(slope_ref[hid] * LOG2_E) * band_bias for hid in head_ids]
    n_sub = bq // SUBQ
    for sb in range(n_sub):
        kpos = kj + (u0 + sb * SUBQ - half)
        in_seq = None
        if sb == 0:
            in_seq = kpos >= lo
        if sb == n_sub - 1:
            in_seq = kpos < hi if in_seq is None else in_seq & (kpos < hi)

        def scores2(s, h):
            s = s * scale2 + head_bias[h]
            return s if in_seq is None else jnp.where(in_seq, s, MASKED)

        rows = slice(sb * SUBQ, (sb + 1) * SUBQ)
        if shared_kv:
            k = kcat[sb * SUBQ:sb * SUBQ + win, :]
            v = vcat[sb * SUBQ:sb * SUBQ + win, :]
            q4 = jnp.concatenate([q_ref[rows, h * HEAD_DIM:(h + 1) * HEAD_DIM] for h in heads], axis=0)
            s4 = lax.dot_general(q4, k, (((1,), (1,)), ((), ())), preferred_element_type=_F32)
            ps, ms, ls = [], [], []
            for h in heads:
                s = scores2(s4[h * SUBQ:(h + 1) * SUBQ], h)
                m = jnp.max(s, axis=-1, keepdims=True)
                p = jnp.exp2(s - m)
                ms.append(m)
                ls.append(jnp.sum(p, axis=-1, keepdims=True))
                ps.append(p.astype(_BF))
            pv4 = jnp.dot(jnp.concatenate(ps, axis=0), v, preferred_element_type=_F32)
            for h in heads:
                l = ls[h] + jnp.exp2(sink_ref[head_ids[h]] * LOG2_E - ms[h])
                o_ref[rows, h * HEAD_DIM:(h + 1) * HEAD_DIM] = (
                    pv4[h * SUBQ:(h + 1) * SUBQ] / l).astype(o_ref.dtype)
            continue
        lse_tile = jnp.zeros((SUBQ, LANES), _F32)
        for h in heads:
            cols = slice(h * HEAD_DIM, (h + 1) * HEAD_DIM)
            k = kcat[sb * SUBQ:sb * SUBQ + win, cols]
            v = vcat[sb * SUBQ:sb * SUBQ + win, cols]
            s = lax.dot_general(q_ref[rows, cols], k, (((1,), (1,)), ((), ())), preferred_element_type=_F32)
            s = scores2(s, h)
            m = jnp.max(s, axis=-1, keepdims=True)
            p = jnp.exp2(s - m)
            l = jnp.sum(p, axis=-1, keepdims=True)
            pv = jnp.dot(p.astype(_BF), v, preferred_element_type=_F32)
            lse_tile = jnp.where(lane == h, m * LN_2 + jnp.log(l), lse_tile)
            o_ref[rows, cols] = (pv / l).astype(o_ref.dtype)
        lse_ref[rows, :] = lse_tile


def _band_attention(qkv, slopes, sinks, *, n_inner, bq, half, step, shared_kv, has_sink,
                    prompt_rows, prompt_seq, sample_seq, name):
    lead, rows, _ = qkv.shape
    assert prompt_seq % bq == 0 and sample_seq % bq == 0 and bq % SUBQ == 0 and bq % half == 0
    assert half <= SUBQ
    nq = rows // bq
    hb = bq // half
    last_halo = rows // half - 1
    width = HEADS_PER_STEP * HEAD_DIM
    if shared_kv:
        kv_w = HEAD_DIM
        k_base, v_base = width * n_inner // HEAD_DIM, width * n_inner // HEAD_DIM + n_inner
        pp = 1
        lead_of = lambda c: 0
        q_col = lambda c: c
        kv_col = lambda base: (lambda c: base + c)
    else:
        kv_w = width
        k_base, v_base = 1, 2
        pp = min(n_inner, max(1, ATTN_ROWS_PER_STEP // bq))
        n_inner //= pp
        lead_of = lambda c: c
        q_col = lambda c: 0
        kv_col = lambda base: (lambda c: base)

    def main_map(col):
        return lambda i, c: (lead_of(c), i, col(c))

    def left_map(col):
        return lambda i, c: (lead_of(c), jnp.maximum(i * hb - 1, 0), col(c))

    def right_map(col):
        return lambda i, c: (lead_of(c), jnp.minimum((i + 1) * hb, last_halo), col(c))

    smem = pl.BlockSpec(memory_space=pltpu.SMEM)
    in_specs = [
        smem, smem,
        pl.BlockSpec((pp, bq, width), main_map(q_col)),
        pl.BlockSpec((pp, half, kv_w), left_map(kv_col(k_base))),
        pl.BlockSpec((pp, bq, kv_w), main_map(kv_col(k_base))),
        pl.BlockSpec((pp, half, kv_w), right_map(kv_col(k_base))),
        pl.BlockSpec((pp, half, kv_w), left_map(kv_col(v_base))),
        pl.BlockSpec((pp, bq, kv_w), main_map(kv_col(v_base))),
        pl.BlockSpec((pp, half, kv_w), right_map(kv_col(v_base))),
    ]
    if has_sink:
        out_specs = pl.BlockSpec((bq, width), lambda i, c: (i, c))
        out_shape = jax.ShapeDtypeStruct((rows, n_inner * width), _BF)
    else:
        out_specs = (pl.BlockSpec((pp, bq, width), lambda i, c: (c, i, 0)),
                     pl.BlockSpec((pp, bq, LANES), lambda i, c: (c, i, 0)))
        out_shape = (jax.ShapeDtypeStruct((lead, rows, width), _BF),
                     jax.ShapeDtypeStruct((lead, rows, LANES), _F32))
    return pl.pallas_call(
        functools.partial(_band_attn_kernel, phases=pp, bq=bq, half=half, step=step, shared_kv=shared_kv,
                          has_sink=has_sink, prompt_rows=prompt_rows, prompt_seq=prompt_seq,
                          sample_seq=sample_seq),
        grid=(nq, n_inner),
        in_specs=in_specs,
        out_specs=out_specs,
        out_shape=out_shape,
        scratch_shapes=[pltpu.VMEM((pp, bq + 2 * half, kv_w), _BF), pltpu.VMEM((pp, bq + 2 * half, kv_w), _BF)],
        compiler_params=_params("parallel", "arbitrary"),
        name=name,
    )(slopes, sinks, qkv, qkv, qkv, qkv, qkv, qkv, qkv)


def _mix_kernel(xp_ref, xs_ref, oa_ref, ob0_ref, ob1_ref, ob2_ref, l0_ref, l1_ref, l2_ref, g_ref,
                wpa_ref, wpb_ref, wout_ref, n2_ref, wr_ref, br_ref, h_ref, hn_ref, ri_ref, rw_ref, cnt_ref,
                o_scr, l_scr, carry_ref, lg_scr, mg_scr, *, n_tiles, n_prompt_tiles, d, tm):
    i = pl.program_id(0)

    @pl.when(i == 0)
    def _():
        carry_ref[...] = jnp.zeros_like(carry_ref)
        lg_scr[...] = jnp.zeros_like(lg_scr)
        mg_scr[...] = jnp.zeros_like(mg_scr)

    _route_tile(lg_scr[...], jnp.where(i > 1, 1.0, 0.0), ri_ref, rw_ref, cnt_ref, carry_ref)

    tile2 = jnp.clip(i - 1, 0, n_tiles - 1)
    x = jnp.where(tile2 < n_prompt_tiles, xp_ref[...], xs_ref[...])
    h_new = x + jnp.dot(mg_scr[(i + 1) % 2], wout_ref[...], preferred_element_type=_F32)
    h_ref[...] = h_new
    hn = h_new * lax.rsqrt(jnp.mean(h_new * h_new, axis=-1, keepdims=True) + EPS) * n2_ref[...]
    _store_row_tiles(hn_ref, _pack_halves(hn))
    hn_hi = hn.astype(_BF)
    hn_lo = (hn - hn_hi.astype(_F32)).astype(_BF)
    r = (jnp.dot(hn_hi, wr_ref[...], preferred_element_type=_F32)
         + jnp.dot(hn_lo, wr_ref[...], preferred_element_type=_F32))
    lg_scr[...] = r[:, :LANES] + r[:, LANES:] + br_ref[...]


    def token_order(o_ref, l_ref, r, slot):
        if r == 1:
            return [o_ref[0, :, h * HEAD_DIM:(h + 1) * HEAD_DIM].astype(_F32) for h in range(B_HPG)], l_ref[0]
        for p in range(r):
            for h in range(B_HPG):
                o_scr[slot, h, pl.ds(p, tm // r, stride=r), :] = (
                    o_ref[p, :, h * HEAD_DIM:(h + 1) * HEAD_DIM].astype(_F32))
            l_scr[slot, pl.ds(p, tm // r, stride=r), :] = l_ref[p]
        return [o_scr[slot, h] for h in range(B_HPG)], l_scr[slot]

    o0, l0 = token_order(ob0_ref, l0_ref, DILATED_GROUPS[0][1], 0)
    o1, l1 = token_order(ob1_ref, l1_ref, DILATED_GROUPS[1][1], 0)
    o2, l2 = token_order(ob2_ref, l2_ref, DILATED_GROUPS[2][1], 1)
    mx = jnp.maximum(jnp.maximum(l0, l1), l2)
    e0, e1, e2 = jnp.exp(l0 - mx), jnp.exp(l1 - mx), jnp.exp(l2 - mx)
    den = e0 + e1 + e2
    a0, a1, a2 = e0 / den, e1 / den, e2 / den
    parts = []
    for h in range(B_HPG):
        parts.append(a0[:, h:h + 1] * o0[h] + a1[:, h:h + 1] * o1[h] + a2[:, h:h + 1] * o2[h])
    ob = jnp.concatenate(parts, axis=1).astype(_BF)

    ta = jnp.dot(oa_ref[...], wpa_ref[...], preferred_element_type=_F32)
    tb = jnp.dot(ob, wpb_ref[...], preferred_element_type=_F32)
    merged = g_ref[:, :d].astype(_F32) * ta + g_ref[:, d:].astype(_F32) * tb
    mg_scr[i % 2] = merged.astype(_BF)


def _mix(xp, xs, oa, obs, lses, gates, wpa, wpb, wout, n2, wr, br):
    n_p, d = xp.shape
    n_s = xs.shape[0]
    t = n_p + n_s
    tm = ROW_TILE
    npt = n_p // tm
    n_tiles = t // tm
    def stage_tile(lag):
        return lambda i: jnp.clip(i - lag, 0, n_tiles - 1)

    row = lambda i: (stage_tile(0)(i), 0)
    row2 = lambda i: (stage_tile(1)(i), 0)
    routed = lambda i: (stage_tile(2)(i), 0)
    const = lambda i: (0, 0)
    width = B_HPG * HEAD_DIM

    def resident(shape):
        return pl.BlockSpec(shape, const, pipeline_mode=pl.Buffered(1))

    def phase_blocks(cols):
        return [pl.BlockSpec((r, tm // r, cols), lambda i: (0, stage_tile(0)(i), 0)) for _, r in DILATED_GROUPS]

    in_specs = [
        pl.BlockSpec((tm, d), lambda i: (jnp.minimum(stage_tile(1)(i), npt - 1), 0)),
        pl.BlockSpec((tm, d), lambda i: (jnp.maximum(stage_tile(1)(i) - npt, 0), 0)),
        pl.BlockSpec((tm, oa.shape[1]), row),
        *phase_blocks(width), *phase_blocks(LANES),
        pl.BlockSpec((tm, 2 * d), row),
        resident(wpa.shape), resident(wpb.shape), resident(wout.shape),
        resident((1, d)), resident(wr.shape), resident((1, LANES)),
    ]
    return pl.pallas_call(
        functools.partial(_mix_kernel, n_tiles=n_tiles, n_prompt_tiles=npt, d=d, tm=tm),
        grid=(n_tiles + 2,),
        in_specs=in_specs,
        out_specs=(pl.BlockSpec((tm, d), row2), pl.BlockSpec((tm * (d // 2 // LANES), LANES), row2),
                   pl.BlockSpec((tm, LANES), routed), pl.BlockSpec((tm, LANES), routed),
                   pl.BlockSpec((SUBLANES, LANES), const)),
        out_shape=(jax.ShapeDtypeStruct((t, d), _F32),
                   jax.ShapeDtypeStruct((t * (d // 2 // LANES), LANES), jnp.uint32),
                   jax.ShapeDtypeStruct((t, LANES), jnp.int32),
                   jax.ShapeDtypeStruct((t, LANES), _F32),
                   jax.ShapeDtypeStruct((SUBLANES, LANES), jnp.int32)),
        scratch_shapes=[pltpu.VMEM((2, B_HPG, tm, HEAD_DIM), _F32), pltpu.VMEM((2, tm, LANES), _F32),
                        pltpu.VMEM((SUBLANES, LANES), _F32), pltpu.VMEM((tm, LANES), _F32),
                        pltpu.VMEM((2, tm, d), _BF)],
        compiler_params=_params("arbitrary"),
        name="mix_out_router",
    )(xp, xs, oa, *obs, *lses, gates, wpa, wpb, wout, n2, wr, br)


def _route_tile(lg, live, ri_ref, rw_ref, cnt_ref, carry_ref):
    tr = lg.shape[0]
    lane_i = lax.broadcasted_iota(jnp.int32, (tr, LANES), 1)
    lane = lane_i.astype(_F32)
    no_lane = float(LANES)
    is_grp = lane_i < N_GROUPS
    glog = jnp.where(is_grp, lg, MASKED)
    gmax = jnp.max(glog, axis=-1, keepdims=True)
    grp = jnp.min(jnp.where(glog == gmax, lane, no_lane), axis=-1, keepdims=True)
    gsum = jnp.sum(jnp.where(is_grp, jnp.exp(glog - gmax), 0.0), axis=-1, keepdims=True)
    pgrp = 1.0 / gsum
    lane_grp = ((lane_i - N_GROUPS) // EXPERTS_PER_GROUP).astype(_F32)
    in_grp = (lane_i >= N_GROUPS) & (lane_i < N_GROUPS + N_EXPERTS) & (lane_grp == grp)
    elog = jnp.where(in_grp, lg, MASKED)
    t1 = jnp.max(elog, axis=-1, keepdims=True)
    i1 = jnp.min(jnp.where(elog == t1, lane, no_lane), axis=-1, keepdims=True)
    elog2 = jnp.where(lane == i1, MASKED, elog)
    t2 = jnp.max(elog2, axis=-1, keepdims=True)
    i2 = jnp.min(jnp.where(elog2 == t2, lane, no_lane), axis=-1, keepdims=True)
    e21 = jnp.exp(t2 - t1)
    w1 = pgrp / (1.0 + e21)
    w2 = pgrp * e21 / (1.0 + e21)
    eid1 = i1 - N_GROUPS
    eid2 = i2 - N_GROUPS
    hot1 = lane == eid1
    hot2 = lane == eid2
    onehot = (jnp.where(hot1, 1.0, 0.0) + jnp.where(hot2, 1.0, 0.0)) * live
    r_i = lax.broadcasted_iota(jnp.int32, (tr, tr), 0)
    c_i = lax.broadcasted_iota(jnp.int32, (tr, tr), 1)
    lower = jnp.where(c_i < r_i, 1.0, 0.0).astype(_BF)
    before = jnp.dot(lower, onehot.astype(_BF), preferred_element_type=_F32) + carry_ref[0:1, :]
    rank1 = jnp.sum(jnp.where(hot1, before, 0.0), axis=-1, keepdims=True)
    rank2 = jnp.sum(jnp.where(hot2, before, 0.0), axis=-1, keepdims=True)
    total = carry_ref[0:1, :] + jnp.sum(onehot, axis=0, keepdims=True)
    carry_ref[...] = jnp.broadcast_to(total, carry_ref.shape)
    cnt_ref[...] = jnp.broadcast_to(total, cnt_ref.shape).astype(jnp.int32)
    ri = jnp.where(lane_i == 0, eid1, jnp.where(lane_i == 1, eid2, jnp.where(lane_i == 2, rank1,
                   jnp.where(lane_i == 3, rank2, 0.0))))
    ri_ref[...] = ri.astype(jnp.int32)
    rw_ref[...] = jnp.where(lane_i == 0, w1, jnp.where(lane_i == 1, w2, 0.0))


def _dispatch_kernel(dest_ref, pad_start_ref, pad_n_ref, nvalid_ref, hn_ref, xr_ref, zeros, sem, pad_sem,
                     *, td, sc):
    i = pl.program_id(0)

    def rows_of(ref, row0, n):
        start = row0 * sc if isinstance(row0, int) else pl.multiple_of(row0 * sc, n * sc)
        return ref.at[pl.ds(start, n * sc), :]

    @pl.when(i == 0)
    def _():
        zeros[...] = jnp.zeros_like(zeros)

        def pad_chunk(e, r):
            row0 = pl.multiple_of(pad_start_ref[e] + r * SUBLANES, SUBLANES)
            return pltpu.make_async_copy(rows_of(zeros, 0, SUBLANES), rows_of(xr_ref, row0, SUBLANES), pad_sem)

        def per_expert(act):
            def body(e, carry):
                lax.fori_loop(0, pad_n_ref[e], lambda r, c: (act(pad_chunk(e, r)), c)[1], 0)
                return carry
            return body

        def tail_block(b):
            return pltpu.make_async_copy(zeros, rows_of(xr_ref, b * EXPERT_ROWS, EXPERT_ROWS), pad_sem)

        n_blocks = xr_ref.shape[0] // (EXPERT_ROWS * sc)
        lax.fori_loop(0, N_EXPERTS, per_expert(lambda cp: cp.start()), 0)
        lax.fori_loop(nvalid_ref[0], n_blocks, lambda b, c: (tail_block(b).start(), c)[1], 0)
        lax.fori_loop(0, N_EXPERTS, per_expert(lambda cp: cp.wait()), 0)
        lax.fori_loop(nvalid_ref[0], n_blocks, lambda b, c: (tail_block(b).wait(), c)[1], 0)

    def start(j, carry):
        for k in range(2):
            dst = dest_ref[2 * (i * td + j) + k]
            pltpu.make_async_copy(rows_of(hn_ref, j, 1), rows_of(xr_ref, dst, 1), sem).start()
        return carry

    lax.fori_loop(0, td, start, 0, unroll=DMA_UNROLL)
    for _ in range(2):
        pltpu.make_async_copy(hn_ref, rows_of(xr_ref, 0, td), sem).wait()


def _dispatch(dest, pad_start, pad_n, nvalid, hn, t, rows):
    sc = hn.shape[0] // t
    td = DISPATCH_TILE
    return pl.pallas_call(
        functools.partial(_dispatch_kernel, td=td, sc=sc),
        grid_spec=pltpu.PrefetchScalarGridSpec(
            num_scalar_prefetch=4,
            grid=(t // td,),
            in_specs=[pl.BlockSpec((td * sc, LANES), lambda i, *_: (i, 0))],
            out_specs=pl.BlockSpec(memory_space=pl.ANY),
            scratch_shapes=[pltpu.VMEM((EXPERT_ROWS * sc, LANES), hn.dtype), pltpu.SemaphoreType.DMA(()),
                            pltpu.SemaphoreType.DMA(())],
        ),
        out_shape=jax.ShapeDtypeStruct((rows * sc, LANES), hn.dtype),
        compiler_params=_params("arbitrary"),
        name="dispatch",
    )(dest, pad_start, pad_n, nvalid, hn)


def _expert_kernel(blk_e_ref, nvalid_ref, first_ref, slot_ref, next_e_ref, x_ref, w1_hbm, w3_hbm, w2_hbm, y_ref,
                   w1_buf, w3_buf, w2_buf, sems):
    b = pl.program_id(0)
    valid = b < nvalid_ref[0]

    def fetch(e, slot):
        return (pltpu.make_async_copy(w1_hbm.at[e], w1_buf.at[slot], sems.at[slot, 0]),
                pltpu.make_async_copy(w3_hbm.at[e], w3_buf.at[slot], sems.at[slot, 1]),
                pltpu.make_async_copy(w2_hbm.at[e], w2_buf.at[slot], sems.at[slot, 2]))

    @pl.when(b == 0)
    def _():
        for cp in fetch(blk_e_ref[0], 0):
            cp.start()

    @pl.when(valid & (first_ref[b] == 1))
    def _():
        for cp in fetch(blk_e_ref[b], slot_ref[b]):
            cp.wait()

        @pl.when(next_e_ref[b] >= 0)
        def _():
            for cp in fetch(next_e_ref[b], 1 - slot_ref[b]):
                cp.start()

    @pl.when(valid)
    def _():
        slot = slot_ref[b]
        packed_x = _load_row_tiles(x_ref, EXPERT_ROWS)
        xb = jnp.concatenate([half.astype(_BF) for half in _unpack_halves(packed_x)], axis=1)
        a = jnp.dot(xb, w1_buf[slot].astype(_BF), preferred_element_type=_F32)
        u = jnp.dot(xb, w3_buf[slot].astype(_BF), preferred_element_type=_F32)
        hmid = (a / (1.0 + jnp.exp(-a))) * u
        y = jnp.dot(hmid.astype(_BF), w2_buf[slot].astype(_BF), preferred_element_type=_F32)
        _store_row_tiles(y_ref, _pack_halves(y))

    @pl.when(jnp.logical_not(valid))
    def _():
        y_ref[...] = jnp.zeros_like(y_ref)


def _experts(blk_e, nvalid, first, slot, next_e, xr, w1, w3, w2):
    _, d, f = w1.shape
    sc = d // 2 // LANES
    blk = EXPERT_ROWS * sc
    nblk = xr.shape[0] // blk
    any_space = pl.BlockSpec(memory_space=pl.ANY)
    return pl.pallas_call(
        _expert_kernel,
        grid_spec=pltpu.PrefetchScalarGridSpec(
            num_scalar_prefetch=5,
            grid=(nblk,),
            in_specs=[pl.BlockSpec((blk, LANES), lambda b, be, nv, *_: (jnp.minimum(b, nv[0] - 1), 0)),
                      any_space, any_space, any_space],
            out_specs=pl.BlockSpec((blk, LANES), lambda b, *_: (b, 0)),
            scratch_shapes=[pltpu.VMEM((2, d, f), w1.dtype), pltpu.VMEM((2, d, f), w3.dtype),
                            pltpu.VMEM((2, f, d), w2.dtype), pltpu.SemaphoreType.DMA((2, 3))],
        ),
        out_shape=jax.ShapeDtypeStruct(xr.shape, jnp.uint32),
        compiler_params=_params("arbitrary"),
        name="experts",
    )(blk_e, nvalid, first, slot, next_e, xr, w1, w3, w2)


def _combine_kernel(dest_ref, h_ref, rw_ref, g_ref, yr_ref, op_ref, os_ref, ybuf, sems, *, tc, sc, n_prompt_tiles):
    i = pl.program_id(0)
    slot = i % 2

    def gather(tile, s):
        def start(j, carry):
            for k in range(2):
                src = dest_ref[2 * (tile * tc + j) + k]
                pltpu.make_async_copy(_row_tile(yr_ref, src, sc), _row_tile(ybuf.at[s, k], j, sc),
                                      sems.at[s]).start()
            return carry

        lax.fori_loop(0, tc, start, 0, unroll=DMA_UNROLL)

    @pl.when(i == 0)
    def _():
        gather(0, 0)

    @pl.when(i + 1 < pl.num_programs(0))
    def _():
        gather(i + 1, 1 - slot)

    for k in range(2):
        pltpu.make_async_copy(yr_ref.at[pl.ds(0, tc * sc), :], ybuf.at[slot, k], sems.at[slot]).wait()

    rw = rw_ref[...]
    y0 = _unpack_halves(_load_row_tiles(ybuf.at[slot, 0], tc))
    y1 = _unpack_halves(_load_row_tiles(ybuf.at[slot, 1], tc))
    y = jnp.concatenate([rw[:, 0:1] * a + rw[:, 1:2] * b for a, b in zip(y0, y1)], axis=1)
    z = h_ref[...] + y
    out = z * lax.rsqrt(jnp.mean(z * z, axis=-1, keepdims=True) + EPS) * g_ref[...]

    @pl.when(i < n_prompt_tiles)
    def _():
        op_ref[...] = out

    @pl.when(i >= n_prompt_tiles)
    def _():
        os_ref[...] = out


def _combine(dest, h, rw, g, yr, *, n_p):
    t, d = h.shape
    tc = ROW_TILE
    npt = n_p // tc
    sc = d // 2 // LANES
    return pl.pallas_call(
        functools.partial(_combine_kernel, tc=tc, sc=sc, n_prompt_tiles=npt),
        grid_spec=pltpu.PrefetchScalarGridSpec(
            num_scalar_prefetch=1,
            grid=(t // tc,),
            in_specs=[
                pl.BlockSpec((tc, d), lambda i, dest: (i, 0)),
                pl.BlockSpec((tc, LANES), lambda i, dest: (i, 0)),
                pl.BlockSpec((1, d), lambda i, dest: (0, 0)),
                pl.BlockSpec(memory_space=pl.ANY),
            ],
            out_specs=(pl.BlockSpec((tc, d), lambda i, dest: (jnp.minimum(i, npt - 1), 0)),
                       pl.BlockSpec((tc, d), lambda i, dest: (jnp.maximum(i - npt, 0), 0))),
            scratch_shapes=[pltpu.VMEM((2, 2, tc * sc, LANES), yr.dtype), pltpu.SemaphoreType.DMA((2,))],
        ),
        out_shape=(jax.ShapeDtypeStruct((n_p, d), _F32), jax.ShapeDtypeStruct((t - n_p, d), _F32)),
        compiler_params=_params("arbitrary"),
        name="combine_norm",
    )(dest, h, rw, g, yr)


def _layer(xp, xs, prompt_seq, sample_seq, norm1, w_in, sink, w_proj_a, w_proj_b, w_gate, b_gate, w_out,
           norm2, w_rg, b_rg, w_re, b_re, w1, w3, w2, norm_final):
    n_p, d = xp.shape
    n_s = xs.shape[0]
    t = n_p + n_s
    a_q = A_HEADS * HEAD_DIM
    a_cols = a_q + 2 * A_KV * HEAD_DIM

    proj_a, q0, q1, q2, gates = _project(xp, xs, norm1.reshape(1, d), w_in, w_gate, b_gate.reshape(1, 2 * d),
                                         a_cols=a_cols)

    oa = _band_attention(
        proj_a.reshape(1, t, a_cols), jnp.asarray(_alibi(A_HEADS)), sink.astype(_F32), n_inner=A_KV,
        bq=min(2048, prompt_seq, sample_seq), half=A_HALF, step=1, shared_kv=True, has_sink=True,
        prompt_rows=n_p, prompt_seq=prompt_seq, sample_seq=sample_seq, name="attn_window")

    slopes_b = _alibi(B_HEADS)
    obs, lses = [], []
    for gi, ((w, r), qkv) in enumerate(zip(DILATED_GROUPS, (q0, q1, q2))):
        o_g, lse_g = _band_attention(
            qkv, jnp.asarray(slopes_b[gi * B_HPG:(gi + 1) * B_HPG]), jnp.zeros((B_HPG,), _F32), n_inner=r,
            bq=min(1024, prompt_seq // r, sample_seq // r), half=w // (2 * r), step=r, shared_kv=False,
            has_sink=False, prompt_rows=n_p // r, prompt_seq=prompt_seq // r, sample_seq=sample_seq // r,
            name=f"attn_dilated_{r}")
        obs.append(o_g)
        lses.append(lse_g)

    n_r = N_GROUPS + N_EXPERTS
    wr = jnp.concatenate([w_rg, jnp.transpose(w_re, (1, 0, 2)).reshape(d, N_EXPERTS)], axis=1)
    wr = jnp.pad(wr, ((0, 0), (0, LANES - n_r)))
    wr_hi = wr.astype(_BF)
    wr_lo = (wr - wr_hi.astype(_F32)).astype(_BF)
    wr2 = jnp.concatenate([wr_hi, wr_lo], axis=1)
    br = jnp.pad(jnp.concatenate([b_rg, b_re.reshape(-1)]), (0, LANES - n_r)).reshape(1, LANES).astype(_F32)

    h, hn, ri, rw, cnt = _mix(xp, xs, oa, obs, lses, gates, w_proj_a.astype(_BF), w_proj_b.astype(_BF),
                              w_out.astype(_BF), norm2.reshape(1, d), wr2, br)
    counts = cnt[0, :N_EXPERTS]
    eid = ri[:, 0:2]
    rank = ri[:, 2:4]
    pcounts = (counts + EXPERT_ROWS - 1) // EXPERT_ROWS * EXPERT_ROWS
    pends = jnp.cumsum(pcounts)
    pstarts = pends - pcounts
    expert_ids = jnp.arange(N_EXPERTS, dtype=jnp.int32)
    start_of = jnp.sum(jnp.where(eid[:, :, None] == expert_ids, pstarts.astype(jnp.int32), 0), axis=-1)
    dest = (start_of + rank).reshape(-1).astype(jnp.int32)
    nblk = (2 * t + N_EXPERTS * (EXPERT_ROWS - 1) + EXPERT_ROWS - 1) // EXPERT_ROWS
    nvalid = (pends[-1] // EXPERT_ROWS).astype(jnp.int32)
    blk_start = jnp.minimum(jnp.arange(nblk, dtype=jnp.int32), nvalid - 1) * EXPERT_ROWS
    blk_e = jnp.sum(pends[None, :] <= blk_start[:, None], axis=1).astype(jnp.int32)

    pad_start = (pstarts + counts) // SUBLANES * SUBLANES
    xr = _dispatch(dest, pad_start.astype(jnp.int32), ((pends - pad_start) // SUBLANES).astype(jnp.int32),
                   nvalid.reshape(1), hn, t, nblk * EXPERT_ROWS)
    blk_ids = jnp.arange(nblk, dtype=jnp.int32)
    first = ((blk_ids == 0) | (blk_e != jnp.roll(blk_e, 1))).astype(jnp.int32)
    slot = ((jnp.cumsum(first) - 1) % 2).astype(jnp.int32)
    later = (expert_ids[None, :] > expert_ids[:, None]) & (counts[None, :] > 0)
    next_of = jnp.min(jnp.where(later, expert_ids[None, :], N_EXPERTS), axis=1)
    next_e = jnp.sum(jnp.where(blk_e[:, None] == expert_ids, next_of, 0), axis=1)
    next_e = jnp.where(next_e < N_EXPERTS, next_e, -1).astype(jnp.int32)
    yr = _experts(blk_e, nvalid.reshape(1), first, slot, next_e, xr, w1, w3, w2)
    return _combine(dest, h, rw, norm_final.reshape(1, d), yr, n_p=n_p)


def kernel(x_prompt, x_sample, norm1, w_in, attn_sink, w_proj_a, w_proj_b, w_gate, b_gate, w_out, norm2,
           w_router_group, b_router_group, w_router_expert, b_router_expert, w_expert_gate, w_expert_up,
           w_expert_down, norm_final):
    assert norm1.shape[0] == 1, "one layer"
    d = x_prompt.shape[-1]
    xp = x_prompt.reshape(-1, d)
    xs = x_sample.reshape(-1, d)
    yp, ys = _layer(xp, xs, x_prompt.shape[1], x_sample.shape[1], norm1[0], w_in[0], attn_sink[0], w_proj_a[0],
                    w_proj_b[0], w_gate[0], b_gate[0], w_out[0], norm2[0], w_router_group[0],
                    b_router_group[0], w_router_expert[0], b_router_expert[0], w_expert_gate[0],
                    w_expert_up[0], w_expert_down[0], norm_final)
    return yp.reshape(x_prompt.shape), ys.reshape(x_sample.shape)
```

```python
import functools
import math

import jax
import jax.numpy as jnp
import numpy as np
from jax import lax
from jax.experimental import pallas as pl
from jax.experimental.pallas import tpu as pltpu

HEAD_DIM = 128
A_HEADS = 16
A_KV = 4
A_HALF = 128
DILATED_GROUPS = ((128, 1), (512, 4), (2048, 16))
N_DIL = len(DILATED_GROUPS)
B_HPG = 4
B_HEADS = N_DIL * B_HPG
N_GROUPS = 8
EXPERTS_PER_GROUP = 8
N_EXPERTS = N_GROUPS * EXPERTS_PER_GROUP
EPS = 1e-6

LANES = 128
SUBLANES = 8
MASKED = -1e30
LOG2_E = math.log2(math.e)
LN_2 = math.log(2.0)
VMEM_LIMIT = 56 * 1024 * 1024
SUBQ = 128
HEADS_PER_STEP = 4
EXPERT_ROWS = 256
WEIGHT_SLOTS = 3
DMA_UNROLL = 8
PROJ_TM, PROJ_TN = 2048, 512
MAX_ROW_STRIDE = 4
ROW_TILE = 256
DISPATCH_TILE = 1024
ATTN_ROWS_PER_STEP = 1024

_BF = jnp.bfloat16
_F32 = jnp.float32


def _alibi(n):
    return np.power(2.0, -8.0 * (np.arange(n) + 1) / n).astype(np.float32)


def _params(*sem):
    return pltpu.CompilerParams(dimension_semantics=sem, vmem_limit_bytes=VMEM_LIMIT)


def _pack_halves(x):
    n = x.shape[1] // 2

    def bf16_bits(v):
        return lax.bitcast_convert_type(v.astype(_BF).astype(_F32), jnp.uint32)

    return bf16_bits(x[:, n:]) | (bf16_bits(x[:, :n]) >> 16)


def _unpack_halves(p):
    lo = lax.bitcast_convert_type(p << 16, _F32)
    hi = lax.bitcast_convert_type(p & jnp.uint32(0xFFFF0000), _F32)
    return lo, hi


def _store_row_tiles(ref, packed):
    rows, words = packed.shape
    s_count = words // LANES
    for s in range(s_count):
        ref[pl.ds(s, rows, stride=s_count), :] = packed[:, s * LANES:(s + 1) * LANES]


def _load_row_tiles(ref, rows):
    s_count = ref.shape[0] // rows
    return jnp.concatenate([ref[pl.ds(s, rows, stride=s_count), :] for s in range(s_count)], axis=1)


def _row_tile(ref, row, s_count):
    return ref.at[pl.ds(pl.multiple_of(row * s_count, s_count), s_count), :]


def _proj_kernel(g_ref, b_ref, xp_hbm, xs_hbm, win_hbm, wg_hbm, oa_hbm, q0_hbm, q1_hbm, q2_hbm, gt_hbm,
                 xbuf, xn_ref, wbuf, acc_ref, obuf, x_sem, w_sems, o_sems,
                 *, n_prompt_tiles, n_row_tiles, tm, tn, na, ng):
    i = pl.program_id(0)
    nb = 3
    tiles_per_row = na + N_DIL * nb + ng
    rows = pl.ds(pl.multiple_of(i * tm, tm), tm)

    def cols(jj):
        return pl.ds(pl.multiple_of(jj * tn, tn), tn)

    def x_copy(x_hbm, row_tile):
        return pltpu.make_async_copy(x_hbm.at[pl.ds(pl.multiple_of(row_tile * tm, tm), tm), :], xbuf, x_sem)

    def start_x(row_tile):
        @pl.when(row_tile < n_prompt_tiles)
        def _():
            x_copy(xp_hbm, row_tile).start()

        @pl.when(row_tile >= n_prompt_tiles)
        def _():
            x_copy(xs_hbm, row_tile - n_prompt_tiles).start()

    def w_copy(w_hbm, col, slot):
        return pltpu.make_async_copy(w_hbm.at[:, cols(col)], wbuf.at[slot], w_sems.at[slot])

    def wait_staging(slot):
        pltpu.make_async_copy(obuf.at[slot], oa_hbm.at[pl.ds(0, tm), pl.ds(0, tn)], o_sems.at[slot]).wait()

    def store_plain(out_hbm):
        def store(acc, jj, slot):
            obuf[slot] = acc.astype(_BF)
            pltpu.make_async_copy(obuf.at[slot], out_hbm.at[rows, cols(jj)], o_sems.at[slot]).start()
        return store

    def store_gate(acc, jj, slot):
        z = acc + b_ref[jj]
        obuf[slot] = (0.5 * jnp.tanh(0.5 * z) + 0.5).astype(_BF)
        pltpu.make_async_copy(obuf.at[slot], gt_hbm.at[rows, cols(jj)], o_sems.at[slot]).start()

    def store_phases(q_hbm, r):
        n = tm // r

        def store(acc, jj, slot):
            if r == 1:
                obuf[slot] = acc.astype(_BF)
            else:
                for cb in range(tn // LANES):
                    acc_ref[0, cb] = acc[:, cb * LANES:(cb + 1) * LANES]
                src, stride = 0, r
                if r > MAX_ROW_STRIDE:
                    stride = r // MAX_ROW_STRIDE
                    m = tm // MAX_ROW_STRIDE
                    for c in range(MAX_ROW_STRIDE):
                        for cb in range(tn // LANES):
                            acc_ref[1, cb, c * m:(c + 1) * m, :] = acc_ref[0, cb, pl.ds(c, m, stride=MAX_ROW_STRIDE), :]
                    src = 1
                for p in range(r):
                    start = p if src == 0 else (p % MAX_ROW_STRIDE) * (tm // MAX_ROW_STRIDE) + p // MAX_ROW_STRIDE
                    for cb in range(tn // LANES):
                        obuf[slot, p * n:(p + 1) * n, cb * LANES:(cb + 1) * LANES] = (
                            acc_ref[src, cb, pl.ds(start, n, stride=stride), :].astype(_BF))
            for p in range(r):
                pltpu.make_async_copy(obuf.at[slot, pl.ds(p * n, n), :],
                                      q_hbm.at[p, pl.ds(pl.multiple_of(i * n, n), n), cols(jj)],
                                      o_sems.at[slot]).start()
        return store

    parts = [(na, win_hbm, lambda jj: jj, store_plain(oa_hbm))]
    for gi, (q_hbm, (_, r)) in enumerate(zip((q0_hbm, q1_hbm, q2_hbm), DILATED_GROUPS)):
        parts.append((nb, win_hbm, lambda jj, gi=gi: na + jj * N_DIL + gi, store_phases(q_hbm, r)))
    parts.append((ng, wg_hbm, lambda jj: jj, store_gate))

    @pl.when(i == 0)
    def _():
        start_x(0)
        w_copy(win_hbm, 0, 0).start()

    x_copy(xp_hbm, 0).wait()
    x = xbuf[...]
    y = x * lax.rsqrt(jnp.mean(x * x, axis=-1, keepdims=True) + EPS)
    xn_ref[...] = (y * g_ref[...]).astype(_BF)

    @pl.when(i + 1 < n_row_tiles)
    def _():
        start_x(i + 1)

    base = 0
    for k, (length, w_hbm, w_col, store) in enumerate(parts):
        def tile(jj, carry, base=base, length=length, w_hbm=w_hbm, w_col=w_col, store=store, k=k):
            n = i * tiles_per_row + base + jj
            slot = n % 2

            @pl.when(jj + 1 < length)
            def _():
                w_copy(w_hbm, w_col(jj + 1), 1 - slot).start()

            @pl.when(jj + 1 == length)
            def _():
                if k + 1 < len(parts):
                    w_copy(parts[k + 1][1], parts[k + 1][2](0), 1 - slot).start()
                else:
                    @pl.when(i + 1 < n_row_tiles)
                    def _():
                        w_copy(win_hbm, 0, 1 - slot).start()

            @pl.when(n >= 2)
            def _():
                wait_staging(slot)

            w_copy(w_hbm, w_col(jj), slot).wait()
            acc = jnp.dot(xn_ref[...], wbuf[slot].astype(_BF), preferred_element_type=_F32)
            store(acc, jj, slot)
            return carry

        lax.fori_loop(0, length, tile, 0)
        base += length

    @pl.when(i == n_row_tiles - 1)
    def _():
        wait_staging(0)
        wait_staging(1)


def _project(xp, xs, g, w_in, w_gate, b_gate, *, a_cols):
    n_p, d = xp.shape
    n_s = xs.shape[0]
    t = n_p + n_s
    tm, tn = PROJ_TM, PROJ_TN
    assert tn == B_HPG * HEAD_DIM and w_in.shape[1] == a_cols + 3 * N_DIL * tn
    assert n_p % tm == 0 and n_s % tm == 0
    gate_cols = w_gate.shape[1]
    na, ng = a_cols // tn, gate_cols // tn
    any_space = pl.BlockSpec(memory_space=pl.ANY)
    return pl.pallas_call(
        functools.partial(_proj_kernel, n_prompt_tiles=n_p // tm, n_row_tiles=t // tm, tm=tm, tn=tn, na=na, ng=ng),
        grid=(t // tm,),
        in_specs=[pl.BlockSpec((1, d), lambda i: (0, 0)), pl.BlockSpec((ng, 1, tn), lambda i: (0, 0, 0)),
                  any_space, any_space, any_space, any_space],
        out_specs=[any_space] * 5,
        out_shape=[jax.ShapeDtypeStruct((t, a_cols), _BF)]
                  + [jax.ShapeDtypeStruct((r, t // r, 3 * tn), _BF) for _, r in DILATED_GROUPS]
                  + [jax.ShapeDtypeStruct((t, gate_cols), _BF)],
        scratch_shapes=[pltpu.VMEM((tm, d), xp.dtype), pltpu.VMEM((tm, d), _BF), pltpu.VMEM((2, d, tn), w_in.dtype),
                        pltpu.VMEM((2, tn // LANES, tm, LANES), _F32), pltpu.VMEM((2, tm, tn), _BF),
                        pltpu.SemaphoreType.DMA(()), pltpu.SemaphoreType.DMA((2,)), pltpu.SemaphoreType.DMA((2,))],
        compiler_params=_params("arbitrary"),
        name="norm_proj",
    )(g, b_gate.reshape(ng, 1, tn), xp, xs, w_in, w_gate)


def _band_attn_kernel(slope_ref, sink_ref, q_ref, kl_ref, km_ref, kr_ref, vl_ref, vm_ref, vr_ref, *rest,
                      phases, has_sink, **static):
    if has_sink:
        o_ref, kcat, vcat = rest
    else:
        o_ref, lse_ref, kcat, vcat = rest
    for ph in range(phases):
        _band_attn_phase(slope_ref, sink_ref, *(r.at[ph] for r in (q_ref, kl_ref, km_ref, kr_ref, vl_ref, vm_ref,
                                                                  vr_ref)),
                         o_ref if has_sink else o_ref.at[ph], None if has_sink else lse_ref.at[ph],
                         kcat.at[ph], vcat.at[ph], has_sink=has_sink, **static)


def _band_attn_phase(slope_ref, sink_ref, q_ref, kl_ref, km_ref, kr_ref, vl_ref, vm_ref, vr_ref,
                     o_ref, lse_ref, kcat, vcat, *, bq, half, step, shared_kv, has_sink, prompt_rows,
                     prompt_seq, sample_seq):
    i = pl.program_id(0)
    c = pl.program_id(1)
    win = SUBQ + 2 * half

    kcat[0:half, :] = kl_ref[...]
    kcat[half:half + bq, :] = km_ref[...]
    kcat[half + bq:, :] = kr_ref[...]
    vcat[0:half, :] = vl_ref[...]
    vcat[half:half + bq, :] = vm_ref[...]
    vcat[half + bq:, :] = vr_ref[...]

    u0 = i * bq
    in_prompt = u0 < prompt_rows
    lo = jnp.where(in_prompt, (u0 // prompt_seq) * prompt_seq,
                   prompt_rows + ((u0 - prompt_rows) // sample_seq) * sample_seq)
    hi = lo + jnp.where(in_prompt, prompt_seq, sample_seq)

    qi = lax.broadcasted_iota(jnp.int32, (SUBQ, win), 0)
    kj = lax.broadcasted_iota(jnp.int32, (SUBQ, win), 1)
    rel = kj - half - qi
    absrel = jnp.abs(rel)
    band_bias = jnp.where(absrel <= half, -(absrel * step).astype(_F32), MASKED)
    lane = lax.broadcasted_iota(jnp.int32, (SUBQ, LANES), 1)

    heads = range(HEADS_PER_STEP)
    head_ids = [c * HEADS_PER_STEP + h if shared_kv else h for h in heads]
    scale2 = HEAD_DIM ** -0.5 * LOG2_E
    head_bias = [(slope_ref[hid] * LOG2_E) * band_bias for hid in head_ids]
    n_sub = bq // SUBQ
    for sb in range(n_sub):
        kpos = kj + (u0 + sb * SUBQ - half)
        in_seq = None
        if sb == 0:
            in_seq = kpos >= lo
        if sb == n_sub - 1:
            in_seq = kpos < hi if in_seq is None else in_seq & (kpos < hi)

        def scores2(s, h):
            s = s * scale2 + head_bias[h]
            return s if in_seq is None else jnp.where(in_seq, s, MASKED)

        rows = slice(sb * SUBQ, (sb + 1) * SUBQ)
        if shared_kv:
            k = kcat[sb * SUBQ:sb * SUBQ + win, :]
            v = vcat[sb * SUBQ:sb * SUBQ + win, :]
            q4 = jnp.concatenate([q_ref[rows, h * HEAD_DIM:(h + 1) * HEAD_DIM] for h in heads], axis=0)
            s4 = lax.dot_general(q4, k, (((1,), (1,)), ((), ())), preferred_element_type=_F32)
            ps, ms, ls = [], [], []
            for h in heads:
                s = scores2(s4[h * SUBQ:(h + 1) * SUBQ], h)
                m = jnp.max(s, axis=-1, keepdims=True)
                p = jnp.exp2(s - m)
                ms.append(m)
                ls.append(jnp.sum(p, axis=-1, keepdims=True))
                ps.append(p.astype(_BF))
            pv4 = jnp.dot(jnp.concatenate(ps, axis=0), v, preferred_element_type=_F32)
            for h in heads:
                l = ls[h] + jnp.exp2(sink_ref[head_ids[h]] * LOG2_E - ms[h])
                o_ref[rows, h * HEAD_DIM:(h + 1) * HEAD_DIM] = (
                    pv4[h * SUBQ:(h + 1) * SUBQ] / l).astype(o_ref.dtype)
            continue
        lse_tile = jnp.zeros((SUBQ, LANES), _F32)
        for h in heads:
            cols = slice(h * HEAD_DIM, (h + 1) * HEAD_DIM)
            k = kcat[sb * SUBQ:sb * SUBQ + win, cols]
            v = vcat[sb * SUBQ:sb * SUBQ + win, cols]
            s = lax.dot_general(q_ref[rows, cols], k, (((1,), (1,)), ((), ())), preferred_element_type=_F32)
            s = scores2(s, h)
            m = jnp.max(s, axis=-1, keepdims=True)
            p = jnp.exp2(s - m)
            l = jnp.sum(p, axis=-1, keepdims=True)
            pv = jnp.dot(p.astype(_BF), v, preferred_element_type=_F32)
            lse_tile = jnp.where(lane == h, m * LN_2 + jnp.log(l), lse_tile)
            o_ref[rows, cols] = (pv / l).astype(o_ref.dtype)
        lse_ref[rows, :] = lse_tile


def _band_attention(qkv, slopes, sinks, *, n_inner, bq, half, step, shared_kv, has_sink,
                    prompt_rows, prompt_seq, sample_seq, name):
    lead, rows, _ = qkv.shape
    assert prompt_seq % bq == 0 and sample_seq % bq == 0 and bq % SUBQ == 0 and bq % half == 0
    assert half <= SUBQ
    nq = rows // bq
    hb = bq // half
    last_halo = rows // half - 1
    width = HEADS_PER_STEP * HEAD_DIM
    if shared_kv:
        kv_w = HEAD_DIM
        k_base, v_base = width * n_inner // HEAD_DIM, width * n_inner // HEAD_DIM + n_inner
        pp = 1
        lead_of = lambda c: 0
        q_col = lambda c: c
        kv_col = lambda base: (lambda c: base + c)
    else:
        kv_w = width
        k_base, v_base = 1, 2
        pp = min(n_inner, max(1, ATTN_ROWS_PER_STEP // bq))
        n_inner //= pp
        lead_of = lambda c: c
        q_col = lambda c: 0
        kv_col = lambda base: (lambda c: base)

    def main_map(col):
        return lambda i, c: (lead_of(c), i, col(c))

    def left_map(col):
        return lambda i, c: (lead_of(c), jnp.maximum(i * hb - 1, 0), col(c))

    def right_map(col):
        return lambda i, c: (lead_of(c), jnp.minimum((i + 1) * hb, last_halo), col(c))

    smem = pl.BlockSpec(memory_space=pltpu.SMEM)
    in_specs = [
        smem, smem,
        pl.BlockSpec((pp, bq, width), main_map(q_col)),
        pl.BlockSpec((pp, half, kv_w), left_map(kv_col(k_base))),
        pl.BlockSpec((pp, bq, kv_w), main_map(kv_col(k_base))),
        pl.BlockSpec((pp, half, kv_w), right_map(kv_col(k_base))),
        pl.BlockSpec((pp, half, kv_w), left_map(kv_col(v_base))),
        pl.BlockSpec((pp, bq, kv_w), main_map(kv_col(v_base))),
        pl.BlockSpec((pp, half, kv_w), right_map(kv_col(v_base))),
    ]
    if has_sink:
        out_specs = pl.BlockSpec((bq, width), lambda i, c: (i, c))
        out_shape = jax.ShapeDtypeStruct((rows, n_inner * width), _BF)
    else:
        out_specs = (pl.BlockSpec((pp, bq, width), lambda i, c: (c, i, 0)),
                     pl.BlockSpec((pp, bq, LANES), lambda i, c: (c, i, 0)))
        out_shape = (jax.ShapeDtypeStruct((lead, rows, width), _BF),
                     jax.ShapeDtypeStruct((lead, rows, LANES), _F32))
    return pl.pallas_call(
        functools.partial(_band_attn_kernel, phases=pp, bq=bq, half=half, step=step, shared_kv=shared_kv,
                          has_sink=has_sink, prompt_rows=prompt_rows, prompt_seq=prompt_seq,
                          sample_seq=sample_seq),
        grid=(nq, n_inner),
        in_specs=in_specs,
        out_specs=out_specs,
        out_shape=out_shape,
        scratch_shapes=[pltpu.VMEM((pp, bq + 2 * half, kv_w), _BF), pltpu.VMEM((pp, bq + 2 * half, kv_w), _BF)],
        compiler_params=_params("parallel", "arbitrary"),
        name=name,
    )(slopes, sinks, qkv, qkv, qkv, qkv, qkv, qkv, qkv)


def _mix_kernel(xp_ref, xs_ref, oa_ref, ob0_ref, ob1_ref, ob2_ref, l0_ref, l1_ref, l2_ref, g_ref,
                wpa_ref, wpb_ref, wout_ref, n2_ref, wr_ref, br_ref, h_ref, hn_ref, ri_ref, rw_ref, cnt_ref,
                o_scr, l_scr, carry_ref, lg_scr, mg_scr, *, n_tiles, n_prompt_tiles, d, tm):
    i = pl.program_id(0)

    @pl.when(i == 0)
    def _():
        carry_ref[...] = jnp.zeros_like(carry_ref)
        lg_scr[...] = jnp.zeros_like(lg_scr)
        mg_scr[...] = jnp.zeros_like(mg_scr)

    _route_tile(lg_scr[...], jnp.where(i > 1, 1.0, 0.0), ri_ref, rw_ref, cnt_ref, carry_ref)

    tile2 = jnp.clip(i - 1, 0, n_tiles - 1)
    x = jnp.where(tile2 < n_prompt_tiles, xp_ref[...], xs_ref[...])
    h_new = x + jnp.dot(mg_scr[(i + 1) % 2], wout_ref[...], preferred_element_type=_F32)
    h_ref[...] = h_new
    hn = h_new * lax.rsqrt(jnp.mean(h_new * h_new, axis=-1, keepdims=True) + EPS) * n2_ref[...]
    _store_row_tiles(hn_ref, _pack_halves(hn))
    hn_hi = hn.astype(_BF)
    hn_lo = (hn - hn_hi.astype(_F32)).astype(_BF)
    r = (jnp.dot(hn_hi, wr_ref[...], preferred_element_type=_F32)
         + jnp.dot(hn_lo, wr_ref[...], preferred_element_type=_F32))
    lg_scr[...] = r[:, :LANES] + r[:, LANES:] + br_ref[...]


    def token_order(o_ref, l_ref, r, slot):
        if r == 1:
            return [o_ref[0, :, h * HEAD_DIM:(h + 1) * HEAD_DIM].astype(_F32) for h in range(B_HPG)], l_ref[0]
        for p in range(r):
            for h in range(B_HPG):
                o_scr[slot, h, pl.ds(p, tm // r, stride=r), :] = (
                    o_ref[p, :, h * HEAD_DIM:(h + 1) * HEAD_DIM].astype(_F32))
            l_scr[slot, pl.ds(p, tm // r, stride=r), :] = l_ref[p]
        return [o_scr[slot, h] for h in range(B_HPG)], l_scr[slot]

    o0, l0 = token_order(ob0_ref, l0_ref, DILATED_GROUPS[0][1], 0)
    o1, l1 = token_order(ob1_ref, l1_ref, DILATED_GROUPS[1][1], 0)
    o2, l2 = token_order(ob2_ref, l2_ref, DILATED_GROUPS[2][1], 1)
    mx = jnp.maximum(jnp.maximum(l0, l1), l2)
    e0, e1, e2 = jnp.exp(l0 - mx), jnp.exp(l1 - mx), jnp.exp(l2 - mx)
    den = e0 + e1 + e2
    a0, a1, a2 = e0 / den, e1 / den, e2 / den
    parts = []
    for h in range(B_HPG):
        parts.append(a0[:, h:h + 1] * o0[h] + a1[:, h:h + 1] * o1[h] + a2[:, h:h + 1] * o2[h])
    ob = jnp.concatenate(parts, axis=1).astype(_BF)

    ta = jnp.dot(oa_ref[...], wpa_ref[...], preferred_element_type=_F32)
    tb = jnp.dot(ob, wpb_ref[...], preferred_element_type=_F32)
    merged = g_ref[:, :d].astype(_F32) * ta + g_ref[:, d:].astype(_F32) * tb
    mg_scr[i % 2] = merged.astype(_BF)


def _mix(xp, xs, oa, obs, lses, gates, wpa, wpb, wout, n2, wr, br):
    n_p, d = xp.shape
    n_s = xs.shape[0]
    t = n_p + n_s
    tm = ROW_TILE
    npt = n_p // tm
    n_tiles = t // tm
    def stage_tile(lag):
        return lambda i: jnp.clip(i - lag, 0, n_tiles - 1)

    row = lambda i: (stage_tile(0)(i), 0)
    row2 = lambda i: (stage_tile(1)(i), 0)
    routed = lambda i: (stage_tile(2)(i), 0)
    const = lambda i: (0, 0)
    width = B_HPG * HEAD_DIM

    def resident(shape):
        return pl.BlockSpec(shape, const, pipeline_mode=pl.Buffered(1))

    def phase_blocks(cols):
        return [pl.BlockSpec((r, tm // r, cols), lambda i: (0, stage_tile(0)(i), 0)) for _, r in DILATED_GROUPS]

    in_specs = [
        pl.BlockSpec((tm, d), lambda i: (jnp.minimum(stage_tile(1)(i), npt - 1), 0)),
        pl.BlockSpec((tm, d), lambda i: (jnp.maximum(stage_tile(1)(i) - npt, 0), 0)),
        pl.BlockSpec((tm, oa.shape[1]), row),
        *phase_blocks(width), *phase_blocks(LANES),
        pl.BlockSpec((tm, 2 * d), row),
        resident(wpa.shape), resident(wpb.shape), resident(wout.shape),
        resident((1, d)), resident(wr.shape), resident((1, LANES)),
    ]
    return pl.pallas_call(
        functools.partial(_mix_kernel, n_tiles=n_tiles, n_prompt_tiles=npt, d=d, tm=tm),
        grid=(n_tiles + 2,),
        in_specs=in_specs,
        out_specs=(pl.BlockSpec((tm, d), row2), pl.BlockSpec((tm * (d // 2 // LANES), LANES), row2),
                   pl.BlockSpec((tm, LANES), routed), pl.BlockSpec((tm, LANES), routed),
                   pl.BlockSpec((SUBLANES, LANES), const)),
        out_shape=(jax.ShapeDtypeStruct((t, d), _F32),
                   jax.ShapeDtypeStruct((t * (d // 2 // LANES), LANES), jnp.uint32),
                   jax.ShapeDtypeStruct((t, LANES), jnp.int32),
                   jax.ShapeDtypeStruct((t, LANES), _F32),
                   jax.ShapeDtypeStruct((SUBLANES, LANES), jnp.int32)),
        scratch_shapes=[pltpu.VMEM((2, B_HPG, tm, HEAD_DIM), _F32), pltpu.VMEM((2, tm, LANES), _F32),
                        pltpu.VMEM((SUBLANES, LANES), _F32), pltpu.VMEM((tm, LANES), _F32),
                        pltpu.VMEM((2, tm, d), _BF)],
        compiler_params=_params("arbitrary"),
        name="mix_out_router",
    )(xp, xs, oa, *obs, *lses, gates, wpa, wpb, wout, n2, wr, br)


def _route_tile(lg, live, ri_ref, rw_ref, cnt_ref, carry_ref):
    tr = lg.shape[0]
    lane_i = lax.broadcasted_iota(jnp.int32, (tr, LANES), 1)
    lane = lane_i.astype(_F32)
    no_lane = float(LANES)
    is_grp = lane_i < N_GROUPS
    glog = jnp.where(is_grp, lg, MASKED)
    gmax = jnp.max(glog, axis=-1, keepdims=True)
    grp = jnp.min(jnp.where(glog == gmax, lane, no_lane), axis=-1, keepdims=True)
    gsum = jnp.sum(jnp.where(is_grp, jnp.exp(glog - gmax), 0.0), axis=-1, keepdims=True)
    pgrp = 1.0 / gsum
    lane_grp = ((lane_i - N_GROUPS) // EXPERTS_PER_GROUP).astype(_F32)
    in_grp = (lane_i >= N_GROUPS) & (lane_i < N_GROUPS + N_EXPERTS) & (lane_grp == grp)
    elog = jnp.where(in_grp, lg, MASKED)
    t1 = jnp.max(elog, axis=-1, keepdims=True)
    i1 = jnp.min(jnp.where(elog == t1, lane, no_lane), axis=-1, keepdims=True)
    elog2 = jnp.where(lane == i1, MASKED, elog)
    t2 = jnp.max(elog2, axis=-1, keepdims=True)
    i2 = jnp.min(jnp.where(elog2 == t2, lane, no_lane), axis=-1, keepdims=True)
    e21 = jnp.exp(t2 - t1)
    w1 = pgrp / (1.0 + e21)
    w2 = pgrp * e21 / (1.0 + e21)
    eid1 = i1 - N_GROUPS
    eid2 = i2 - N_GROUPS
    hot1 = lane == eid1
    hot2 = lane == eid2
    onehot = (jnp.where(hot1, 1.0, 0.0) + jnp.where(hot2, 1.0, 0.0)) * live
    r_i = lax.broadcasted_iota(jnp.int32, (tr, tr), 0)
    c_i = lax.broadcasted_iota(jnp.int32, (tr, tr), 1)
    lower = jnp.where(c_i < r_i, 1.0, 0.0).astype(_BF)
    before = jnp.dot(lower, onehot.astype(_BF), preferred_element_type=_F32) + carry_ref[0:1, :]
    rank1 = jnp.sum(jnp.where(hot1, before, 0.0), axis=-1, keepdims=True)
    rank2 = jnp.sum(jnp.where(hot2, before, 0.0), axis=-1, keepdims=True)
    total = carry_ref[0:1, :] + jnp.sum(onehot, axis=0, keepdims=True)
    carry_ref[...] = jnp.broadcast_to(total, carry_ref.shape)
    cnt_ref[...] = jnp.broadcast_to(total, cnt_ref.shape).astype(jnp.int32)
    ri = jnp.where(lane_i == 0, eid1, jnp.where(lane_i == 1, eid2, jnp.where(lane_i == 2, rank1,
                   jnp.where(lane_i == 3, rank2, 0.0))))
    ri_ref[...] = ri.astype(jnp.int32)
    rw_ref[...] = jnp.where(lane_i == 0, w1, jnp.where(lane_i == 1, w2, 0.0))


def _dispatch_kernel(dest_ref, pad_start_ref, pad_n_ref, nvalid_ref, hn_ref, xr_ref, zeros, sem, pad_sem,
                     *, td, sc):
    i = pl.program_id(0)

    def rows_of(ref, row0, n):
        start = row0 * sc if isinstance(row0, int) else pl.multiple_of(row0 * sc, n * sc)
        return ref.at[pl.ds(start, n * sc), :]

    @pl.when(i == 0)
    def _():
        zeros[...] = jnp.zeros_like(zeros)

        def pad_chunk(e, r):
            row0 = pl.multiple_of(pad_start_ref[e] + r * SUBLANES, SUBLANES)
            return pltpu.make_async_copy(rows_of(zeros, 0, SUBLANES), rows_of(xr_ref, row0, SUBLANES), pad_sem)

        def per_expert(act):
            def body(e, carry):
                lax.fori_loop(0, pad_n_ref[e], lambda r, c: (act(pad_chunk(e, r)), c)[1], 0)
                return carry
            return body

        def tail_block(b):
            return pltpu.make_async_copy(zeros, rows_of(xr_ref, b * EXPERT_ROWS, EXPERT_ROWS), pad_sem)

        n_blocks = xr_ref.shape[0] // (EXPERT_ROWS * sc)
        lax.fori_loop(0, N_EXPERTS, per_expert(lambda cp: cp.start()), 0)
        lax.fori_loop(nvalid_ref[0], n_blocks, lambda b, c: (tail_block(b).start(), c)[1], 0)
        lax.fori_loop(0, N_EXPERTS, per_expert(lambda cp: cp.wait()), 0)
        lax.fori_loop(nvalid_ref[0], n_blocks, lambda b, c: (tail_block(b).wait(), c)[1], 0)

    def start(j, carry):
        for k in range(2):
            dst = dest_ref[2 * (i * td + j) + k]
            pltpu.make_async_copy(rows_of(hn_ref, j, 1), rows_of(xr_ref, dst, 1), sem).start()
        return carry

    lax.fori_loop(0, td, start, 0, unroll=DMA_UNROLL)
    for _ in range(2):
        pltpu.make_async_copy(hn_ref, rows_of(xr_ref, 0, td), sem).wait()


def _dispatch(dest, pad_start, pad_n, nvalid, hn, t, rows):
    sc = hn.shape[0] // t
    td = DISPATCH_TILE
    return pl.pallas_call(
        functools.partial(_dispatch_kernel, td=td, sc=sc),
        grid_spec=pltpu.PrefetchScalarGridSpec(
            num_scalar_prefetch=4,
            grid=(t // td,),
            in_specs=[pl.BlockSpec((td * sc, LANES), lambda i, *_: (i, 0))],
            out_specs=pl.BlockSpec(memory_space=pl.ANY),
            scratch_shapes=[pltpu.VMEM((EXPERT_ROWS * sc, LANES), hn.dtype), pltpu.SemaphoreType.DMA(()),
                            pltpu.SemaphoreType.DMA(())],
        ),
        out_shape=jax.ShapeDtypeStruct((rows * sc, LANES), hn.dtype),
        compiler_params=_params("arbitrary"),
        name="dispatch",
    )(dest, pad_start, pad_n, nvalid, hn)


def _expert_kernel(blk_e_ref, nvalid_ref, first_ref, slot_ref, next_e_ref, ahead_e_ref, x_ref, w1_hbm, w3_hbm,
                   w2_hbm, y_ref, w1_buf, w3_buf, w2_buf, sems):
    b = pl.program_id(0)
    valid = b < nvalid_ref[0]

    def fetch(e, slot):
        return (pltpu.make_async_copy(w1_hbm.at[e], w1_buf.at[slot], sems.at[slot, 0]),
                pltpu.make_async_copy(w3_hbm.at[e], w3_buf.at[slot], sems.at[slot, 1]),
                pltpu.make_async_copy(w2_hbm.at[e], w2_buf.at[slot], sems.at[slot, 2]))

    @pl.when(b == 0)
    def _():
        for cp in fetch(blk_e_ref[0], 0):
            cp.start()

        @pl.when(next_e_ref[0] >= 0)
        def _():
            for cp in fetch(next_e_ref[0], 1):
                cp.start()

    @pl.when(valid & (first_ref[b] == 1))
    def _():
        for cp in fetch(blk_e_ref[b], slot_ref[b]):
            cp.wait()

        @pl.when(ahead_e_ref[b] >= 0)
        def _():
            for cp in fetch(ahead_e_ref[b], (slot_ref[b] + WEIGHT_SLOTS - 1) % WEIGHT_SLOTS):
                cp.start()

    @pl.when(valid)
    def _():
        slot = slot_ref[b]
        packed_x = _load_row_tiles(x_ref, EXPERT_ROWS)
        xb = jnp.concatenate([half.astype(_BF) for half in _unpack_halves(packed_x)], axis=1)
        a = jnp.dot(xb, w1_buf[slot].astype(_BF), preferred_element_type=_F32)
        u = jnp.dot(xb, w3_buf[slot].astype(_BF), preferred_element_type=_F32)
        hmid = (a / (1.0 + jnp.exp(-a))) * u
        y = jnp.dot(hmid.astype(_BF), w2_buf[slot].astype(_BF), preferred_element_type=_F32)
        _store_row_tiles(y_ref, _pack_halves(y))

    @pl.when(jnp.logical_not(valid))
    def _():
        y_ref[...] = jnp.zeros_like(y_ref)


def _experts(blk_e, nvalid, first, slot, next_e, ahead_e, xr, w1, w3, w2):
    _, d, f = w1.shape
    sc = d // 2 // LANES
    blk = EXPERT_ROWS * sc
    nblk = xr.shape[0] // blk
    any_space = pl.BlockSpec(memory_space=pl.ANY)
    return pl.pallas_call(
        _expert_kernel,
        grid_spec=pltpu.PrefetchScalarGridSpec(
            num_scalar_prefetch=6,
            grid=(nblk,),
            in_specs=[pl.BlockSpec((blk, LANES), lambda b, be, nv, *_: (jnp.minimum(b, nv[0] - 1), 0)),
                      any_space, any_space, any_space],
            out_specs=pl.BlockSpec((blk, LANES), lambda b, *_: (b, 0)),
            scratch_shapes=[pltpu.VMEM((WEIGHT_SLOTS, d, f), w1.dtype), pltpu.VMEM((WEIGHT_SLOTS, d, f), w3.dtype),
                            pltpu.VMEM((WEIGHT_SLOTS, f, d), w2.dtype), pltpu.SemaphoreType.DMA((WEIGHT_SLOTS, 3))],
        ),
        out_shape=jax.ShapeDtypeStruct(xr.shape, jnp.uint32),
        compiler_params=_params("arbitrary"),
        name="experts",
    )(blk_e, nvalid, first, slot, next_e, ahead_e, xr, w1, w3, w2)


def _combine_kernel(dest_ref, h_ref, rw_ref, g_ref, yr_ref, op_ref, os_ref, ybuf, sems, *, tc, sc, n_prompt_tiles):
    i = pl.program_id(0)
    slot = i % 2

    def gather(tile, s):
        def start(j, carry):
            for k in range(2):
                src = dest_ref[2 * (tile * tc + j) + k]
                pltpu.make_async_copy(_row_tile(yr_ref, src, sc), _row_tile(ybuf.at[s, k], j, sc),
                                      sems.at[s]).start()
            return carry

        lax.fori_loop(0, tc, start, 0, unroll=DMA_UNROLL)

    @pl.when(i == 0)
    def _():
        gather(0, 0)

    @pl.when(i + 1 < pl.num_programs(0))
    def _():
        gather(i + 1, 1 - slot)

    for k in range(2):
        pltpu.make_async_copy(yr_ref.at[pl.ds(0, tc * sc), :], ybuf.at[slot, k], sems.at[slot]).wait()

    rw = rw_ref[...]
    y0 = _unpack_halves(_load_row_tiles(ybuf.at[slot, 0], tc))
    y1 = _unpack_halves(_load_row_tiles(ybuf.at[slot, 1], tc))
    y = jnp.concatenate([rw[:, 0:1] * a + rw[:, 1:2] * b for a, b in zip(y0, y1)], axis=1)
    z = h_ref[...] + y
    out = z * lax.rsqrt(jnp.mean(z * z, axis=-1, keepdims=True) + EPS) * g_ref[...]

    @pl.when(i < n_prompt_tiles)
    def _():
        op_ref[...] = out

    @pl.when(i >= n_prompt_tiles)
    def _():
        os_ref[...] = out


def _combine(dest, h, rw, g, yr, *, n_p):
    t, d = h.shape
    tc = ROW_TILE
    npt = n_p // tc
    sc = d // 2 // LANES
    return pl.pallas_call(
        functools.partial(_combine_kernel, tc=tc, sc=sc, n_prompt_tiles=npt),
        grid_spec=pltpu.PrefetchScalarGridSpec(
            num_scalar_prefetch=1,
            grid=(t // tc,),
            in_specs=[
                pl.BlockSpec((tc, d), lambda i, dest: (i, 0)),
                pl.BlockSpec((tc, LANES), lambda i, dest: (i, 0)),
                pl.BlockSpec((1, d), lambda i, dest: (0, 0)),
                pl.BlockSpec(memory_space=pl.ANY),
            ],
            out_specs=(pl.BlockSpec((tc, d), lambda i, dest: (jnp.minimum(i, npt - 1), 0)),
                       pl.BlockSpec((tc, d), lambda i, dest: (jnp.maximum(i - npt, 0), 0))),
            scratch_shapes=[pltpu.VMEM((2, 2, tc * sc, LANES), yr.dtype), pltpu.SemaphoreType.DMA((2,))],
        ),
        out_shape=(jax.ShapeDtypeStruct((n_p, d), _F32), jax.ShapeDtypeStruct((t - n_p, d), _F32)),
        compiler_params=_params("arbitrary"),
        name="combine_norm",
    )(dest, h, rw, g, yr)


def _layer(xp, xs, prompt_seq, sample_seq, norm1, w_in, sink, w_proj_a, w_proj_b, w_gate, b_gate, w_out,
           norm2, w_rg, b_rg, w_re, b_re, w1, w3, w2, norm_final):
    n_p, d = xp.shape
    n_s = xs.shape[0]
    t = n_p + n_s
    a_q = A_HEADS * HEAD_DIM
    a_cols = a_q + 2 * A_KV * HEAD_DIM

    proj_a, q0, q1, q2, gates = _project(xp, xs, norm1.reshape(1, d), w_in, w_gate, b_gate.reshape(1, 2 * d),
                                         a_cols=a_cols)

    oa = _band_attention(
        proj_a.reshape(1, t, a_cols), jnp.asarray(_alibi(A_HEADS)), sink.astype(_F32), n_inner=A_KV,
        bq=min(2048, prompt_seq, sample_seq), half=A_HALF, step=1, shared_kv=True, has_sink=True,
        prompt_rows=n_p, prompt_seq=prompt_seq, sample_seq=sample_seq, name="attn_window")

    slopes_b = _alibi(B_HEADS)
    obs, lses = [], []
    for gi, ((w, r), qkv) in enumerate(zip(DILATED_GROUPS, (q0, q1, q2))):
        o_g, lse_g = _band_attention(
            qkv, jnp.asarray(slopes_b[gi * B_HPG:(gi + 1) * B_HPG]), jnp.zeros((B_HPG,), _F32), n_inner=r,
            bq=min(2048, prompt_seq // r, sample_seq // r), half=w // (2 * r), step=r, shared_kv=False,
            has_sink=False, prompt_rows=n_p // r, prompt_seq=prompt_seq // r, sample_seq=sample_seq // r,
            name=f"attn_dilated_{r}")
        obs.append(o_g)
        lses.append(lse_g)

    n_r = N_GROUPS + N_EXPERTS
    wr = jnp.concatenate([w_rg, jnp.transpose(w_re, (1, 0, 2)).reshape(d, N_EXPERTS)], axis=1)
    wr = jnp.pad(wr, ((0, 0), (0, LANES - n_r)))
    wr_hi = wr.astype(_BF)
    wr_lo = (wr - wr_hi.astype(_F32)).astype(_BF)
    wr2 = jnp.concatenate([wr_hi, wr_lo], axis=1)
    br = jnp.pad(jnp.concatenate([b_rg, b_re.reshape(-1)]), (0, LANES - n_r)).reshape(1, LANES).astype(_F32)

    h, hn, ri, rw, cnt = _mix(xp, xs, oa, obs, lses, gates, w_proj_a.astype(_BF), w_proj_b.astype(_BF),
                              w_out.astype(_BF), norm2.reshape(1, d), wr2, br)
    counts = cnt[0, :N_EXPERTS]
    eid = ri[:, 0:2]
    rank = ri[:, 2:4]
    pcounts = (counts + EXPERT_ROWS - 1) // EXPERT_ROWS * EXPERT_ROWS
    pends = jnp.cumsum(pcounts)
    pstarts = pends - pcounts
    expert_ids = jnp.arange(N_EXPERTS, dtype=jnp.int32)
    start_of = jnp.sum(jnp.where(eid[:, :, None] == expert_ids, pstarts.astype(jnp.int32), 0), axis=-1)
    dest = (start_of + rank).reshape(-1).astype(jnp.int32)
    nblk = (2 * t + N_EXPERTS * (EXPERT_ROWS - 1) + EXPERT_ROWS - 1) // EXPERT_ROWS
    nvalid = (pends[-1] // EXPERT_ROWS).astype(jnp.int32)
    blk_start = jnp.minimum(jnp.arange(nblk, dtype=jnp.int32), nvalid - 1) * EXPERT_ROWS
    blk_e = jnp.sum(pends[None, :] <= blk_start[:, None], axis=1).astype(jnp.int32)

    pad_start = (pstarts + counts) // SUBLANES * SUBLANES
    xr = _dispatch(dest, pad_start.astype(jnp.int32), ((pends - pad_start) // SUBLANES).astype(jnp.int32),
                   nvalid.reshape(1), hn, t, nblk * EXPERT_ROWS)
    blk_ids = jnp.arange(nblk, dtype=jnp.int32)
    first = ((blk_ids == 0) | (blk_e != jnp.roll(blk_e, 1))).astype(jnp.int32)
    slot = ((jnp.cumsum(first) - 1) % WEIGHT_SLOTS).astype(jnp.int32)
    later = (expert_ids[None, :] > expert_ids[:, None]) & (counts[None, :] > 0)
    next_of = jnp.min(jnp.where(later, expert_ids[None, :], N_EXPERTS), axis=1)
    lookup = lambda table, idx: jnp.sum(jnp.where(idx[:, None] == expert_ids, table, 0), axis=1)
    next2_of = jnp.where(next_of < N_EXPERTS, lookup(next_of, jnp.minimum(next_of, N_EXPERTS - 1)), N_EXPERTS)
    as_id = lambda e: jnp.where(e < N_EXPERTS, e, -1).astype(jnp.int32)
    yr = _experts(blk_e, nvalid.reshape(1), first, slot, as_id(lookup(next_of, blk_e)),
                  as_id(lookup(next2_of, blk_e)), xr, w1, w3, w2)
    return _combine(dest, h, rw, norm_final.reshape(1, d), yr, n_p=n_p)


def kernel(x_prompt, x_sample, norm1, w_in, attn_sink, w_proj_a, w_proj_b, w_gate, b_gate, w_out, norm2,
           w_router_group, b_router_group, w_router_expert, b_router_expert, w_expert_gate, w_expert_up,
           w_expert_down, norm_final):
    assert norm1.shape[0] == 1, "one layer"
    d = x_prompt.shape[-1]
    xp = x_prompt.reshape(-1, d)
    xs = x_sample.reshape(-1, d)
    yp, ys = _layer(xp, xs, x_prompt.shape[1], x_sample.shape[1], norm1[0], w_in[0], attn_sink[0], w_proj_a[0],
                    w_proj_b[0], w_gate[0], b_gate[0], w_out[0], norm2[0], w_router_group[0],
                    b_router_group[0], w_router_expert[0], b_router_expert[0], w_expert_gate[0],
                    w_expert_up[0], w_expert_down[0], norm_final)
    return yp.reshape(x_prompt.shape), ys.reshape(x_sample.shape)
```

```python
import functools
import math

import jax
import jax.numpy as jnp
import numpy as np
from jax import lax
from jax.experimental import pallas as pl
from jax.experimental.pallas import tpu as pltpu

HEAD_DIM = 128
A_HEADS = 16
A_KV = 4
A_HALF = 128
DILATED_GROUPS = ((128, 1), (512, 4), (2048, 16))
N_DIL = len(DILATED_GROUPS)
B_HPG = 4
B_HEADS = N_DIL * B_HPG
N_GROUPS = 8
EXPERTS_PER_GROUP = 8
N_EXPERTS = N_GROUPS * EXPERTS_PER_GROUP
EPS = 1e-6

LANES = 128
SUBLANES = 8
MASKED = -1e30
LOG2_E = math.log2(math.e)
LN_2 = math.log(2.0)
VMEM_LIMIT = 56 * 1024 * 1024
SUBQ = 128
HEADS_PER_STEP = 4
EXPERT_ROWS = 256
DMA_UNROLL = 8
PROJ_TM, PROJ_TN = 2048, 512
MAX_ROW_STRIDE = 4
ROW_TILE = 256
DISPATCH_TILE = 1024
ATTN_ROWS_PER_STEP = 1024

_BF = jnp.bfloat16
_F32 = jnp.float32


def _alibi(n):
    return np.power(2.0, -8.0 * (np.arange(n) + 1) / n).astype(np.float32)


def _params(*sem):
    return pltpu.CompilerParams(dimension_semantics=sem, vmem_limit_bytes=VMEM_LIMIT)


def _pack_halves(x):
    n = x.shape[1] // 2

    def bf16_bits(v):
        return lax.bitcast_convert_type(v.astype(_BF).astype(_F32), jnp.uint32)

    return bf16_bits(x[:, n:]) | (bf16_bits(x[:, :n]) >> 16)


def _unpack_halves(p):
    lo = lax.bitcast_convert_type(p << 16, _F32)
    hi = lax.bitcast_convert_type(p & jnp.uint32(0xFFFF0000), _F32)
    return lo, hi


def _store_row_tiles(ref, packed):
    rows, words = packed.shape
    s_count = words // LANES
    for s in range(s_count):
        ref[pl.ds(s, rows, stride=s_count), :] = packed[:, s * LANES:(s + 1) * LANES]


def _load_row_tiles(ref, rows):
    s_count = ref.shape[0] // rows
    return jnp.concatenate([ref[pl.ds(s, rows, stride=s_count), :] for s in range(s_count)], axis=1)


def _row_tile(ref, row, s_count):
    return ref.at[pl.ds(pl.multiple_of(row * s_count, s_count), s_count), :]


def _proj_kernel(g_ref, b_ref, xp_hbm, xs_hbm, win_hbm, wg_hbm, oa_hbm, q0_hbm, q1_hbm, q2_hbm, gt_hbm,
                 xbuf, xn_ref, wbuf, acc_ref, obuf, x_sem, w_sems, o_sems,
                 *, n_prompt_tiles, n_row_tiles, tm, tn, na, ng):
    i = pl.program_id(0)
    nb = 3
    tiles_per_row = na + N_DIL * nb + ng
    rows = pl.ds(pl.multiple_of(i * tm, tm), tm)

    def cols(jj):
        return pl.ds(pl.multiple_of(jj * tn, tn), tn)

    def x_copy(x_hbm, row_tile):
        return pltpu.make_async_copy(x_hbm.at[pl.ds(pl.multiple_of(row_tile * tm, tm), tm), :], xbuf, x_sem)

    def start_x(row_tile):
        @pl.when(row_tile < n_prompt_tiles)
        def _():
            x_copy(xp_hbm, row_tile).start()

        @pl.when(row_tile >= n_prompt_tiles)
        def _():
            x_copy(xs_hbm, row_tile - n_prompt_tiles).start()

    def w_copy(w_hbm, col, slot):
        return pltpu.make_async_copy(w_hbm.at[:, cols(col)], wbuf.at[slot], w_sems.at[slot])

    def wait_staging(slot):
        pltpu.make_async_copy(obuf.at[slot], oa_hbm.at[pl.ds(0, tm), pl.ds(0, tn)], o_sems.at[slot]).wait()

    def store_plain(out_hbm):
        def store(acc, jj, slot):
            obuf[slot] = acc.astype(_BF)
            pltpu.make_async_copy(obuf.at[slot], out_hbm.at[rows, cols(jj)], o_sems.at[slot]).start()
        return store

    def store_gate(acc, jj, slot):
        z = acc + b_ref[jj]
        obuf[slot] = (0.5 * jnp.tanh(0.5 * z) + 0.5).astype(_BF)
        pltpu.make_async_copy(obuf.at[slot], gt_hbm.at[rows, cols(jj)], o_sems.at[slot]).start()

    def store_phases(q_hbm, r):
        n = tm // r

        def store(acc, jj, slot):
            if r == 1:
                obuf[slot] = acc.astype(_BF)
            else:
                for cb in range(tn // LANES):
                    acc_ref[0, cb] = acc[:, cb * LANES:(cb + 1) * LANES]
                src, stride = 0, r
                if r > MAX_ROW_STRIDE:
                    stride = r // MAX_ROW_STRIDE
                    m = tm // MAX_ROW_STRIDE
                    for c in range(MAX_ROW_STRIDE):
                        for cb in range(tn // LANES):
                            acc_ref[1, cb, c * m:(c + 1) * m, :] = acc_ref[0, cb, pl.ds(c, m, stride=MAX_ROW_STRIDE), :]
                    src = 1
                for p in range(r):
                    start = p if src == 0 else (p % MAX_ROW_STRIDE) * (tm // MAX_ROW_STRIDE) + p // MAX_ROW_STRIDE
                    for cb in range(tn // LANES):
                        obuf[slot, p * n:(p + 1) * n, cb * LANES:(cb + 1) * LANES] = (
                            acc_ref[src, cb, pl.ds(start, n, stride=stride), :].astype(_BF))
            for p in range(r):
                pltpu.make_async_copy(obuf.at[slot, pl.ds(p * n, n), :],
                                      q_hbm.at[p, pl.ds(pl.multiple_of(i * n, n), n), cols(jj)],
                                      o_sems.at[slot]).start()
        return store

    parts = [(na, win_hbm, lambda jj: jj, store_plain(oa_hbm))]
    for gi, (q_hbm, (_, r)) in enumerate(zip((q0_hbm, q1_hbm, q2_hbm), DILATED_GROUPS)):
        parts.append((nb, win_hbm, lambda jj, gi=gi: na + jj * N_DIL + gi, store_phases(q_hbm, r)))
    parts.append((ng, wg_hbm, lambda jj: jj, store_gate))

    @pl.when(i == 0)
    def _():
        start_x(0)
        w_copy(win_hbm, 0, 0).start()

    x_copy(xp_hbm, 0).wait()
    x = xbuf[...]
    y = x * lax.rsqrt(jnp.mean(x * x, axis=-1, keepdims=True) + EPS)
    xn_ref[...] = (y * g_ref[...]).astype(_BF)

    @pl.when(i + 1 < n_row_tiles)
    def _():
        start_x(i + 1)

    base = 0
    for k, (length, w_hbm, w_col, store) in enumerate(parts):
        def tile(jj, carry, base=base, length=length, w_hbm=w_hbm, w_col=w_col, store=store, k=k):
            n = i * tiles_per_row + base + jj
            slot = n % 2

            @pl.when(jj + 1 < length)
            def _():
                w_copy(w_hbm, w_col(jj + 1), 1 - slot).start()

            @pl.when(jj + 1 == length)
            def _():
                if k + 1 < len(parts):
                    w_copy(parts[k + 1][1], parts[k + 1][2](0), 1 - slot).start()
                else:
                    @pl.when(i + 1 < n_row_tiles)
                    def _():
                        w_copy(win_hbm, 0, 1 - slot).start()

            @pl.when(n >= 2)
            def _():
                wait_staging(slot)

            w_copy(w_hbm, w_col(jj), slot).wait()
            acc = jnp.dot(xn_ref[...], wbuf[slot].astype(_BF), preferred_element_type=_F32)
            store(acc, jj, slot)
            return carry

        lax.fori_loop(0, length, tile, 0)
        base += length

    @pl.when(i == n_row_tiles - 1)
    def _():
        wait_staging(0)
        wait_staging(1)


def _project(xp, xs, g, w_in, w_gate, b_gate, *, a_cols):
    n_p, d = xp.shape
    n_s = xs.shape[0]
    t = n_p + n_s
    tm, tn = PROJ_TM, PROJ_TN
    assert tn == B_HPG * HEAD_DIM and w_in.shape[1] == a_cols + 3 * N_DIL * tn
    assert n_p % tm == 0 and n_s % tm == 0
    gate_cols = w_gate.shape[1]
    na, ng = a_cols // tn, gate_cols // tn
    any_space = pl.BlockSpec(memory_space=pl.ANY)
    return pl.pallas_call(
        functools.partial(_proj_kernel, n_prompt_tiles=n_p // tm, n_row_tiles=t // tm, tm=tm, tn=tn, na=na, ng=ng),
        grid=(t // tm,),
        in_specs=[pl.BlockSpec((1, d), lambda i: (0, 0)), pl.BlockSpec((ng, 1, tn), lambda i: (0, 0, 0)),
                  any_space, any_space, any_space, any_space],
        out_specs=[any_space] * 5,
        out_shape=[jax.ShapeDtypeStruct((t, a_cols), _BF)]
                  + [jax.ShapeDtypeStruct((r, t // r, 3 * tn), _BF) for _, r in DILATED_GROUPS]
                  + [jax.ShapeDtypeStruct((t, gate_cols), _BF)],
        scratch_shapes=[pltpu.VMEM((tm, d), xp.dtype), pltpu.VMEM((tm, d), _BF), pltpu.VMEM((2, d, tn), w_in.dtype),
                        pltpu.VMEM((2, tn // LANES, tm, LANES), _F32), pltpu.VMEM((2, tm, tn), _BF),
                        pltpu.SemaphoreType.DMA(()), pltpu.SemaphoreType.DMA((2,)), pltpu.SemaphoreType.DMA((2,))],
        compiler_params=_params("arbitrary"),
        name="norm_proj",
    )(g, b_gate.reshape(ng, 1, tn), xp, xs, w_in, w_gate)


def _band_attn_kernel(slope_ref, sink_ref, q_ref, kl_ref, km_ref, kr_ref, vl_ref, vm_ref, vr_ref, *rest,
                      phases, has_sink, **static):
    if has_sink:
        o_ref, kcat, vcat = rest
    else:
        o_ref, lse_ref, kcat, vcat = rest
    for ph in range(phases):
        _band_attn_phase(slope_ref, sink_ref, *(r.at[ph] for r in (q_ref, kl_ref, km_ref, kr_ref, vl_ref, vm_ref,
                                                                  vr_ref)),
                         o_ref if has_sink else o_ref.at[ph], None if has_sink else lse_ref.at[ph],
                         kcat.at[ph], vcat.at[ph], has_sink=has_sink, **static)


def _band_attn_phase(slope_ref, sink_ref, q_ref, kl_ref, km_ref, kr_ref, vl_ref, vm_ref, vr_ref,
                     o_ref, lse_ref, kcat, vcat, *, bq, half, step, shared_kv, has_sink, prompt_rows,
                     prompt_seq, sample_seq):
    i = pl.program_id(0)
    c = pl.program_id(1)
    win = SUBQ + 2 * half

    kcat[0:half, :] = kl_ref[...]
    kcat[half:half + bq, :] = km_ref[...]
    kcat[half + bq:, :] = kr_ref[...]
    vcat[0:half, :] = vl_ref[...]
    vcat[half:half + bq, :] = vm_ref[...]
    vcat[half + bq:, :] = vr_ref[...]

    u0 = i * bq
    in_prompt = u0 < prompt_rows
    lo = jnp.where(in_prompt, (u0 // prompt_seq) * prompt_seq,
                   prompt_rows + ((u0 - prompt_rows) // sample_seq) * sample_seq)
    hi = lo + jnp.where(in_prompt, prompt_seq, sample_seq)

    qi = lax.broadcasted_iota(jnp.int32, (SUBQ, win), 0)
    kj = lax.broadcasted_iota(jnp.int32, (SUBQ, win), 1)
    rel = kj - half - qi
    absrel = jnp.abs(rel)
    band_bias = jnp.where(absrel <= half, -(absrel * step).astype(_F32), MASKED)
    lane = lax.broadcasted_iota(jnp.int32, (SUBQ, LANES), 1)

    heads = range(HEADS_PER_STEP)
    head_ids = [c * HEADS_PER_STEP + h if shared_kv else h for h in heads]
    scale2 = HEAD_DIM ** -0.5 * LOG2_E
    head_bias = [(slope_ref[hid] * LOG2_E) * band_bias for hid in head_ids]
    n_sub = bq // SUBQ
    for sb in range(n_sub):
        kpos = kj + (u0 + sb * SUBQ - half)
        in_seq = None
        if sb == 0:
            in_seq = kpos >= lo
        if sb == n_sub - 1:
            in_seq = kpos < hi if in_seq is None else in_seq & (kpos < hi)

        def scores2(s, h):
            s = s * scale2 + head_bias[h]
            return s if in_seq is None else jnp.where(in_seq, s, MASKED)

        rows = slice(sb * SUBQ, (sb + 1) * SUBQ)
        if shared_kv:
            k = kcat[sb * SUBQ:sb * SUBQ + win, :]
            v = vcat[sb * SUBQ:sb * SUBQ + win, :]
            q4 = jnp.concatenate([q_ref[rows, h * HEAD_DIM:(h + 1) * HEAD_DIM] for h in heads], axis=0)
            s4 = lax.dot_general(q4, k, (((1,), (1,)), ((), ())), preferred_element_type=_F32)
            ps, ms, ls = [], [], []
            for h in heads:
                s = scores2(s4[h * SUBQ:(h + 1) * SUBQ], h)
                m = jnp.max(s, axis=-1, keepdims=True)
                p = jnp.exp2(s - m)
                ms.append(m)
                ls.append(jnp.sum(p, axis=-1, keepdims=True))
                ps.append(p.astype(_BF))
            pv4 = jnp.dot(jnp.concatenate(ps, axis=0), v, preferred_element_type=_F32)
            for h in heads:
                l = ls[h] + jnp.exp2(sink_ref[head_ids[h]] * LOG2_E - ms[h])
                o_ref[rows, h * HEAD_DIM:(h + 1) * HEAD_DIM] = (
                    pv4[h * SUBQ:(h + 1) * SUBQ] / l).astype(o_ref.dtype)
            continue
        lse_tile = jnp.zeros((SUBQ, LANES), _F32)
        for h in heads:
            cols = slice(h * HEAD_DIM, (h + 1) * HEAD_DIM)
            k = kcat[sb * SUBQ:sb * SUBQ + win, cols]
            v = vcat[sb * SUBQ:sb * SUBQ + win, cols]
            s = lax.dot_general(q_ref[rows, cols], k, (((1,), (1,)), ((), ())), preferred_element_type=_F32)
            s = scores2(s, h)
            m = jnp.max(s, axis=-1, keepdims=True)
            p = jnp.exp2(s - m)
            l = jnp.sum(p, axis=-1, keepdims=True)
            pv = jnp.dot(p.astype(_BF), v, preferred_element_type=_F32)
            lse_tile = jnp.where(lane == h, m * LN_2 + jnp.log(l), lse_tile)
            o_ref[rows, cols] = (pv / l).astype(o_ref.dtype)
        lse_ref[rows, :] = lse_tile


def _band_attention(qkv, slopes, sinks, *, n_inner, bq, half, step, shared_kv, has_sink,
                    prompt_rows, prompt_seq, sample_seq, name):
    lead, rows, _ = qkv.shape
    assert prompt_seq % bq == 0 and sample_seq % bq == 0 and bq % SUBQ == 0 and bq % half == 0
    assert half <= SUBQ
    nq = rows // bq
    hb = bq // half
    last_halo = rows // half - 1
    width = HEADS_PER_STEP * HEAD_DIM
    if shared_kv:
        kv_w = HEAD_DIM
        k_base, v_base = width * n_inner // HEAD_DIM, width * n_inner // HEAD_DIM + n_inner
        pp = 1
        lead_of = lambda c: 0
        q_col = lambda c: c
        kv_col = lambda base: (lambda c: base + c)
    else:
        kv_w = width
        k_base, v_base = 1, 2
        pp = min(n_inner, max(1, ATTN_ROWS_PER_STEP // bq))
        n_inner //= pp
        lead_of = lambda c: c
        q_col = lambda c: 0
        kv_col = lambda base: (lambda c: base)

    def main_map(col):
        return lambda i, c: (lead_of(c), i, col(c))

    def left_map(col):
        return lambda i, c: (lead_of(c), jnp.maximum(i * hb - 1, 0), col(c))

    def right_map(col):
        return lambda i, c: (lead_of(c), jnp.minimum((i + 1) * hb, last_halo), col(c))

    smem = pl.BlockSpec(memory_space=pltpu.SMEM)
    in_specs = [
        smem, smem,
        pl.BlockSpec((pp, bq, width), main_map(q_col)),
        pl.BlockSpec((pp, half, kv_w), left_map(kv_col(k_base))),
        pl.BlockSpec((pp, bq, kv_w), main_map(kv_col(k_base))),
        pl.BlockSpec((pp, half, kv_w), right_map(kv_col(k_base))),
        pl.BlockSpec((pp, half, kv_w), left_map(kv_col(v_base))),
        pl.BlockSpec((pp, bq, kv_w), main_map(kv_col(v_base))),
        pl.BlockSpec((pp, half, kv_w), right_map(kv_col(v_base))),
    ]
    if has_sink:
        out_specs = pl.BlockSpec((bq, width), lambda i, c: (i, c))
        out_shape = jax.ShapeDtypeStruct((rows, n_inner * width), _BF)
    else:
        out_specs = (pl.BlockSpec((pp, bq, width), lambda i, c: (c, i, 0)),
                     pl.BlockSpec((pp, bq, LANES), lambda i, c: (c, i, 0)))
        out_shape = (jax.ShapeDtypeStruct((lead, rows, width), _BF),
                     jax.ShapeDtypeStruct((lead, rows, LANES), _F32))
    return pl.pallas_call(
        functools.partial(_band_attn_kernel, phases=pp, bq=bq, half=half, step=step, shared_kv=shared_kv,
                          has_sink=has_sink, prompt_rows=prompt_rows, prompt_seq=prompt_seq,
                          sample_seq=sample_seq),
        grid=(nq, n_inner),
        in_specs=in_specs,
        out_specs=out_specs,
        out_shape=out_shape,
        scratch_shapes=[pltpu.VMEM((pp, bq + 2 * half, kv_w), _BF), pltpu.VMEM((pp, bq + 2 * half, kv_w), _BF)],
        compiler_params=_params("parallel", "arbitrary"),
        name=name,
    )(slopes, sinks, qkv, qkv, qkv, qkv, qkv, qkv, qkv)


def _mix_kernel(xp_ref, xs_ref, oa_ref, ob0_ref, ob1_ref, ob2_ref, l0_ref, l1_ref, l2_ref, g_ref,
                wpa_ref, wpb_ref, wout_ref, n2_ref, wr_ref, br_ref, h_ref, hn_ref, ri_ref, rw_ref, cnt_ref,
                o_scr, l_scr, carry_ref, lg_scr, mg_scr, *, n_tiles, n_prompt_tiles, d, tm):
    i = pl.program_id(0)

    @pl.when(i == 0)
    def _():
        carry_ref[...] = jnp.zeros_like(carry_ref)
        lg_scr[...] = jnp.zeros_like(lg_scr)
        mg_scr[...] = jnp.zeros_like(mg_scr)

    _route_tile(lg_scr[...], jnp.where(i > 1, 1.0, 0.0), ri_ref, rw_ref, cnt_ref, carry_ref)

    tile2 = jnp.clip(i - 1, 0, n_tiles - 1)
    x = jnp.where(tile2 < n_prompt_tiles, xp_ref[...], xs_ref[...])
    h_new = x + jnp.dot(mg_scr[(i + 1) % 2], wout_ref[...], preferred_element_type=_F32)
    h_ref[...] = h_new
    hn = h_new * lax.rsqrt(jnp.mean(h_new * h_new, axis=-1, keepdims=True) + EPS) * n2_ref[...]
    _store_row_tiles(hn_ref, _pack_halves(hn))
    hn_hi = hn.astype(_BF)
    hn_lo = (hn - hn_hi.astype(_F32)).astype(_BF)
    r = (jnp.dot(hn_hi, wr_ref[...], preferred_element_type=_F32)
         + jnp.dot(hn_lo, wr_ref[...], preferred_element_type=_F32))
    lg_scr[...] = r[:, :LANES] + r[:, LANES:] + br_ref[...]


    def token_order(o_ref, l_ref, r, slot):
        if r == 1:
            return [o_ref[0, :, h * HEAD_DIM:(h + 1) * HEAD_DIM].astype(_F32) for h in range(B_HPG)], l_ref[0]
        for p in range(r):
            for h in range(B_HPG):
                o_scr[slot, h, pl.ds(p, tm // r, stride=r), :] = (
                    o_ref[p, :, h * HEAD_DIM:(h + 1) * HEAD_DIM].astype(_F32))
            l_scr[slot, pl.ds(p, tm // r, stride=r), :] = l_ref[p]
        return [o_scr[slot, h] for h in range(B_HPG)], l_scr[slot]

    o0, l0 = token_order(ob0_ref, l0_ref, DILATED_GROUPS[0][1], 0)
    o1, l1 = token_order(ob1_ref, l1_ref, DILATED_GROUPS[1][1], 0)
    o2, l2 = token_order(ob2_ref, l2_ref, DILATED_GROUPS[2][1], 1)
    mx = jnp.maximum(jnp.maximum(l0, l1), l2)
    e0, e1, e2 = jnp.exp(l0 - mx), jnp.exp(l1 - mx), jnp.exp(l2 - mx)
    den = e0 + e1 + e2
    a0, a1, a2 = e0 / den, e1 / den, e2 / den
    parts = []
    for h in range(B_HPG):
        parts.append(a0[:, h:h + 1] * o0[h] + a1[:, h:h + 1] * o1[h] + a2[:, h:h + 1] * o2[h])
    ob = jnp.concatenate(parts, axis=1).astype(_BF)

    ta = jnp.dot(oa_ref[...], wpa_ref[...], preferred_element_type=_F32)
    tb = jnp.dot(ob, wpb_ref[...], preferred_element_type=_F32)
    merged = g_ref[:, :d].astype(_F32) * ta + g_ref[:, d:].astype(_F32) * tb
    mg_scr[i % 2] = merged.astype(_BF)


def _mix(xp, xs, oa, obs, lses, gates, wpa, wpb, wout, n2, wr, br):
    n_p, d = xp.shape
    n_s = xs.shape[0]
    t = n_p + n_s
    tm = ROW_TILE
    npt = n_p // tm
    n_tiles = t // tm
    def stage_tile(lag):
        return lambda i: jnp.clip(i - lag, 0, n_tiles - 1)

    row = lambda i: (stage_tile(0)(i), 0)
    row2 = lambda i: (stage_tile(1)(i), 0)
    routed = lambda i: (stage_tile(2)(i), 0)
    const = lambda i: (0, 0)
    width = B_HPG * HEAD_DIM

    def resident(shape):
        return pl.BlockSpec(shape, const, pipeline_mode=pl.Buffered(1))

    def phase_blocks(cols):
        return [pl.BlockSpec((r, tm // r, cols), lambda i: (0, stage_tile(0)(i), 0)) for _, r in DILATED_GROUPS]

    in_specs = [
        pl.BlockSpec((tm, d), lambda i: (jnp.minimum(stage_tile(1)(i), npt - 1), 0)),
        pl.BlockSpec((tm, d), lambda i: (jnp.maximum(stage_tile(1)(i) - npt, 0), 0)),
        pl.BlockSpec((tm, oa.shape[1]), row),
        *phase_blocks(width), *phase_blocks(LANES),
        pl.BlockSpec((tm, 2 * d), row),
        resident(wpa.shape), resident(wpb.shape), resident(wout.shape),
        resident((1, d)), resident(wr.shape), resident((1, LANES)),
    ]
    return pl.pallas_call(
        functools.partial(_mix_kernel, n_tiles=n_tiles, n_prompt_tiles=npt, d=d, tm=tm),
        grid=(n_tiles + 2,),
        in_specs=in_specs,
        out_specs=(pl.BlockSpec((tm, d), row2), pl.BlockSpec((tm * (d // 2 // LANES), LANES), row2),
                   pl.BlockSpec((tm, LANES), routed), pl.BlockSpec((tm, LANES), routed),
                   pl.BlockSpec((SUBLANES, LANES), const)),
        out_shape=(jax.ShapeDtypeStruct((t, d), _F32),
                   jax.ShapeDtypeStruct((t * (d // 2 // LANES), LANES), jnp.uint32),
                   jax.ShapeDtypeStruct((t, LANES), jnp.int32),
                   jax.ShapeDtypeStruct((t, LANES), _F32),
                   jax.ShapeDtypeStruct((SUBLANES, LANES), jnp.int32)),
        scratch_shapes=[pltpu.VMEM((2, B_HPG, tm, HEAD_DIM), _F32), pltpu.VMEM((2, tm, LANES), _F32),
                        pltpu.VMEM((SUBLANES, LANES), _F32), pltpu.VMEM((tm, LANES), _F32),
                        pltpu.VMEM((2, tm, d), _BF)],
        compiler_params=_params("arbitrary"),
        name="mix_out_router",
    )(xp, xs, oa, *obs, *lses, gates, wpa, wpb, wout, n2, wr, br)


def _route_tile(lg, live, ri_ref, rw_ref, cnt_ref, carry_ref):
    tr = lg.shape[0]
    lane_i = lax.broadcasted_iota(jnp.int32, (tr, LANES), 1)
    lane = lane_i.astype(_F32)
    no_lane = float(LANES)
    is_grp = lane_i < N_GROUPS
    glog = jnp.where(is_grp, lg, MASKED)
    gmax = jnp.max(glog, axis=-1, keepdims=True)
    grp = jnp.min(jnp.where(glog == gmax, lane, no_lane), axis=-1, keepdims=True)
    gsum = jnp.sum(jnp.where(is_grp, jnp.exp(glog - gmax), 0.0), axis=-1, keepdims=True)
    pgrp = 1.0 / gsum
    lane_grp = ((lane_i - N_GROUPS) // EXPERTS_PER_GROUP).astype(_F32)
    in_grp = (lane_i >= N_GROUPS) & (lane_i < N_GROUPS + N_EXPERTS) & (lane_grp == grp)
    elog = jnp.where(in_grp, lg, MASKED)
    t1 = jnp.max(elog, axis=-1, keepdims=True)
    i1 = jnp.min(jnp.where(elog == t1, lane, no_lane), axis=-1, keepdims=True)
    elog2 = jnp.where(lane == i1, MASKED, elog)
    t2 = jnp.max(elog2, axis=-1, keepdims=True)
    i2 = jnp.min(jnp.where(elog2 == t2, lane, no_lane), axis=-1, keepdims=True)
    e21 = jnp.exp(t2 - t1)
    w1 = pgrp / (1.0 + e21)
    w2 = pgrp * e21 / (1.0 + e21)
    eid1 = i1 - N_GROUPS
    eid2 = i2 - N_GROUPS
    hot1 = lane == eid1
    hot2 = lane == eid2
    onehot = (jnp.where(hot1, 1.0, 0.0) + jnp.where(hot2, 1.0, 0.0)) * live
    r_i = lax.broadcasted_iota(jnp.int32, (tr, tr), 0)
    c_i = lax.broadcasted_iota(jnp.int32, (tr, tr), 1)
    lower = jnp.where(c_i < r_i, 1.0, 0.0).astype(_BF)
    before = jnp.dot(lower, onehot.astype(_BF), preferred_element_type=_F32) + carry_ref[0:1, :]
    rank1 = jnp.sum(jnp.where(hot1, before, 0.0), axis=-1, keepdims=True)
    rank2 = jnp.sum(jnp.where(hot2, before, 0.0), axis=-1, keepdims=True)
    total = carry_ref[0:1, :] + jnp.sum(onehot, axis=0, keepdims=True)
    carry_ref[...] = jnp.broadcast_to(total, carry_ref.shape)
    cnt_ref[...] = jnp.broadcast_to(total, cnt_ref.shape).astype(jnp.int32)
    ri = jnp.where(lane_i == 0, eid1, jnp.where(lane_i == 1, eid2, jnp.where(lane_i == 2, rank1,
                   jnp.where(lane_i == 3, rank2, 0.0))))
    ri_ref[...] = ri.astype(jnp.int32)
    rw_ref[...] = jnp.where(lane_i == 0, w1, jnp.where(lane_i == 1, w2, 0.0))


def _dispatch_kernel(dest_ref, pad_start_ref, pad_n_ref, nvalid_ref, hn_ref, xr_ref, zeros, sem, pad_sem,
                     *, td, sc):
    i = pl.program_id(0)

    def rows_of(ref, row0, n):
        start = row0 * sc if isinstance(row0, int) else pl.multiple_of(row0 * sc, n * sc)
        return ref.at[pl.ds(start, n * sc), :]

    @pl.when(i == 0)
    def _():
        zeros[...] = jnp.zeros_like(zeros)

        def pad_chunk(e, r):
            row0 = pl.multiple_of(pad_start_ref[e] + r * SUBLANES, SUBLANES)
            return pltpu.make_async_copy(rows_of(zeros, 0, SUBLANES), rows_of(xr_ref, row0, SUBLANES), pad_sem)

        def per_expert(act):
            def body(e, carry):
                lax.fori_loop(0, pad_n_ref[e], lambda r, c: (act(pad_chunk(e, r)), c)[1], 0)
                return carry
            return body

        def tail_block(b):
            return pltpu.make_async_copy(zeros, rows_of(xr_ref, b * EXPERT_ROWS, EXPERT_ROWS), pad_sem)

        n_blocks = xr_ref.shape[0] // (EXPERT_ROWS * sc)
        lax.fori_loop(0, N_EXPERTS, per_expert(lambda cp: cp.start()), 0)
        lax.fori_loop(nvalid_ref[0], n_blocks, lambda b, c: (tail_block(b).start(), c)[1], 0)
        lax.fori_loop(0, N_EXPERTS, per_expert(lambda cp: cp.wait()), 0)
        lax.fori_loop(nvalid_ref[0], n_blocks, lambda b, c: (tail_block(b).wait(), c)[1], 0)

    def start(j, carry):
        for k in range(2):
            dst = dest_ref[2 * (i * td + j) + k]
            pltpu.make_async_copy(rows_of(hn_ref, j, 1), rows_of(xr_ref, dst, 1), sem).start(priority=k)
        return carry

    lax.fori_loop(0, td, start, 0, unroll=DMA_UNROLL)
    for _ in range(2):
        pltpu.make_async_copy(hn_ref, rows_of(xr_ref, 0, td), sem).wait()


def _dispatch(dest, pad_start, pad_n, nvalid, hn, t, rows):
    sc = hn.shape[0] // t
    td = DISPATCH_TILE
    return pl.pallas_call(
        functools.partial(_dispatch_kernel, td=td, sc=sc),
        grid_spec=pltpu.PrefetchScalarGridSpec(
            num_scalar_prefetch=4,
            grid=(t // td,),
            in_specs=[pl.BlockSpec((td * sc, LANES), lambda i, *_: (i, 0))],
            out_specs=pl.BlockSpec(memory_space=pl.ANY),
            scratch_shapes=[pltpu.VMEM((EXPERT_ROWS * sc, LANES), hn.dtype), pltpu.SemaphoreType.DMA(()),
                            pltpu.SemaphoreType.DMA(())],
        ),
        out_shape=jax.ShapeDtypeStruct((rows * sc, LANES), hn.dtype),
        compiler_params=_params("arbitrary"),
        name="dispatch",
    )(dest, pad_start, pad_n, nvalid, hn)


def _expert_kernel(blk_e_ref, nvalid_ref, first_ref, slot_ref, next_e_ref, x_ref, w1_hbm, w3_hbm, w2_hbm, y_ref,
                   w1_buf, w3_buf, w2_buf, sems):
    b = pl.program_id(0)
    valid = b < nvalid_ref[0]

    def fetch(e, slot):
        return (pltpu.make_async_copy(w1_hbm.at[e], w1_buf.at[slot], sems.at[slot, 0]),
                pltpu.make_async_copy(w3_hbm.at[e], w3_buf.at[slot], sems.at[slot, 1]),
                pltpu.make_async_copy(w2_hbm.at[e], w2_buf.at[slot], sems.at[slot, 2]))

    @pl.when(b == 0)
    def _():
        for cp in fetch(blk_e_ref[0], 0):
            cp.start()

    @pl.when(valid & (first_ref[b] == 1))
    def _():
        for cp in fetch(blk_e_ref[b], slot_ref[b]):
            cp.wait()

        @pl.when(next_e_ref[b] >= 0)
        def _():
            for cp in fetch(next_e_ref[b], 1 - slot_ref[b]):
                cp.start()

    @pl.when(valid)
    def _():
        slot = slot_ref[b]
        packed_x = _load_row_tiles(x_ref, EXPERT_ROWS)
        xb = jnp.concatenate([half.astype(_BF) for half in _unpack_halves(packed_x)], axis=1)
        a = jnp.dot(xb, w1_buf[slot].astype(_BF), preferred_element_type=_F32)
        u = jnp.dot(xb, w3_buf[slot].astype(_BF), preferred_element_type=_F32)
        hmid = (a / (1.0 + jnp.exp(-a))) * u
        y = jnp.dot(hmid.astype(_BF), w2_buf[slot].astype(_BF), preferred_element_type=_F32)
        _store_row_tiles(y_ref, _pack_halves(y))

    @pl.when(jnp.logical_not(valid))
    def _():
        y_ref[...] = jnp.zeros_like(y_ref)


def _experts(blk_e, nvalid, first, slot, next_e, xr, w1, w3, w2):
    _, d, f = w1.shape
    sc = d // 2 // LANES
    blk = EXPERT_ROWS * sc
    nblk = xr.shape[0] // blk
    any_space = pl.BlockSpec(memory_space=pl.ANY)
    return pl.pallas_call(
        _expert_kernel,
        grid_spec=pltpu.PrefetchScalarGridSpec(
            num_scalar_prefetch=5,
            grid=(nblk,),
            in_specs=[pl.BlockSpec((blk, LANES), lambda b, be, nv, *_: (jnp.minimum(b, nv[0] - 1), 0)),
                      any_space, any_space, any_space],
            out_specs=pl.BlockSpec((blk, LANES), lambda b, *_: (b, 0)),
            scratch_shapes=[pltpu.VMEM((2, d, f), w1.dtype), pltpu.VMEM((2, d, f), w3.dtype),
                            pltpu.VMEM((2, f, d), w2.dtype), pltpu.SemaphoreType.DMA((2, 3))],
        ),
        out_shape=jax.ShapeDtypeStruct(xr.shape, jnp.uint32),
        compiler_params=_params("arbitrary"),
        name="experts",
    )(blk_e, nvalid, first, slot, next_e, xr, w1, w3, w2)


def _combine_kernel(dest_ref, h_ref, rw_ref, g_ref, yr_ref, op_ref, os_ref, ybuf, sems, *, tc, sc, n_prompt_tiles):
    i = pl.program_id(0)
    slot = i % 2

    def gather(tile, s):
        def start(j, carry):
            for k in range(2):
                src = dest_ref[2 * (tile * tc + j) + k]
                pltpu.make_async_copy(_row_tile(yr_ref, src, sc), _row_tile(ybuf.at[s, k], j, sc),
                                      sems.at[s]).start(priority=k)
            return carry

        lax.fori_loop(0, tc, start, 0, unroll=DMA_UNROLL)

    @pl.when(i == 0)
    def _():
        gather(0, 0)

    @pl.when(i + 1 < pl.num_programs(0))
    def _():
        gather(i + 1, 1 - slot)

    for k in range(2):
        pltpu.make_async_copy(yr_ref.at[pl.ds(0, tc * sc), :], ybuf.at[slot, k], sems.at[slot]).wait()

    rw = rw_ref[...]
    y0 = _unpack_halves(_load_row_tiles(ybuf.at[slot, 0], tc))
    y1 = _unpack_halves(_load_row_tiles(ybuf.at[slot, 1], tc))
    y = jnp.concatenate([rw[:, 0:1] * a + rw[:, 1:2] * b for a, b in zip(y0, y1)], axis=1)
    z = h_ref[...] + y
    out = z * lax.rsqrt(jnp.mean(z * z, axis=-1, keepdims=True) + EPS) * g_ref[...]

    @pl.when(i < n_prompt_tiles)
    def _():
        op_ref[...] = out

    @pl.when(i >= n_prompt_tiles)
    def _():
        os_ref[...] = out


def _combine(dest, h, rw, g, yr, *, n_p):
    t, d = h.shape
    tc = ROW_TILE
    npt = n_p // tc
    sc = d // 2 // LANES
    return pl.pallas_call(
        functools.partial(_combine_kernel, tc=tc, sc=sc, n_prompt_tiles=npt),
        grid_spec=pltpu.PrefetchScalarGridSpec(
            num_scalar_prefetch=1,
            grid=(t // tc,),
            in_specs=[
                pl.BlockSpec((tc, d), lambda i, dest: (i, 0)),
                pl.BlockSpec((tc, LANES), lambda i, dest: (i, 0)),
                pl.BlockSpec((1, d), lambda i, dest: (0, 0)),
                pl.BlockSpec(memory_space=pl.ANY),
            ],
            out_specs=(pl.BlockSpec((tc, d), lambda i, dest: (jnp.minimum(i, npt - 1), 0)),
                       pl.BlockSpec((tc, d), lambda i, dest: (jnp.maximum(i - npt, 0), 0))),
            scratch_shapes=[pltpu.VMEM((2, 2, tc * sc, LANES), yr.dtype), pltpu.SemaphoreType.DMA((2,))],
        ),
        out_shape=(jax.ShapeDtypeStruct((n_p, d), _F32), jax.ShapeDtypeStruct((t - n_p, d), _F32)),
        compiler_params=_params("arbitrary"),
        name="combine_norm",
    )(dest, h, rw, g, yr)


def _layer(xp, xs, prompt_seq, sample_seq, norm1, w_in, sink, w_proj_a, w_proj_b, w_gate, b_gate, w_out,
           norm2, w_rg, b_rg, w_re, b_re, w1, w3, w2, norm_final):
    n_p, d = xp.shape
    n_s = xs.shape[0]
    t = n_p + n_s
    a_q = A_HEADS * HEAD_DIM
    a_cols = a_q + 2 * A_KV * HEAD_DIM

    proj_a, q0, q1, q2, gates = _project(xp, xs, norm1.reshape(1, d), w_in, w_gate, b_gate.reshape(1, 2 * d),
                                         a_cols=a_cols)

    oa = _band_attention(
        proj_a.reshape(1, t, a_cols), jnp.asarray(_alibi(A_HEADS)), sink.astype(_F32), n_inner=A_KV,
        bq=min(2048, prompt_seq, sample_seq), half=A_HALF, step=1, shared_kv=True, has_sink=True,
        prompt_rows=n_p, prompt_seq=prompt_seq, sample_seq=sample_seq, name="attn_window")

    slopes_b = _alibi(B_HEADS)
    obs, lses = [], []
    for gi, ((w, r), qkv) in enumerate(zip(DILATED_GROUPS, (q0, q1, q2))):
        o_g, lse_g = _band_attention(
            qkv, jnp.asarray(slopes_b[gi * B_HPG:(gi + 1) * B_HPG]), jnp.zeros((B_HPG,), _F32), n_inner=r,
            bq=min(2048, prompt_seq // r, sample_seq // r), half=w // (2 * r), step=r, shared_kv=False,
            has_sink=False, prompt_rows=n_p // r, prompt_seq=prompt_seq // r, sample_seq=sample_seq // r,
            name=f"attn_dilated_{r}")
        obs.append(o_g)
        lses.append(lse_g)

    n_r = N_GROUPS + N_EXPERTS
    wr = jnp.concatenate([w_rg, jnp.transpose(w_re, (1, 0, 2)).reshape(d, N_EXPERTS)], axis=1)
    wr = jnp.pad(wr, ((0, 0), (0, LANES - n_r)))
    wr_hi = wr.astype(_BF)
    wr_lo = (wr - wr_hi.astype(_F32)).astype(_BF)
    wr2 = jnp.concatenate([wr_hi, wr_lo], axis=1)
    br = jnp.pad(jnp.concatenate([b_rg, b_re.reshape(-1)]), (0, LANES - n_r)).reshape(1, LANES).astype(_F32)

    h, hn, ri, rw, cnt = _mix(xp, xs, oa, obs, lses, gates, w_proj_a.astype(_BF), w_proj_b.astype(_BF),
                              w_out.astype(_BF), norm2.reshape(1, d), wr2, br)
    counts = cnt[0, :N_EXPERTS]
    eid = ri[:, 0:2]
    rank = ri[:, 2:4]
    pcounts = (counts + EXPERT_ROWS - 1) // EXPERT_ROWS * EXPERT_ROWS
    pends = jnp.cumsum(pcounts)
    pstarts = pends - pcounts
    expert_ids = jnp.arange(N_EXPERTS, dtype=jnp.int32)
    start_of = jnp.sum(jnp.where(eid[:, :, None] == expert_ids, pstarts.astype(jnp.int32), 0), axis=-1)
    dest = (start_of + rank).reshape(-1).astype(jnp.int32)
    nblk = (2 * t + N_EXPERTS * (EXPERT_ROWS - 1) + EXPERT_ROWS - 1) // EXPERT_ROWS
    nvalid = (pends[-1] // EXPERT_ROWS).astype(jnp.int32)
    blk_start = jnp.minimum(jnp.arange(nblk, dtype=jnp.int32), nvalid - 1) * EXPERT_ROWS
    blk_e = jnp.sum(pends[None, :] <= blk_start[:, None], axis=1).astype(jnp.int32)

    pad_start = (pstarts + counts) // SUBLANES * SUBLANES
    xr = _dispatch(dest, pad_start.astype(jnp.int32), ((pends - pad_start) // SUBLANES).astype(jnp.int32),
                   nvalid.reshape(1), hn, t, nblk * EXPERT_ROWS)
    blk_ids = jnp.arange(nblk, dtype=jnp.int32)
    first = ((blk_ids == 0) | (blk_e != jnp.roll(blk_e, 1))).astype(jnp.int32)
    slot = ((jnp.cumsum(first) - 1) % 2).astype(jnp.int32)
    later = (expert_ids[None, :] > expert_ids[:, None]) & (counts[None, :] > 0)
    next_of = jnp.min(jnp.where(later, expert_ids[None, :], N_EXPERTS), axis=1)
    next_e = jnp.sum(jnp.where(blk_e[:, None] == expert_ids, next_of, 0), axis=1)
    next_e = jnp.where(next_e < N_EXPERTS, next_e, -1).astype(jnp.int32)
    yr = _experts(blk_e, nvalid.reshape(1), first, slot, next_e, xr, w1, w3, w2)
    return _combine(dest, h, rw, norm_final.reshape(1, d), yr, n_p=n_p)


def kernel(x_prompt, x_sample, norm1, w_in, attn_sink, w_proj_a, w_proj_b, w_gate, b_gate, w_out, norm2,
           w_router_group, b_router_group, w_router_expert, b_router_expert, w_expert_gate, w_expert_up,
           w_expert_down, norm_final):
    assert norm1.shape[0] == 1, "one layer"
    d = x_prompt.shape[-1]
    xp = x_prompt.reshape(-1, d)
    xs = x_sample.reshape(-1, d)
    yp, ys = _layer(xp, xs, x_prompt.shape[1], x_sample.shape[1], norm1[0], w_in[0], attn_sink[0], w_proj_a[0],
                    w_proj_b[0], w_gate[0], b_gate[0], w_out[0], norm2[0], w_router_group[0],
                    b_router_group[0], w_router_expert[0], b_router_expert[0], w_expert_gate[0],
                    w_expert_up[0], w_expert_down[0], norm_final)
    return yp.reshape(x_prompt.shape), ys.reshape(x_sample.shape)
```
